```python
import jax, jax.numpy as jnp
from jax import lax
import numpy as np

D_MODEL = 1024
BATCH = 16
SEQ = 2048
DEPTH = 1

RWKV_HEADS = 8
RWKV_HEAD_DIM = 64
D_RWKV = RWKV_HEADS * RWKV_HEAD_DIM
DECAY_LORA = 64
ICLR_LORA = 64
GATE_LORA = 128
GN_EPS = 64e-5
L2_EPS = 1e-12
MLA_HEADS = 8
MLA_NOPE_DIM = 64
MLA_ROPE_DIM = 32
MLA_V_DIM = 64
D_MLA = MLA_HEADS * MLA_V_DIM
Q_LORA_RANK = 768
KV_LORA_RANK = 256
ROPE_THETA = 10000.0
MLA_SCALE = (MLA_NOPE_DIM + MLA_ROPE_DIM) ** -0.5
Q_BLOCK = 128
D_MIX = D_RWKV + D_MLA
RWKV_SPLITS = (D_RWKV, D_RWKV, D_RWKV, DECAY_LORA, DECAY_LORA, ICLR_LORA, ICLR_LORA, GATE_LORA)
MLA_SPLITS = (Q_LORA_RANK, KV_LORA_RANK, MLA_ROPE_DIM)
RWKV_COLS = 3 * D_RWKV + 2 * DECAY_LORA + 2 * ICLR_LORA + GATE_LORA
MLA_COLS = Q_LORA_RANK + KV_LORA_RANK + MLA_ROPE_DIM
D_IN_PROJ = RWKV_COLS + MLA_COLS
D_FF = 2816
CONV_WIDTH = 3
NORM_EPS = 1e-6

kernel_name = 'hymba_rwkv7_mla_convffn_bidir'


def split_cols(z, sizes):
    bounds = np.cumsum(sizes)[:-1].tolist()
    return jnp.split(z, bounds, axis=-1)


def rms_norm(x, g):
    xf = x.astype(jnp.float32)
    y = xf * lax.rsqrt(jnp.mean(xf * xf, axis=-1, keepdims=True) + NORM_EPS)
    return (y * g.astype(jnp.float32)).astype(x.dtype)


def centred_token_shift(z, mu_prev, mu_next):
    z_prev = jnp.pad(z[:, :-1], ((0, 0), (1, 0), (0, 0)))
    z_next = jnp.pad(z[:, 1:], ((0, 0), (0, 1), (0, 0)))
    return z + mu_prev * (z_prev - z) + mu_next * (z_next - z)


def wkv7_scan(r, w, k, v, kk, kka, reverse):
    B, T, H, N = r.shape

    def step(S, inp):
        r_t, w_t, k_t, v_t, kk_t, kka_t = inp
        s_kk = jnp.einsum('bhij,bhj->bhi', S, kk_t)
        S = (S * w_t[:, :, None, :]
             - s_kk[..., None] * kka_t[:, :, None, :]
             + v_t[..., None] * k_t[:, :, None, :])
        return S, jnp.einsum('bhij,bhj->bhi', S, r_t)

    xs = tuple(jnp.swapaxes(t, 0, 1) for t in (r, w, k, v, kk, kka))
    S0 = jnp.zeros((B, H, N, N), jnp.float32)
    _, ys = lax.scan(step, S0, xs, reverse=reverse)
    return jnp.swapaxes(ys, 0, 1)


def rwkv7_decay(wd, w0, w2):
    logit = w0 + jnp.tanh(wd) @ w2
    return jnp.exp(-jnp.exp(-jax.nn.softplus(-logit) - 0.5))


def rwkv7_bidir_mixer(z, mu_prev, mu_next, w0f, w2f, w0b, w2b, a0f, a2f, a0b, a2b,
                      g2, k_k, k_a, r_k, lnx_g, lnx_b):
    B, T, _ = z.shape
    H, N = RWKV_HEADS, RWKV_HEAD_DIM
    f32 = lambda t: t.astype(jnp.float32)
    heads = lambda t: t.reshape(B, T, H, N)
    z = centred_token_shift(f32(z), f32(mu_prev), f32(mu_next))
    r, k, v, wdf, wdb, adf, adb, gd = split_cols(z, RWKV_SPLITS)
    w_f = rwkv7_decay(wdf, f32(w0f), f32(w2f))
    w_b = rwkv7_decay(wdb, f32(w0b), f32(w2b))
    a_f = jax.nn.sigmoid(f32(a0f) + adf @ f32(a2f))
    a_b = jax.nn.sigmoid(f32(a0b) + adb @ f32(a2b))
    g = jax.nn.sigmoid(gd) @ f32(g2)
    kk = heads(k * f32(k_k))
    kk = kk / jnp.maximum(jnp.linalg.norm(kk, axis=-1, keepdims=True), L2_EPS)
    k_f = k * (1.0 + (a_f - 1.0) * f32(k_a))
    k_b = k * (1.0 + (a_b - 1.0) * f32(k_a))
    r_h, v_h = heads(r), heads(v)
    y = (wkv7_scan(r_h, heads(w_f), heads(k_f), v_h, kk, kk * heads(a_f), False)
         + wkv7_scan(r_h, heads(w_b), heads(k_b), v_h, kk, kk * heads(a_b), True))
    mu = jnp.mean(y, axis=-1, keepdims=True)
    var = jnp.mean(jnp.square(y - mu), axis=-1, keepdims=True)
    y = ((y - mu) * lax.rsqrt(var + GN_EPS)).reshape(B, T, D_RWKV) * f32(lnx_g) + f32(lnx_b)
    bonus = jnp.sum(r_h * heads(k_f + k_b) * f32(r_k), axis=-1, keepdims=True) * v_h
    return (y + bonus.reshape(B, T, D_RWKV)) * g


def rope_tables(T):
    inv_freq = jnp.power(ROPE_THETA, -jnp.arange(0, MLA_ROPE_DIM, 2, dtype=jnp.float32) / MLA_ROPE_DIM)
    ang = jnp.arange(T, dtype=jnp.float32)[:, None] * inv_freq[None, :]
    ang = jnp.concatenate([ang, ang], axis=-1)
    return jnp.cos(ang), jnp.sin(ang)


def apply_rope(x, cos, sin):
    x1, x2 = jnp.split(x, 2, axis=-1)
    rot = jnp.concatenate([-x2, x1], axis=-1)
    return x * cos.astype(x.dtype) + rot * sin.astype(x.dtype)


def bidir_block_attention(q_nope, q_rope, k_nope, k_rope, v):
    B, T, H, _ = q_nope.shape
    nb = T // Q_BLOCK
    to_blocks = lambda t: jnp.moveaxis(t.reshape(B, nb, Q_BLOCK, H, t.shape[-1]), 1, 0)

    def one_block(qs):
        qn, qr = qs
        s = (jnp.einsum('bqhd,bkhd->bhqk', qn, k_nope)
             + jnp.einsum('bqhr,bkr->bhqk', qr, k_rope))
        p = jax.nn.softmax(s.astype(jnp.float32) * MLA_SCALE, axis=-1).astype(v.dtype)
        return jnp.einsum('bhqk,bkhd->bqhd', p, v)

    o = lax.map(one_block, (to_blocks(q_nope), to_blocks(q_rope)))
    return jnp.moveaxis(o, 0, 1).reshape(B, T, H, MLA_V_DIM)


def mla_mixer(z, q_norm_g, w_uq, kv_norm_g, w_ukv, out_g, cos, sin):
    B, T, _ = z.shape
    H = MLA_HEADS
    c_q, c_kv, k_rope = split_cols(z, MLA_SPLITS)
    q = (rms_norm(c_q, q_norm_g) @ w_uq).reshape(B, T, H, MLA_NOPE_DIM + MLA_ROPE_DIM)
    q_nope = q[..., :MLA_NOPE_DIM]
    q_rope = apply_rope(q[..., MLA_NOPE_DIM:], cos[:, None, :], sin[:, None, :])
    kv = (rms_norm(c_kv, kv_norm_g) @ w_ukv).reshape(B, T, H, MLA_NOPE_DIM + MLA_V_DIM)
    k_nope, v = kv[..., :MLA_NOPE_DIM], kv[..., MLA_NOPE_DIM:]
    k_rope = apply_rope(k_rope, cos, sin)
    o = bidir_block_attention(q_nope, q_rope, k_nope, k_rope, v)
    return rms_norm(o.reshape(B, T, D_MLA), out_g)


def conv_ffn(n, w_up, conv_w, conv_b, w_down):
    u = n @ w_up
    pad = CONV_WIDTH // 2
    u = lax.conv_general_dilated(u, conv_w[:, None, :].astype(u.dtype), window_strides=(1,),
                                 padding=((pad, pad),), dimension_numbers=('NWC', 'WIO', 'NWC'),
                                 feature_group_count=2 * D_FF) + conv_b
    gate, val = jnp.split(u, 2, axis=-1)
    return (jax.nn.silu(gate) * val) @ w_down


def _fwd_setup_inputs(seed: int = 0) -> dict:
    key = jax.random.key(seed)
    k = jax.random.split(key, 31)
    L = DEPTH
    nrm = lambda kk, shape, scale: scale * jax.random.normal(kk, shape, jnp.float32)
    gain = lambda kk, shape: 1.0 + 0.02 * jax.random.normal(kk, shape, jnp.float32)
    unif = lambda kk, shape, lo, hi: jax.random.uniform(kk, shape, jnp.float32, lo, hi)
    return {
        'x': nrm(k[0], (BATCH, SEQ, D_MODEL), 1.0),
        'ln_mix_g': gain(k[1], (L, D_MODEL)),
        'w_in': nrm(k[2], (L, D_MODEL, D_IN_PROJ), D_MODEL ** -0.5),
        'shift_mu_prev': unif(k[3], (L, RWKV_COLS), 0.0, 0.5),
        'shift_mu_next': unif(k[4], (L, RWKV_COLS), 0.0, 0.5),
        'decay_w0_fwd': unif(k[5], (L, D_RWKV), -6.5, -1.5),
        'decay_w2_fwd': nrm(k[6], (L, DECAY_LORA, D_RWKV), 0.5 * DECAY_LORA ** -0.5),
        'decay_w0_bwd': unif(k[7], (L, D_RWKV), -6.5, -1.5),
        'decay_w2_bwd': nrm(k[8], (L, DECAY_LORA, D_RWKV), 0.5 * DECAY_LORA ** -0.5),
        'iclr_a0_fwd': nrm(k[9], (L, D_RWKV), 0.5),
        'iclr_a2_fwd': nrm(k[10], (L, ICLR_LORA, D_RWKV), 0.5 * ICLR_LORA ** -0.5),
        'iclr_a0_bwd': nrm(k[11], (L, D_RWKV), 0.5),
        'iclr_a2_bwd': nrm(k[12], (L, ICLR_LORA, D_RWKV), 0.5 * ICLR_LORA ** -0.5),
        'gate_g2': nrm(k[13], (L, GATE_LORA, D_RWKV), GATE_LORA ** -0.5),
        'k_k': 0.85 + nrm(k[14], (L, D_RWKV), 0.02),
        'k_a': gain(k[15], (L, D_RWKV)),
        'r_k': nrm(k[16], (L, RWKV_HEADS, RWKV_HEAD_DIM), 0.1),
        'ln_x_g': gain(k[17], (L, D_RWKV)),
        'ln_x_b': nrm(k[18], (L, D_RWKV), 0.02),
        'q_norm_g': gain(k[19], (L, Q_LORA_RANK)),
        'w_uq': nrm(k[20], (L, Q_LORA_RANK, MLA_HEADS * (MLA_NOPE_DIM + MLA_ROPE_DIM)), Q_LORA_RANK ** -0.5),
        'kv_norm_g': gain(k[21], (L, KV_LORA_RANK)),
        'w_ukv': nrm(k[22], (L, KV_LORA_RANK, MLA_HEADS * (MLA_NOPE_DIM + MLA_V_DIM)), KV_LORA_RANK ** -0.5),
        'mla_out_g': gain(k[23], (L, D_MLA)),
        'w_out': nrm(k[24], (L, D_MIX, D_MODEL), D_MIX ** -0.5),
        'ln_ffn_g': gain(k[25], (L, D_MODEL)),
        'w_ffn_up': nrm(k[26], (L, D_MODEL, 2 * D_FF), D_MODEL ** -0.5),
        'ffn_conv_w': nrm(k[27], (L, CONV_WIDTH, 2 * D_FF), CONV_WIDTH ** -0.5),
        'ffn_conv_b': nrm(k[28], (L, 2 * D_FF), 0.02),
        'w_ffn_down': nrm(k[29], (L, D_FF, D_MODEL), D_FF ** -0.5),
        'ln_final_g': gain(k[30], (D_MODEL,)),
    }


def _fwd_reference(x, ln_mix_g, w_in, shift_mu_prev, shift_mu_next, decay_w0_fwd, decay_w2_fwd,
              decay_w0_bwd, decay_w2_bwd, iclr_a0_fwd, iclr_a2_fwd, iclr_a0_bwd, iclr_a2_bwd,
              gate_g2, k_k, k_a, r_k, ln_x_g, ln_x_b, q_norm_g, w_uq, kv_norm_g, w_ukv,
              mla_out_g, w_out, ln_ffn_g, w_ffn_up, ffn_conv_w, ffn_conv_b, w_ffn_down, ln_final_g):
    T = x.shape[1]
    cos, sin = rope_tables(T)
    h = x
    for l in range(DEPTH):
        n = rms_norm(h, ln_mix_g[l])
        z = n @ w_in[l]
        z_rwkv, z_mla = z[..., :RWKV_COLS], z[..., RWKV_COLS:]
        y_rwkv = rwkv7_bidir_mixer(z_rwkv, shift_mu_prev[l], shift_mu_next[l],
                                   decay_w0_fwd[l], decay_w2_fwd[l], decay_w0_bwd[l], decay_w2_bwd[l],
                                   iclr_a0_fwd[l], iclr_a2_fwd[l], iclr_a0_bwd[l], iclr_a2_bwd[l],
                                   gate_g2[l], k_k[l], k_a[l], r_k[l], ln_x_g[l], ln_x_b[l]).astype(h.dtype)
        y_mla = mla_mixer(z_mla, q_norm_g[l], w_uq[l], kv_norm_g[l], w_ukv[l], mla_out_g[l], cos, sin)
        h = h + jnp.concatenate([y_rwkv, y_mla], axis=-1) @ w_out[l]
        h = h + conv_ffn(rms_norm(h, ln_ffn_g[l]), w_ffn_up[l], ffn_conv_w[l], ffn_conv_b[l], w_ffn_down[l])
    return rms_norm(h, ln_final_g)


import jax as _jax
import jax.numpy as _jnp

TWIN_FORMAT = 'train_step'
FWD_PARAMS = ['x', 'ln_mix_g', 'w_in', 'shift_mu_prev', 'shift_mu_next', 'decay_w0_fwd', 'decay_w2_fwd', 'decay_w0_bwd', 'decay_w2_bwd', 'iclr_a0_fwd', 'iclr_a2_fwd', 'iclr_a0_bwd', 'iclr_a2_bwd', 'gate_g2', 'k_k', 'k_a', 'r_k', 'ln_x_g', 'ln_x_b', 'q_norm_g', 'w_uq', 'kv_norm_g', 'w_ukv', 'mla_out_g', 'w_out', 'ln_ffn_g', 'w_ffn_up', 'ffn_conv_w', 'ffn_conv_b', 'w_ffn_down', 'ln_final_g']
TWIN_WEIGHTS = ['ln_mix_g', 'w_in', 'shift_mu_prev', 'shift_mu_next', 'decay_w0_fwd', 'decay_w2_fwd', 'decay_w0_bwd', 'decay_w2_bwd', 'iclr_a0_fwd', 'iclr_a2_fwd', 'iclr_a0_bwd', 'iclr_a2_bwd', 'gate_g2', 'k_k', 'k_a', 'r_k', 'ln_x_g', 'ln_x_b', 'q_norm_g', 'w_uq', 'kv_norm_g', 'w_ukv', 'mla_out_g', 'w_out', 'ln_ffn_g', 'w_ffn_up', 'ffn_conv_w', 'ffn_conv_b', 'w_ffn_down', 'ln_final_g']
TWIN_DIFF_INPUT = 'x'
TWIN_INPUTS = ['x', 'ln_mix_g', 'w_in', 'shift_mu_prev', 'shift_mu_next', 'decay_w0_fwd', 'decay_w2_fwd', 'decay_w0_bwd', 'decay_w2_bwd', 'iclr_a0_fwd', 'iclr_a2_fwd', 'iclr_a0_bwd', 'iclr_a2_bwd', 'gate_g2', 'k_k', 'k_a', 'r_k', 'ln_x_g', 'ln_x_b', 'q_norm_g', 'w_uq', 'kv_norm_g', 'w_ukv', 'mla_out_g', 'w_out', 'ln_ffn_g', 'w_ffn_up', 'ffn_conv_w', 'ffn_conv_b', 'w_ffn_down', 'ln_final_g', 'loss_target', 'm_ln_mix_g', 'm_w_in', 'm_shift_mu_prev', 'm_shift_mu_next', 'm_decay_w0_fwd', 'm_decay_w2_fwd', 'm_decay_w0_bwd', 'm_decay_w2_bwd', 'm_iclr_a0_fwd', 'm_iclr_a2_fwd', 'm_iclr_a0_bwd', 'm_iclr_a2_bwd', 'm_gate_g2', 'm_k_k', 'm_k_a', 'm_r_k', 'm_ln_x_g', 'm_ln_x_b', 'm_q_norm_g', 'm_w_uq', 'm_kv_norm_g', 'm_w_ukv', 'm_mla_out_g', 'm_w_out', 'm_ln_ffn_g', 'm_w_ffn_up', 'm_ffn_conv_w', 'm_ffn_conv_b', 'm_w_ffn_down', 'm_ln_final_g', 'v_ln_mix_g', 'v_w_in', 'v_shift_mu_prev', 'v_shift_mu_next', 'v_decay_w0_fwd', 'v_decay_w2_fwd', 'v_decay_w0_bwd', 'v_decay_w2_bwd', 'v_iclr_a0_fwd', 'v_iclr_a2_fwd', 'v_iclr_a0_bwd', 'v_iclr_a2_bwd', 'v_gate_g2', 'v_k_k', 'v_k_a', 'v_r_k', 'v_ln_x_g', 'v_ln_x_b', 'v_q_norm_g', 'v_w_uq', 'v_kv_norm_g', 'v_w_ukv', 'v_mla_out_g', 'v_w_out', 'v_ln_ffn_g', 'v_w_ffn_up', 'v_ffn_conv_w', 'v_ffn_conv_b', 'v_w_ffn_down', 'v_ln_final_g']
TWIN_OUTPUTS = ['loss', 'grad_x', 'grad_ln_mix_g', 'grad_w_in', 'grad_shift_mu_prev', 'grad_shift_mu_next', 'grad_decay_w0_fwd', 'grad_decay_w2_fwd', 'grad_decay_w0_bwd', 'grad_decay_w2_bwd', 'grad_iclr_a0_fwd', 'grad_iclr_a2_fwd', 'grad_iclr_a0_bwd', 'grad_iclr_a2_bwd', 'grad_gate_g2', 'grad_k_k', 'grad_k_a', 'grad_r_k', 'grad_ln_x_g', 'grad_ln_x_b', 'grad_q_norm_g', 'grad_w_uq', 'grad_kv_norm_g', 'grad_w_ukv', 'grad_mla_out_g', 'grad_w_out', 'grad_ln_ffn_g', 'grad_w_ffn_up', 'grad_ffn_conv_w', 'grad_ffn_conv_b', 'grad_w_ffn_down', 'grad_ln_final_g', 'delta_ln_mix_g', 'delta_w_in', 'delta_shift_mu_prev', 'delta_shift_mu_next', 'delta_decay_w0_fwd', 'delta_decay_w2_fwd', 'delta_decay_w0_bwd', 'delta_decay_w2_bwd', 'delta_iclr_a0_fwd', 'delta_iclr_a2_fwd', 'delta_iclr_a0_bwd', 'delta_iclr_a2_bwd', 'delta_gate_g2', 'delta_k_k', 'delta_k_a', 'delta_r_k', 'delta_ln_x_g', 'delta_ln_x_b', 'delta_q_norm_g', 'delta_w_uq', 'delta_kv_norm_g', 'delta_w_ukv', 'delta_mla_out_g', 'delta_w_out', 'delta_ln_ffn_g', 'delta_w_ffn_up', 'delta_ffn_conv_w', 'delta_ffn_conv_b', 'delta_w_ffn_down', 'delta_ln_final_g', 'new_m_ln_mix_g', 'new_m_w_in', 'new_m_shift_mu_prev', 'new_m_shift_mu_next', 'new_m_decay_w0_fwd', 'new_m_decay_w2_fwd', 'new_m_decay_w0_bwd', 'new_m_decay_w2_bwd', 'new_m_iclr_a0_fwd', 'new_m_iclr_a2_fwd', 'new_m_iclr_a0_bwd', 'new_m_iclr_a2_bwd', 'new_m_gate_g2', 'new_m_k_k', 'new_m_k_a', 'new_m_r_k', 'new_m_ln_x_g', 'new_m_ln_x_b', 'new_m_q_norm_g', 'new_m_w_uq', 'new_m_kv_norm_g', 'new_m_w_ukv', 'new_m_mla_out_g', 'new_m_w_out', 'new_m_ln_ffn_g', 'new_m_w_ffn_up', 'new_m_ffn_conv_w', 'new_m_ffn_conv_b', 'new_m_w_ffn_down', 'new_m_ln_final_g', 'new_v_ln_mix_g', 'new_v_w_in', 'new_v_shift_mu_prev', 'new_v_shift_mu_next', 'new_v_decay_w0_fwd', 'new_v_decay_w2_fwd', 'new_v_decay_w0_bwd', 'new_v_decay_w2_bwd', 'new_v_iclr_a0_fwd', 'new_v_iclr_a2_fwd', 'new_v_iclr_a0_bwd', 'new_v_iclr_a2_bwd', 'new_v_gate_g2', 'new_v_k_k', 'new_v_k_a', 'new_v_r_k', 'new_v_ln_x_g', 'new_v_ln_x_b', 'new_v_q_norm_g', 'new_v_w_uq', 'new_v_kv_norm_g', 'new_v_w_ukv', 'new_v_mla_out_g', 'new_v_w_out', 'new_v_ln_ffn_g', 'new_v_w_ffn_up', 'new_v_ffn_conv_w', 'new_v_ffn_conv_b', 'new_v_w_ffn_down', 'new_v_ln_final_g']
TWIN_LEAF_KINDS = {'loss': 'loss', 'grad_x': 'grad_x', 'grad_ln_mix_g': 'grad_w', 'grad_w_in': 'grad_w', 'grad_shift_mu_prev': 'grad_w', 'grad_shift_mu_next': 'grad_w', 'grad_decay_w0_fwd': 'grad_w', 'grad_decay_w2_fwd': 'grad_w', 'grad_decay_w0_bwd': 'grad_w', 'grad_decay_w2_bwd': 'grad_w', 'grad_iclr_a0_fwd': 'grad_w', 'grad_iclr_a2_fwd': 'grad_w', 'grad_iclr_a0_bwd': 'grad_w', 'grad_iclr_a2_bwd': 'grad_w', 'grad_gate_g2': 'grad_w', 'grad_k_k': 'grad_w', 'grad_k_a': 'grad_w', 'grad_r_k': 'grad_w', 'grad_ln_x_g': 'grad_w', 'grad_ln_x_b': 'grad_w', 'grad_q_norm_g': 'grad_w', 'grad_w_uq': 'grad_w', 'grad_kv_norm_g': 'grad_w', 'grad_w_ukv': 'grad_w', 'grad_mla_out_g': 'grad_w', 'grad_w_out': 'grad_w', 'grad_ln_ffn_g': 'grad_w', 'grad_w_ffn_up': 'grad_w', 'grad_ffn_conv_w': 'grad_w', 'grad_ffn_conv_b': 'grad_w', 'grad_w_ffn_down': 'grad_w', 'grad_ln_final_g': 'grad_w', 'delta_ln_mix_g': 'delta_w', 'delta_w_in': 'delta_w', 'delta_shift_mu_prev': 'delta_w', 'delta_shift_mu_next': 'delta_w', 'delta_decay_w0_fwd': 'delta_w', 'delta_decay_w2_fwd': 'delta_w', 'delta_decay_w0_bwd': 'delta_w', 'delta_decay_w2_bwd': 'delta_w', 'delta_iclr_a0_fwd': 'delta_w', 'delta_iclr_a2_fwd': 'delta_w', 'delta_iclr_a0_bwd': 'delta_w', 'delta_iclr_a2_bwd': 'delta_w', 'delta_gate_g2': 'delta_w', 'delta_k_k': 'delta_w', 'delta_k_a': 'delta_w', 'delta_r_k': 'delta_w', 'delta_ln_x_g': 'delta_w', 'delta_ln_x_b': 'delta_w', 'delta_q_norm_g': 'delta_w', 'delta_w_uq': 'delta_w', 'delta_kv_norm_g': 'delta_w', 'delta_w_ukv': 'delta_w', 'delta_mla_out_g': 'delta_w', 'delta_w_out': 'delta_w', 'delta_ln_ffn_g': 'delta_w', 'delta_w_ffn_up': 'delta_w', 'delta_ffn_conv_w': 'delta_w', 'delta_ffn_conv_b': 'delta_w', 'delta_w_ffn_down': 'delta_w', 'delta_ln_final_g': 'delta_w', 'new_m_ln_mix_g': 'new_m', 'new_m_w_in': 'new_m', 'new_m_shift_mu_prev': 'new_m', 'new_m_shift_mu_next': 'new_m', 'new_m_decay_w0_fwd': 'new_m', 'new_m_decay_w2_fwd': 'new_m', 'new_m_decay_w0_bwd': 'new_m', 'new_m_decay_w2_bwd': 'new_m', 'new_m_iclr_a0_fwd': 'new_m', 'new_m_iclr_a2_fwd': 'new_m', 'new_m_iclr_a0_bwd': 'new_m', 'new_m_iclr_a2_bwd': 'new_m', 'new_m_gate_g2': 'new_m', 'new_m_k_k': 'new_m', 'new_m_k_a': 'new_m', 'new_m_r_k': 'new_m', 'new_m_ln_x_g': 'new_m', 'new_m_ln_x_b': 'new_m', 'new_m_q_norm_g': 'new_m', 'new_m_w_uq': 'new_m', 'new_m_kv_norm_g': 'new_m', 'new_m_w_ukv': 'new_m', 'new_m_mla_out_g': 'new_m', 'new_m_w_out': 'new_m', 'new_m_ln_ffn_g': 'new_m', 'new_m_w_ffn_up': 'new_m', 'new_m_ffn_conv_w': 'new_m', 'new_m_ffn_conv_b': 'new_m', 'new_m_w_ffn_down': 'new_m', 'new_m_ln_final_g': 'new_m', 'new_v_ln_mix_g': 'new_v', 'new_v_w_in': 'new_v', 'new_v_shift_mu_prev': 'new_v', 'new_v_shift_mu_next': 'new_v', 'new_v_decay_w0_fwd': 'new_v', 'new_v_decay_w2_fwd': 'new_v', 'new_v_decay_w0_bwd': 'new_v', 'new_v_decay_w2_bwd': 'new_v', 'new_v_iclr_a0_fwd': 'new_v', 'new_v_iclr_a2_fwd': 'new_v', 'new_v_iclr_a0_bwd': 'new_v', 'new_v_iclr_a2_bwd': 'new_v', 'new_v_gate_g2': 'new_v', 'new_v_k_k': 'new_v', 'new_v_k_a': 'new_v', 'new_v_r_k': 'new_v', 'new_v_ln_x_g': 'new_v', 'new_v_ln_x_b': 'new_v', 'new_v_q_norm_g': 'new_v', 'new_v_w_uq': 'new_v', 'new_v_kv_norm_g': 'new_v', 'new_v_w_ukv': 'new_v', 'new_v_mla_out_g': 'new_v', 'new_v_w_out': 'new_v', 'new_v_ln_ffn_g': 'new_v', 'new_v_w_ffn_up': 'new_v', 'new_v_ffn_conv_w': 'new_v', 'new_v_ffn_conv_b': 'new_v', 'new_v_w_ffn_down': 'new_v', 'new_v_ln_final_g': 'new_v'}


def _forward(args):
    return _fwd_reference(*[args[k] for k in FWD_PARAMS])


def _output_shape():
    out = _jax.eval_shape(lambda: _forward(_fwd_setup_inputs(0)))
    return out.shape, out.dtype

N_MICROBATCH = 1
ADAM_LR = 0.001
ADAM_B1 = 0.9
ADAM_B2 = 0.999
ADAM_EPS = 1e-08
ADAM_WD = 0.01
ADAM_STEP = 10
PER_EXAMPLE_BATCH_AXIS = {'x': 0, 'loss_target': 0}
SHARED_INPUTS = []
_WEIGHT_DTYPES = {'ln_mix_g': _jnp.float32, 'w_in': _jnp.float32, 'shift_mu_prev': _jnp.float32, 'shift_mu_next': _jnp.float32, 'decay_w0_fwd': _jnp.float32, 'decay_w2_fwd': _jnp.float32, 'decay_w0_bwd': _jnp.float32, 'decay_w2_bwd': _jnp.float32, 'iclr_a0_fwd': _jnp.float32, 'iclr_a2_fwd': _jnp.float32, 'iclr_a0_bwd': _jnp.float32, 'iclr_a2_bwd': _jnp.float32, 'gate_g2': _jnp.float32, 'k_k': _jnp.float32, 'k_a': _jnp.float32, 'r_k': _jnp.float32, 'ln_x_g': _jnp.float32, 'ln_x_b': _jnp.float32, 'q_norm_g': _jnp.float32, 'w_uq': _jnp.float32, 'kv_norm_g': _jnp.float32, 'w_ukv': _jnp.float32, 'mla_out_g': _jnp.float32, 'w_out': _jnp.float32, 'ln_ffn_g': _jnp.float32, 'w_ffn_up': _jnp.float32, 'ffn_conv_w': _jnp.float32, 'ffn_conv_b': _jnp.float32, 'w_ffn_down': _jnp.float32, 'ln_final_g': _jnp.float32}
MOMENT_SCALE = {'ln_mix_g': 2.206157e-01, 'w_in': 1.187990e-01, 'shift_mu_prev': 1.404889e-01, 'shift_mu_next': 1.444456e-01, 'decay_w0_fwd': 2.322670e-02, 'decay_w2_fwd': 2.404075e-03, 'decay_w0_bwd': 2.170217e-02, 'decay_w2_bwd': 2.336586e-03, 'iclr_a0_fwd': 2.404488e-02, 'iclr_a2_fwd': 1.712589e-02, 'iclr_a0_bwd': 2.226162e-02, 'iclr_a2_bwd': 1.736656e-02, 'gate_g2': 8.203845e-02, 'k_k': 1.185846e-01, 'k_a': 1.196255e-01, 'r_k': 2.123638e-01, 'ln_x_g': 7.964345e-02, 'ln_x_b': 8.751902e-02, 'q_norm_g': 1.266331e-01, 'w_uq': 1.257498e-01, 'kv_norm_g': 5.103196e-01, 'w_ukv': 1.527512e-01, 'mla_out_g': 1.512220e-01, 'w_out': 1.236725e-01, 'ln_ffn_g': 1.116611e-01, 'w_ffn_up': 4.702513e-02, 'ffn_conv_w': 4.704835e-02, 'ffn_conv_b': 4.589394e-02, 'w_ffn_down': 7.702018e-02, 'ln_final_g': 3.189438e+01}


def _to_microbatches(a, axis):
    t = _jnp.moveaxis(a, axis, 0)
    t = t.reshape((N_MICROBATCH, t.shape[0] // N_MICROBATCH) + t.shape[1:])
    return _jnp.moveaxis(t, 1, axis + 1)


def setup_inputs(seed: int = 0) -> dict:
    inp = _fwd_setup_inputs(seed)
    key = _jax.random.fold_in(_jax.random.key(seed), 7919)
    shape, _ = _output_shape()
    out = dict(inp)
    out["loss_target"] = _jax.random.normal(_jax.random.fold_in(key, 0), shape, _jnp.float32)
    for i, name in enumerate(TWIN_WEIGHTS):
        w = inp[name].astype(_jnp.float32)
        if MOMENT_SCALE is None:
            s = _jnp.sqrt(_jnp.mean(_jnp.square(w)) + 1e-30)
        else:
            s = MOMENT_SCALE[name]
        km, kv = _jax.random.split(_jax.random.fold_in(key, i + 1))
        out[name] = w
        out["m_" + name] = s * _jax.random.normal(km, w.shape, _jnp.float32)
        out["v_" + name] = (s * s) * _jax.random.uniform(kv, w.shape, _jnp.float32, 0.5, 1.5)
    if N_MICROBATCH > 1:
        for name, axis in PER_EXAMPLE_BATCH_AXIS.items():
            out[name] = _to_microbatches(out[name], axis)
    return {'x': out['x'], 'ln_mix_g': out['ln_mix_g'], 'w_in': out['w_in'], 'shift_mu_prev': out['shift_mu_prev'], 'shift_mu_next': out['shift_mu_next'], 'decay_w0_fwd': out['decay_w0_fwd'], 'decay_w2_fwd': out['decay_w2_fwd'], 'decay_w0_bwd': out['decay_w0_bwd'], 'decay_w2_bwd': out['decay_w2_bwd'], 'iclr_a0_fwd': out['iclr_a0_fwd'], 'iclr_a2_fwd': out['iclr_a2_fwd'], 'iclr_a0_bwd': out['iclr_a0_bwd'], 'iclr_a2_bwd': out['iclr_a2_bwd'], 'gate_g2': out['gate_g2'], 'k_k': out['k_k'], 'k_a': out['k_a'], 'r_k': out['r_k'], 'ln_x_g': out['ln_x_g'], 'ln_x_b': out['ln_x_b'], 'q_norm_g': out['q_norm_g'], 'w_uq': out['w_uq'], 'kv_norm_g': out['kv_norm_g'], 'w_ukv': out['w_ukv'], 'mla_out_g': out['mla_out_g'], 'w_out': out['w_out'], 'ln_ffn_g': out['ln_ffn_g'], 'w_ffn_up': out['w_ffn_up'], 'ffn_conv_w': out['ffn_conv_w'], 'ffn_conv_b': out['ffn_conv_b'], 'w_ffn_down': out['w_ffn_down'], 'ln_final_g': out['ln_final_g'], 'loss_target': out['loss_target'], 'm_ln_mix_g': out['m_ln_mix_g'], 'm_w_in': out['m_w_in'], 'm_shift_mu_prev': out['m_shift_mu_prev'], 'm_shift_mu_next': out['m_shift_mu_next'], 'm_decay_w0_fwd': out['m_decay_w0_fwd'], 'm_decay_w2_fwd': out['m_decay_w2_fwd'], 'm_decay_w0_bwd': out['m_decay_w0_bwd'], 'm_decay_w2_bwd': out['m_decay_w2_bwd'], 'm_iclr_a0_fwd': out['m_iclr_a0_fwd'], 'm_iclr_a2_fwd': out['m_iclr_a2_fwd'], 'm_iclr_a0_bwd': out['m_iclr_a0_bwd'], 'm_iclr_a2_bwd': out['m_iclr_a2_bwd'], 'm_gate_g2': out['m_gate_g2'], 'm_k_k': out['m_k_k'], 'm_k_a': out['m_k_a'], 'm_r_k': out['m_r_k'], 'm_ln_x_g': out['m_ln_x_g'], 'm_ln_x_b': out['m_ln_x_b'], 'm_q_norm_g': out['m_q_norm_g'], 'm_w_uq': out['m_w_uq'], 'm_kv_norm_g': out['m_kv_norm_g'], 'm_w_ukv': out['m_w_ukv'], 'm_mla_out_g': out['m_mla_out_g'], 'm_w_out': out['m_w_out'], 'm_ln_ffn_g': out['m_ln_ffn_g'], 'm_w_ffn_up': out['m_w_ffn_up'], 'm_ffn_conv_w': out['m_ffn_conv_w'], 'm_ffn_conv_b': out['m_ffn_conv_b'], 'm_w_ffn_down': out['m_w_ffn_down'], 'm_ln_final_g': out['m_ln_final_g'], 'v_ln_mix_g': out['v_ln_mix_g'], 'v_w_in': out['v_w_in'], 'v_shift_mu_prev': out['v_shift_mu_prev'], 'v_shift_mu_next': out['v_shift_mu_next'], 'v_decay_w0_fwd': out['v_decay_w0_fwd'], 'v_decay_w2_fwd': out['v_decay_w2_fwd'], 'v_decay_w0_bwd': out['v_decay_w0_bwd'], 'v_decay_w2_bwd': out['v_decay_w2_bwd'], 'v_iclr_a0_fwd': out['v_iclr_a0_fwd'], 'v_iclr_a2_fwd': out['v_iclr_a2_fwd'], 'v_iclr_a0_bwd': out['v_iclr_a0_bwd'], 'v_iclr_a2_bwd': out['v_iclr_a2_bwd'], 'v_gate_g2': out['v_gate_g2'], 'v_k_k': out['v_k_k'], 'v_k_a': out['v_k_a'], 'v_r_k': out['v_r_k'], 'v_ln_x_g': out['v_ln_x_g'], 'v_ln_x_b': out['v_ln_x_b'], 'v_q_norm_g': out['v_q_norm_g'], 'v_w_uq': out['v_w_uq'], 'v_kv_norm_g': out['v_kv_norm_g'], 'v_w_ukv': out['v_w_ukv'], 'v_mla_out_g': out['v_mla_out_g'], 'v_w_out': out['v_w_out'], 'v_ln_ffn_g': out['v_ln_ffn_g'], 'v_w_ffn_up': out['v_w_ffn_up'], 'v_ffn_conv_w': out['v_ffn_conv_w'], 'v_ffn_conv_b': out['v_ffn_conv_b'], 'v_w_ffn_down': out['v_w_ffn_down'], 'v_ln_final_g': out['v_ln_final_g']}


def _loss(weights, diff, rest, loss_target):
    with _jax.named_scope("forward"):
        args = {**rest, TWIN_DIFF_INPUT: diff, **{k: w.astype(_WEIGHT_DTYPES[k]) for k, w in weights.items()}}
        y = _forward(args)
    with _jax.named_scope("loss_head"):
        err = _jnp.square(y.astype(_jnp.float32) - loss_target)
        return 0.5 * _jnp.sum(_jnp.mean(err, axis=-1)) if err.ndim else 0.5 * err


def _adamw(w, g, m, v):
    m = ADAM_B1 * m + (1.0 - ADAM_B1) * g
    v = ADAM_B2 * v + (1.0 - ADAM_B2) * _jnp.square(g)
    m_hat = m / (1.0 - ADAM_B1 ** ADAM_STEP)
    v_hat = v / (1.0 - ADAM_B2 ** ADAM_STEP)
    delta = -ADAM_LR * (m_hat / (_jnp.sqrt(v_hat) + ADAM_EPS) + ADAM_WD * w)
    return delta, m, v


def reference(x, ln_mix_g, w_in, shift_mu_prev, shift_mu_next, decay_w0_fwd, decay_w2_fwd, decay_w0_bwd, decay_w2_bwd, iclr_a0_fwd, iclr_a2_fwd, iclr_a0_bwd, iclr_a2_bwd, gate_g2, k_k, k_a, r_k, ln_x_g, ln_x_b, q_norm_g, w_uq, kv_norm_g, w_ukv, mla_out_g, w_out, ln_ffn_g, w_ffn_up, ffn_conv_w, ffn_conv_b, w_ffn_down, ln_final_g, loss_target, m_ln_mix_g, m_w_in, m_shift_mu_prev, m_shift_mu_next, m_decay_w0_fwd, m_decay_w2_fwd, m_decay_w0_bwd, m_decay_w2_bwd, m_iclr_a0_fwd, m_iclr_a2_fwd, m_iclr_a0_bwd, m_iclr_a2_bwd, m_gate_g2, m_k_k, m_k_a, m_r_k, m_ln_x_g, m_ln_x_b, m_q_norm_g, m_w_uq, m_kv_norm_g, m_w_ukv, m_mla_out_g, m_w_out, m_ln_ffn_g, m_w_ffn_up, m_ffn_conv_w, m_ffn_conv_b, m_w_ffn_down, m_ln_final_g, v_ln_mix_g, v_w_in, v_shift_mu_prev, v_shift_mu_next, v_decay_w0_fwd, v_decay_w2_fwd, v_decay_w0_bwd, v_decay_w2_bwd, v_iclr_a0_fwd, v_iclr_a2_fwd, v_iclr_a0_bwd, v_iclr_a2_bwd, v_gate_g2, v_k_k, v_k_a, v_r_k, v_ln_x_g, v_ln_x_b, v_q_norm_g, v_w_uq, v_kv_norm_g, v_w_ukv, v_mla_out_g, v_w_out, v_ln_ffn_g, v_w_ffn_up, v_ffn_conv_w, v_ffn_conv_b, v_w_ffn_down, v_ln_final_g):
    given = dict(x=x, ln_mix_g=ln_mix_g, w_in=w_in, shift_mu_prev=shift_mu_prev, shift_mu_next=shift_mu_next, decay_w0_fwd=decay_w0_fwd, decay_w2_fwd=decay_w2_fwd, decay_w0_bwd=decay_w0_bwd, decay_w2_bwd=decay_w2_bwd, iclr_a0_fwd=iclr_a0_fwd, iclr_a2_fwd=iclr_a2_fwd, iclr_a0_bwd=iclr_a0_bwd, iclr_a2_bwd=iclr_a2_bwd, gate_g2=gate_g2, k_k=k_k, k_a=k_a, r_k=r_k, ln_x_g=ln_x_g, ln_x_b=ln_x_b, q_norm_g=q_norm_g, w_uq=w_uq, kv_norm_g=kv_norm_g, w_ukv=w_ukv, mla_out_g=mla_out_g, w_out=w_out, ln_ffn_g=ln_ffn_g, w_ffn_up=w_ffn_up, ffn_conv_w=ffn_conv_w, ffn_conv_b=ffn_conv_b, w_ffn_down=w_ffn_down, ln_final_g=ln_final_g, loss_target=loss_target, m_ln_mix_g=m_ln_mix_g, m_w_in=m_w_in, m_shift_mu_prev=m_shift_mu_prev, m_shift_mu_next=m_shift_mu_next, m_decay_w0_fwd=m_decay_w0_fwd, m_decay_w2_fwd=m_decay_w2_fwd, m_decay_w0_bwd=m_decay_w0_bwd, m_decay_w2_bwd=m_decay_w2_bwd, m_iclr_a0_fwd=m_iclr_a0_fwd, m_iclr_a2_fwd=m_iclr_a2_fwd, m_iclr_a0_bwd=m_iclr_a0_bwd, m_iclr_a2_bwd=m_iclr_a2_bwd, m_gate_g2=m_gate_g2, m_k_k=m_k_k, m_k_a=m_k_a, m_r_k=m_r_k, m_ln_x_g=m_ln_x_g, m_ln_x_b=m_ln_x_b, m_q_norm_g=m_q_norm_g, m_w_uq=m_w_uq, m_kv_norm_g=m_kv_norm_g, m_w_ukv=m_w_ukv, m_mla_out_g=m_mla_out_g, m_w_out=m_w_out, m_ln_ffn_g=m_ln_ffn_g, m_w_ffn_up=m_w_ffn_up, m_ffn_conv_w=m_ffn_conv_w, m_ffn_conv_b=m_ffn_conv_b, m_w_ffn_down=m_w_ffn_down, m_ln_final_g=m_ln_final_g, v_ln_mix_g=v_ln_mix_g, v_w_in=v_w_in, v_shift_mu_prev=v_shift_mu_prev, v_shift_mu_next=v_shift_mu_next, v_decay_w0_fwd=v_decay_w0_fwd, v_decay_w2_fwd=v_decay_w2_fwd, v_decay_w0_bwd=v_decay_w0_bwd, v_decay_w2_bwd=v_decay_w2_bwd, v_iclr_a0_fwd=v_iclr_a0_fwd, v_iclr_a2_fwd=v_iclr_a2_fwd, v_iclr_a0_bwd=v_iclr_a0_bwd, v_iclr_a2_bwd=v_iclr_a2_bwd, v_gate_g2=v_gate_g2, v_k_k=v_k_k, v_k_a=v_k_a, v_r_k=v_r_k, v_ln_x_g=v_ln_x_g, v_ln_x_b=v_ln_x_b, v_q_norm_g=v_q_norm_g, v_w_uq=v_w_uq, v_kv_norm_g=v_kv_norm_g, v_w_ukv=v_w_ukv, v_mla_out_g=v_mla_out_g, v_w_out=v_w_out, v_ln_ffn_g=v_ln_ffn_g, v_w_ffn_up=v_w_ffn_up, v_ffn_conv_w=v_ffn_conv_w, v_ffn_conv_b=v_ffn_conv_b, v_w_ffn_down=v_w_ffn_down, v_ln_final_g=v_ln_final_g)
    weights = {n: given[n] for n in TWIN_WEIGHTS}
    shared = {n: given[n] for n in SHARED_INPUTS}
    per_example = {n: given[n] for n in ['x']}
    grad_fn = _jax.value_and_grad(_loss, argnums=(0, 1))

    def one_microbatch(ex, loss_target):
        ex = dict(ex)
        diff = ex.pop(TWIN_DIFF_INPUT)
        return grad_fn(weights, diff, {**shared, **ex}, loss_target)

    if N_MICROBATCH == 1:
        loss, (grad_w, grad_x) = one_microbatch(per_example, given["loss_target"])
    else:
        def body(carry, xs):
            loss_sum, grad_sum = carry
            l_k, (gw_k, gx_k) = one_microbatch(xs[0], xs[1])
            with _jax.named_scope("update"):
                return (loss_sum + l_k, _jax.tree.map(_jnp.add, grad_sum, gw_k)), gx_k

        init = (_jnp.zeros((), _jnp.float32), _jax.tree.map(_jnp.zeros_like, weights))
        (loss, grad_w), grad_x = _jax.lax.scan(body, init, (per_example, given["loss_target"]))
    with _jax.named_scope("update"):
        delta_w, new_m, new_v = {}, {}, {}
        for n in TWIN_WEIGHTS:
            delta_w[n], new_m[n], new_v[n] = _adamw(weights[n], grad_w[n], given["m_" + n], given["v_" + n])
    return (loss, grad_x, *[grad_w[n] for n in TWIN_WEIGHTS], *[delta_w[n] for n in TWIN_WEIGHTS],
            *[new_m[n] for n in TWIN_WEIGHTS], *[new_v[n] for n in TWIN_WEIGHTS])
```

```python
import functools

import jax
import jax.numpy as jnp
from jax import lax
from jax.experimental import pallas as pl
from jax.experimental.pallas import tpu as pltpu

F32 = jnp.float32
BF16 = jnp.bfloat16
MESH = pl.DeviceIdType.MESH

N_DEV = 8
LANES = 128
SUBLANES = 8
PACK_TILE = SUBLANES * LANES
PACK_ROWS = 512
MM_TILE = 512
VMEM_LIMIT = 48 * 1024 * 1024

H = 8
N = 64
D_RWKV = H * N
D_NOPE, D_ROPE, D_V = 64, 32, 64
D_QK = D_NOPE + D_ROPE
MLA_SCALE = D_QK ** -0.5
ROPE_THETA = 10000.0
RWKV_COLS = 1920
MLA_COLS = 1056
D_FF = 2816
NORM_EPS = 1e-6
GN_EPS = 64e-5
L2_EPS = 1e-12
ADAM_LR, ADAM_B1, ADAM_B2, ADAM_EPS, ADAM_WD, ADAM_STEP = 0.001, 0.9, 0.999, 1e-08, 0.01, 10

SCAN_CHUNK = 16
ATT_TQ = 256
SEG = 256

WNAMES = ['ln_mix_g', 'w_in', 'shift_mu_prev', 'shift_mu_next', 'decay_w0_fwd', 'decay_w2_fwd', 'decay_w0_bwd',
          'decay_w2_bwd', 'iclr_a0_fwd', 'iclr_a2_fwd', 'iclr_a0_bwd', 'iclr_a2_bwd', 'gate_g2', 'k_k', 'k_a', 'r_k',
          'ln_x_g', 'ln_x_b', 'q_norm_g', 'w_uq', 'kv_norm_g', 'w_ukv', 'mla_out_g', 'w_out', 'ln_ffn_g', 'w_ffn_up',
          'ffn_conv_w', 'ffn_conv_b', 'w_ffn_down', 'ln_final_g']
COL = ('w_in', 'decay_w2_fwd', 'decay_w2_bwd', 'iclr_a2_fwd', 'iclr_a2_bwd', 'gate_g2', 'w_ukv', 'w_ffn_up', 'ffn_conv_w')
ROW = ('w_uq', 'w_out', 'w_ffn_down')
SHARDED = [n for n in WNAMES if n in COL or n in ROW]
REPLICATED = [n for n in WNAMES if n not in SHARDED]


def _params(*sem):
    return pltpu.CompilerParams(dimension_semantics=sem, vmem_limit_bytes=VMEM_LIMIT)


def _pack(arrs, lead=0):
    parts = []
    for a in arrs:
        head = a.shape[:lead]
        flat = a.reshape(head + (-1,))
        n = flat.shape[-1]
        n_pad = -(-n // PACK_TILE) * PACK_TILE
        flat = jnp.pad(flat, [(0, 0)] * lead + [(0, n_pad - n)])
        parts.append(flat.reshape(head + (n_pad // LANES, LANES)))
    out = jnp.concatenate(parts, axis=lead)
    rows = out.shape[lead]
    rows_pad = -(-rows // PACK_ROWS) * PACK_ROWS
    return jnp.pad(out, [(0, 0)] * lead + [(0, rows_pad - rows), (0, 0)])


def _unpack(packed, shapes, lead=0):
    outs, row = [], 0
    head = packed.shape[:lead]
    for shp in shapes:
        n = 1
        for s in shp:
            n *= s
        rows = -(-n // PACK_TILE) * SUBLANES
        blk = lax.slice_in_dim(packed, row, row + rows, axis=lead)
        flat = blk.reshape(head + (rows * LANES,))
        outs.append(lax.slice_in_dim(flat, 0, n, axis=lead).reshape(head + tuple(shp)))
        row += rows
    return outs


def _all_gather(x, name):
    rows = x.shape[0]

    def body(x_ref, out_ref, send_sems, recv_sems, local_sem):
        mx, my, mc = lax.axis_index("x"), lax.axis_index("y"), lax.axis_index("c")
        me, sibling = (mx, my, mc), (mx, my, 1 - mc)
        chips = [(1 - mx, my), (mx, 1 - my), (1 - mx, 1 - my)]

        def slot(px, py, pc):
            return out_ref.at[4 * px + 2 * py + pc]

        def copy(k, block, to, src=None):
            return pltpu.make_async_remote_copy(
                src_ref=slot(*block) if src is None else src, dst_ref=slot(*block),
                send_sem=send_sems.at[k], recv_sem=recv_sems.at[k], device_id=to, device_id_type=MESH)

        mine = pltpu.make_async_copy(x_ref, slot(*me), local_sem)
        mine.start()
        first = [copy(0, me, sibling, src=x_ref)]
        first += [copy(1 + j, me, (*chip, mc), src=x_ref) for j, chip in enumerate(chips)]
        for cp in first:
            cp.start()
        passed = [copy(4 + j, (*chip, mc), sibling) for j, chip in enumerate(chips)]
        for j, chip in enumerate(chips):
            copy(1 + j, (*chip, mc), me).wait_recv()
            passed[j].start()
        copy(0, sibling, me).wait_recv()
        for j, chip in enumerate(chips):
            copy(4 + j, (*chip, 1 - mc), me).wait_recv()
        for cp in first + passed:
            cp.wait_send()
        mine.wait()

    return pl.pallas_call(
        body, name=name,
        out_shape=jax.ShapeDtypeStruct((N_DEV, rows, LANES), x.dtype),
        in_specs=[pl.BlockSpec(memory_space=pl.ANY)],
        out_specs=pl.BlockSpec(memory_space=pl.ANY),
        scratch_shapes=[pltpu.SemaphoreType.DMA((7,)), pltpu.SemaphoreType.DMA((7,)), pltpu.SemaphoreType.DMA(())],
    )(x)


def _grad_exchange(g, name):
    rows = g.shape[1]

    def body(g_ref, out_ref, send_sems, recv_sems, local_sem):
        mx, my, mc = lax.axis_index("x"), lax.axis_index("y"), lax.axis_index("c")
        me = 4 * mx + 2 * my + mc

        def flip(v, bit):
            return 1 - v if bit else v

        mine = pltpu.make_async_copy(g_ref.at[me], out_ref.at[me], local_sem)
        mine.start()
        copies = []
        for k in range(1, N_DEV):
            px, py, pc = flip(mx, k & 4), flip(my, k & 2), flip(mc, k & 1)
            peer = 4 * px + 2 * py + pc
            copies.append(pltpu.make_async_remote_copy(
                src_ref=g_ref.at[peer], dst_ref=out_ref.at[me],
                send_sem=send_sems.at[k - 1], recv_sem=recv_sems.at[k - 1],
                device_id=(px, py, pc), device_id_type=MESH))
        for cp in copies:
            cp.start()
        for cp in copies:
            cp.wait_recv()
        for cp in copies:
            cp.wait_send()
        mine.wait()

    return pl.pallas_call(
        body, name=name,
        out_shape=jax.ShapeDtypeStruct((N_DEV, rows, LANES), g.dtype),
        in_specs=[pl.BlockSpec(memory_space=pl.ANY)],
        out_specs=pl.BlockSpec(memory_space=pl.ANY),
        scratch_shapes=[pltpu.SemaphoreType.DMA((7,)), pltpu.SemaphoreType.DMA((7,)), pltpu.SemaphoreType.DMA(())],
    )(g)


def _sum_adamw(parts, w, m, v, name):
    rows = w.shape[0]
    c1 = 1.0 - ADAM_B1 ** ADAM_STEP
    c2 = 1.0 - ADAM_B2 ** ADAM_STEP

    def body(p_ref, w_ref, m_ref, v_ref, g_out, d_out, m_out, v_out):
        g = p_ref[0]
        for q in range(1, N_DEV):
            g = g + p_ref[q]
        m_new = ADAM_B1 * m_ref[...] + (1.0 - ADAM_B1) * g
        v_new = ADAM_B2 * v_ref[...] + (1.0 - ADAM_B2) * (g * g)
        m_hat = m_new / c1
        v_hat = v_new / c2
        g_out[...] = g
        d_out[...] = -ADAM_LR * (m_hat / (jnp.sqrt(v_hat) + ADAM_EPS) + ADAM_WD * w_ref[...])
        m_out[...] = m_new
        v_out[...] = v_new

    blk = pl.BlockSpec((PACK_ROWS, LANES), lambda i: (i, 0))
    out = jax.ShapeDtypeStruct((rows, LANES), F32)
    return pl.pallas_call(
        body, name=name, grid=(rows // PACK_ROWS,),
        in_specs=[pl.BlockSpec((N_DEV, PACK_ROWS, LANES), lambda i: (0, i, 0)), blk, blk, blk],
        out_specs=[blk, blk, blk, blk], out_shape=[out, out, out, out],
        compiler_params=_params("parallel"),
    )(parts, w, m, v)


def _tile(dim, cap=MM_TILE):
    if dim <= cap:
        return dim
    for t in range(cap, LANES - 1, -LANES):
        if dim % t == 0:
            return t
    return dim


def _mm_call(a, b, form, name):
    if form == 'nn':
        (m, k), n = a.shape, b.shape[1]
    elif form == 'nt':
        (m, k), n = a.shape, b.shape[0]
    else:
        (k, m), n = a.shape, b.shape[1]
    tm, tn, tk = _tile(m), _tile(n), _tile(k)
    nk = k // tk
    contract = {'nn': ((1,), (0,)), 'nt': ((1,), (1,)), 'tn': ((0,), (0,))}[form]

    def body(a_ref, b_ref, o_ref, acc_ref):
        kk = pl.program_id(2)

        @pl.when(kk == 0)
        def _():
            acc_ref[...] = jnp.zeros_like(acc_ref)

        acc_ref[...] += lax.dot_general(a_ref[...].astype(BF16), b_ref[...].astype(BF16), (contract, ((), ())),
                                        preferred_element_type=F32)

        @pl.when(kk == nk - 1)
        def _():
            o_ref[...] = acc_ref[...]

    a_spec = pl.BlockSpec((tk, tm), lambda i, j, l: (l, i)) if form == 'tn' else pl.BlockSpec((tm, tk), lambda i, j, l: (i, l))
    b_spec = pl.BlockSpec((tn, tk), lambda i, j, l: (j, l)) if form == 'nt' else pl.BlockSpec((tk, tn), lambda i, j, l: (l, j))
    return pl.pallas_call(
        body, name=name, grid=(m // tm, n // tn, nk),
        in_specs=[a_spec, b_spec], out_specs=pl.BlockSpec((tm, tn), lambda i, j, l: (i, j)),
        out_shape=jax.ShapeDtypeStruct((m, n), F32),
        scratch_shapes=[pltpu.VMEM((tm, tn), F32)],
        compiler_params=_params("parallel", "parallel", "arbitrary"),
    )(a, b)


def _make_mm(name):
    @jax.custom_vjp
    def mm(a, b):
        return _mm_call(a, b, 'nn', name + '_fwd')

    def fwd(a, b):
        return _mm_call(a, b, 'nn', name + '_fwd'), (a, b)

    def bwd(res, g):
        a, b = res
        return _mm_call(g, b, 'nt', name + '_da'), _mm_call(a, g, 'tn', name + '_db')

    mm.defvjp(fwd, bwd)
    return mm


def _rms_fwd_call(x, g, name):
    m, d = x.shape
    tm = _tile(m)

    def body(x_ref, g_ref, o_ref):
        xv = x_ref[...]
        rinv = lax.rsqrt(jnp.mean(xv * xv, axis=-1, keepdims=True) + NORM_EPS)
        o_ref[...] = xv * rinv * g_ref[...]

    return pl.pallas_call(
        body, name=name, grid=(m // tm,),
        in_specs=[pl.BlockSpec((tm, d), lambda i: (i, 0)), pl.BlockSpec((1, d), lambda i: (0, 0))],
        out_specs=pl.BlockSpec((tm, d), lambda i: (i, 0)), out_shape=jax.ShapeDtypeStruct((m, d), F32),
        compiler_params=_params("parallel"),
    )(x, g)


def _rms_bwd_call(x, g, dy, name):
    m, d = x.shape
    tm = _tile(m)

    def body(x_ref, g_ref, dy_ref, dx_ref, dg_ref):
        @pl.when(pl.program_id(0) == 0)
        def _():
            dg_ref[...] = jnp.zeros_like(dg_ref)

        xv, dyv = x_ref[...], dy_ref[...]
        rinv = lax.rsqrt(jnp.mean(xv * xv, axis=-1, keepdims=True) + NORM_EPS)
        xh = xv * rinv
        dg_ref[...] += jnp.sum(dyv * xh, axis=0, keepdims=True)
        dxh = dyv * g_ref[...]
        dx_ref[...] = rinv * (dxh - xh * jnp.mean(dxh * xh, axis=-1, keepdims=True))

    return pl.pallas_call(
        body, name=name, grid=(m // tm,),
        in_specs=[pl.BlockSpec((tm, d), lambda i: (i, 0)), pl.BlockSpec((1, d), lambda i: (0, 0)),
                  pl.BlockSpec((tm, d), lambda i: (i, 0))],
        out_specs=[pl.BlockSpec((tm, d), lambda i: (i, 0)), pl.BlockSpec((1, d), lambda i: (0, 0))],
        out_shape=[jax.ShapeDtypeStruct((m, d), F32), jax.ShapeDtypeStruct((1, d), F32)],
        compiler_params=_params("arbitrary"),
    )(x, g, dy)


def _make_rms(name):
    @jax.custom_vjp
    def rms(x, g):
        return _rms_fwd_call(x, g, name + '_fwd')

    def fwd(x, g):
        return _rms_fwd_call(x, g, name + '_fwd'), (x, g)

    def bwd(res, dy):
        x, g = res
        dx, dg = _rms_bwd_call(x, g, dy, name + '_bwd')
        return dx, dg

    rms.defvjp(fwd, bwd)
    return rms


def _softmax_rows(q_ref, k_ref):
    s = lax.dot_general(q_ref[0].astype(BF16), k_ref[0].astype(BF16), (((1,), (1,)), ((), ())),
                        preferred_element_type=F32) * MLA_SCALE
    p = jnp.exp(s - jnp.max(s, axis=-1, keepdims=True))
    return p, jnp.sum(p, axis=-1, keepdims=True)


def _attn_fwd_call(q, k, v, name):
    bh, t, _ = q.shape
    tq = min(ATT_TQ, t)

    def body(q_ref, k_ref, v_ref, o_ref):
        p, l = _softmax_rows(q_ref, k_ref)
        o_ref[0] = jnp.dot(p.astype(BF16), v_ref[0].astype(BF16), preferred_element_type=F32) / l

    return pl.pallas_call(
        body, name=name, grid=(bh, t // tq),
        in_specs=[pl.BlockSpec((1, tq, D_QK), lambda b, i: (b, i, 0)), pl.BlockSpec((1, t, D_QK), lambda b, i: (b, 0, 0)),
                  pl.BlockSpec((1, t, D_V), lambda b, i: (b, 0, 0))],
        out_specs=pl.BlockSpec((1, tq, D_V), lambda b, i: (b, i, 0)),
        out_shape=jax.ShapeDtypeStruct((bh, t, D_V), F32),
        compiler_params=_params("parallel", "parallel"),
    )(q, k, v)


def _attn_bwd_call(q, k, v, o, do, name):
    bh, t, _ = q.shape
    tq = min(ATT_TQ, t)

    def body(q_ref, k_ref, v_ref, o_ref, do_ref, dq_ref, dk_ref, dv_ref):
        @pl.when(pl.program_id(1) == 0)
        def _():
            dk_ref[...] = jnp.zeros_like(dk_ref)
            dv_ref[...] = jnp.zeros_like(dv_ref)

        p, l = _softmax_rows(q_ref, k_ref)
        p = p / l
        dov = do_ref[0]
        do_b = dov.astype(BF16)
        delta = jnp.sum(dov * o_ref[0], axis=-1, keepdims=True)
        dp = lax.dot_general(do_b, v_ref[0].astype(BF16), (((1,), (1,)), ((), ())), preferred_element_type=F32)
        ds = (p * (dp - delta) * MLA_SCALE).astype(BF16)
        dq_ref[0] = jnp.dot(ds, k_ref[0].astype(BF16), preferred_element_type=F32)
        dk_ref[0] += lax.dot_general(ds, q_ref[0].astype(BF16), (((0,), (0,)), ((), ())), preferred_element_type=F32)
        dv_ref[0] += lax.dot_general(p.astype(BF16), do_b, (((0,), (0,)), ((), ())), preferred_element_type=F32)

    qspec = pl.BlockSpec((1, tq, D_QK), lambda b, i: (b, i, 0))
    kspec = pl.BlockSpec((1, t, D_QK), lambda b, i: (b, 0, 0))
    vspec = pl.BlockSpec((1, t, D_V), lambda b, i: (b, 0, 0))
    ospec = pl.BlockSpec((1, tq, D_V), lambda b, i: (b, i, 0))
    return pl.pallas_call(
        body, name=name, grid=(bh, t // tq),
        in_specs=[qspec, kspec, vspec, ospec, ospec], out_specs=[qspec, kspec, vspec],
        out_shape=[jax.ShapeDtypeStruct(q.shape, F32), jax.ShapeDtypeStruct(k.shape, F32), jax.ShapeDtypeStruct(v.shape, F32)],
        compiler_params=_params("parallel", "arbitrary"),
    )(q, k, v, o, do)


@jax.custom_vjp
def _attention(q, k, v):
    return _attn_fwd_call(q, k, v, 'attn_fwd')


def _attention_fwd(q, k, v):
    o = _attn_fwd_call(q, k, v, 'attn_fwd')
    return o, (q, k, v, o)


def _attention_bwd(res, do):
    q, k, v, o = res
    return tuple(_attn_bwd_call(q, k, v, o, do, 'attn_bwd'))


_attention.defvjp(_attention_fwd, _attention_bwd)


def _seg_ones():
    r = lax.broadcasted_iota(jnp.int32, (SEG, SEG), 0) >> 6
    c = lax.broadcasted_iota(jnp.int32, (SEG, SEG), 1) >> 6
    return (r == c).astype(BF16)


def _eye_mask():
    r = lax.broadcasted_iota(jnp.int32, (N, D_RWKV), 0)
    c = lax.broadcasted_iota(jnp.int32, (N, D_RWKV), 1) & (N - 1)
    return (r == c).astype(F32)


def _seg_sum(x, ones):
    hi = x.astype(BF16)
    lo = (x - hi.astype(F32)).astype(BF16)
    outs = []
    for q in range(D_RWKV // SEG):
        sl = slice(SEG * q, SEG * (q + 1))
        outs.append(jnp.dot(hi[:, sl], ones, preferred_element_type=F32)
                    + jnp.dot(lo[:, sl], ones, preferred_element_type=F32))
    return jnp.concatenate(outs, axis=1)


def _col_sum(x):
    return jnp.sum(x, axis=0, keepdims=True)


def _scan_specs(b, t, rev):
    nc = t // SCAN_CHUNK
    if rev:
        return (pl.BlockSpec((1, SCAN_CHUNK, D_RWKV), lambda bi, c: (bi, nc - 1 - c, 0)),
                pl.BlockSpec((1, SCAN_CHUNK, N, D_RWKV), lambda bi, c: (bi, nc - 1 - c, 0, 0)))
    return (pl.BlockSpec((1, SCAN_CHUNK, D_RWKV), lambda bi, c: (bi, c, 0)),
            pl.BlockSpec((1, SCAN_CHUNK, N, D_RWKV), lambda bi, c: (bi, c, 0, 0)))


def _scan_fwd_call(r, v, kk, wf, kf, qf, wb, kb, qb):
    b, t, _ = r.shape

    def body(rf, vf, kkf, wf_, kf_, qf_, rb, vb, kkb, wb_, kb_, qb_, yf, yb, sf, sb, stf, stb):
        @pl.when(pl.program_id(1) == 0)
        def _():
            stf[...] = jnp.zeros_like(stf)
            stb[...] = jnp.zeros_like(stb)

        ones, mask = _seg_ones(), _eye_mask()

        def one(ti, r_, v_, kk_, w_, k_, q_, y_, s_, st):
            row = lambda ref: ref[0, pl.ds(ti, 1), :]
            s = st[...]
            s_[0, ti] = s
            u = _seg_sum(s * row(kk_), ones)
            vcol = _seg_sum(mask * row(v_), ones)
            s = s * row(w_) - u * row(q_) + vcol * row(k_)
            st[...] = s
            ycol = _seg_sum(s * row(r_), ones)
            y_[0, pl.ds(ti, 1), :] = _col_sum(mask * ycol)

        def step(i, carry):
            one(i, rf, vf, kkf, wf_, kf_, qf_, yf, sf, stf)
            one(SCAN_CHUNK - 1 - i, rb, vb, kkb, wb_, kb_, qb_, yb, sb, stb)
            return carry

        lax.fori_loop(0, SCAN_CHUNK, step, 0)

    fr, fs = _scan_specs(b, t, False)
    br, bs = _scan_specs(b, t, True)
    y_shape = jax.ShapeDtypeStruct((b, t, D_RWKV), F32)
    s_shape = jax.ShapeDtypeStruct((b, t, N, D_RWKV), F32)
    return pl.pallas_call(
        body, name='scan_fwd', grid=(b, t // SCAN_CHUNK),
        in_specs=[fr] * 6 + [br] * 6, out_specs=[fr, br, fs, bs], out_shape=[y_shape, y_shape, s_shape, s_shape],
        scratch_shapes=[pltpu.VMEM((N, D_RWKV), F32), pltpu.VMEM((N, D_RWKV), F32)],
        compiler_params=_params("parallel", "arbitrary"),
    )(r, v, kk, wf, kf, qf, r, v, kk, wb, kb, qb)


def _scan_bwd_call(r, v, kk, wf, kf, qf, wb, kb, qb, sf, sb, dyf, dyb):
    b, t, _ = r.shape

    def body(rf, vf, kkf, wf_, kf_, qf_, sf_, dyf_, rb, vb, kkb, wb_, kb_, qb_, sb_, dyb_,
             drf, dvf, dkkf, dwf, dkf, dqf, drb, dvb, dkkb, dwb, dkb, dqb, gf, gb):
        @pl.when(pl.program_id(1) == 0)
        def _():
            gf[...] = jnp.zeros_like(gf)
            gb[...] = jnp.zeros_like(gb)

        ones, mask = _seg_ones(), _eye_mask()

        def one(ti, r_, v_, kk_, w_, k_, q_, s_, dy_, dr_, dv_, dkk_, dw_, dk_, dq_, gst):
            row = lambda ref: ref[0, pl.ds(ti, 1), :]

            def put(ref, val):
                ref[0, pl.ds(ti, 1), :] = val

            sp = s_[0, ti]
            rr, vr, kkr, wr, kr, qr = row(r_), row(v_), row(kk_), row(w_), row(k_), row(q_)
            vcol = _seg_sum(mask * vr, ones)
            dycol = _seg_sum(mask * row(dy_), ones)
            u = _seg_sum(sp * kkr, ones)
            sc = sp * wr - u * qr + vcol * kr
            put(dr_, _col_sum(sc * dycol))
            g = gst[...] + dycol * rr
            put(dv_, _col_sum(mask * _seg_sum(g * kr, ones)))
            put(dk_, _col_sum(g * vcol))
            put(dw_, _col_sum(g * sp))
            du = -_seg_sum(g * qr, ones)
            put(dq_, -_col_sum(g * u))
            put(dkk_, _col_sum(sp * du))
            gst[...] = g * wr + du * kkr

        def step(i, carry):
            one(SCAN_CHUNK - 1 - i, rf, vf, kkf, wf_, kf_, qf_, sf_, dyf_, drf, dvf, dkkf, dwf, dkf, dqf, gf)
            one(i, rb, vb, kkb, wb_, kb_, qb_, sb_, dyb_, drb, dvb, dkkb, dwb, dkb, dqb, gb)
            return carry

        lax.fori_loop(0, SCAN_CHUNK, step, 0)

    fr, fs = _scan_specs(b, t, True)
    br, bs = _scan_specs(b, t, False)
    y_shape = jax.ShapeDtypeStruct((b, t, D_RWKV), F32)
    return pl.pallas_call(
        body, name='scan_bwd', grid=(b, t // SCAN_CHUNK),
        in_specs=[fr] * 6 + [fs, fr] + [br] * 6 + [bs, br],
        out_specs=[fr] * 6 + [br] * 6, out_shape=[y_shape] * 12,
        scratch_shapes=[pltpu.VMEM((N, D_RWKV), F32), pltpu.VMEM((N, D_RWKV), F32)],
        compiler_params=_params("parallel", "arbitrary"),
    )(r, v, kk, wf, kf, qf, sf, dyf, r, v, kk, wb, kb, qb, sb, dyb)


@jax.custom_vjp
def _wkv_scan(r, v, kk, wf, kf, qf, wb, kb, qb):
    yf, yb, _, _ = _scan_fwd_call(r, v, kk, wf, kf, qf, wb, kb, qb)
    return yf, yb


def _wkv_scan_fwd(r, v, kk, wf, kf, qf, wb, kb, qb):
    yf, yb, sf, sb = _scan_fwd_call(r, v, kk, wf, kf, qf, wb, kb, qb)
    return (yf, yb), (r, v, kk, wf, kf, qf, wb, kb, qb, sf, sb)


def _wkv_scan_bwd(res, dy):
    r, v, kk, wf, kf, qf, wb, kb, qb, sf, sb = res
    (drf, dvf, dkkf, dwf, dkf, dqf, drb, dvb, dkkb, dwb, dkb, dqb) = _scan_bwd_call(
        r, v, kk, wf, kf, qf, wb, kb, qb, sf, sb, dy[0], dy[1])
    return drf + drb, dvf + dvb, dkkf + dkkb, dwf, dkf, dqf, dwb, dkb, dqb


_wkv_scan.defvjp(_wkv_scan_fwd, _wkv_scan_bwd)


def _shift(z, left):
    if left:
        return jnp.pad(z[:, :-1], ((0, 0), (1, 0), (0, 0)))
    return jnp.pad(z[:, 1:], ((0, 0), (0, 1), (0, 0)))


def _rope_tables(t):
    inv_freq = jnp.power(ROPE_THETA, -jnp.arange(0, D_ROPE, 2, dtype=F32) / D_ROPE)
    ang = jnp.arange(t, dtype=F32)[:, None] * inv_freq[None, :]
    ang = jnp.concatenate([ang, ang], axis=-1)
    return jnp.cos(ang), jnp.sin(ang)


def _rope(x, cos, sin):
    x1, x2 = jnp.split(x, 2, axis=-1)
    return x * cos + jnp.concatenate([-x2, x1], axis=-1) * sin


def _rwkv_mixer(z, full, rep):
    b, t, _ = z.shape
    m = b * t
    z = z + rep['shift_mu_prev'] * (_shift(z, True) - z) + rep['shift_mu_next'] * (_shift(z, False) - z)
    z = z.reshape(m, RWKV_COLS)
    r, k, v = z[:, :512], z[:, 512:1024], z[:, 1024:1536]
    wdf, wdb, adf, adb, gd = z[:, 1536:1600], z[:, 1600:1664], z[:, 1664:1728], z[:, 1728:1792], z[:, 1792:1920]

    def decay(wd, w0, w2, name):
        logit = w0 + _make_mm(name)(jnp.tanh(wd), w2)
        return jnp.exp(-jnp.exp(-jax.nn.softplus(-logit) - 0.5))

    w_f = decay(wdf, rep['decay_w0_fwd'], full['decay_w2_fwd'], 'mm_decay_f')
    w_b = decay(wdb, rep['decay_w0_bwd'], full['decay_w2_bwd'], 'mm_decay_b')
    a_f = jax.nn.sigmoid(rep['iclr_a0_fwd'] + _make_mm('mm_iclr_f')(adf, full['iclr_a2_fwd']))
    a_b = jax.nn.sigmoid(rep['iclr_a0_bwd'] + _make_mm('mm_iclr_b')(adb, full['iclr_a2_bwd']))
    g = _make_mm('mm_gate')(jax.nn.sigmoid(gd), full['gate_g2'])
    kk = (k * rep['k_k']).reshape(m, H, N)
    kk = (kk / jnp.maximum(jnp.sqrt(jnp.sum(kk * kk, axis=-1, keepdims=True)), L2_EPS)).reshape(m, D_RWKV)
    k_f = k * (1.0 + (a_f - 1.0) * rep['k_a'])
    k_b = k * (1.0 + (a_b - 1.0) * rep['k_a'])
    seq = lambda a: a.reshape(b, t, D_RWKV)
    y_f, y_b = _wkv_scan(seq(r), seq(v), seq(kk), seq(w_f), seq(k_f), seq(kk * a_f), seq(w_b), seq(k_b), seq(kk * a_b))
    y = (y_f + y_b).reshape(m, H, N)
    mu = jnp.mean(y, axis=-1, keepdims=True)
    var = jnp.mean(jnp.square(y - mu), axis=-1, keepdims=True)
    y = ((y - mu) * lax.rsqrt(var + GN_EPS)).reshape(m, D_RWKV) * rep['ln_x_g'] + rep['ln_x_b']
    r_h, v_h = r.reshape(m, H, N), v.reshape(m, H, N)
    bonus = jnp.sum(r_h * (k_f + k_b).reshape(m, H, N) * rep['r_k'], axis=-1, keepdims=True) * v_h
    return (y + bonus.reshape(m, D_RWKV)) * g


def _mla_mixer(z, full, rep, b, t):
    m = b * t
    c_q, c_kv, k_rope = z[:, :768], z[:, 768:1024], z[:, 1024:1056]
    cos, sin = _rope_tables(t)
    q = _make_mm('mm_uq')(_make_rms('rms_q')(c_q, rep['q_norm_g']), full['w_uq']).reshape(b, t, H, D_QK)
    q = jnp.concatenate([q[..., :D_NOPE], _rope(q[..., D_NOPE:], cos[:, None, :], sin[:, None, :])], axis=-1)
    kv = _make_mm('mm_ukv')(_make_rms('rms_kv')(c_kv, rep['kv_norm_g']), full['w_ukv']).reshape(b, t, H, D_NOPE + D_V)
    k_rope = _rope(k_rope.reshape(b, t, D_ROPE), cos, sin)
    k = jnp.concatenate([kv[..., :D_NOPE], jnp.broadcast_to(k_rope[:, :, None, :], (b, t, H, D_ROPE))], axis=-1)
    heads = lambda a: a.transpose(0, 2, 1, 3).reshape(b * H, t, a.shape[-1])
    o = _attention(heads(q), heads(k), heads(kv[..., D_NOPE:]))
    o = o.reshape(b, H, t, D_V).transpose(0, 2, 1, 3).reshape(m, H * D_V)
    return _make_rms('rms_mla_out')(o, rep['mla_out_g'])


def _local_loss(full, rep, x, target):
    b, t, d = x.shape
    m = b * t
    xf = x.reshape(m, d)
    n1 = _make_rms('rms_mix')(xf, rep['ln_mix_g'])
    d_in = full['w_in'].shape[1]
    d_in_pad = -(-d_in // MM_TILE) * MM_TILE
    z = _make_mm('mm_in')(n1, jnp.pad(full['w_in'], ((0, 0), (0, d_in_pad - d_in))))
    y_rwkv = _rwkv_mixer(z[:, :RWKV_COLS].reshape(b, t, RWKV_COLS), full, rep)
    y_mla = _mla_mixer(z[:, RWKV_COLS:d_in], full, rep, b, t)
    h = xf + _make_mm('mm_out')(jnp.concatenate([y_rwkv, y_mla], axis=-1), full['w_out'])
    n2 = _make_rms('rms_ffn')(h, rep['ln_ffn_g'])
    u = _make_mm('mm_up')(n2, full['w_ffn_up']).reshape(b, t, 2 * D_FF)
    cw = full['ffn_conv_w']
    u = cw[0] * _shift(u, True) + cw[1] * u + cw[2] * _shift(u, False) + rep['ffn_conv_b']
    act = (jax.nn.silu(u[..., :D_FF]) * u[..., D_FF:]).reshape(m, D_FF)
    h = h + _make_mm('mm_down')(act, full['w_ffn_down'])
    out = _make_rms('rms_final')(h, rep['ln_final_g'])
    err = jnp.square(out - target.reshape(m, d))
    return 0.5 * jnp.sum(jnp.mean(err, axis=-1))


def _mat(a):
    if a.ndim == 1:
        return a.reshape(1, -1)
    if a.ndim == 3:
        return a.reshape(a.shape[1:])
    return a


def _join(shards, name):
    if name in ROW:
        return shards.reshape(-1, shards.shape[-1])
    return shards.transpose(1, 0, 2).reshape(shards.shape[1], -1)


def _cut(whole, name):
    r, c = whole.shape
    if name in ROW:
        return whole.reshape(N_DEV, r // N_DEV, c)
    return whole.reshape(r, N_DEV, c // N_DEV).transpose(1, 0, 2)


def kernel(x, ln_mix_g, w_in, shift_mu_prev, shift_mu_next, decay_w0_fwd, decay_w2_fwd, decay_w0_bwd, decay_w2_bwd, iclr_a0_fwd, iclr_a2_fwd, iclr_a0_bwd, iclr_a2_bwd, gate_g2, k_k, k_a, r_k, ln_x_g, ln_x_b, q_norm_g, w_uq, kv_norm_g, w_ukv, mla_out_g, w_out, ln_ffn_g, w_ffn_up, ffn_conv_w, ffn_conv_b, w_ffn_down, ln_final_g, loss_target, m_ln_mix_g, m_w_in, m_shift_mu_prev, m_shift_mu_next, m_decay_w0_fwd, m_decay_w2_fwd, m_decay_w0_bwd, m_decay_w2_bwd, m_iclr_a0_fwd, m_iclr_a2_fwd, m_iclr_a0_bwd, m_iclr_a2_bwd, m_gate_g2, m_k_k, m_k_a, m_r_k, m_ln_x_g, m_ln_x_b, m_q_norm_g, m_w_uq, m_kv_norm_g, m_w_ukv, m_mla_out_g, m_w_out, m_ln_ffn_g, m_w_ffn_up, m_ffn_conv_w, m_ffn_conv_b, m_w_ffn_down, m_ln_final_g, v_ln_mix_g, v_w_in, v_shift_mu_prev, v_shift_mu_next, v_decay_w0_fwd, v_decay_w2_fwd, v_decay_w0_bwd, v_decay_w2_bwd, v_iclr_a0_fwd, v_iclr_a2_fwd, v_iclr_a0_bwd, v_iclr_a2_bwd, v_gate_g2, v_k_k, v_k_a, v_r_k, v_ln_x_g, v_ln_x_b, v_q_norm_g, v_w_uq, v_kv_norm_g, v_w_ukv, v_mla_out_g, v_w_out, v_ln_ffn_g, v_w_ffn_up, v_ffn_conv_w, v_ffn_conv_b, v_w_ffn_down, v_ln_final_g):
    given = dict(locals())
    w = {n: given[n] for n in WNAMES}
    mom = {n: given['m_' + n] for n in WNAMES}
    var = {n: given['v_' + n] for n in WNAMES}

    shard_shapes = [_mat(w[n]).shape for n in SHARDED]
    w_pack = _pack([w[n] for n in SHARDED])
    gathered = _unpack(_all_gather(w_pack, 'gather_weights'), shard_shapes, lead=1)
    full = {n: _join(s, n) for n, s in zip(SHARDED, gathered)}
    rep = {n: _mat(w[n]) for n in REPLICATED}
    rep['r_k'] = w['r_k'].reshape(H, N)

    loss_local, (g_full, g_rep, g_x) = jax.value_and_grad(_local_loss, argnums=(0, 1, 2))(full, rep, x, loss_target)

    g_pack = _pack([_cut(g_full[n], n) for n in SHARDED], lead=1)
    parts = _grad_exchange(g_pack, 'exchange_grads')
    s_out = _sum_adamw(parts, w_pack, _pack([mom[n] for n in SHARDED]), _pack([var[n] for n in SHARDED]), 'adamw_sharded')
    s_out = [_unpack(o, [w[n].shape for n in SHARDED]) for o in s_out]

    zero = jnp.zeros((1,), F32)
    r_pack = _pack([g_rep[n] for n in REPLICATED] + [loss_local.reshape(1)])
    r_parts = _all_gather(r_pack, 'gather_small')
    r_out = _sum_adamw(r_parts, _pack([w[n] for n in REPLICATED] + [zero]), _pack([mom[n] for n in REPLICATED] + [zero]),
                       _pack([var[n] for n in REPLICATED] + [zero]), 'adamw_replicated')
    r_out = [_unpack(o, [w[n].shape for n in REPLICATED] + [(1,)]) for o in r_out]

    loss = r_out[0][-1].reshape(())
    outs = [loss, g_x]
    for kind in range(4):
        by_name = dict(zip(SHARDED, s_out[kind]))
        by_name.update(zip(REPLICATED, r_out[kind][:-1]))
        outs += [by_name[n] for n in WNAMES]
    return tuple(outs)
```

```python
import functools

import jax
import jax.numpy as jnp
from jax import lax
from jax.experimental import pallas as pl
from jax.experimental.pallas import tpu as pltpu

F32 = jnp.float32
BF16 = jnp.bfloat16
MESH = pl.DeviceIdType.MESH

N_DEV = 8
LANES = 128
SUBLANES = 8
PACK_TILE = SUBLANES * LANES
PACK_ROWS = 512
MM_TILE = 512
VMEM_LIMIT = 48 * 1024 * 1024

H = 8
N = 64
D_RWKV = H * N
D_NOPE, D_ROPE, D_V = 64, 32, 64
D_QK = D_NOPE + D_ROPE
MLA_SCALE = D_QK ** -0.5
ROPE_THETA = 10000.0
RWKV_COLS = 1920
MLA_COLS = 1056
D_FF = 2816
NORM_EPS = 1e-6
GN_EPS = 64e-5
L2_EPS = 1e-12
ADAM_LR, ADAM_B1, ADAM_B2, ADAM_EPS, ADAM_WD, ADAM_STEP = 0.001, 0.9, 0.999, 1e-08, 0.01, 10

SCAN_CHUNK = 16
ATT_TQ = 256
SEG = 256

WNAMES = ['ln_mix_g', 'w_in', 'shift_mu_prev', 'shift_mu_next', 'decay_w0_fwd', 'decay_w2_fwd', 'decay_w0_bwd',
          'decay_w2_bwd', 'iclr_a0_fwd', 'iclr_a2_fwd', 'iclr_a0_bwd', 'iclr_a2_bwd', 'gate_g2', 'k_k', 'k_a', 'r_k',
          'ln_x_g', 'ln_x_b', 'q_norm_g', 'w_uq', 'kv_norm_g', 'w_ukv', 'mla_out_g', 'w_out', 'ln_ffn_g', 'w_ffn_up',
          'ffn_conv_w', 'ffn_conv_b', 'w_ffn_down', 'ln_final_g']
COL = ('w_in', 'decay_w2_fwd', 'decay_w2_bwd', 'iclr_a2_fwd', 'iclr_a2_bwd', 'gate_g2', 'w_ukv', 'w_ffn_up', 'ffn_conv_w')
ROW = ('w_uq', 'w_out', 'w_ffn_down')
SHARDED = [n for n in WNAMES if n in COL or n in ROW]
REPLICATED = [n for n in WNAMES if n not in SHARDED]


def _params(*sem):
    return pltpu.CompilerParams(dimension_semantics=sem, vmem_limit_bytes=VMEM_LIMIT)


def _pack(arrs, lead=0):
    parts = []
    for a in arrs:
        head = a.shape[:lead]
        flat = a.reshape(head + (-1,))
        n = flat.shape[-1]
        n_pad = -(-n // PACK_TILE) * PACK_TILE
        flat = jnp.pad(flat, [(0, 0)] * lead + [(0, n_pad - n)])
        parts.append(flat.reshape(head + (n_pad // LANES, LANES)))
    out = jnp.concatenate(parts, axis=lead)
    rows = out.shape[lead]
    rows_pad = -(-rows // PACK_ROWS) * PACK_ROWS
    return jnp.pad(out, [(0, 0)] * lead + [(0, rows_pad - rows), (0, 0)])


def _unpack(packed, shapes, lead=0):
    outs, row = [], 0
    head = packed.shape[:lead]
    for shp in shapes:
        n = 1
        for s in shp:
            n *= s
        rows = -(-n // PACK_TILE) * SUBLANES
        blk = lax.slice_in_dim(packed, row, row + rows, axis=lead)
        flat = blk.reshape(head + (rows * LANES,))
        outs.append(lax.slice_in_dim(flat, 0, n, axis=lead).reshape(head + tuple(shp)))
        row += rows
    return outs


def _all_gather(x, name):
    rows = x.shape[0]

    def body(x_ref, out_ref, send_sems, recv_sems, local_sem):
        mx, my, mc = lax.axis_index("x"), lax.axis_index("y"), lax.axis_index("c")
        me, sibling = (mx, my, mc), (mx, my, 1 - mc)
        chips = [(1 - mx, my), (mx, 1 - my), (1 - mx, 1 - my)]

        def slot(px, py, pc):
            return out_ref.at[4 * px + 2 * py + pc]

        def copy(k, block, to, src=None):
            return pltpu.make_async_remote_copy(
                src_ref=slot(*block) if src is None else src, dst_ref=slot(*block),
                send_sem=send_sems.at[k], recv_sem=recv_sems.at[k], device_id=to, device_id_type=MESH)

        mine = pltpu.make_async_copy(x_ref, slot(*me), local_sem)
        mine.start()
        first = [copy(0, me, sibling, src=x_ref)]
        first += [copy(1 + j, me, (*chip, mc), src=x_ref) for j, chip in enumerate(chips)]
        for cp in first:
            cp.start()
        passed = [copy(4 + j, (*chip, mc), sibling) for j, chip in enumerate(chips)]
        for j, chip in enumerate(chips):
            copy(1 + j, (*chip, mc), me).wait_recv()
            passed[j].start()
        copy(0, sibling, me).wait_recv()
        for j, chip in enumerate(chips):
            copy(4 + j, (*chip, 1 - mc), me).wait_recv()
        for cp in first + passed:
            cp.wait_send()
        mine.wait()

    return pl.pallas_call(
        body, name=name,
        out_shape=jax.ShapeDtypeStruct((N_DEV, rows, LANES), x.dtype),
        in_specs=[pl.BlockSpec(memory_space=pl.ANY)],
        out_specs=pl.BlockSpec(memory_space=pl.ANY),
        scratch_shapes=[pltpu.SemaphoreType.DMA((7,)), pltpu.SemaphoreType.DMA((7,)), pltpu.SemaphoreType.DMA(())],
    )(x)


def _grad_exchange(g, name):
    rows = g.shape[1]

    def body(g_ref, out_ref, send_sems, recv_sems, local_sem):
        mx, my, mc = lax.axis_index("x"), lax.axis_index("y"), lax.axis_index("c")
        me = 4 * mx + 2 * my + mc

        def flip(v, bit):
            return 1 - v if bit else v

        mine = pltpu.make_async_copy(g_ref.at[me], out_ref.at[me], local_sem)
        mine.start()
        copies = []
        for k in range(1, N_DEV):
            px, py, pc = flip(mx, k & 4), flip(my, k & 2), flip(mc, k & 1)
            peer = 4 * px + 2 * py + pc
            copies.append(pltpu.make_async_remote_copy(
                src_ref=g_ref.at[peer], dst_ref=out_ref.at[me],
                send_sem=send_sems.at[k - 1], recv_sem=recv_sems.at[k - 1],
                device_id=(px, py, pc), device_id_type=MESH))
        for cp in copies:
            cp.start()
        for cp in copies:
            cp.wait_recv()
        for cp in copies:
            cp.wait_send()
        mine.wait()

    return pl.pallas_call(
        body, name=name,
        out_shape=jax.ShapeDtypeStruct((N_DEV, rows, LANES), g.dtype),
        in_specs=[pl.BlockSpec(memory_space=pl.ANY)],
        out_specs=pl.BlockSpec(memory_space=pl.ANY),
        scratch_shapes=[pltpu.SemaphoreType.DMA((7,)), pltpu.SemaphoreType.DMA((7,)), pltpu.SemaphoreType.DMA(())],
    )(g)


def _sum_adamw(parts, w, m, v, name):
    rows = w.shape[0]
    c1 = 1.0 - ADAM_B1 ** ADAM_STEP
    c2 = 1.0 - ADAM_B2 ** ADAM_STEP

    def body(p_ref, w_ref, m_ref, v_ref, g_out, d_out, m_out, v_out):
        g = p_ref[0]
        for q in range(1, N_DEV):
            g = g + p_ref[q]
        m_new = ADAM_B1 * m_ref[...] + (1.0 - ADAM_B1) * g
        v_new = ADAM_B2 * v_ref[...] + (1.0 - ADAM_B2) * (g * g)
        m_hat = m_new / c1
        v_hat = v_new / c2
        g_out[...] = g
        d_out[...] = -ADAM_LR * (m_hat / (jnp.sqrt(v_hat) + ADAM_EPS) + ADAM_WD * w_ref[...])
        m_out[...] = m_new
        v_out[...] = v_new

    blk = pl.BlockSpec((PACK_ROWS, LANES), lambda i: (i, 0))
    out = jax.ShapeDtypeStruct((rows, LANES), F32)
    return pl.pallas_call(
        body, name=name, grid=(rows // PACK_ROWS,),
        in_specs=[pl.BlockSpec((N_DEV, PACK_ROWS, LANES), lambda i: (0, i, 0)), blk, blk, blk],
        out_specs=[blk, blk, blk, blk], out_shape=[out, out, out, out],
        compiler_params=_params("parallel"),
    )(parts, w, m, v)


def _tile(dim, cap=MM_TILE):
    if dim <= cap:
        return dim
    for t in range(cap, LANES - 1, -LANES):
        if dim % t == 0:
            return t
    return dim


def _mm_call(a, b, form, name):
    if form == 'nn':
        (m, k), n = a.shape, b.shape[1]
    elif form == 'nt':
        (m, k), n = a.shape, b.shape[0]
    else:
        (k, m), n = a.shape, b.shape[1]
    tm, tn, tk = _tile(m), _tile(n), _tile(k)
    nk = k // tk
    contract = {'nn': ((1,), (0,)), 'nt': ((1,), (1,)), 'tn': ((0,), (0,))}[form]

    def body(a_ref, b_ref, o_ref, acc_ref):
        kk = pl.program_id(2)

        @pl.when(kk == 0)
        def _():
            acc_ref[...] = jnp.zeros_like(acc_ref)

        acc_ref[...] += lax.dot_general(a_ref[...].astype(BF16), b_ref[...].astype(BF16), (contract, ((), ())),
                                        preferred_element_type=F32)

        @pl.when(kk == nk - 1)
        def _():
            o_ref[...] = acc_ref[...]

    a_spec = pl.BlockSpec((tk, tm), lambda i, j, l: (l, i)) if form == 'tn' else pl.BlockSpec((tm, tk), lambda i, j, l: (i, l))
    b_spec = pl.BlockSpec((tn, tk), lambda i, j, l: (j, l)) if form == 'nt' else pl.BlockSpec((tk, tn), lambda i, j, l: (l, j))
    return pl.pallas_call(
        body, name=name, grid=(m // tm, n // tn, nk),
        in_specs=[a_spec, b_spec], out_specs=pl.BlockSpec((tm, tn), lambda i, j, l: (i, j)),
        out_shape=jax.ShapeDtypeStruct((m, n), F32),
        scratch_shapes=[pltpu.VMEM((tm, tn), F32)],
        compiler_params=_params("parallel", "parallel", "arbitrary"),
    )(a, b)


def _make_mm(name):
    @jax.custom_vjp
    def mm(a, b):
        return _mm_call(a, b, 'nn', name + '_fwd')

    def fwd(a, b):
        return _mm_call(a, b, 'nn', name + '_fwd'), (a, b)

    def bwd(res, g):
        a, b = res
        return _mm_call(g, b, 'nt', name + '_da'), _mm_call(a, g, 'tn', name + '_db')

    mm.defvjp(fwd, bwd)
    return mm


def _rms_fwd_call(x, g, name):
    m, d = x.shape
    tm = _tile(m)

    def body(x_ref, g_ref, o_ref):
        xv = x_ref[...]
        rinv = lax.rsqrt(jnp.mean(xv * xv, axis=-1, keepdims=True) + NORM_EPS)
        o_ref[...] = xv * rinv * g_ref[...]

    return pl.pallas_call(
        body, name=name, grid=(m // tm,),
        in_specs=[pl.BlockSpec((tm, d), lambda i: (i, 0)), pl.BlockSpec((1, d), lambda i: (0, 0))],
        out_specs=pl.BlockSpec((tm, d), lambda i: (i, 0)), out_shape=jax.ShapeDtypeStruct((m, d), F32),
        compiler_params=_params("parallel"),
    )(x, g)


def _rms_bwd_call(x, g, dy, name):
    m, d = x.shape
    tm = _tile(m)

    def body(x_ref, g_ref, dy_ref, dx_ref, dg_ref):
        @pl.when(pl.program_id(0) == 0)
        def _():
            dg_ref[...] = jnp.zeros_like(dg_ref)

        xv, dyv = x_ref[...], dy_ref[...]
        rinv = lax.rsqrt(jnp.mean(xv * xv, axis=-1, keepdims=True) + NORM_EPS)
        xh = xv * rinv
        dg_ref[...] += jnp.sum(dyv * xh, axis=0, keepdims=True)
        dxh = dyv * g_ref[...]
        dx_ref[...] = rinv * (dxh - xh * jnp.mean(dxh * xh, axis=-1, keepdims=True))

    return pl.pallas_call(
        body, name=name, grid=(m // tm,),
        in_specs=[pl.BlockSpec((tm, d), lambda i: (i, 0)), pl.BlockSpec((1, d), lambda i: (0, 0)),
                  pl.BlockSpec((tm, d), lambda i: (i, 0))],
        out_specs=[pl.BlockSpec((tm, d), lambda i: (i, 0)), pl.BlockSpec((1, d), lambda i: (0, 0))],
        out_shape=[jax.ShapeDtypeStruct((m, d), F32), jax.ShapeDtypeStruct((1, d), F32)],
        compiler_params=_params("arbitrary"),
    )(x, g, dy)


def _make_rms(name):
    @jax.custom_vjp
    def rms(x, g):
        return _rms_fwd_call(x, g, name + '_fwd')

    def fwd(x, g):
        return _rms_fwd_call(x, g, name + '_fwd'), (x, g)

    def bwd(res, dy):
        x, g = res
        dx, dg = _rms_bwd_call(x, g, dy, name + '_bwd')
        return dx, dg

    rms.defvjp(fwd, bwd)
    return rms


def _softmax_rows(q_ref, k_ref):
    s = lax.dot_general(q_ref[0].astype(BF16), k_ref[0].astype(BF16), (((1,), (1,)), ((), ())),
                        preferred_element_type=F32) * MLA_SCALE
    p = jnp.exp(s - jnp.max(s, axis=-1, keepdims=True))
    return p, jnp.sum(p, axis=-1, keepdims=True)


def _attn_fwd_call(q, k, v, name):
    bh, t, _ = q.shape
    tq = min(ATT_TQ, t)

    def body(q_ref, k_ref, v_ref, o_ref):
        p, l = _softmax_rows(q_ref, k_ref)
        o_ref[0] = jnp.dot(p.astype(BF16), v_ref[0].astype(BF16), preferred_element_type=F32) / l

    return pl.pallas_call(
        body, name=name, grid=(bh, t // tq),
        in_specs=[pl.BlockSpec((1, tq, D_QK), lambda b, i: (b, i, 0)), pl.BlockSpec((1, t, D_QK), lambda b, i: (b, 0, 0)),
                  pl.BlockSpec((1, t, D_V), lambda b, i: (b, 0, 0))],
        out_specs=pl.BlockSpec((1, tq, D_V), lambda b, i: (b, i, 0)),
        out_shape=jax.ShapeDtypeStruct((bh, t, D_V), F32),
        compiler_params=_params("parallel", "parallel"),
    )(q, k, v)


def _attn_bwd_call(q, k, v, o, do, name):
    bh, t, _ = q.shape
    tq = min(ATT_TQ, t)

    def body(q_ref, k_ref, v_ref, o_ref, do_ref, dq_ref, dk_ref, dv_ref):
        @pl.when(pl.program_id(1) == 0)
        def _():
            dk_ref[...] = jnp.zeros_like(dk_ref)
            dv_ref[...] = jnp.zeros_like(dv_ref)

        p, l = _softmax_rows(q_ref, k_ref)
        p = p / l
        dov = do_ref[0]
        do_b = dov.astype(BF16)
        delta = jnp.sum(dov * o_ref[0], axis=-1, keepdims=True)
        dp = lax.dot_general(do_b, v_ref[0].astype(BF16), (((1,), (1,)), ((), ())), preferred_element_type=F32)
        ds = (p * (dp - delta) * MLA_SCALE).astype(BF16)
        dq_ref[0] = jnp.dot(ds, k_ref[0].astype(BF16), preferred_element_type=F32)
        dk_ref[0] += lax.dot_general(ds, q_ref[0].astype(BF16), (((0,), (0,)), ((), ())), preferred_element_type=F32)
        dv_ref[0] += lax.dot_general(p.astype(BF16), do_b, (((0,), (0,)), ((), ())), preferred_element_type=F32)

    qspec = pl.BlockSpec((1, tq, D_QK), lambda b, i: (b, i, 0))
    kspec = pl.BlockSpec((1, t, D_QK), lambda b, i: (b, 0, 0))
    vspec = pl.BlockSpec((1, t, D_V), lambda b, i: (b, 0, 0))
    ospec = pl.BlockSpec((1, tq, D_V), lambda b, i: (b, i, 0))
    return pl.pallas_call(
        body, name=name, grid=(bh, t // tq),
        in_specs=[qspec, kspec, vspec, ospec, ospec], out_specs=[qspec, kspec, vspec],
        out_shape=[jax.ShapeDtypeStruct(q.shape, F32), jax.ShapeDtypeStruct(k.shape, F32), jax.ShapeDtypeStruct(v.shape, F32)],
        compiler_params=_params("parallel", "arbitrary"),
    )(q, k, v, o, do)


@jax.custom_vjp
def _attention(q, k, v):
    return _attn_fwd_call(q, k, v, 'attn_fwd')


def _attention_fwd(q, k, v):
    o = _attn_fwd_call(q, k, v, 'attn_fwd')
    return o, (q, k, v, o)


def _attention_bwd(res, do):
    q, k, v, o = res
    return tuple(_attn_bwd_call(q, k, v, o, do, 'attn_bwd'))


_attention.defvjp(_attention_fwd, _attention_bwd)


SROWS = N * D_RWKV // SEG


def _seg_ones():
    r = lax.broadcasted_iota(jnp.int32, (SEG, SEG), 0) >> 6
    c = lax.broadcasted_iota(jnp.int32, (SEG, SEG), 1) >> 6
    return (r == c).astype(BF16)


def _eye_mask():
    r = lax.broadcasted_iota(jnp.int32, (SROWS, SEG), 0) & (N - 1)
    c = lax.broadcasted_iota(jnp.int32, (SROWS, SEG), 1) & (N - 1)
    return r == c


def _row2(ref, bi, ti, dtype=F32):
    parts = [jnp.broadcast_to(ref[bi, pl.ds(ti, 1), pl.ds(SEG * q, SEG)].astype(dtype), (N, SEG))
             for q in range(D_RWKV // SEG)]
    return jnp.concatenate(parts, axis=0)


def _split2(x):
    hi = x.astype(BF16)
    return hi, (x - hi.astype(F32)).astype(BF16)


def _col_sum(x):
    return jnp.concatenate([jnp.sum(x[N * q:N * (q + 1)], axis=0, keepdims=True) for q in range(D_RWKV // SEG)], axis=1)


def _scan_specs(b, t, rev):
    nc = t // SCAN_CHUNK
    if rev:
        return (pl.BlockSpec((b, SCAN_CHUNK, D_RWKV), lambda c: (0, nc - 1 - c, 0)),
                pl.BlockSpec((b, SCAN_CHUNK, SROWS, SEG), lambda c: (0, nc - 1 - c, 0, 0)))
    return (pl.BlockSpec((b, SCAN_CHUNK, D_RWKV), lambda c: (0, c, 0)),
            pl.BlockSpec((b, SCAN_CHUNK, SROWS, SEG), lambda c: (0, c, 0, 0)))


def _scan_fwd_call(r, v, kk, wf, kf, qf, wb, kb, qb):
    b, t, _ = r.shape

    last = SCAN_CHUNK - 1

    def body(rf, vf, kkf, wf_, kf_, qf_, rb, vb, kkb, wb_, kb_, qb_, yf, yb, sf, sb, *states):
        @pl.when(pl.program_id(0) == 0)
        def _():
            for st in states:
                st[...] = jnp.zeros_like(st)

        ones, mask = _seg_ones(), _eye_mask()
        zero16 = jnp.zeros((), BF16)
        chains = []
        for bi in range(b):
            chains.append((rf, vf, kkf, wf_, kf_, qf_, yf, sf, states[2 * bi], bi, False))
            chains.append((rb, vb, kkb, wb_, kb_, qb_, yb, sb, states[2 * bi + 1], bi, True))

        def tix(i, rev):
            return last - i if rev else i

        def put_y(y_, bi, ti, ycol):
            y_[bi, pl.ds(ti, 1), :] = _col_sum(jnp.where(mask, ycol, 0.0))

        def steps(i, with_y):
            parts = []
            for (r_, v_, kk_, w_, k_, q_, y_, s_, st, bi, rev) in chains:
                ti = tix(i, rev)
                s = st[...]
                s_[bi, ti] = s
                parts += list(_split2(s * _row2(kk_, bi, ti)))
                parts.append(jnp.where(mask, _row2(v_, bi, ti, BF16), zero16))
                if with_y:
                    parts.append((s * _row2(r_, bi, tix(i - 1, rev))).astype(BF16))
            res = jnp.dot(jnp.concatenate(parts, axis=0), ones, preferred_element_type=F32)
            off = 0
            for (r_, v_, kk_, w_, k_, q_, y_, s_, st, bi, rev) in chains:
                ti = tix(i, rev)
                u = res[off:off + SROWS] + res[off + SROWS:off + 2 * SROWS]
                vcol = res[off + 2 * SROWS:off + 3 * SROWS]
                off += 3 * SROWS
                if with_y:
                    put_y(y_, bi, tix(i - 1, rev), res[off:off + SROWS])
                    off += SROWS
                st[...] = st[...] * _row2(w_, bi, ti) - u * _row2(q_, bi, ti) + vcol * _row2(k_, bi, ti)

        steps(0, False)

        def loop(i, carry):
            steps(i, True)
            return carry

        lax.fori_loop(1, SCAN_CHUNK, loop, 0)
        parts = [(c[8][...] * _row2(c[0], c[9], tix(last, c[10]))).astype(BF16) for c in chains]
        res = jnp.dot(jnp.concatenate(parts, axis=0), ones, preferred_element_type=F32)
        for n, c in enumerate(chains):
            put_y(c[6], c[9], tix(last, c[10]), res[n * SROWS:(n + 1) * SROWS])

    fr, fs = _scan_specs(b, t, False)
    br, bs = _scan_specs(b, t, True)
    y_shape = jax.ShapeDtypeStruct((b, t, D_RWKV), F32)
    s_shape = jax.ShapeDtypeStruct((b, t, SROWS, SEG), F32)
    return pl.pallas_call(
        body, name='scan_fwd', grid=(t // SCAN_CHUNK,),
        in_specs=[fr] * 6 + [br] * 6, out_specs=[fr, br, fs, bs], out_shape=[y_shape, y_shape, s_shape, s_shape],
        scratch_shapes=[pltpu.VMEM((SROWS, SEG), F32)] * (2 * b),
        compiler_params=_params("arbitrary"),
    )(r, v, kk, wf, kf, qf, r, v, kk, wb, kb, qb)


def _scan_bwd_call(r, v, kk, wf, kf, qf, wb, kb, qb, sf, sb, dyf, dyb):
    b, t, _ = r.shape

    last = SCAN_CHUNK - 1

    def body(rf, vf, kkf, wf_, kf_, qf_, sf_, dyf_, rb, vb, kkb, wb_, kb_, qb_, sb_, dyb_,
             drf, dvf, dkkf, dwf, dkf, dqf, drb, dvb, dkkb, dwb, dkb, dqb, *scratch):
        @pl.when(pl.program_id(0) == 0)
        def _():
            for n in range(2 * b):
                scratch[4 * n][...] = jnp.zeros_like(scratch[4 * n])

        ones, mask = _seg_ones(), _eye_mask()
        zero16 = jnp.zeros((), BF16)
        chains = []
        for bi in range(b):
            chains.append((rf, vf, kkf, wf_, kf_, qf_, sf_, dyf_, (drf, dvf, dkkf, dwf, dkf, dqf),
                           scratch[8 * bi:8 * bi + 4], bi, True))
            chains.append((rb, vb, kkb, wb_, kb_, qb_, sb_, dyb_, (drb, dvb, dkkb, dwb, dkb, dqb),
                           scratch[8 * bi + 4:8 * bi + 8], bi, False))

        def tix(i, rev):
            return last - i if rev else i

        def state_free_parts(v_, dy_, kk_, s_, bi, ti):
            return [jnp.where(mask, _row2(v_, bi, ti, BF16), zero16), jnp.where(mask, _row2(dy_, bi, ti, BF16), zero16),
                    (s_[bi, ti] * _row2(kk_, bi, ti)).astype(BF16)]

        def keep(scr, res, off):
            for n in range(3):
                scr[1 + n][...] = res[off + n * SROWS:off + (n + 1) * SROWS]
            return off + 3 * SROWS

        def first():
            parts = []
            for (r_, v_, kk_, w_, k_, q_, s_, dy_, outs, scr, bi, rev) in chains:
                parts += state_free_parts(v_, dy_, kk_, s_, bi, tix(0, rev))
            res = jnp.dot(jnp.concatenate(parts, axis=0), ones, preferred_element_type=F32)
            off = 0
            for c in chains:
                off = keep(c[9], res, off)

        def steps(i, has_next, recompute):
            parts = []
            for (r_, v_, kk_, w_, k_, q_, s_, dy_, outs, scr, bi, rev) in chains:
                ti = tix(i, rev)
                gst, vc, dc, uc = scr
                dycol = dc[...]
                if recompute:
                    sc = s_[bi, ti] * _row2(w_, bi, ti) - uc[...] * _row2(q_, bi, ti) + vc[...] * _row2(k_, bi, ti)
                else:
                    sc = s_[bi, tix(i - 1, rev)]
                outs[0][bi, pl.ds(ti, 1), :] = _col_sum(sc * dycol)
                g = gst[...] + dycol * _row2(r_, bi, ti)
                gst[...] = g
                parts += list(_split2(g * _row2(q_, bi, ti)))
                parts.append((g * _row2(k_, bi, ti)).astype(BF16))
                if has_next:
                    parts += state_free_parts(v_, dy_, kk_, s_, bi, tix(i + 1, rev))
            res = jnp.dot(jnp.concatenate(parts, axis=0), ones, preferred_element_type=F32)
            off = 0
            for (r_, v_, kk_, w_, k_, q_, s_, dy_, outs, scr, bi, rev) in chains:
                ti = tix(i, rev)
                gst, vc, dc, uc = scr
                dr_, dv_, dkk_, dw_, dk_, dq_ = outs

                def put(ref, val, sign=1.0):
                    ref[bi, pl.ds(ti, 1), :] = sign * _col_sum(val)

                gq = res[off:off + SROWS] + res[off + SROWS:off + 2 * SROWS]
                put(dv_, jnp.where(mask, res[off + 2 * SROWS:off + 3 * SROWS], 0.0))
                off += 3 * SROWS
                g, sp = gst[...], s_[bi, ti]
                put(dk_, g * vc[...])
                put(dw_, g * sp)
                put(dq_, g * uc[...], -1.0)
                put(dkk_, sp * gq, -1.0)
                gst[...] = g * _row2(w_, bi, ti) - gq * _row2(kk_, bi, ti)
                if has_next:
                    off = keep(scr, res, off)

        first()
        steps(0, True, True)

        def loop(i, carry):
            steps(i, True, False)
            return carry

        lax.fori_loop(1, last, loop, 0)
        steps(last, False, False)

    fr, fs = _scan_specs(b, t, True)
    br, bs = _scan_specs(b, t, False)
    y_shape = jax.ShapeDtypeStruct((b, t, D_RWKV), F32)
    return pl.pallas_call(
        body, name='scan_bwd', grid=(t // SCAN_CHUNK,),
        in_specs=[fr] * 6 + [fs, fr] + [br] * 6 + [bs, br],
        out_specs=[fr] * 6 + [br] * 6, out_shape=[y_shape] * 12,
        scratch_shapes=[pltpu.VMEM((SROWS, SEG), F32)] * (8 * b),
        compiler_params=_params("arbitrary"),
    )(r, v, kk, wf, kf, qf, sf, dyf, r, v, kk, wb, kb, qb, sb, dyb)


@jax.custom_vjp
def _wkv_scan(r, v, kk, wf, kf, qf, wb, kb, qb):
    yf, yb, _, _ = _scan_fwd_call(r, v, kk, wf, kf, qf, wb, kb, qb)
    return yf, yb


def _wkv_scan_fwd(r, v, kk, wf, kf, qf, wb, kb, qb):
    yf, yb, sf, sb = _scan_fwd_call(r, v, kk, wf, kf, qf, wb, kb, qb)
    return (yf, yb), (r, v, kk, wf, kf, qf, wb, kb, qb, sf, sb)


def _wkv_scan_bwd(res, dy):
    r, v, kk, wf, kf, qf, wb, kb, qb, sf, sb = res
    (drf, dvf, dkkf, dwf, dkf, dqf, drb, dvb, dkkb, dwb, dkb, dqb) = _scan_bwd_call(
        r, v, kk, wf, kf, qf, wb, kb, qb, sf, sb, dy[0], dy[1])
    return drf + drb, dvf + dvb, dkkf + dkkb, dwf, dkf, dqf, dwb, dkb, dqb


_wkv_scan.defvjp(_wkv_scan_fwd, _wkv_scan_bwd)


def _shift(z, left):
    if left:
        return jnp.pad(z[:, :-1], ((0, 0), (1, 0), (0, 0)))
    return jnp.pad(z[:, 1:], ((0, 0), (0, 1), (0, 0)))


def _rope_tables(t):
    inv_freq = jnp.power(ROPE_THETA, -jnp.arange(0, D_ROPE, 2, dtype=F32) / D_ROPE)
    ang = jnp.arange(t, dtype=F32)[:, None] * inv_freq[None, :]
    ang = jnp.concatenate([ang, ang], axis=-1)
    return jnp.cos(ang), jnp.sin(ang)


def _rope(x, cos, sin):
    x1, x2 = jnp.split(x, 2, axis=-1)
    return x * cos + jnp.concatenate([-x2, x1], axis=-1) * sin


def _rwkv_mixer(z, full, rep):
    b, t, _ = z.shape
    m = b * t
    z = z + rep['shift_mu_prev'] * (_shift(z, True) - z) + rep['shift_mu_next'] * (_shift(z, False) - z)
    z = z.reshape(m, RWKV_COLS)
    r, k, v = z[:, :512], z[:, 512:1024], z[:, 1024:1536]
    wdf, wdb, adf, adb, gd = z[:, 1536:1600], z[:, 1600:1664], z[:, 1664:1728], z[:, 1728:1792], z[:, 1792:1920]

    def decay(wd, w0, w2, name):
        logit = w0 + _make_mm(name)(jnp.tanh(wd), w2)
        return jnp.exp(-jnp.exp(-jax.nn.softplus(-logit) - 0.5))

    w_f = decay(wdf, rep['decay_w0_fwd'], full['decay_w2_fwd'], 'mm_decay_f')
    w_b = decay(wdb, rep['decay_w0_bwd'], full['decay_w2_bwd'], 'mm_decay_b')
    a_f = jax.nn.sigmoid(rep['iclr_a0_fwd'] + _make_mm('mm_iclr_f')(adf, full['iclr_a2_fwd']))
    a_b = jax.nn.sigmoid(rep['iclr_a0_bwd'] + _make_mm('mm_iclr_b')(adb, full['iclr_a2_bwd']))
    g = _make_mm('mm_gate')(jax.nn.sigmoid(gd), full['gate_g2'])
    kk = (k * rep['k_k']).reshape(m, H, N)
    kk = (kk / jnp.maximum(jnp.sqrt(jnp.sum(kk * kk, axis=-1, keepdims=True)), L2_EPS)).reshape(m, D_RWKV)
    k_f = k * (1.0 + (a_f - 1.0) * rep['k_a'])
    k_b = k * (1.0 + (a_b - 1.0) * rep['k_a'])
    seq = lambda a: a.reshape(b, t, D_RWKV)
    y_f, y_b = _wkv_scan(seq(r), seq(v), seq(kk), seq(w_f), seq(k_f), seq(kk * a_f), seq(w_b), seq(k_b), seq(kk * a_b))
    y = (y_f + y_b).reshape(m, H, N)
    mu = jnp.mean(y, axis=-1, keepdims=True)
    var = jnp.mean(jnp.square(y - mu), axis=-1, keepdims=True)
    y = ((y - mu) * lax.rsqrt(var + GN_EPS)).reshape(m, D_RWKV) * rep['ln_x_g'] + rep['ln_x_b']
    r_h, v_h = r.reshape(m, H, N), v.reshape(m, H, N)
    bonus = jnp.sum(r_h * (k_f + k_b).reshape(m, H, N) * rep['r_k'], axis=-1, keepdims=True) * v_h
    return (y + bonus.reshape(m, D_RWKV)) * g


def _mla_mixer(z, full, rep, b, t):
    m = b * t
    c_q, c_kv, k_rope = z[:, :768], z[:, 768:1024], z[:, 1024:1056]
    cos, sin = _rope_tables(t)
    q = _make_mm('mm_uq')(_make_rms('rms_q')(c_q, rep['q_norm_g']), full['w_uq']).reshape(b, t, H, D_QK)
    q = jnp.concatenate([q[..., :D_NOPE], _rope(q[..., D_NOPE:], cos[:, None, :], sin[:, None, :])], axis=-1)
    kv = _make_mm('mm_ukv')(_make_rms('rms_kv')(c_kv, rep['kv_norm_g']), full['w_ukv']).reshape(b, t, H, D_NOPE + D_V)
    k_rope = _rope(k_rope.reshape(b, t, D_ROPE), cos, sin)
    k = jnp.concatenate([kv[..., :D_NOPE], jnp.broadcast_to(k_rope[:, :, None, :], (b, t, H, D_ROPE))], axis=-1)
    heads = lambda a: a.transpose(0, 2, 1, 3).reshape(b * H, t, a.shape[-1])
    o = _attention(heads(q), heads(k), heads(kv[..., D_NOPE:]))
    o = o.reshape(b, H, t, D_V).transpose(0, 2, 1, 3).reshape(m, H * D_V)
    return _make_rms('rms_mla_out')(o, rep['mla_out_g'])


def _local_loss(full, rep, x, target):
    b, t, d = x.shape
    m = b * t
    xf = x.reshape(m, d)
    n1 = _make_rms('rms_mix')(xf, rep['ln_mix_g'])
    d_in = full['w_in'].shape[1]
    d_in_pad = -(-d_in // MM_TILE) * MM_TILE
    z = _make_mm('mm_in')(n1, jnp.pad(full['w_in'], ((0, 0), (0, d_in_pad - d_in))))
    y_rwkv = _rwkv_mixer(z[:, :RWKV_COLS].reshape(b, t, RWKV_COLS), full, rep)
    y_mla = _mla_mixer(z[:, RWKV_COLS:d_in], full, rep, b, t)
    h = xf + _make_mm('mm_out')(jnp.concatenate([y_rwkv, y_mla], axis=-1), full['w_out'])
    n2 = _make_rms('rms_ffn')(h, rep['ln_ffn_g'])
    u = _make_mm('mm_up')(n2, full['w_ffn_up']).reshape(b, t, 2 * D_FF)
    cw = full['ffn_conv_w']
    u = cw[0] * _shift(u, True) + cw[1] * u + cw[2] * _shift(u, False) + rep['ffn_conv_b']
    act = (jax.nn.silu(u[..., :D_FF]) * u[..., D_FF:]).reshape(m, D_FF)
    h = h + _make_mm('mm_down')(act, full['w_ffn_down'])
    out = _make_rms('rms_final')(h, rep['ln_final_g'])
    err = jnp.square(out - target.reshape(m, d))
    return 0.5 * jnp.sum(jnp.mean(err, axis=-1))


def _mat(a):
    if a.ndim == 1:
        return a.reshape(1, -1)
    if a.ndim == 3:
        return a.reshape(a.shape[1:])
    return a


def _join(shards, name):
    if name in ROW:
        return shards.reshape(-1, shards.shape[-1])
    return shards.transpose(1, 0, 2).reshape(shards.shape[1], -1)


def _cut(whole, name):
    r, c = whole.shape
    if name in ROW:
        return whole.reshape(N_DEV, r // N_DEV, c)
    return whole.reshape(r, N_DEV, c // N_DEV).transpose(1, 0, 2)


def kernel(x, ln_mix_g, w_in, shift_mu_prev, shift_mu_next, decay_w0_fwd, decay_w2_fwd, decay_w0_bwd, decay_w2_bwd, iclr_a0_fwd, iclr_a2_fwd, iclr_a0_bwd, iclr_a2_bwd, gate_g2, k_k, k_a, r_k, ln_x_g, ln_x_b, q_norm_g, w_uq, kv_norm_g, w_ukv, mla_out_g, w_out, ln_ffn_g, w_ffn_up, ffn_conv_w, ffn_conv_b, w_ffn_down, ln_final_g, loss_target, m_ln_mix_g, m_w_in, m_shift_mu_prev, m_shift_mu_next, m_decay_w0_fwd, m_decay_w2_fwd, m_decay_w0_bwd, m_decay_w2_bwd, m_iclr_a0_fwd, m_iclr_a2_fwd, m_iclr_a0_bwd, m_iclr_a2_bwd, m_gate_g2, m_k_k, m_k_a, m_r_k, m_ln_x_g, m_ln_x_b, m_q_norm_g, m_w_uq, m_kv_norm_g, m_w_ukv, m_mla_out_g, m_w_out, m_ln_ffn_g, m_w_ffn_up, m_ffn_conv_w, m_ffn_conv_b, m_w_ffn_down, m_ln_final_g, v_ln_mix_g, v_w_in, v_shift_mu_prev, v_shift_mu_next, v_decay_w0_fwd, v_decay_w2_fwd, v_decay_w0_bwd, v_decay_w2_bwd, v_iclr_a0_fwd, v_iclr_a2_fwd, v_iclr_a0_bwd, v_iclr_a2_bwd, v_gate_g2, v_k_k, v_k_a, v_r_k, v_ln_x_g, v_ln_x_b, v_q_norm_g, v_w_uq, v_kv_norm_g, v_w_ukv, v_mla_out_g, v_w_out, v_ln_ffn_g, v_w_ffn_up, v_ffn_conv_w, v_ffn_conv_b, v_w_ffn_down, v_ln_final_g):
    given = dict(locals())
    w = {n: given[n] for n in WNAMES}
    mom = {n: given['m_' + n] for n in WNAMES}
    var = {n: given['v_' + n] for n in WNAMES}

    shard_shapes = [_mat(w[n]).shape for n in SHARDED]
    w_pack = _pack([w[n] for n in SHARDED])
    gathered = _unpack(_all_gather(w_pack, 'gather_weights'), shard_shapes, lead=1)
    full = {n: _join(s, n) for n, s in zip(SHARDED, gathered)}
    rep = {n: _mat(w[n]) for n in REPLICATED}
    rep['r_k'] = w['r_k'].reshape(H, N)

    loss_local, (g_full, g_rep, g_x) = jax.value_and_grad(_local_loss, argnums=(0, 1, 2))(full, rep, x, loss_target)

    g_pack = _pack([_cut(g_full[n], n) for n in SHARDED], lead=1)
    parts = _grad_exchange(g_pack, 'exchange_grads')
    s_out = _sum_adamw(parts, w_pack, _pack([mom[n] for n in SHARDED]), _pack([var[n] for n in SHARDED]), 'adamw_sharded')
    s_out = [_unpack(o, [w[n].shape for n in SHARDED]) for o in s_out]

    zero = jnp.zeros((1,), F32)
    r_pack = _pack([g_rep[n] for n in REPLICATED] + [loss_local.reshape(1)])
    r_parts = _all_gather(r_pack, 'gather_small')
    r_out = _sum_adamw(r_parts, _pack([w[n] for n in REPLICATED] + [zero]), _pack([mom[n] for n in REPLICATED] + [zero]),
                       _pack([var[n] for n in REPLICATED] + [zero]), 'adamw_replicated')
    r_out = [_unpack(o, [w[n].shape for n in REPLICATED] + [(1,)]) for o in r_out]

    loss = r_out[0][-1].reshape(())
    outs = [loss, g_x]
    for kind in range(4):
        by_name = dict(zip(SHARDED, s_out[kind]))
        by_name.update(zip(REPLICATED, r_out[kind][:-1]))
        outs += [by_name[n] for n in WNAMES]
    return tuple(outs)
```

```python
import functools

import jax
import jax.numpy as jnp
from jax import lax
from jax.experimental import pallas as pl
from jax.experimental.pallas import tpu as pltpu

F32 = jnp.float32
BF16 = jnp.bfloat16
MESH = pl.DeviceIdType.MESH

N_DEV = 8
LANES = 128
SUBLANES = 8
PACK_TILE = SUBLANES * LANES
PACK_ROWS = 512
MM_TILE = 512
MM_TILE_WIDE = 1408
MM_K_WHOLE = 2816
VMEM_LIMIT = 48 * 1024 * 1024

H = 8
N = 64
D_RWKV = H * N
D_NOPE, D_ROPE, D_V = 64, 32, 64
D_QK = D_NOPE + D_ROPE
MLA_SCALE = D_QK ** -0.5
ROPE_THETA = 10000.0
RWKV_COLS = 1920
MLA_COLS = 1056
D_FF = 2816
NORM_EPS = 1e-6
GN_EPS = 64e-5
L2_EPS = 1e-12
ADAM_LR, ADAM_B1, ADAM_B2, ADAM_EPS, ADAM_WD, ADAM_STEP = 0.001, 0.9, 0.999, 1e-08, 0.01, 10

SCAN_CHUNK = 16
ATT_TQ = 256
SEG = 256
FFN_COLS = 256

WNAMES = ['ln_mix_g', 'w_in', 'shift_mu_prev', 'shift_mu_next', 'decay_w0_fwd', 'decay_w2_fwd', 'decay_w0_bwd',
          'decay_w2_bwd', 'iclr_a0_fwd', 'iclr_a2_fwd', 'iclr_a0_bwd', 'iclr_a2_bwd', 'gate_g2', 'k_k', 'k_a', 'r_k',
          'ln_x_g', 'ln_x_b', 'q_norm_g', 'w_uq', 'kv_norm_g', 'w_ukv', 'mla_out_g', 'w_out', 'ln_ffn_g', 'w_ffn_up',
          'ffn_conv_w', 'ffn_conv_b', 'w_ffn_down', 'ln_final_g']
COL = ('w_in', 'decay_w2_fwd', 'decay_w2_bwd', 'iclr_a2_fwd', 'iclr_a2_bwd', 'gate_g2', 'w_ukv', 'w_ffn_up', 'ffn_conv_w')
ROW = ('w_uq', 'w_out', 'w_ffn_down')
SHARDED = [n for n in WNAMES if n in COL or n in ROW]
REPLICATED = [n for n in WNAMES if n not in SHARDED]


def _params(*sem):
    return pltpu.CompilerParams(dimension_semantics=sem, vmem_limit_bytes=VMEM_LIMIT)


def _pack(arrs, lead=0):
    parts = []
    for a in arrs:
        head = a.shape[:lead]
        flat = a.reshape(head + (-1,))
        n = flat.shape[-1]
        n_pad = -(-n // PACK_TILE) * PACK_TILE
        flat = jnp.pad(flat, [(0, 0)] * lead + [(0, n_pad - n)])
        parts.append(flat.reshape(head + (n_pad // LANES, LANES)))
    out = jnp.concatenate(parts, axis=lead)
    rows = out.shape[lead]
    rows_pad = -(-rows // PACK_ROWS) * PACK_ROWS
    return jnp.pad(out, [(0, 0)] * lead + [(0, rows_pad - rows), (0, 0)])


def _unpack(packed, shapes, lead=0):
    outs, row = [], 0
    head = packed.shape[:lead]
    for shp in shapes:
        n = 1
        for s in shp:
            n *= s
        rows = -(-n // PACK_TILE) * SUBLANES
        blk = lax.slice_in_dim(packed, row, row + rows, axis=lead)
        flat = blk.reshape(head + (rows * LANES,))
        outs.append(lax.slice_in_dim(flat, 0, n, axis=lead).reshape(head + tuple(shp)))
        row += rows
    return outs


def _all_gather(x, name):
    rows = x.shape[0]

    def body(x_ref, out_ref, send_sems, recv_sems, local_sem):
        mx, my, mc = lax.axis_index("x"), lax.axis_index("y"), lax.axis_index("c")
        me, sibling = (mx, my, mc), (mx, my, 1 - mc)
        chips = [(1 - mx, my), (mx, 1 - my), (1 - mx, 1 - my)]

        def slot(px, py, pc):
            return out_ref.at[4 * px + 2 * py + pc]

        def copy(k, block, to, src=None):
            return pltpu.make_async_remote_copy(
                src_ref=slot(*block) if src is None else src, dst_ref=slot(*block),
                send_sem=send_sems.at[k], recv_sem=recv_sems.at[k], device_id=to, device_id_type=MESH)

        mine = pltpu.make_async_copy(x_ref, slot(*me), local_sem)
        mine.start()
        first = [copy(0, me, sibling, src=x_ref)]
        first += [copy(1 + j, me, (*chip, mc), src=x_ref) for j, chip in enumerate(chips)]
        for cp in first:
            cp.start()
        passed = [copy(4 + j, (*chip, mc), sibling) for j, chip in enumerate(chips)]
        for j, chip in enumerate(chips):
            copy(1 + j, (*chip, mc), me).wait_recv()
            passed[j].start()
        copy(0, sibling, me).wait_recv()
        for j, chip in enumerate(chips):
            copy(4 + j, (*chip, 1 - mc), me).wait_recv()
        for cp in first + passed:
            cp.wait_send()
        mine.wait()

    return pl.pallas_call(
        body, name=name,
        out_shape=jax.ShapeDtypeStruct((N_DEV, rows, LANES), x.dtype),
        in_specs=[pl.BlockSpec(memory_space=pl.ANY)],
        out_specs=pl.BlockSpec(memory_space=pl.ANY),
        scratch_shapes=[pltpu.SemaphoreType.DMA((7,)), pltpu.SemaphoreType.DMA((7,)), pltpu.SemaphoreType.DMA(())],
    )(x)


def _grad_exchange(g, name):
    rows = g.shape[1]

    def body(g_ref, out_ref, send_sems, recv_sems, local_sem):
        mx, my, mc = lax.axis_index("x"), lax.axis_index("y"), lax.axis_index("c")
        me = 4 * mx + 2 * my + mc

        def flip(v, bit):
            return 1 - v if bit else v

        mine = pltpu.make_async_copy(g_ref.at[me], out_ref.at[me], local_sem)
        mine.start()
        copies = []
        for k in range(1, N_DEV):
            px, py, pc = flip(mx, k & 4), flip(my, k & 2), flip(mc, k & 1)
            peer = 4 * px + 2 * py + pc
            copies.append(pltpu.make_async_remote_copy(
                src_ref=g_ref.at[peer], dst_ref=out_ref.at[me],
                send_sem=send_sems.at[k - 1], recv_sem=recv_sems.at[k - 1],
                device_id=(px, py, pc), device_id_type=MESH))
        for cp in copies:
            cp.start()
        for cp in copies:
            cp.wait_recv()
        for cp in copies:
            cp.wait_send()
        mine.wait()

    return pl.pallas_call(
        body, name=name,
        out_shape=jax.ShapeDtypeStruct((N_DEV, rows, LANES), g.dtype),
        in_specs=[pl.BlockSpec(memory_space=pl.ANY)],
        out_specs=pl.BlockSpec(memory_space=pl.ANY),
        scratch_shapes=[pltpu.SemaphoreType.DMA((7,)), pltpu.SemaphoreType.DMA((7,)), pltpu.SemaphoreType.DMA(())],
    )(g)


def _sum_adamw(parts, w, m, v, name):
    rows = w.shape[0]
    c1 = 1.0 - ADAM_B1 ** ADAM_STEP
    c2 = 1.0 - ADAM_B2 ** ADAM_STEP

    def body(p_ref, w_ref, m_ref, v_ref, g_out, d_out, m_out, v_out):
        g = p_ref[0]
        for q in range(1, N_DEV):
            g = g + p_ref[q]
        m_new = ADAM_B1 * m_ref[...] + (1.0 - ADAM_B1) * g
        v_new = ADAM_B2 * v_ref[...] + (1.0 - ADAM_B2) * (g * g)
        m_hat = m_new / c1
        v_hat = v_new / c2
        g_out[...] = g
        d_out[...] = -ADAM_LR * (m_hat / (jnp.sqrt(v_hat) + ADAM_EPS) + ADAM_WD * w_ref[...])
        m_out[...] = m_new
        v_out[...] = v_new

    blk = pl.BlockSpec((PACK_ROWS, LANES), lambda i: (i, 0))
    out = jax.ShapeDtypeStruct((rows, LANES), F32)
    return pl.pallas_call(
        body, name=name, grid=(rows // PACK_ROWS,),
        in_specs=[pl.BlockSpec((N_DEV, PACK_ROWS, LANES), lambda i: (0, i, 0)), blk, blk, blk],
        out_specs=[blk, blk, blk, blk], out_shape=[out, out, out, out],
        compiler_params=_params("parallel"),
    )(parts, w, m, v)


def _tile(dim, cap=MM_TILE):
    if dim <= cap:
        return dim
    for t in range(cap, LANES - 1, -LANES):
        if dim % t == 0:
            return t
    return dim


def _mm_call(a, b, form, name):
    if form == 'nn':
        (m, k), n = a.shape, b.shape[1]
    elif form == 'nt':
        (m, k), n = a.shape, b.shape[0]
    else:
        (k, m), n = a.shape, b.shape[1]
    tk = k if (form == 'nn' and k <= MM_K_WHOLE) else _tile(k, MM_TILE_WIDE)
    tm = _tile(m, MM_TILE_WIDE if form == 'tn' else MM_TILE)
    tn = _tile(n, MM_TILE_WIDE)
    nk = k // tk
    contract = {'nn': ((1,), (0,)), 'nt': ((1,), (1,)), 'tn': ((0,), (0,))}[form]

    def body(a_ref, b_ref, o_ref):
        part = lax.dot_general(a_ref[...].astype(BF16), b_ref[...].astype(BF16), (contract, ((), ())),
                               preferred_element_type=F32)
        if nk == 1:
            o_ref[...] = part
        else:
            @pl.when(pl.program_id(2) == 0)
            def _():
                o_ref[...] = part

            @pl.when(pl.program_id(2) > 0)
            def _():
                o_ref[...] += part

    a_spec = pl.BlockSpec((tk, tm), lambda j, i, l: (l, i)) if form == 'tn' else pl.BlockSpec((tm, tk), lambda j, i, l: (i, l))
    b_spec = pl.BlockSpec((tn, tk), lambda j, i, l: (j, l)) if form == 'nt' else pl.BlockSpec((tk, tn), lambda j, i, l: (l, j))
    return pl.pallas_call(
        body, name=name, grid=(n // tn, m // tm, nk),
        in_specs=[a_spec, b_spec], out_specs=pl.BlockSpec((tm, tn), lambda j, i, l: (i, j)),
        out_shape=jax.ShapeDtypeStruct((m, n), F32),
        compiler_params=_params("parallel", "parallel", "arbitrary"),
    )(a, b)


def _make_mm(name):
    @jax.custom_vjp
    def mm(a, b):
        return _mm_call(a, b.astype(BF16), 'nn', name + '_fwd')

    def fwd(a, b):
        b16 = b.astype(BF16)
        return _mm_call(a, b16, 'nn', name + '_fwd'), (a, b16)

    def bwd(res, g):
        a, b16 = res
        return _mm_call(g, b16, 'nt', name + '_da'), _mm_call(a, g, 'tn', name + '_db')

    mm.defvjp(fwd, bwd)
    return mm


def _rms_fwd_call(x, g, name):
    m, d = x.shape
    tm = _tile(m)

    def body(x_ref, g_ref, o_ref):
        xv = x_ref[...]
        rinv = lax.rsqrt(jnp.mean(xv * xv, axis=-1, keepdims=True) + NORM_EPS)
        o_ref[...] = xv * rinv * g_ref[...]

    return pl.pallas_call(
        body, name=name, grid=(m // tm,),
        in_specs=[pl.BlockSpec((tm, d), lambda i: (i, 0)), pl.BlockSpec((1, d), lambda i: (0, 0))],
        out_specs=pl.BlockSpec((tm, d), lambda i: (i, 0)), out_shape=jax.ShapeDtypeStruct((m, d), F32),
        compiler_params=_params("parallel"),
    )(x, g)


def _rms_bwd_call(x, g, dy, name):
    m, d = x.shape
    tm = _tile(m)

    def body(x_ref, g_ref, dy_ref, dx_ref, dg_ref):
        @pl.when(pl.program_id(0) == 0)
        def _():
            dg_ref[...] = jnp.zeros_like(dg_ref)

        xv, dyv = x_ref[...], dy_ref[...]
        rinv = lax.rsqrt(jnp.mean(xv * xv, axis=-1, keepdims=True) + NORM_EPS)
        xh = xv * rinv
        dg_ref[...] += jnp.sum(dyv * xh, axis=0, keepdims=True)
        dxh = dyv * g_ref[...]
        dx_ref[...] = rinv * (dxh - xh * jnp.mean(dxh * xh, axis=-1, keepdims=True))

    return pl.pallas_call(
        body, name=name, grid=(m // tm,),
        in_specs=[pl.BlockSpec((tm, d), lambda i: (i, 0)), pl.BlockSpec((1, d), lambda i: (0, 0)),
                  pl.BlockSpec((tm, d), lambda i: (i, 0))],
        out_specs=[pl.BlockSpec((tm, d), lambda i: (i, 0)), pl.BlockSpec((1, d), lambda i: (0, 0))],
        out_shape=[jax.ShapeDtypeStruct((m, d), F32), jax.ShapeDtypeStruct((1, d), F32)],
        compiler_params=_params("arbitrary"),
    )(x, g, dy)


def _make_rms(name):
    @jax.custom_vjp
    def rms(x, g):
        return _rms_fwd_call(x, g, name + '_fwd')

    def fwd(x, g):
        return _rms_fwd_call(x, g, name + '_fwd'), (x, g)

    def bwd(res, dy):
        x, g = res
        dx, dg = _rms_bwd_call(x, g, dy, name + '_bwd')
        return dx, dg

    rms.defvjp(fwd, bwd)
    return rms


def _time_shifts(x):
    t = x.shape[0]
    rows = lax.broadcasted_iota(jnp.int32, x.shape, 0)
    return (jnp.where(rows == 0, 0.0, pltpu.roll(x, 1, 0)), jnp.where(rows == t - 1, 0.0, pltpu.roll(x, t - 1, 0)))


def _conv3(x, cw_ref, cb_ref):
    xp, xn = _time_shifts(x)
    return cw_ref[0:1, :] * xp + cw_ref[1:2, :] * x + cw_ref[2:3, :] * xn + cb_ref[...]


def _glu_specs(b, t, f):
    tc = _tile(f, FFN_COLS)
    seq = pl.BlockSpec((1, t, tc), lambda j, bi: (bi, 0, j))
    cw = pl.BlockSpec((3, tc), lambda j, bi: (0, j))
    cb = pl.BlockSpec((1, tc), lambda j, bi: (0, j))
    return tc, seq, cw, cb


def _glu_fwd_call(ug, uv, cwg, cwv, cbg, cbv):
    b, t, f = ug.shape
    tc, seq, cw, cb = _glu_specs(b, t, f)

    def body(ug_ref, uv_ref, cwg_ref, cwv_ref, cbg_ref, cbv_ref, o_ref):
        g = _conv3(ug_ref[0], cwg_ref, cbg_ref)
        o_ref[0] = g * jax.nn.sigmoid(g) * _conv3(uv_ref[0], cwv_ref, cbv_ref)

    return pl.pallas_call(
        body, name='glu_fwd', grid=(f // tc, b), in_specs=[seq, seq, cw, cw, cb, cb], out_specs=seq,
        out_shape=jax.ShapeDtypeStruct((b, t, f), F32), compiler_params=_params("parallel", "parallel"),
    )(ug, uv, cwg, cwv, cbg, cbv)


def _glu_bwd_call(ug, uv, cwg, cwv, cbg, cbv, dact):
    b, t, f = ug.shape
    tc, seq, cw, cb = _glu_specs(b, t, f)

    def body(ug_ref, uv_ref, cwg_ref, cwv_ref, cbg_ref, cbv_ref, da_ref,
             dug_ref, duv_ref, dcwg_ref, dcwv_ref, dcbg_ref, dcbv_ref):
        @pl.when(pl.program_id(1) == 0)
        def _():
            for ref in (dcwg_ref, dcwv_ref, dcbg_ref, dcbv_ref):
                ref[...] = jnp.zeros_like(ref)

        g = _conv3(ug_ref[0], cwg_ref, cbg_ref)
        v = _conv3(uv_ref[0], cwv_ref, cbv_ref)
        sig = jax.nn.sigmoid(g)
        da = da_ref[0]
        dv = da * (g * sig)
        dg = da * v * (sig * (1.0 + g * (1.0 - sig)))

        def conv_bwd(dc, x_ref, cw_ref, dx_ref, dcw_ref, dcb_ref):
            dcp, dcn = _time_shifts(dc)
            dx_ref[0] = cw_ref[0:1, :] * dcn + cw_ref[1:2, :] * dc + cw_ref[2:3, :] * dcp
            x = x_ref[0]
            xp, xn = _time_shifts(x)
            for n, xs in enumerate((xp, x, xn)):
                dcw_ref[n:n + 1, :] += jnp.sum(dc * xs, axis=0, keepdims=True)
            dcb_ref[...] += jnp.sum(dc, axis=0, keepdims=True)

        conv_bwd(dg, ug_ref, cwg_ref, dug_ref, dcwg_ref, dcbg_ref)
        conv_bwd(dv, uv_ref, cwv_ref, duv_ref, dcwv_ref, dcbv_ref)

    big = jax.ShapeDtypeStruct((b, t, f), F32)
    return pl.pallas_call(
        body, name='glu_bwd', grid=(f // tc, b), in_specs=[seq, seq, cw, cw, cb, cb, seq],
        out_specs=[seq, seq, cw, cw, cb, cb],
        out_shape=[big, big, jax.ShapeDtypeStruct((3, f), F32), jax.ShapeDtypeStruct((3, f), F32),
                   jax.ShapeDtypeStruct((1, f), F32), jax.ShapeDtypeStruct((1, f), F32)],
        compiler_params=_params("parallel", "arbitrary"),
    )(ug, uv, cwg, cwv, cbg, cbv, dact)


@jax.custom_vjp
def _conv_glu(ug, uv, cwg, cwv, cbg, cbv):
    return _glu_fwd_call(ug, uv, cwg, cwv, cbg, cbv)


def _conv_glu_fwd(*args):
    return _glu_fwd_call(*args), args


def _conv_glu_bwd(res, dact):
    return tuple(_glu_bwd_call(*res, dact))


_conv_glu.defvjp(_conv_glu_fwd, _conv_glu_bwd)


def _head_sum_call(x):
    m, d = x.shape
    tm = _tile(m)

    def body(x_ref, o_ref):
        ones = _seg_ones()
        for q in range(d // SEG):
            sl = slice(SEG * q, SEG * (q + 1))
            hi, lo = _split2(x_ref[:, sl])
            o_ref[:, sl] = (jnp.dot(hi, ones, preferred_element_type=F32) + jnp.dot(lo, ones, preferred_element_type=F32))

    return pl.pallas_call(
        body, name='head_sum', grid=(m // tm,), in_specs=[pl.BlockSpec((tm, d), lambda i: (i, 0))],
        out_specs=pl.BlockSpec((tm, d), lambda i: (i, 0)), out_shape=jax.ShapeDtypeStruct((m, d), F32),
        compiler_params=_params("parallel"),
    )(x)


@jax.custom_vjp
def _head_sum(x):
    return _head_sum_call(x)


_head_sum.defvjp(lambda x: (_head_sum_call(x), None), lambda _, g: (_head_sum_call(g),))


def _softmax_rows(q_ref, k_ref):
    s = lax.dot_general(q_ref[0].astype(BF16), k_ref[0].astype(BF16), (((1,), (1,)), ((), ())),
                        preferred_element_type=F32) * MLA_SCALE
    p = jnp.exp(s - jnp.max(s, axis=-1, keepdims=True))
    return p, jnp.sum(p, axis=-1, keepdims=True)


def _attn_fwd_call(q, k, v, name):
    bh, t, _ = q.shape
    tq = min(ATT_TQ, t)

    def body(q_ref, k_ref, v_ref, o_ref):
        p, l = _softmax_rows(q_ref, k_ref)
        o_ref[0] = jnp.dot(p.astype(BF16), v_ref[0].astype(BF16), preferred_element_type=F32) / l

    return pl.pallas_call(
        body, name=name, grid=(bh, t // tq),
        in_specs=[pl.BlockSpec((1, tq, D_QK), lambda b, i: (b, i, 0)), pl.BlockSpec((1, t, D_QK), lambda b, i: (b, 0, 0)),
                  pl.BlockSpec((1, t, D_V), lambda b, i: (b, 0, 0))],
        out_specs=pl.BlockSpec((1, tq, D_V), lambda b, i: (b, i, 0)),
        out_shape=jax.ShapeDtypeStruct((bh, t, D_V), F32),
        compiler_params=_params("parallel", "parallel"),
    )(q, k, v)


def _attn_bwd_call(q, k, v, o, do, name):
    bh, t, _ = q.shape
    tq = min(ATT_TQ, t)

    def body(q_ref, k_ref, v_ref, o_ref, do_ref, dq_ref, dk_ref, dv_ref):
        @pl.when(pl.program_id(1) == 0)
        def _():
            dk_ref[...] = jnp.zeros_like(dk_ref)
            dv_ref[...] = jnp.zeros_like(dv_ref)

        p, l = _softmax_rows(q_ref, k_ref)
        p = p / l
        dov = do_ref[0]
        do_b = dov.astype(BF16)
        delta = jnp.sum(dov * o_ref[0], axis=-1, keepdims=True)
        dp = lax.dot_general(do_b, v_ref[0].astype(BF16), (((1,), (1,)), ((), ())), preferred_element_type=F32)
        ds = (p * (dp - delta) * MLA_SCALE).astype(BF16)
        dq_ref[0] = jnp.dot(ds, k_ref[0].astype(BF16), preferred_element_type=F32)
        dk_ref[0] += lax.dot_general(ds, q_ref[0].astype(BF16), (((0,), (0,)), ((), ())), preferred_element_type=F32)
        dv_ref[0] += lax.dot_general(p.astype(BF16), do_b, (((0,), (0,)), ((), ())), preferred_element_type=F32)

    qspec = pl.BlockSpec((1, tq, D_QK), lambda b, i: (b, i, 0))
    kspec = pl.BlockSpec((1, t, D_QK), lambda b, i: (b, 0, 0))
    vspec = pl.BlockSpec((1, t, D_V), lambda b, i: (b, 0, 0))
    ospec = pl.BlockSpec((1, tq, D_V), lambda b, i: (b, i, 0))
    return pl.pallas_call(
        body, name=name, grid=(bh, t // tq),
        in_specs=[qspec, kspec, vspec, ospec, ospec], out_specs=[qspec, kspec, vspec],
        out_shape=[jax.ShapeDtypeStruct(q.shape, F32), jax.ShapeDtypeStruct(k.shape, F32), jax.ShapeDtypeStruct(v.shape, F32)],
        compiler_params=_params("parallel", "arbitrary"),
    )(q, k, v, o, do)


@jax.custom_vjp
def _attention(q, k, v):
    return _attn_fwd_call(q, k, v, 'attn_fwd')


def _attention_fwd(q, k, v):
    o = _attn_fwd_call(q, k, v, 'attn_fwd')
    return o, (q, k, v, o)


def _attention_bwd(res, do):
    q, k, v, o = res
    return tuple(_attn_bwd_call(q, k, v, o, do, 'attn_bwd'))


_attention.defvjp(_attention_fwd, _attention_bwd)


SROWS = N * D_RWKV // SEG


def _seg_ones():
    r = lax.broadcasted_iota(jnp.int32, (SEG, SEG), 0) >> 6
    c = lax.broadcasted_iota(jnp.int32, (SEG, SEG), 1) >> 6
    return (r == c).astype(BF16)


def _eye_mask():
    r = lax.broadcasted_iota(jnp.int32, (SROWS, SEG), 0) & (N - 1)
    c = lax.broadcasted_iota(jnp.int32, (SROWS, SEG), 1) & (N - 1)
    return r == c


def _row2(ref, bi, ti, dtype=F32):
    parts = [jnp.broadcast_to(ref[bi, pl.ds(ti, 1), pl.ds(SEG * q, SEG)].astype(dtype), (N, SEG))
             for q in range(D_RWKV // SEG)]
    return jnp.concatenate(parts, axis=0)


def _split2(x):
    hi = x.astype(BF16)
    return hi, (x - hi.astype(F32)).astype(BF16)


def _col_sum(x):
    return jnp.concatenate([jnp.sum(x[N * q:N * (q + 1)], axis=0, keepdims=True) for q in range(D_RWKV // SEG)], axis=1)


def _scan_specs(b, t, rev):
    nc = t // SCAN_CHUNK
    if rev:
        return (pl.BlockSpec((b, SCAN_CHUNK, D_RWKV), lambda c: (0, nc - 1 - c, 0)),
                pl.BlockSpec((b, SCAN_CHUNK, SROWS, SEG), lambda c: (0, nc - 1 - c, 0, 0)))
    return (pl.BlockSpec((b, SCAN_CHUNK, D_RWKV), lambda c: (0, c, 0)),
            pl.BlockSpec((b, SCAN_CHUNK, SROWS, SEG), lambda c: (0, c, 0, 0)))


def _scan_fwd_call(r, v, kk, wf, kf, qf, wb, kb, qb):
    b, t, _ = r.shape

    last = SCAN_CHUNK - 1

    def body(rf, vf, kkf, wf_, kf_, qf_, rb, vb, kkb, wb_, kb_, qb_, yf, yb, sf, sb, *states):
        @pl.when(pl.program_id(0) == 0)
        def _():
            for st in states:
                st[...] = jnp.zeros_like(st)

        ones, mask = _seg_ones(), _eye_mask()
        zero16 = jnp.zeros((), BF16)
        chains = []
        for bi in range(b):
            chains.append((rf, vf, kkf, wf_, kf_, qf_, yf, sf, states[2 * bi], bi, False))
            chains.append((rb, vb, kkb, wb_, kb_, qb_, yb, sb, states[2 * bi + 1], bi, True))

        def tix(i, rev):
            return last - i if rev else i

        def put_y(y_, bi, ti, ycol):
            y_[bi, pl.ds(ti, 1), :] = _col_sum(jnp.where(mask, ycol, 0.0))

        def steps(i, with_y):
            parts = []
            for (r_, v_, kk_, w_, k_, q_, y_, s_, st, bi, rev) in chains:
                ti = tix(i, rev)
                s = st[...]
                s_[bi, ti] = s
                parts += list(_split2(s * _row2(kk_, bi, ti)))
                parts.append(jnp.where(mask, _row2(v_, bi, ti, BF16), zero16))
                if with_y:
                    parts.append((s * _row2(r_, bi, tix(i - 1, rev))).astype(BF16))
            res = jnp.dot(jnp.concatenate(parts, axis=0), ones, preferred_element_type=F32)
            off = 0
            for (r_, v_, kk_, w_, k_, q_, y_, s_, st, bi, rev) in chains:
                ti = tix(i, rev)
                u = res[off:off + SROWS] + res[off + SROWS:off + 2 * SROWS]
                vcol = res[off + 2 * SROWS:off + 3 * SROWS]
                off += 3 * SROWS
                if with_y:
                    put_y(y_, bi, tix(i - 1, rev), res[off:off + SROWS])
                    off += SROWS
                st[...] = st[...] * _row2(w_, bi, ti) - u * _row2(q_, bi, ti) + vcol * _row2(k_, bi, ti)

        steps(0, False)

        def loop(i, carry):
            steps(i, True)
            return carry

        lax.fori_loop(1, SCAN_CHUNK, loop, 0)
        parts = [(c[8][...] * _row2(c[0], c[9], tix(last, c[10]))).astype(BF16) for c in chains]
        res = jnp.dot(jnp.concatenate(parts, axis=0), ones, preferred_element_type=F32)
        for n, c in enumerate(chains):
            put_y(c[6], c[9], tix(last, c[10]), res[n * SROWS:(n + 1) * SROWS])

    fr, fs = _scan_specs(b, t, False)
    br, bs = _scan_specs(b, t, True)
    y_shape = jax.ShapeDtypeStruct((b, t, D_RWKV), F32)
    s_shape = jax.ShapeDtypeStruct((b, t, SROWS, SEG), F32)
    return pl.pallas_call(
        body, name='scan_fwd', grid=(t // SCAN_CHUNK,),
        in_specs=[fr] * 6 + [br] * 6, out_specs=[fr, br, fs, bs], out_shape=[y_shape, y_shape, s_shape, s_shape],
        scratch_shapes=[pltpu.VMEM((SROWS, SEG), F32)] * (2 * b),
        compiler_params=_params("arbitrary"),
    )(r, v, kk, wf, kf, qf, r, v, kk, wb, kb, qb)


def _scan_bwd_call(r, v, kk, wf, kf, qf, wb, kb, qb, sf, sb, dyf, dyb):
    b, t, _ = r.shape

    last = SCAN_CHUNK - 1

    def body(rf, vf, kkf, wf_, kf_, qf_, sf_, dyf_, rb, vb, kkb, wb_, kb_, qb_, sb_, dyb_,
             drf, dvf, dkkf, dwf, dkf, dqf, drb, dvb, dkkb, dwb, dkb, dqb, *scratch):
        @pl.when(pl.program_id(0) == 0)
        def _():
            for n in range(2 * b):
                scratch[4 * n][...] = jnp.zeros_like(scratch[4 * n])

        ones, mask = _seg_ones(), _eye_mask()
        zero16 = jnp.zeros((), BF16)
        chains = []
        for bi in range(b):
            chains.append((rf, vf, kkf, wf_, kf_, qf_, sf_, dyf_, (drf, dvf, dkkf, dwf, dkf, dqf),
                           scratch[8 * bi:8 * bi + 4], bi, True))
            chains.append((rb, vb, kkb, wb_, kb_, qb_, sb_, dyb_, (drb, dvb, dkkb, dwb, dkb, dqb),
                           scratch[8 * bi + 4:8 * bi + 8], bi, False))

        def tix(i, rev):
            return last - i if rev else i

        def state_free_parts(v_, dy_, kk_, s_, bi, ti):
            return [jnp.where(mask, _row2(v_, bi, ti, BF16), zero16), jnp.where(mask, _row2(dy_, bi, ti, BF16), zero16),
                    (s_[bi, ti] * _row2(kk_, bi, ti)).astype(BF16)]

        def keep(scr, res, off):
            for n in range(3):
                scr[1 + n][...] = res[off + n * SROWS:off + (n + 1) * SROWS]
            return off + 3 * SROWS

        def first():
            parts = []
            for (r_, v_, kk_, w_, k_, q_, s_, dy_, outs, scr, bi, rev) in chains:
                parts += state_free_parts(v_, dy_, kk_, s_, bi, tix(0, rev))
            res = jnp.dot(jnp.concatenate(parts, axis=0), ones, preferred_element_type=F32)
            off = 0
            for c in chains:
                off = keep(c[9], res, off)

        def steps(i, has_next, recompute):
            parts = []
            for (r_, v_, kk_, w_, k_, q_, s_, dy_, outs, scr, bi, rev) in chains:
                ti = tix(i, rev)
                gst, vc, dc, uc = scr
                dycol = dc[...]
                if recompute:
                    sc = s_[bi, ti] * _row2(w_, bi, ti) - uc[...] * _row2(q_, bi, ti) + vc[...] * _row2(k_, bi, ti)
                else:
                    sc = s_[bi, tix(i - 1, rev)]
                outs[0][bi, pl.ds(ti, 1), :] = _col_sum(sc * dycol)
                g = gst[...] + dycol * _row2(r_, bi, ti)
                gst[...] = g
                parts += list(_split2(g * _row2(q_, bi, ti)))
                parts.append((g * _row2(k_, bi, ti)).astype(BF16))
                if has_next:
                    parts += state_free_parts(v_, dy_, kk_, s_, bi, tix(i + 1, rev))
            res = jnp.dot(jnp.concatenate(parts, axis=0), ones, preferred_element_type=F32)
            off = 0
            for (r_, v_, kk_, w_, k_, q_, s_, dy_, outs, scr, bi, rev) in chains:
                ti = tix(i, rev)
                gst, vc, dc, uc = scr
                dr_, dv_, dkk_, dw_, dk_, dq_ = outs

                def put(ref, val, sign=1.0):
                    ref[bi, pl.ds(ti, 1), :] = sign * _col_sum(val)

                gq = res[off:off + SROWS] + res[off + SROWS:off + 2 * SROWS]
                put(dv_, jnp.where(mask, res[off + 2 * SROWS:off + 3 * SROWS], 0.0))
                off += 3 * SROWS
                g, sp = gst[...], s_[bi, ti]
                put(dk_, g * vc[...])
                put(dw_, g * sp)
                put(dq_, g * uc[...], -1.0)
                put(dkk_, sp * gq, -1.0)
                gst[...] = g * _row2(w_, bi, ti) - gq * _row2(kk_, bi, ti)
                if has_next:
                    off = keep(scr, res, off)

        first()
        steps(0, True, True)

        def loop(i, carry):
            steps(i, True, False)
            return carry

        lax.fori_loop(1, last, loop, 0)
        steps(last, False, False)

    fr, fs = _scan_specs(b, t, True)
    br, bs = _scan_specs(b, t, False)
    y_shape = jax.ShapeDtypeStruct((b, t, D_RWKV), F32)
    return pl.pallas_call(
        body, name='scan_bwd', grid=(t // SCAN_CHUNK,),
        in_specs=[fr] * 6 + [fs, fr] + [br] * 6 + [bs, br],
        out_specs=[fr] * 6 + [br] * 6, out_shape=[y_shape] * 12,
        scratch_shapes=[pltpu.VMEM((SROWS, SEG), F32)] * (8 * b),
        compiler_params=_params("arbitrary"),
    )(r, v, kk, wf, kf, qf, sf, dyf, r, v, kk, wb, kb, qb, sb, dyb)


@jax.custom_vjp
def _wkv_scan(r, v, kk, wf, kf, qf, wb, kb, qb):
    yf, yb, _, _ = _scan_fwd_call(r, v, kk, wf, kf, qf, wb, kb, qb)
    return yf, yb


def _wkv_scan_fwd(r, v, kk, wf, kf, qf, wb, kb, qb):
    yf, yb, sf, sb = _scan_fwd_call(r, v, kk, wf, kf, qf, wb, kb, qb)
    return (yf, yb), (r, v, kk, wf, kf, qf, wb, kb, qb, sf, sb)


def _wkv_scan_bwd(res, dy):
    r, v, kk, wf, kf, qf, wb, kb, qb, sf, sb = res
    (drf, dvf, dkkf, dwf, dkf, dqf, drb, dvb, dkkb, dwb, dkb, dqb) = _scan_bwd_call(
        r, v, kk, wf, kf, qf, wb, kb, qb, sf, sb, dy[0], dy[1])
    return drf + drb, dvf + dvb, dkkf + dkkb, dwf, dkf, dqf, dwb, dkb, dqb


_wkv_scan.defvjp(_wkv_scan_fwd, _wkv_scan_bwd)


def _shift(z, left):
    if left:
        return jnp.pad(z[:, :-1], ((0, 0), (1, 0), (0, 0)))
    return jnp.pad(z[:, 1:], ((0, 0), (0, 1), (0, 0)))


def _rope_tables(t):
    inv_freq = jnp.power(ROPE_THETA, -jnp.arange(0, D_ROPE, 2, dtype=F32) / D_ROPE)
    ang = jnp.arange(t, dtype=F32)[:, None] * inv_freq[None, :]
    ang = jnp.concatenate([ang, ang], axis=-1)
    return jnp.cos(ang), jnp.sin(ang)


def _rope(x, cos, sin):
    x1, x2 = jnp.split(x, 2, axis=-1)
    return x * cos + jnp.concatenate([-x2, x1], axis=-1) * sin


def _rwkv_mixer(z, full, rep):
    b, t, _ = z.shape
    m = b * t
    z = z + rep['shift_mu_prev'] * (_shift(z, True) - z) + rep['shift_mu_next'] * (_shift(z, False) - z)
    z = z.reshape(m, RWKV_COLS)
    r, k, v = z[:, :512], z[:, 512:1024], z[:, 1024:1536]
    wdf, wdb, adf, adb, gd = z[:, 1536:1600], z[:, 1600:1664], z[:, 1664:1728], z[:, 1728:1792], z[:, 1792:1920]

    def decay(wd, w0, w2, name):
        logit = w0 + _make_mm(name)(jnp.tanh(wd), w2)
        return jnp.exp(-jnp.exp(-jax.nn.softplus(-logit) - 0.5))

    w_f = decay(wdf, rep['decay_w0_fwd'], full['decay_w2_fwd'], 'mm_decay_f')
    w_b = decay(wdb, rep['decay_w0_bwd'], full['decay_w2_bwd'], 'mm_decay_b')
    a_f = jax.nn.sigmoid(rep['iclr_a0_fwd'] + _make_mm('mm_iclr_f')(adf, full['iclr_a2_fwd']))
    a_b = jax.nn.sigmoid(rep['iclr_a0_bwd'] + _make_mm('mm_iclr_b')(adb, full['iclr_a2_bwd']))
    g = _make_mm('mm_gate')(jax.nn.sigmoid(gd), full['gate_g2'])
    kk = k * rep['k_k']
    kk = kk / jnp.maximum(jnp.sqrt(_head_sum(kk * kk)), L2_EPS)
    k_f = k * (1.0 + (a_f - 1.0) * rep['k_a'])
    k_b = k * (1.0 + (a_b - 1.0) * rep['k_a'])
    seq = lambda a: a.reshape(b, t, D_RWKV)
    y_f, y_b = _wkv_scan(seq(r), seq(v), seq(kk), seq(w_f), seq(k_f), seq(kk * a_f), seq(w_b), seq(k_b), seq(kk * a_b))
    y = (y_f + y_b).reshape(m, D_RWKV)
    yc = y - _head_sum(y) * (1.0 / N)
    var = _head_sum(yc * yc) * (1.0 / N)
    y = yc * lax.rsqrt(var + GN_EPS) * rep['ln_x_g'] + rep['ln_x_b']
    bonus = _head_sum(r * (k_f + k_b) * rep['r_k'].reshape(1, D_RWKV)) * v
    return (y + bonus) * g


def _mla_mixer(z, full, rep, b, t):
    m = b * t
    c_q, c_kv, k_rope = z[:, :768], z[:, 768:1024], z[:, 1024:1056]
    cos, sin = _rope_tables(t)
    q = _make_mm('mm_uq')(_make_rms('rms_q')(c_q, rep['q_norm_g']), full['w_uq']).reshape(b, t, H, D_QK)
    q = jnp.concatenate([q[..., :D_NOPE], _rope(q[..., D_NOPE:], cos[:, None, :], sin[:, None, :])], axis=-1)
    kv = _make_mm('mm_ukv')(_make_rms('rms_kv')(c_kv, rep['kv_norm_g']), full['w_ukv']).reshape(b, t, H, D_NOPE + D_V)
    k_rope = _rope(k_rope.reshape(b, t, D_ROPE), cos, sin)
    k = jnp.concatenate([kv[..., :D_NOPE], jnp.broadcast_to(k_rope[:, :, None, :], (b, t, H, D_ROPE))], axis=-1)
    heads = lambda a: a.transpose(0, 2, 1, 3).reshape(b * H, t, a.shape[-1])
    o = _attention(heads(q), heads(k), heads(kv[..., D_NOPE:]))
    o = o.reshape(b, H, t, D_V).transpose(0, 2, 1, 3).reshape(m, H * D_V)
    return _make_rms('rms_mla_out')(o, rep['mla_out_g'])


def _local_loss(full, rep, x, target):
    b, t, d = x.shape
    m = b * t
    xf = x.reshape(m, d)
    n1 = _make_rms('rms_mix')(xf, rep['ln_mix_g'])
    d_in = full['w_in'].shape[1]
    d_in_pad = -(-d_in // MM_TILE) * MM_TILE
    z = _make_mm('mm_in')(n1, jnp.pad(full['w_in'], ((0, 0), (0, d_in_pad - d_in))))
    y_rwkv = _rwkv_mixer(z[:, :RWKV_COLS].reshape(b, t, RWKV_COLS), full, rep)
    y_mla = _mla_mixer(z[:, RWKV_COLS:d_in], full, rep, b, t)
    h = xf + _make_mm('mm_out')(jnp.concatenate([y_rwkv, y_mla], axis=-1), full['w_out'])
    n2 = _make_rms('rms_ffn')(h, rep['ln_ffn_g'])
    w_up, cw, cb = full['w_ffn_up'], full['ffn_conv_w'], rep['ffn_conv_b']
    u_gate = _make_mm('mm_up_gate')(n2, w_up[:, :D_FF]).reshape(b, t, D_FF)
    u_val = _make_mm('mm_up_val')(n2, w_up[:, D_FF:]).reshape(b, t, D_FF)
    act = _conv_glu(u_gate, u_val, cw[:, :D_FF], cw[:, D_FF:], cb[:, :D_FF], cb[:, D_FF:]).reshape(m, D_FF)
    h = h + _make_mm('mm_down')(act, full['w_ffn_down'])
    out = _make_rms('rms_final')(h, rep['ln_final_g'])
    err = jnp.square(out - target.reshape(m, d))
    return 0.5 * jnp.sum(jnp.mean(err, axis=-1))


def _mat(a):
    if a.ndim == 1:
        return a.reshape(1, -1)
    if a.ndim == 3:
        return a.reshape(a.shape[1:])
    return a


def _join(shards, name):
    if name in ROW:
        return shards.reshape(-1, shards.shape[-1])
    return shards.transpose(1, 0, 2).reshape(shards.shape[1], -1)


def _cut(whole, name):
    r, c = whole.shape
    if name in ROW:
        return whole.reshape(N_DEV, r // N_DEV, c)
    return whole.reshape(r, N_DEV, c // N_DEV).transpose(1, 0, 2)


def kernel(x, ln_mix_g, w_in, shift_mu_prev, shift_mu_next, decay_w0_fwd, decay_w2_fwd, decay_w0_bwd, decay_w2_bwd, iclr_a0_fwd, iclr_a2_fwd, iclr_a0_bwd, iclr_a2_bwd, gate_g2, k_k, k_a, r_k, ln_x_g, ln_x_b, q_norm_g, w_uq, kv_norm_g, w_ukv, mla_out_g, w_out, ln_ffn_g, w_ffn_up, ffn_conv_w, ffn_conv_b, w_ffn_down, ln_final_g, loss_target, m_ln_mix_g, m_w_in, m_shift_mu_prev, m_shift_mu_next, m_decay_w0_fwd, m_decay_w2_fwd, m_decay_w0_bwd, m_decay_w2_bwd, m_iclr_a0_fwd, m_iclr_a2_fwd, m_iclr_a0_bwd, m_iclr_a2_bwd, m_gate_g2, m_k_k, m_k_a, m_r_k, m_ln_x_g, m_ln_x_b, m_q_norm_g, m_w_uq, m_kv_norm_g, m_w_ukv, m_mla_out_g, m_w_out, m_ln_ffn_g, m_w_ffn_up, m_ffn_conv_w, m_ffn_conv_b, m_w_ffn_down, m_ln_final_g, v_ln_mix_g, v_w_in, v_shift_mu_prev, v_shift_mu_next, v_decay_w0_fwd, v_decay_w2_fwd, v_decay_w0_bwd, v_decay_w2_bwd, v_iclr_a0_fwd, v_iclr_a2_fwd, v_iclr_a0_bwd, v_iclr_a2_bwd, v_gate_g2, v_k_k, v_k_a, v_r_k, v_ln_x_g, v_ln_x_b, v_q_norm_g, v_w_uq, v_kv_norm_g, v_w_ukv, v_mla_out_g, v_w_out, v_ln_ffn_g, v_w_ffn_up, v_ffn_conv_w, v_ffn_conv_b, v_w_ffn_down, v_ln_final_g):
    given = dict(locals())
    w = {n: given[n] for n in WNAMES}
    mom = {n: given['m_' + n] for n in WNAMES}
    var = {n: given['v_' + n] for n in WNAMES}

    shard_shapes = [_mat(w[n]).shape for n in SHARDED]
    w_pack = _pack([w[n] for n in SHARDED])
    gathered = _unpack(_all_gather(w_pack, 'gather_weights'), shard_shapes, lead=1)
    full = {n: _join(s, n) for n, s in zip(SHARDED, gathered)}
    rep = {n: _mat(w[n]) for n in REPLICATED}
    rep['r_k'] = w['r_k'].reshape(H, N)

    loss_local, (g_full, g_rep, g_x) = jax.value_and_grad(_local_loss, argnums=(0, 1, 2))(full, rep, x, loss_target)

    g_pack = _pack([_cut(g_full[n], n) for n in SHARDED], lead=1)
    parts = _grad_exchange(g_pack, 'exchange_grads')
    s_out = _sum_adamw(parts, w_pack, _pack([mom[n] for n in SHARDED]), _pack([var[n] for n in SHARDED]), 'adamw_sharded')
    s_out = [_unpack(o, [w[n].shape for n in SHARDED]) for o in s_out]

    zero = jnp.zeros((1,), F32)
    r_pack = _pack([g_rep[n] for n in REPLICATED] + [loss_local.reshape(1)])
    r_parts = _all_gather(r_pack, 'gather_small')
    r_out = _sum_adamw(r_parts, _pack([w[n] for n in REPLICATED] + [zero]), _pack([mom[n] for n in REPLICATED] + [zero]),
                       _pack([var[n] for n in REPLICATED] + [zero]), 'adamw_replicated')
    r_out = [_unpack(o, [w[n].shape for n in REPLICATED] + [(1,)]) for o in r_out]

    loss = r_out[0][-1].reshape(())
    outs = [loss, g_x]
    for kind in range(4):
        by_name = dict(zip(SHARDED, s_out[kind]))
        by_name.update(zip(REPLICATED, r_out[kind][:-1]))
        outs += [by_name[n] for n in WNAMES]
    return tuple(outs)
```

```python
import functools

import jax
import jax.numpy as jnp
from jax import lax
from jax.experimental import pallas as pl
from jax.experimental.pallas import tpu as pltpu

F32 = jnp.float32
BF16 = jnp.bfloat16
MESH = pl.DeviceIdType.MESH

N_DEV = 8
LANES = 128
SUBLANES = 8
PACK_TILE = 2 * SUBLANES * LANES
PACK_ROWS = 512
MM_TILE = 512
MM_TILE_WIDE = 1408
MM_K_WHOLE = 2816
VMEM_LIMIT = 48 * 1024 * 1024

H = 8
N = 64
D_RWKV = H * N
D_NOPE, D_ROPE, D_V = 64, 32, 64
D_QK = D_NOPE + D_ROPE
MLA_SCALE = D_QK ** -0.5
ROPE_THETA = 10000.0
RWKV_COLS = 1920
MLA_COLS = 1056
D_FF = 2816
NORM_EPS = 1e-6
GN_EPS = 64e-5
L2_EPS = 1e-12
ADAM_LR, ADAM_B1, ADAM_B2, ADAM_EPS, ADAM_WD, ADAM_STEP = 0.001, 0.9, 0.999, 1e-08, 0.01, 10

SCAN_CHUNK = 16
ATT_TQ = 256
SEG = 256
FFN_COLS = 256
ROW_TILE = 256

WNAMES = ['ln_mix_g', 'w_in', 'shift_mu_prev', 'shift_mu_next', 'decay_w0_fwd', 'decay_w2_fwd', 'decay_w0_bwd',
          'decay_w2_bwd', 'iclr_a0_fwd', 'iclr_a2_fwd', 'iclr_a0_bwd', 'iclr_a2_bwd', 'gate_g2', 'k_k', 'k_a', 'r_k',
          'ln_x_g', 'ln_x_b', 'q_norm_g', 'w_uq', 'kv_norm_g', 'w_ukv', 'mla_out_g', 'w_out', 'ln_ffn_g', 'w_ffn_up',
          'ffn_conv_w', 'ffn_conv_b', 'w_ffn_down', 'ln_final_g']
COL = ('w_in', 'decay_w2_fwd', 'decay_w2_bwd', 'iclr_a2_fwd', 'iclr_a2_bwd', 'gate_g2', 'w_ukv', 'w_ffn_up', 'ffn_conv_w')
ROW = ('w_uq', 'w_out', 'w_ffn_down')
SHARDED = [n for n in WNAMES if n in COL or n in ROW]
REPLICATED = [n for n in WNAMES if n not in SHARDED]
EXACT = ('ffn_conv_w',)


def _params(*sem):
    return pltpu.CompilerParams(dimension_semantics=sem, vmem_limit_bytes=VMEM_LIMIT)


def _pack(arrs, lead=0):
    parts = []
    for a in arrs:
        head = a.shape[:lead]
        flat = a.reshape(head + (-1,))
        n = flat.shape[-1]
        n_pad = -(-n // PACK_TILE) * PACK_TILE
        flat = jnp.pad(flat, [(0, 0)] * lead + [(0, n_pad - n)])
        parts.append(flat.reshape(head + (n_pad // LANES, LANES)))
    out = jnp.concatenate(parts, axis=lead)
    rows = out.shape[lead]
    rows_pad = -(-rows // PACK_ROWS) * PACK_ROWS
    return jnp.pad(out, [(0, 0)] * lead + [(0, rows_pad - rows), (0, 0)])


def _unpack(packed, shapes, lead=0):
    outs, row = [], 0
    head = packed.shape[:lead]
    for shp in shapes:
        n = 1
        for s in shp:
            n *= s
        rows = -(-n // PACK_TILE) * (PACK_TILE // LANES)
        blk = lax.slice_in_dim(packed, row, row + rows, axis=lead)
        flat = blk.reshape(head + (rows * LANES,))
        outs.append(lax.slice_in_dim(flat, 0, n, axis=lead).reshape(head + tuple(shp)))
        row += rows
    return outs


def _all_gather(x, name):
    rows = x.shape[0]

    def body(x_ref, out_ref, send_sems, recv_sems, local_sem):
        mx, my, mc = lax.axis_index("x"), lax.axis_index("y"), lax.axis_index("c")
        me, sibling = (mx, my, mc), (mx, my, 1 - mc)
        chips = [(1 - mx, my), (mx, 1 - my), (1 - mx, 1 - my)]

        def slot(px, py, pc):
            return out_ref.at[4 * px + 2 * py + pc]

        def copy(k, block, to, src=None):
            return pltpu.make_async_remote_copy(
                src_ref=slot(*block) if src is None else src, dst_ref=slot(*block),
                send_sem=send_sems.at[k], recv_sem=recv_sems.at[k], device_id=to, device_id_type=MESH)

        mine = pltpu.make_async_copy(x_ref, slot(*me), local_sem)
        mine.start()
        first = [copy(0, me, sibling, src=x_ref)]
        first += [copy(1 + j, me, (*chip, mc), src=x_ref) for j, chip in enumerate(chips)]
        for cp in first:
            cp.start()
        passed = [copy(4 + j, (*chip, mc), sibling) for j, chip in enumerate(chips)]
        for j, chip in enumerate(chips):
            copy(1 + j, (*chip, mc), me).wait_recv()
            passed[j].start()
        copy(0, sibling, me).wait_recv()
        for j, chip in enumerate(chips):
            copy(4 + j, (*chip, 1 - mc), me).wait_recv()
        for cp in first + passed:
            cp.wait_send()
        mine.wait()

    return pl.pallas_call(
        body, name=name,
        out_shape=jax.ShapeDtypeStruct((N_DEV, rows, LANES), x.dtype),
        in_specs=[pl.BlockSpec(memory_space=pl.ANY)],
        out_specs=pl.BlockSpec(memory_space=pl.ANY),
        scratch_shapes=[pltpu.SemaphoreType.DMA((7,)), pltpu.SemaphoreType.DMA((7,)), pltpu.SemaphoreType.DMA(())],
    )(x)


def _grad_exchange(g, name):
    rows = g.shape[1]

    def body(g_ref, out_ref, send_sems, recv_sems, local_sem):
        mx, my, mc = lax.axis_index("x"), lax.axis_index("y"), lax.axis_index("c")
        me = 4 * mx + 2 * my + mc

        def flip(v, bit):
            return 1 - v if bit else v

        mine = pltpu.make_async_copy(g_ref.at[me], out_ref.at[me], local_sem)
        mine.start()
        copies = []
        for k in range(1, N_DEV):
            px, py, pc = flip(mx, k & 4), flip(my, k & 2), flip(mc, k & 1)
            peer = 4 * px + 2 * py + pc
            copies.append(pltpu.make_async_remote_copy(
                src_ref=g_ref.at[peer], dst_ref=out_ref.at[me],
                send_sem=send_sems.at[k - 1], recv_sem=recv_sems.at[k - 1],
                device_id=(px, py, pc), device_id_type=MESH))
        for cp in copies:
            cp.start()
        for cp in copies:
            cp.wait_recv()
        for cp in copies:
            cp.wait_send()
        mine.wait()

    return pl.pallas_call(
        body, name=name,
        out_shape=jax.ShapeDtypeStruct((N_DEV, rows, LANES), g.dtype),
        in_specs=[pl.BlockSpec(memory_space=pl.ANY)],
        out_specs=pl.BlockSpec(memory_space=pl.ANY),
        scratch_shapes=[pltpu.SemaphoreType.DMA((7,)), pltpu.SemaphoreType.DMA((7,)), pltpu.SemaphoreType.DMA(())],
    )(g)


def _sum_adamw(parts, w, m, v, name):
    rows = w.shape[0]
    c1 = 1.0 - ADAM_B1 ** ADAM_STEP
    c2 = 1.0 - ADAM_B2 ** ADAM_STEP

    def body(p_ref, w_ref, m_ref, v_ref, g_out, d_out, m_out, v_out):
        g = p_ref[0].astype(F32)
        for q in range(1, N_DEV):
            g = g + p_ref[q].astype(F32)
        m_new = ADAM_B1 * m_ref[...] + (1.0 - ADAM_B1) * g
        v_new = ADAM_B2 * v_ref[...] + (1.0 - ADAM_B2) * (g * g)
        m_hat = m_new / c1
        v_hat = v_new / c2
        g_out[...] = g
        d_out[...] = -ADAM_LR * (m_hat / (jnp.sqrt(v_hat) + ADAM_EPS) + ADAM_WD * w_ref[...])
        m_out[...] = m_new
        v_out[...] = v_new

    blk = pl.BlockSpec((PACK_ROWS, LANES), lambda i: (i, 0))
    out = jax.ShapeDtypeStruct((rows, LANES), F32)
    return pl.pallas_call(
        body, name=name, grid=(rows // PACK_ROWS,),
        in_specs=[pl.BlockSpec((N_DEV, PACK_ROWS, LANES), lambda i: (0, i, 0)), blk, blk, blk],
        out_specs=[blk, blk, blk, blk], out_shape=[out, out, out, out],
        compiler_params=_params("parallel"),
    )(parts, w, m, v)


def _tile(dim, cap=MM_TILE):
    if dim <= cap:
        return dim
    for t in range(cap, LANES - 1, -LANES):
        if dim % t == 0:
            return t
    return dim


def _mm_call(a, b, form, name):
    if form == 'nn':
        (m, k), n = a.shape, b.shape[1]
    elif form == 'nt':
        (m, k), n = a.shape, b.shape[0]
    else:
        (k, m), n = a.shape, b.shape[1]
    tk = k if (form == 'nn' and k <= MM_K_WHOLE) else _tile(k, MM_TILE_WIDE)
    tm = _tile(m, MM_TILE_WIDE if form == 'tn' else MM_TILE)
    tn = _tile(n, MM_TILE_WIDE)
    nk = k // tk
    contract = {'nn': ((1,), (0,)), 'nt': ((1,), (1,)), 'tn': ((0,), (0,))}[form]

    def body(a_ref, b_ref, o_ref):
        part = lax.dot_general(a_ref[...].astype(BF16), b_ref[...].astype(BF16), (contract, ((), ())),
                               preferred_element_type=F32)
        if nk == 1:
            o_ref[...] = part
        else:
            @pl.when(pl.program_id(2) == 0)
            def _():
                o_ref[...] = part

            @pl.when(pl.program_id(2) > 0)
            def _():
                o_ref[...] += part

    a_spec = pl.BlockSpec((tk, tm), lambda j, i, l: (l, i)) if form == 'tn' else pl.BlockSpec((tm, tk), lambda j, i, l: (i, l))
    b_spec = pl.BlockSpec((tn, tk), lambda j, i, l: (j, l)) if form == 'nt' else pl.BlockSpec((tk, tn), lambda j, i, l: (l, j))
    return pl.pallas_call(
        body, name=name, grid=(n // tn, m // tm, nk),
        in_specs=[a_spec, b_spec], out_specs=pl.BlockSpec((tm, tn), lambda j, i, l: (i, j)),
        out_shape=jax.ShapeDtypeStruct((m, n), F32),
        compiler_params=_params("parallel", "parallel", "arbitrary"),
    )(a, b)


def _make_mm(name):
    @jax.custom_vjp
    def mm(a, b):
        return _mm_call(a, b.astype(BF16), 'nn', name + '_fwd')

    def fwd(a, b):
        b16 = b.astype(BF16)
        return _mm_call(a, b16, 'nn', name + '_fwd'), (a, b16)

    def bwd(res, g):
        a, b16 = res
        return _mm_call(g, b16, 'nt', name + '_da'), _mm_call(a, g, 'tn', name + '_db')

    mm.defvjp(fwd, bwd)
    return mm


def _rms_fwd_call(x, g, name):
    m, d = x.shape
    tm = _tile(m)

    def body(x_ref, g_ref, o_ref):
        xv = x_ref[...]
        rinv = lax.rsqrt(jnp.mean(xv * xv, axis=-1, keepdims=True) + NORM_EPS)
        o_ref[...] = xv * rinv * g_ref[...]

    return pl.pallas_call(
        body, name=name, grid=(m // tm,),
        in_specs=[pl.BlockSpec((tm, d), lambda i: (i, 0)), pl.BlockSpec((1, d), lambda i: (0, 0))],
        out_specs=pl.BlockSpec((tm, d), lambda i: (i, 0)), out_shape=jax.ShapeDtypeStruct((m, d), F32),
        compiler_params=_params("parallel"),
    )(x, g)


def _rms_bwd_call(x, g, dy, name):
    m, d = x.shape
    tm = _tile(m)

    def body(x_ref, g_ref, dy_ref, dx_ref, dg_ref):
        @pl.when(pl.program_id(0) == 0)
        def _():
            dg_ref[...] = jnp.zeros_like(dg_ref)

        xv, dyv = x_ref[...], dy_ref[...]
        rinv = lax.rsqrt(jnp.mean(xv * xv, axis=-1, keepdims=True) + NORM_EPS)
        xh = xv * rinv
        dg_ref[...] += jnp.sum(dyv * xh, axis=0, keepdims=True)
        dxh = dyv * g_ref[...]
        dx_ref[...] = rinv * (dxh - xh * jnp.mean(dxh * xh, axis=-1, keepdims=True))

    return pl.pallas_call(
        body, name=name, grid=(m // tm,),
        in_specs=[pl.BlockSpec((tm, d), lambda i: (i, 0)), pl.BlockSpec((1, d), lambda i: (0, 0)),
                  pl.BlockSpec((tm, d), lambda i: (i, 0))],
        out_specs=[pl.BlockSpec((tm, d), lambda i: (i, 0)), pl.BlockSpec((1, d), lambda i: (0, 0))],
        out_shape=[jax.ShapeDtypeStruct((m, d), F32), jax.ShapeDtypeStruct((1, d), F32)],
        compiler_params=_params("arbitrary"),
    )(x, g, dy)


def _make_rms(name):
    @jax.custom_vjp
    def rms(x, g):
        return _rms_fwd_call(x, g, name + '_fwd')

    def fwd(x, g):
        return _rms_fwd_call(x, g, name + '_fwd'), (x, g)

    def bwd(res, dy):
        x, g = res
        dx, dg = _rms_bwd_call(x, g, dy, name + '_bwd')
        return dx, dg

    rms.defvjp(fwd, bwd)
    return rms


def _time_shifts(x):
    t = x.shape[0]
    rows = lax.broadcasted_iota(jnp.int32, x.shape, 0)
    return (jnp.where(rows == 0, 0.0, pltpu.roll(x, 1, 0)), jnp.where(rows == t - 1, 0.0, pltpu.roll(x, t - 1, 0)))


def _conv3(x, cw_ref, cb_ref):
    xp, xn = _time_shifts(x)
    return cw_ref[0:1, :] * xp + cw_ref[1:2, :] * x + cw_ref[2:3, :] * xn + cb_ref[...]


def _glu_specs(b, t, f):
    tc = _tile(f, FFN_COLS)
    seq = pl.BlockSpec((1, t, tc), lambda j, bi: (bi, 0, j))
    cw = pl.BlockSpec((3, tc), lambda j, bi: (0, j))
    cb = pl.BlockSpec((1, tc), lambda j, bi: (0, j))
    return tc, seq, cw, cb


def _glu_fwd_call(ug, uv, cwg, cwv, cbg, cbv):
    b, t, f = ug.shape
    tc, seq, cw, cb = _glu_specs(b, t, f)

    def body(ug_ref, uv_ref, cwg_ref, cwv_ref, cbg_ref, cbv_ref, o_ref):
        g = _conv3(ug_ref[0], cwg_ref, cbg_ref)
        o_ref[0] = g * jax.nn.sigmoid(g) * _conv3(uv_ref[0], cwv_ref, cbv_ref)

    return pl.pallas_call(
        body, name='glu_fwd', grid=(f // tc, b), in_specs=[seq, seq, cw, cw, cb, cb], out_specs=seq,
        out_shape=jax.ShapeDtypeStruct((b, t, f), F32), compiler_params=_params("parallel", "parallel"),
    )(ug, uv, cwg, cwv, cbg, cbv)


def _glu_bwd_call(ug, uv, cwg, cwv, cbg, cbv, dact):
    b, t, f = ug.shape
    tc, seq, cw, cb = _glu_specs(b, t, f)

    def body(ug_ref, uv_ref, cwg_ref, cwv_ref, cbg_ref, cbv_ref, da_ref,
             dug_ref, duv_ref, dcwg_ref, dcwv_ref, dcbg_ref, dcbv_ref):
        @pl.when(pl.program_id(1) == 0)
        def _():
            for ref in (dcwg_ref, dcwv_ref, dcbg_ref, dcbv_ref):
                ref[...] = jnp.zeros_like(ref)

        g = _conv3(ug_ref[0], cwg_ref, cbg_ref)
        v = _conv3(uv_ref[0], cwv_ref, cbv_ref)
        sig = jax.nn.sigmoid(g)
        da = da_ref[0]
        dv = da * (g * sig)
        dg = da * v * (sig * (1.0 + g * (1.0 - sig)))

        def conv_bwd(dc, x_ref, cw_ref, dx_ref, dcw_ref, dcb_ref):
            dcp, dcn = _time_shifts(dc)
            dx_ref[0] = cw_ref[0:1, :] * dcn + cw_ref[1:2, :] * dc + cw_ref[2:3, :] * dcp
            x = x_ref[0]
            xp, xn = _time_shifts(x)
            for n, xs in enumerate((xp, x, xn)):
                dcw_ref[n:n + 1, :] += jnp.sum(dc * xs, axis=0, keepdims=True)
            dcb_ref[...] += jnp.sum(dc, axis=0, keepdims=True)

        conv_bwd(dg, ug_ref, cwg_ref, dug_ref, dcwg_ref, dcbg_ref)
        conv_bwd(dv, uv_ref, cwv_ref, duv_ref, dcwv_ref, dcbv_ref)

    big = jax.ShapeDtypeStruct((b, t, f), F32)
    return pl.pallas_call(
        body, name='glu_bwd', grid=(f // tc, b), in_specs=[seq, seq, cw, cw, cb, cb, seq],
        out_specs=[seq, seq, cw, cw, cb, cb],
        out_shape=[big, big, jax.ShapeDtypeStruct((3, f), F32), jax.ShapeDtypeStruct((3, f), F32),
                   jax.ShapeDtypeStruct((1, f), F32), jax.ShapeDtypeStruct((1, f), F32)],
        compiler_params=_params("parallel", "arbitrary"),
    )(ug, uv, cwg, cwv, cbg, cbv, dact)


@jax.custom_vjp
def _conv_glu(ug, uv, cwg, cwv, cbg, cbv):
    return _glu_fwd_call(ug, uv, cwg, cwv, cbg, cbv)


def _conv_glu_fwd(*args):
    return _glu_fwd_call(*args), args


def _conv_glu_bwd(res, dact):
    return tuple(_glu_bwd_call(*res, dact))


_conv_glu.defvjp(_conv_glu_fwd, _conv_glu_bwd)


def _softmax_rows(q_ref, k_ref):
    s = lax.dot_general(q_ref[0].astype(BF16), k_ref[0].astype(BF16), (((1,), (1,)), ((), ())),
                        preferred_element_type=F32) * MLA_SCALE
    p = jnp.exp(s - jnp.max(s, axis=-1, keepdims=True))
    return p, jnp.sum(p, axis=-1, keepdims=True)


def _attn_fwd_call(q, k, v, name):
    bh, t, _ = q.shape
    tq = min(ATT_TQ, t)

    def body(q_ref, k_ref, v_ref, o_ref):
        p, l = _softmax_rows(q_ref, k_ref)
        o_ref[0] = jnp.dot(p.astype(BF16), v_ref[0].astype(BF16), preferred_element_type=F32) / l

    return pl.pallas_call(
        body, name=name, grid=(bh, t // tq),
        in_specs=[pl.BlockSpec((1, tq, D_QK), lambda b, i: (b, i, 0)), pl.BlockSpec((1, t, D_QK), lambda b, i: (b, 0, 0)),
                  pl.BlockSpec((1, t, D_V), lambda b, i: (b, 0, 0))],
        out_specs=pl.BlockSpec((1, tq, D_V), lambda b, i: (b, i, 0)),
        out_shape=jax.ShapeDtypeStruct((bh, t, D_V), F32),
        compiler_params=_params("parallel", "parallel"),
    )(q, k, v)


def _attn_bwd_call(q, k, v, o, do, name):
    bh, t, _ = q.shape
    tq = min(ATT_TQ, t)

    def body(q_ref, k_ref, v_ref, o_ref, do_ref, dq_ref, dk_ref, dv_ref):
        @pl.when(pl.program_id(1) == 0)
        def _():
            dk_ref[...] = jnp.zeros_like(dk_ref)
            dv_ref[...] = jnp.zeros_like(dv_ref)

        p, l = _softmax_rows(q_ref, k_ref)
        p = p / l
        dov = do_ref[0]
        do_b = dov.astype(BF16)
        delta = jnp.sum(dov * o_ref[0], axis=-1, keepdims=True)
        dp = lax.dot_general(do_b, v_ref[0].astype(BF16), (((1,), (1,)), ((), ())), preferred_element_type=F32)
        ds = (p * (dp - delta) * MLA_SCALE).astype(BF16)
        dq_ref[0] = jnp.dot(ds, k_ref[0].astype(BF16), preferred_element_type=F32)
        dk_ref[0] += lax.dot_general(ds, q_ref[0].astype(BF16), (((0,), (0,)), ((), ())), preferred_element_type=F32)
        dv_ref[0] += lax.dot_general(p.astype(BF16), do_b, (((0,), (0,)), ((), ())), preferred_element_type=F32)

    qspec = pl.BlockSpec((1, tq, D_QK), lambda b, i: (b, i, 0))
    kspec = pl.BlockSpec((1, t, D_QK), lambda b, i: (b, 0, 0))
    vspec = pl.BlockSpec((1, t, D_V), lambda b, i: (b, 0, 0))
    ospec = pl.BlockSpec((1, tq, D_V), lambda b, i: (b, i, 0))
    return pl.pallas_call(
        body, name=name, grid=(bh, t // tq),
        in_specs=[qspec, kspec, vspec, ospec, ospec], out_specs=[qspec, kspec, vspec],
        out_shape=[jax.ShapeDtypeStruct(q.shape, F32), jax.ShapeDtypeStruct(k.shape, F32), jax.ShapeDtypeStruct(v.shape, F32)],
        compiler_params=_params("parallel", "arbitrary"),
    )(q, k, v, o, do)


@jax.custom_vjp
def _attention(q, k, v):
    return _attn_fwd_call(q, k, v, 'attn_fwd')


def _attention_fwd(q, k, v):
    o = _attn_fwd_call(q, k, v, 'attn_fwd')
    return o, (q, k, v, o)


def _attention_bwd(res, do):
    q, k, v, o = res
    return tuple(_attn_bwd_call(q, k, v, o, do, 'attn_bwd'))


_attention.defvjp(_attention_fwd, _attention_bwd)


SROWS = N * D_RWKV // SEG


def _seg_ones():
    r = lax.broadcasted_iota(jnp.int32, (SEG, SEG), 0) >> 6
    c = lax.broadcasted_iota(jnp.int32, (SEG, SEG), 1) >> 6
    return (r == c).astype(BF16)


def _eye_mask():
    r = lax.broadcasted_iota(jnp.int32, (SROWS, SEG), 0) & (N - 1)
    c = lax.broadcasted_iota(jnp.int32, (SROWS, SEG), 1) & (N - 1)
    return r == c


def _row2(ref, bi, ti, dtype=F32):
    parts = [jnp.broadcast_to(ref[bi, pl.ds(ti, 1), pl.ds(SEG * q, SEG)].astype(dtype), (N, SEG))
             for q in range(D_RWKV // SEG)]
    return jnp.concatenate(parts, axis=0)


def _split2(x):
    hi = x.astype(BF16)
    return hi, (x - hi.astype(F32)).astype(BF16)


def _col_sum(x):
    return jnp.concatenate([jnp.sum(x[N * q:N * (q + 1)], axis=0, keepdims=True) for q in range(D_RWKV // SEG)], axis=1)


def _scan_specs(b, t, rev):
    nc = t // SCAN_CHUNK
    if rev:
        return (pl.BlockSpec((b, SCAN_CHUNK, D_RWKV), lambda c: (0, nc - 1 - c, 0)),
                pl.BlockSpec((b, SCAN_CHUNK, SROWS, SEG), lambda c: (0, nc - 1 - c, 0, 0)))
    return (pl.BlockSpec((b, SCAN_CHUNK, D_RWKV), lambda c: (0, c, 0)),
            pl.BlockSpec((b, SCAN_CHUNK, SROWS, SEG), lambda c: (0, c, 0, 0)))


def _scan_fwd_call(r, v, kk, wf, kf, qf, wb, kb, qb):
    b, t, _ = r.shape

    last = SCAN_CHUNK - 1

    def body(rf, vf, kkf, wf_, kf_, qf_, rb, vb, kkb, wb_, kb_, qb_, yf, yb, sf, sb, *states):
        @pl.when(pl.program_id(0) == 0)
        def _():
            for st in states:
                st[...] = jnp.zeros_like(st)

        ones, mask = _seg_ones(), _eye_mask()
        zero16 = jnp.zeros((), BF16)
        chains = []
        for bi in range(b):
            chains.append((rf, vf, kkf, wf_, kf_, qf_, yf, sf, states[2 * bi], bi, False))
            chains.append((rb, vb, kkb, wb_, kb_, qb_, yb, sb, states[2 * bi + 1], bi, True))

        def tix(i, rev):
            return last - i if rev else i

        def put_y(y_, bi, ti, ycol):
            y_[bi, pl.ds(ti, 1), :] = _col_sum(jnp.where(mask, ycol, 0.0))

        def steps(i, with_y):
            parts = []
            for (r_, v_, kk_, w_, k_, q_, y_, s_, st, bi, rev) in chains:
                ti = tix(i, rev)
                s = st[...]
                s_[bi, ti] = s
                parts += list(_split2(s * _row2(kk_, bi, ti)))
                parts.append(jnp.where(mask, _row2(v_, bi, ti, BF16), zero16))
                if with_y:
                    parts.append((s * _row2(r_, bi, tix(i - 1, rev))).astype(BF16))
            res = jnp.dot(jnp.concatenate(parts, axis=0), ones, preferred_element_type=F32)
            off = 0
            for (r_, v_, kk_, w_, k_, q_, y_, s_, st, bi, rev) in chains:
                ti = tix(i, rev)
                u = res[off:off + SROWS] + res[off + SROWS:off + 2 * SROWS]
                vcol = res[off + 2 * SROWS:off + 3 * SROWS]
                off += 3 * SROWS
                if with_y:
                    put_y(y_, bi, tix(i - 1, rev), res[off:off + SROWS])
                    off += SROWS
                st[...] = st[...] * _row2(w_, bi, ti) - u * _row2(q_, bi, ti) + vcol * _row2(k_, bi, ti)

        steps(0, False)

        def loop(i, carry):
            steps(i, True)
            return carry

        lax.fori_loop(1, SCAN_CHUNK, loop, 0)
        parts = [(c[8][...] * _row2(c[0], c[9], tix(last, c[10]))).astype(BF16) for c in chains]
        res = jnp.dot(jnp.concatenate(parts, axis=0), ones, preferred_element_type=F32)
        for n, c in enumerate(chains):
            put_y(c[6], c[9], tix(last, c[10]), res[n * SROWS:(n + 1) * SROWS])

    fr, fs = _scan_specs(b, t, False)
    br, bs = _scan_specs(b, t, True)
    y_shape = jax.ShapeDtypeStruct((b, t, D_RWKV), F32)
    s_shape = jax.ShapeDtypeStruct((b, t, SROWS, SEG), F32)
    return pl.pallas_call(
        body, name='scan_fwd', grid=(t // SCAN_CHUNK,),
        in_specs=[fr] * 6 + [br] * 6, out_specs=[fr, br, fs, bs], out_shape=[y_shape, y_shape, s_shape, s_shape],
        scratch_shapes=[pltpu.VMEM((SROWS, SEG), F32)] * (2 * b),
        compiler_params=_params("arbitrary"),
    )(r, v, kk, wf, kf, qf, r, v, kk, wb, kb, qb)


def _scan_bwd_call(r, v, kk, wf, kf, qf, wb, kb, qb, sf, sb, dyf, dyb):
    b, t, _ = r.shape

    last = SCAN_CHUNK - 1

    def body(rf, vf, kkf, wf_, kf_, qf_, sf_, dyf_, rb, vb, kkb, wb_, kb_, qb_, sb_, dyb_,
             drf, dvf, dkkf, dwf, dkf, dqf, drb, dvb, dkkb, dwb, dkb, dqb, *scratch):
        @pl.when(pl.program_id(0) == 0)
        def _():
            for n in range(2 * b):
                scratch[4 * n][...] = jnp.zeros_like(scratch[4 * n])

        ones, mask = _seg_ones(), _eye_mask()
        zero16 = jnp.zeros((), BF16)
        chains = []
        for bi in range(b):
            chains.append((rf, vf, kkf, wf_, kf_, qf_, sf_, dyf_, (drf, dvf, dkkf, dwf, dkf, dqf),
                           scratch[8 * bi:8 * bi + 4], bi, True))
            chains.append((rb, vb, kkb, wb_, kb_, qb_, sb_, dyb_, (drb, dvb, dkkb, dwb, dkb, dqb),
                           scratch[8 * bi + 4:8 * bi + 8], bi, False))

        def tix(i, rev):
            return last - i if rev else i

        def state_free_parts(v_, dy_, kk_, s_, bi, ti):
            return [jnp.where(mask, _row2(v_, bi, ti, BF16), zero16), jnp.where(mask, _row2(dy_, bi, ti, BF16), zero16),
                    (s_[bi, ti] * _row2(kk_, bi, ti)).astype(BF16)]

        def keep(scr, res, off):
            for n in range(3):
                scr[1 + n][...] = res[off + n * SROWS:off + (n + 1) * SROWS]
            return off + 3 * SROWS

        def first():
            parts = []
            for (r_, v_, kk_, w_, k_, q_, s_, dy_, outs, scr, bi, rev) in chains:
                parts += state_free_parts(v_, dy_, kk_, s_, bi, tix(0, rev))
            res = jnp.dot(jnp.concatenate(parts, axis=0), ones, preferred_element_type=F32)
            off = 0
            for c in chains:
                off = keep(c[9], res, off)

        def steps(i, has_next, recompute):
            parts = []
            for (r_, v_, kk_, w_, k_, q_, s_, dy_, outs, scr, bi, rev) in chains:
                ti = tix(i, rev)
                gst, vc, dc, uc = scr
                dycol = dc[...]
                if recompute:
                    sc = s_[bi, ti] * _row2(w_, bi, ti) - uc[...] * _row2(q_, bi, ti) + vc[...] * _row2(k_, bi, ti)
                else:
                    sc = s_[bi, tix(i - 1, rev)]
                outs[0][bi, pl.ds(ti, 1), :] = _col_sum(sc * dycol)
                g = gst[...] + dycol * _row2(r_, bi, ti)
                gst[...] = g
                parts += list(_split2(g * _row2(q_, bi, ti)))
                parts.append((g * _row2(k_, bi, ti)).astype(BF16))
                if has_next:
                    parts += state_free_parts(v_, dy_, kk_, s_, bi, tix(i + 1, rev))
            res = jnp.dot(jnp.concatenate(parts, axis=0), ones, preferred_element_type=F32)
            off = 0
            for (r_, v_, kk_, w_, k_, q_, s_, dy_, outs, scr, bi, rev) in chains:
                ti = tix(i, rev)
                gst, vc, dc, uc = scr
                dr_, dv_, dkk_, dw_, dk_, dq_ = outs

                def put(ref, val, sign=1.0):
                    ref[bi, pl.ds(ti, 1), :] = sign * _col_sum(val)

                gq = res[off:off + SROWS] + res[off + SROWS:off + 2 * SROWS]
                put(dv_, jnp.where(mask, res[off + 2 * SROWS:off + 3 * SROWS], 0.0))
                off += 3 * SROWS
                g, sp = gst[...], s_[bi, ti]
                put(dk_, g * vc[...])
                put(dw_, g * sp)
                put(dq_, g * uc[...], -1.0)
                put(dkk_, sp * gq, -1.0)
                gst[...] = g * _row2(w_, bi, ti) - gq * _row2(kk_, bi, ti)
                if has_next:
                    off = keep(scr, res, off)

        first()
        steps(0, True, True)

        def loop(i, carry):
            steps(i, True, False)
            return carry

        lax.fori_loop(1, last, loop, 0)
        steps(last, False, False)

    fr, fs = _scan_specs(b, t, True)
    br, bs = _scan_specs(b, t, False)
    y_shape = jax.ShapeDtypeStruct((b, t, D_RWKV), F32)
    return pl.pallas_call(
        body, name='scan_bwd', grid=(t // SCAN_CHUNK,),
        in_specs=[fr] * 6 + [fs, fr] + [br] * 6 + [bs, br],
        out_specs=[fr] * 6 + [br] * 6, out_shape=[y_shape] * 12,
        scratch_shapes=[pltpu.VMEM((SROWS, SEG), F32)] * (8 * b),
        compiler_params=_params("arbitrary"),
    )(r, v, kk, wf, kf, qf, sf, dyf, r, v, kk, wb, kb, qb, sb, dyb)


@jax.custom_vjp
def _wkv_scan(r, v, kk, wf, kf, qf, wb, kb, qb):
    yf, yb, _, _ = _scan_fwd_call(r, v, kk, wf, kf, qf, wb, kb, qb)
    return yf, yb


def _wkv_scan_fwd(r, v, kk, wf, kf, qf, wb, kb, qb):
    yf, yb, sf, sb = _scan_fwd_call(r, v, kk, wf, kf, qf, wb, kb, qb)
    return (yf, yb), (r, v, kk, wf, kf, qf, wb, kb, qb, sf, sb)


def _wkv_scan_bwd(res, dy):
    r, v, kk, wf, kf, qf, wb, kb, qb, sf, sb = res
    (drf, dvf, dkkf, dwf, dkf, dqf, drb, dvb, dkkb, dwb, dkb, dqb) = _scan_bwd_call(
        r, v, kk, wf, kf, qf, wb, kb, qb, sf, sb, dy[0], dy[1])
    return drf + drb, dvf + dvb, dkkf + dkkb, dwf, dkf, dqf, dwb, dkb, dqb


_wkv_scan.defvjp(_wkv_scan_fwd, _wkv_scan_bwd)


def _shift(z, left):
    if left:
        return jnp.pad(z[:, :-1], ((0, 0), (1, 0), (0, 0)))
    return jnp.pad(z[:, 1:], ((0, 0), (0, 1), (0, 0)))


def _rope_tables(t):
    inv_freq = jnp.power(ROPE_THETA, -jnp.arange(0, D_ROPE, 2, dtype=F32) / D_ROPE)
    ang = jnp.arange(t, dtype=F32)[:, None] * inv_freq[None, :]
    ang = jnp.concatenate([ang, ang], axis=-1)
    return jnp.cos(ang), jnp.sin(ang)


def _rope(x, cos, sin):
    x1, x2 = jnp.split(x, 2, axis=-1)
    return x * cos + jnp.concatenate([-x2, x1], axis=-1) * sin


@jax.custom_vjp
def _dot16(a, w):
    return jnp.dot(a.astype(BF16), w.astype(BF16), preferred_element_type=F32)


def _dot16_fwd(a, w):
    a16, w16 = a.astype(BF16), w.astype(BF16)
    return jnp.dot(a16, w16, preferred_element_type=F32), (a16, w16)


def _dot16_bwd(res, g):
    a16, w16 = res
    g16 = g.astype(BF16)
    return (lax.dot_general(g16, w16, (((1,), (1,)), ((), ())), preferred_element_type=F32),
            lax.dot_general(a16, g16, (((0,), (0,)), ((), ())), preferred_element_type=F32))


_dot16.defvjp(_dot16_fwd, _dot16_bwd)


def _head_sum_tile(x):
    outs = []
    ones = _seg_ones()
    for q in range(x.shape[1] // SEG):
        hi, lo = _split2(x[:, SEG * q:SEG * (q + 1)])
        outs.append(jnp.dot(hi, ones, preferred_element_type=F32) + jnp.dot(lo, ones, preferred_element_type=F32))
    return jnp.concatenate(outs, axis=1)


@jax.custom_vjp
def _hsum(x):
    return _head_sum_tile(x)


_hsum.defvjp(lambda x: (_head_sum_tile(x), None), lambda _, g: (_head_sum_tile(g),))


def _softplus(x):
    return jnp.maximum(x, 0.0) + jnp.log(1.0 + jnp.exp(-jnp.abs(x)))


def _rwkv_pre_fn(k, wdf, wdb, adf, adb, gd, w0f, w2f, w0b, w2b, a0f, a2f, a0b, a2b, g2, k_k, k_a):
    w_f = jnp.exp(-jnp.exp(-_softplus(-(w0f + _dot16(jnp.tanh(wdf), w2f))) - 0.5))
    w_b = jnp.exp(-jnp.exp(-_softplus(-(w0b + _dot16(jnp.tanh(wdb), w2b))) - 0.5))
    a_f = jax.nn.sigmoid(a0f + _dot16(adf, a2f))
    a_b = jax.nn.sigmoid(a0b + _dot16(adb, a2b))
    gate = _dot16(jax.nn.sigmoid(gd), g2)
    kk = k * k_k
    kk = kk / jnp.maximum(jnp.sqrt(_hsum(kk * kk)), L2_EPS)
    return (kk, w_f, k * (1.0 + (a_f - 1.0) * k_a), kk * a_f, w_b, k * (1.0 + (a_b - 1.0) * k_a), kk * a_b, gate)


def _rwkv_post_fn(y_f, y_b, r, k_f, k_b, v, gate, ln_g, ln_b, r_k):
    y = y_f + y_b
    yc = y - _hsum(y) * (1.0 / N)
    var = _hsum(yc * yc) * (1.0 / N)
    y = yc * lax.rsqrt(var + GN_EPS) * ln_g + ln_b
    return ((y + _hsum(r * (k_f + k_b) * r_k) * v) * gate,)


def _make_rowwise(fn, name, n_rows, tm):
    def specs(arrs, whole):
        if whole:
            return [pl.BlockSpec(a.shape, lambda i: (0, 0)) for a in arrs]
        return [pl.BlockSpec((tm, a.shape[1]), lambda i: (i, 0)) for a in arrs]

    def out_widths(rows, params):
        tiles = [jax.ShapeDtypeStruct((tm, a.shape[1]), F32) for a in rows]
        return [o.shape[1] for o in jax.eval_shape(fn, *tiles, *params)]

    def fwd_call(rows, params):
        m = rows[0].shape[0]
        n_in = len(rows) + len(params)
        outs = [jax.ShapeDtypeStruct((m, d), F32) for d in out_widths(rows, params)]

        def body(*refs):
            for o_ref, o in zip(refs[n_in:], fn(*[ref[...] for ref in refs[:n_in]])):
                o_ref[...] = o

        return pl.pallas_call(
            body, name=name + '_fwd', grid=(m // tm,), in_specs=specs(rows, False) + specs(params, True),
            out_specs=specs(outs, False), out_shape=outs, compiler_params=_params("parallel"),
        )(*rows, *params)

    def bwd_call(rows, params, cts):
        m = rows[0].shape[0]
        n_in = len(rows) + len(params)
        n_all = n_in + len(cts)
        outs = ([jax.ShapeDtypeStruct(a.shape, F32) for a in rows] + [jax.ShapeDtypeStruct(a.shape, F32) for a in params])

        def body(*refs):
            _, vjp = jax.vjp(fn, *[ref[...] for ref in refs[:n_in]])
            grads = vjp(tuple(ref[...] for ref in refs[n_in:n_all]))
            d_rows, d_params = refs[n_all:n_all + len(rows)], refs[n_all + len(rows):]
            for ref, g in zip(d_rows, grads[:len(rows)]):
                ref[...] = g

            @pl.when(pl.program_id(0) == 0)
            def _():
                for ref in d_params:
                    ref[...] = jnp.zeros_like(ref)

            for ref, g in zip(d_params, grads[len(rows):]):
                ref[...] += g

        return pl.pallas_call(
            body, name=name + '_bwd', grid=(m // tm,),
            in_specs=specs(rows, False) + specs(params, True) + specs(cts, False),
            out_specs=specs(rows, False) + specs(params, True), out_shape=outs, compiler_params=_params("arbitrary"),
        )(*rows, *params, *cts)

    @jax.custom_vjp
    def op(*args):
        return tuple(fwd_call(args[:n_rows], args[n_rows:]))

    def op_fwd(*args):
        return tuple(fwd_call(args[:n_rows], args[n_rows:])), args

    def op_bwd(args, cts):
        return tuple(bwd_call(args[:n_rows], args[n_rows:], cts))

    op.defvjp(op_fwd, op_bwd)
    return op


def _rwkv_mixer(z, full, rep):
    b, t, _ = z.shape
    m = b * t
    z = z + rep['shift_mu_prev'] * (_shift(z, True) - z) + rep['shift_mu_next'] * (_shift(z, False) - z)
    z = z.reshape(m, RWKV_COLS)
    r, k, v = z[:, :512], z[:, 512:1024], z[:, 1024:1536]
    lora_in = (z[:, 1536:1600], z[:, 1600:1664], z[:, 1664:1728], z[:, 1728:1792], z[:, 1792:1920])
    kk, w_f, k_f, q_f, w_b, k_b, q_b, gate = _make_rowwise(_rwkv_pre_fn, 'rwkv_pre', 6, _tile(m, ROW_TILE))(
        k, *lora_in, rep['decay_w0_fwd'], full['decay_w2_fwd'], rep['decay_w0_bwd'], full['decay_w2_bwd'],
        rep['iclr_a0_fwd'], full['iclr_a2_fwd'], rep['iclr_a0_bwd'], full['iclr_a2_bwd'], full['gate_g2'],
        rep['k_k'], rep['k_a'])
    seq = lambda a: a.reshape(b, t, D_RWKV)
    y_f, y_b = _wkv_scan(seq(r), seq(v), seq(kk), seq(w_f), seq(k_f), seq(q_f), seq(w_b), seq(k_b), seq(q_b))
    return _make_rowwise(_rwkv_post_fn, 'rwkv_post', 7, _tile(m, ROW_TILE))(
        y_f.reshape(m, D_RWKV), y_b.reshape(m, D_RWKV), r, k_f, k_b, v, gate,
        rep['ln_x_g'], rep['ln_x_b'], rep['r_k'].reshape(1, D_RWKV))[0]


def _mla_mixer(z, full, rep, b, t):
    m = b * t
    c_q, c_kv, k_rope = z[:, :768], z[:, 768:1024], z[:, 1024:1056]
    cos, sin = _rope_tables(t)
    q = _make_mm('mm_uq')(_make_rms('rms_q')(c_q, rep['q_norm_g']), full['w_uq']).reshape(b, t, H, D_QK)
    q = jnp.concatenate([q[..., :D_NOPE], _rope(q[..., D_NOPE:], cos[:, None, :], sin[:, None, :])], axis=-1)
    kv = _make_mm('mm_ukv')(_make_rms('rms_kv')(c_kv, rep['kv_norm_g']), full['w_ukv']).reshape(b, t, H, D_NOPE + D_V)
    k_rope = _rope(k_rope.reshape(b, t, D_ROPE), cos, sin)
    k = jnp.concatenate([kv[..., :D_NOPE], jnp.broadcast_to(k_rope[:, :, None, :], (b, t, H, D_ROPE))], axis=-1)
    heads = lambda a: a.transpose(0, 2, 1, 3).reshape(b * H, t, a.shape[-1])
    o = _attention(heads(q), heads(k), heads(kv[..., D_NOPE:]))
    o = o.reshape(b, H, t, D_V).transpose(0, 2, 1, 3).reshape(m, H * D_V)
    return _make_rms('rms_mla_out')(o, rep['mla_out_g'])


def _local_loss(full, rep, x, target):
    b, t, d = x.shape
    m = b * t
    xf = x.reshape(m, d)
    n1 = _make_rms('rms_mix')(xf, rep['ln_mix_g'])
    d_in = full['w_in'].shape[1]
    d_in_pad = -(-d_in // MM_TILE) * MM_TILE
    z = _make_mm('mm_in')(n1, jnp.pad(full['w_in'], ((0, 0), (0, d_in_pad - d_in))))
    y_rwkv = _rwkv_mixer(z[:, :RWKV_COLS].reshape(b, t, RWKV_COLS), full, rep)
    y_mla = _mla_mixer(z[:, RWKV_COLS:d_in], full, rep, b, t)
    h = xf + _make_mm('mm_out')(jnp.concatenate([y_rwkv, y_mla], axis=-1), full['w_out'])
    n2 = _make_rms('rms_ffn')(h, rep['ln_ffn_g'])
    w_up, cw, cb = full['w_ffn_up'], full['ffn_conv_w'], rep['ffn_conv_b']
    u_gate = _make_mm('mm_up_gate')(n2, w_up[:, :D_FF]).reshape(b, t, D_FF)
    u_val = _make_mm('mm_up_val')(n2, w_up[:, D_FF:]).reshape(b, t, D_FF)
    act = _conv_glu(u_gate, u_val, cw[:, :D_FF], cw[:, D_FF:], cb[:, :D_FF], cb[:, D_FF:]).reshape(m, D_FF)
    h = h + _make_mm('mm_down')(act, full['w_ffn_down'])
    out = _make_rms('rms_final')(h, rep['ln_final_g'])
    err = jnp.square(out - target.reshape(m, d))
    return 0.5 * jnp.sum(jnp.mean(err, axis=-1))


def _mat(a):
    if a.ndim == 1:
        return a.reshape(1, -1)
    if a.ndim == 3:
        return a.reshape(a.shape[1:])
    return a


def _join(shards, name):
    if name in ROW:
        return shards.reshape(-1, shards.shape[-1])
    return shards.transpose(1, 0, 2).reshape(shards.shape[1], -1)


def _cut(whole, name):
    r, c = whole.shape
    if name in ROW:
        return whole.reshape(N_DEV, r // N_DEV, c)
    return whole.reshape(r, N_DEV, c // N_DEV).transpose(1, 0, 2)


def kernel(x, ln_mix_g, w_in, shift_mu_prev, shift_mu_next, decay_w0_fwd, decay_w2_fwd, decay_w0_bwd, decay_w2_bwd, iclr_a0_fwd, iclr_a2_fwd, iclr_a0_bwd, iclr_a2_bwd, gate_g2, k_k, k_a, r_k, ln_x_g, ln_x_b, q_norm_g, w_uq, kv_norm_g, w_ukv, mla_out_g, w_out, ln_ffn_g, w_ffn_up, ffn_conv_w, ffn_conv_b, w_ffn_down, ln_final_g, loss_target, m_ln_mix_g, m_w_in, m_shift_mu_prev, m_shift_mu_next, m_decay_w0_fwd, m_decay_w2_fwd, m_decay_w0_bwd, m_decay_w2_bwd, m_iclr_a0_fwd, m_iclr_a2_fwd, m_iclr_a0_bwd, m_iclr_a2_bwd, m_gate_g2, m_k_k, m_k_a, m_r_k, m_ln_x_g, m_ln_x_b, m_q_norm_g, m_w_uq, m_kv_norm_g, m_w_ukv, m_mla_out_g, m_w_out, m_ln_ffn_g, m_w_ffn_up, m_ffn_conv_w, m_ffn_conv_b, m_w_ffn_down, m_ln_final_g, v_ln_mix_g, v_w_in, v_shift_mu_prev, v_shift_mu_next, v_decay_w0_fwd, v_decay_w2_fwd, v_decay_w0_bwd, v_decay_w2_bwd, v_iclr_a0_fwd, v_iclr_a2_fwd, v_iclr_a0_bwd, v_iclr_a2_bwd, v_gate_g2, v_k_k, v_k_a, v_r_k, v_ln_x_g, v_ln_x_b, v_q_norm_g, v_w_uq, v_kv_norm_g, v_w_ukv, v_mla_out_g, v_w_out, v_ln_ffn_g, v_w_ffn_up, v_ffn_conv_w, v_ffn_conv_b, v_w_ffn_down, v_ln_final_g):
    given = dict(locals())
    w = {n: given[n] for n in WNAMES}
    mom = {n: given['m_' + n] for n in WNAMES}
    var = {n: given['v_' + n] for n in WNAMES}

    wire = [lax.bitcast_convert_type(_mat(w[n]), BF16) if n in EXACT else _mat(w[n]).astype(BF16) for n in SHARDED]
    gathered = _unpack(_all_gather(_pack(wire), 'gather_weights'), [a.shape for a in wire], lead=1)
    gathered = [lax.bitcast_convert_type(s, F32) if n in EXACT else s.astype(F32) for n, s in zip(SHARDED, gathered)]
    full = {n: _join(s, n) for n, s in zip(SHARDED, gathered)}
    rep = {n: _mat(w[n]) for n in REPLICATED}
    rep['r_k'] = w['r_k'].reshape(H, N)

    loss_local, (g_full, g_rep, g_x) = jax.value_and_grad(_local_loss, argnums=(0, 1, 2))(full, rep, x, loss_target)

    g_pack = _pack([_cut(g_full[n], n).astype(BF16) for n in SHARDED], lead=1)
    parts = _grad_exchange(g_pack, 'exchange_grads')
    s_out = _sum_adamw(parts, _pack([w[n] for n in SHARDED]), _pack([mom[n] for n in SHARDED]),
                       _pack([var[n] for n in SHARDED]), 'adamw_sharded')
    s_out = [_unpack(o, [w[n].shape for n in SHARDED]) for o in s_out]

    zero = jnp.zeros((1,), F32)
    r_pack = _pack([g_rep[n] for n in REPLICATED] + [loss_local.reshape(1)])
    r_parts = _all_gather(r_pack, 'gather_small')
    r_out = _sum_adamw(r_parts, _pack([w[n] for n in REPLICATED] + [zero]), _pack([mom[n] for n in REPLICATED] + [zero]),
                       _pack([var[n] for n in REPLICATED] + [zero]), 'adamw_replicated')
    r_out = [_unpack(o, [w[n].shape for n in REPLICATED] + [(1,)]) for o in r_out]

    loss = r_out[0][-1].reshape(())
    outs = [loss, g_x]
    for kind in range(4):
        by_name = dict(zip(SHARDED, s_out[kind]))
        by_name.update(zip(REPLICATED, r_out[kind][:-1]))
        outs += [by_name[n] for n in WNAMES]
    return tuple(outs)
```

```python
import functools

import jax
import jax.numpy as jnp
from jax import lax
from jax.experimental import pallas as pl
from jax.experimental.pallas import tpu as pltpu

F32 = jnp.float32
BF16 = jnp.bfloat16
MESH = pl.DeviceIdType.MESH

N_DEV = 8
LANES = 128
SUBLANES = 8
PACK_TILE = 2 * SUBLANES * LANES
PACK_ROWS = 512
MM_TILE = 512
MM_TILE_WIDE = 1408
MM_K_WHOLE = 2816
VMEM_LIMIT = 48 * 1024 * 1024

H = 8
N = 64
D_RWKV = H * N
D_NOPE, D_ROPE, D_V = 64, 32, 64
D_QK = D_NOPE + D_ROPE
MLA_SCALE = D_QK ** -0.5
ROPE_THETA = 10000.0
RWKV_COLS = 1920
MLA_COLS = 1056
D_FF = 2816
NORM_EPS = 1e-6
GN_EPS = 64e-5
L2_EPS = 1e-12
ADAM_LR, ADAM_B1, ADAM_B2, ADAM_EPS, ADAM_WD, ADAM_STEP = 0.001, 0.9, 0.999, 1e-08, 0.01, 10

SCAN_CHUNK = 16
ATT_TQ = 256
SEG = 256
FFN_COLS = 256
ROW_TILE = 256
SHIFT_COLS = 384

WNAMES = ['ln_mix_g', 'w_in', 'shift_mu_prev', 'shift_mu_next', 'decay_w0_fwd', 'decay_w2_fwd', 'decay_w0_bwd',
          'decay_w2_bwd', 'iclr_a0_fwd', 'iclr_a2_fwd', 'iclr_a0_bwd', 'iclr_a2_bwd', 'gate_g2', 'k_k', 'k_a', 'r_k',
          'ln_x_g', 'ln_x_b', 'q_norm_g', 'w_uq', 'kv_norm_g', 'w_ukv', 'mla_out_g', 'w_out', 'ln_ffn_g', 'w_ffn_up',
          'ffn_conv_w', 'ffn_conv_b', 'w_ffn_down', 'ln_final_g']
COL = ('w_in', 'decay_w2_fwd', 'decay_w2_bwd', 'iclr_a2_fwd', 'iclr_a2_bwd', 'gate_g2', 'w_ukv', 'w_ffn_up', 'ffn_conv_w')
ROW = ('w_uq', 'w_out', 'w_ffn_down')
SHARDED = [n for n in WNAMES if n in COL or n in ROW]
REPLICATED = [n for n in WNAMES if n not in SHARDED]
EXACT = ('ffn_conv_w',)


def _params(*sem):
    return pltpu.CompilerParams(dimension_semantics=sem, vmem_limit_bytes=VMEM_LIMIT)


def _pack(arrs, lead=0):
    parts = []
    for a in arrs:
        head = a.shape[:lead]
        flat = a.reshape(head + (-1,))
        n = flat.shape[-1]
        n_pad = -(-n // PACK_TILE) * PACK_TILE
        flat = jnp.pad(flat, [(0, 0)] * lead + [(0, n_pad - n)])
        parts.append(flat.reshape(head + (n_pad // LANES, LANES)))
    out = jnp.concatenate(parts, axis=lead)
    rows = out.shape[lead]
    rows_pad = -(-rows // PACK_ROWS) * PACK_ROWS
    return jnp.pad(out, [(0, 0)] * lead + [(0, rows_pad - rows), (0, 0)])


def _unpack(packed, shapes, lead=0):
    outs, row = [], 0
    head = packed.shape[:lead]
    for shp in shapes:
        n = 1
        for s in shp:
            n *= s
        rows = -(-n // PACK_TILE) * (PACK_TILE // LANES)
        blk = lax.slice_in_dim(packed, row, row + rows, axis=lead)
        flat = blk.reshape(head + (rows * LANES,))
        outs.append(lax.slice_in_dim(flat, 0, n, axis=lead).reshape(head + tuple(shp)))
        row += rows
    return outs


def _all_gather(x, name):
    rows = x.shape[0]

    def body(x_ref, out_ref, send_sems, recv_sems, local_sem):
        mx, my, mc = lax.axis_index("x"), lax.axis_index("y"), lax.axis_index("c")
        me, sibling = (mx, my, mc), (mx, my, 1 - mc)
        chips = [(1 - mx, my), (mx, 1 - my), (1 - mx, 1 - my)]

        def slot(px, py, pc):
            return out_ref.at[4 * px + 2 * py + pc]

        def copy(k, block, to, src=None):
            return pltpu.make_async_remote_copy(
                src_ref=slot(*block) if src is None else src, dst_ref=slot(*block),
                send_sem=send_sems.at[k], recv_sem=recv_sems.at[k], device_id=to, device_id_type=MESH)

        mine = pltpu.make_async_copy(x_ref, slot(*me), local_sem)
        mine.start()
        first = [copy(0, me, sibling, src=x_ref)]
        first += [copy(1 + j, me, (*chip, mc), src=x_ref) for j, chip in enumerate(chips)]
        for cp in first:
            cp.start()
        passed = [copy(4 + j, (*chip, mc), sibling) for j, chip in enumerate(chips)]
        for j, chip in enumerate(chips):
            copy(1 + j, (*chip, mc), me).wait_recv()
            passed[j].start()
        copy(0, sibling, me).wait_recv()
        for j, chip in enumerate(chips):
            copy(4 + j, (*chip, 1 - mc), me).wait_recv()
        for cp in first + passed:
            cp.wait_send()
        mine.wait()

    return pl.pallas_call(
        body, name=name,
        out_shape=jax.ShapeDtypeStruct((N_DEV, rows, LANES), x.dtype),
        in_specs=[pl.BlockSpec(memory_space=pl.ANY)],
        out_specs=pl.BlockSpec(memory_space=pl.ANY),
        scratch_shapes=[pltpu.SemaphoreType.DMA((7,)), pltpu.SemaphoreType.DMA((7,)), pltpu.SemaphoreType.DMA(())],
    )(x)


def _grad_exchange(g, name):
    rows = g.shape[1]

    def body(g_ref, out_ref, send_sems, recv_sems, local_sem):
        mx, my, mc = lax.axis_index("x"), lax.axis_index("y"), lax.axis_index("c")
        me = 4 * mx + 2 * my + mc

        def flip(v, bit):
            return 1 - v if bit else v

        mine = pltpu.make_async_copy(g_ref.at[me], out_ref.at[me], local_sem)
        mine.start()
        copies = []
        for k in range(1, N_DEV):
            px, py, pc = flip(mx, k & 4), flip(my, k & 2), flip(mc, k & 1)
            peer = 4 * px + 2 * py + pc
            copies.append(pltpu.make_async_remote_copy(
                src_ref=g_ref.at[peer], dst_ref=out_ref.at[me],
                send_sem=send_sems.at[k - 1], recv_sem=recv_sems.at[k - 1],
                device_id=(px, py, pc), device_id_type=MESH))
        for cp in copies:
            cp.start()
        for cp in copies:
            cp.wait_recv()
        for cp in copies:
            cp.wait_send()
        mine.wait()

    return pl.pallas_call(
        body, name=name,
        out_shape=jax.ShapeDtypeStruct((N_DEV, rows, LANES), g.dtype),
        in_specs=[pl.BlockSpec(memory_space=pl.ANY)],
        out_specs=pl.BlockSpec(memory_space=pl.ANY),
        scratch_shapes=[pltpu.SemaphoreType.DMA((7,)), pltpu.SemaphoreType.DMA((7,)), pltpu.SemaphoreType.DMA(())],
    )(g)


def _sum_adamw(parts, w, m, v, name):
    rows = w.shape[0]
    c1 = 1.0 - ADAM_B1 ** ADAM_STEP
    c2 = 1.0 - ADAM_B2 ** ADAM_STEP

    def body(p_ref, w_ref, m_ref, v_ref, g_out, d_out, m_out, v_out):
        g = p_ref[0].astype(F32)
        for q in range(1, N_DEV):
            g = g + p_ref[q].astype(F32)
        m_new = ADAM_B1 * m_ref[...] + (1.0 - ADAM_B1) * g
        v_new = ADAM_B2 * v_ref[...] + (1.0 - ADAM_B2) * (g * g)
        m_hat = m_new / c1
        v_hat = v_new / c2
        g_out[...] = g
        d_out[...] = -ADAM_LR * (m_hat / (jnp.sqrt(v_hat) + ADAM_EPS) + ADAM_WD * w_ref[...])
        m_out[...] = m_new
        v_out[...] = v_new

    blk = pl.BlockSpec((PACK_ROWS, LANES), lambda i: (i, 0))
    out = jax.ShapeDtypeStruct((rows, LANES), F32)
    return pl.pallas_call(
        body, name=name, grid=(rows // PACK_ROWS,),
        in_specs=[pl.BlockSpec((N_DEV, PACK_ROWS, LANES), lambda i: (0, i, 0)), blk, blk, blk],
        out_specs=[blk, blk, blk, blk], out_shape=[out, out, out, out],
        compiler_params=_params("parallel"),
    )(parts, w, m, v)


def _tile(dim, cap=MM_TILE):
    if dim <= cap:
        return dim
    for t in range(cap, LANES - 1, -LANES):
        if dim % t == 0:
            return t
    return dim


def _mm_call(a, b, form, name):
    if form == 'nn':
        (m, k), n = a.shape, b.shape[1]
    elif form == 'nt':
        (m, k), n = a.shape, b.shape[0]
    else:
        (k, m), n = a.shape, b.shape[1]
    tk = k if (form == 'nn' and k <= MM_K_WHOLE) else _tile(k, MM_TILE_WIDE)
    tm = _tile(m, MM_TILE_WIDE if form == 'tn' else MM_TILE)
    tn = _tile(n, MM_TILE_WIDE)
    nk = k // tk
    contract = {'nn': ((1,), (0,)), 'nt': ((1,), (1,)), 'tn': ((0,), (0,))}[form]

    def body(a_ref, b_ref, o_ref):
        part = lax.dot_general(a_ref[...].astype(BF16), b_ref[...].astype(BF16), (contract, ((), ())),
                               preferred_element_type=F32)
        if nk == 1:
            o_ref[...] = part
        else:
            @pl.when(pl.program_id(2) == 0)
            def _():
                o_ref[...] = part

            @pl.when(pl.program_id(2) > 0)
            def _():
                o_ref[...] += part

    a_spec = pl.BlockSpec((tk, tm), lambda j, i, l: (l, i)) if form == 'tn' else pl.BlockSpec((tm, tk), lambda j, i, l: (i, l))
    b_spec = pl.BlockSpec((tn, tk), lambda j, i, l: (j, l)) if form == 'nt' else pl.BlockSpec((tk, tn), lambda j, i, l: (l, j))
    return pl.pallas_call(
        body, name=name, grid=(n // tn, m // tm, nk),
        in_specs=[a_spec, b_spec], out_specs=pl.BlockSpec((tm, tn), lambda j, i, l: (i, j)),
        out_shape=jax.ShapeDtypeStruct((m, n), F32),
        compiler_params=_params("parallel", "parallel", "arbitrary"),
    )(a, b)


def _make_mm(name):
    @jax.custom_vjp
    def mm(a, b):
        return _mm_call(a, b.astype(BF16), 'nn', name + '_fwd')

    def fwd(a, b):
        b16 = b.astype(BF16)
        return _mm_call(a, b16, 'nn', name + '_fwd'), (a, b16)

    def bwd(res, g):
        a, b16 = res
        return _mm_call(g, b16, 'nt', name + '_da'), _mm_call(a, g, 'tn', name + '_db')

    mm.defvjp(fwd, bwd)
    return mm


def _rms_fwd_call(x, g, name):
    m, d = x.shape
    tm = _tile(m)

    def body(x_ref, g_ref, o_ref):
        xv = x_ref[...]
        rinv = lax.rsqrt(jnp.mean(xv * xv, axis=-1, keepdims=True) + NORM_EPS)
        o_ref[...] = xv * rinv * g_ref[...]

    return pl.pallas_call(
        body, name=name, grid=(m // tm,),
        in_specs=[pl.BlockSpec((tm, d), lambda i: (i, 0)), pl.BlockSpec((1, d), lambda i: (0, 0))],
        out_specs=pl.BlockSpec((tm, d), lambda i: (i, 0)), out_shape=jax.ShapeDtypeStruct((m, d), F32),
        compiler_params=_params("parallel"),
    )(x, g)


def _rms_bwd_call(x, g, dy, name):
    m, d = x.shape
    tm = _tile(m)

    def body(x_ref, g_ref, dy_ref, dx_ref, dg_ref):
        @pl.when(pl.program_id(0) == 0)
        def _():
            dg_ref[...] = jnp.zeros_like(dg_ref)

        xv, dyv = x_ref[...], dy_ref[...]
        rinv = lax.rsqrt(jnp.mean(xv * xv, axis=-1, keepdims=True) + NORM_EPS)
        xh = xv * rinv
        dg_ref[...] += jnp.sum(dyv * xh, axis=0, keepdims=True)
        dxh = dyv * g_ref[...]
        dx_ref[...] = rinv * (dxh - xh * jnp.mean(dxh * xh, axis=-1, keepdims=True))

    return pl.pallas_call(
        body, name=name, grid=(m // tm,),
        in_specs=[pl.BlockSpec((tm, d), lambda i: (i, 0)), pl.BlockSpec((1, d), lambda i: (0, 0)),
                  pl.BlockSpec((tm, d), lambda i: (i, 0))],
        out_specs=[pl.BlockSpec((tm, d), lambda i: (i, 0)), pl.BlockSpec((1, d), lambda i: (0, 0))],
        out_shape=[jax.ShapeDtypeStruct((m, d), F32), jax.ShapeDtypeStruct((1, d), F32)],
        compiler_params=_params("arbitrary"),
    )(x, g, dy)


def _make_rms(name):
    @jax.custom_vjp
    def rms(x, g):
        return _rms_fwd_call(x, g, name + '_fwd')

    def fwd(x, g):
        return _rms_fwd_call(x, g, name + '_fwd'), (x, g)

    def bwd(res, dy):
        x, g = res
        dx, dg = _rms_bwd_call(x, g, dy, name + '_bwd')
        return dx, dg

    rms.defvjp(fwd, bwd)
    return rms


def _time_shifts(x):
    t = x.shape[0]
    rows = lax.broadcasted_iota(jnp.int32, x.shape, 0)
    return (jnp.where(rows == 0, 0.0, pltpu.roll(x, 1, 0)), jnp.where(rows == t - 1, 0.0, pltpu.roll(x, t - 1, 0)))


def _conv3(x, cw_ref, cb_ref):
    xp, xn = _time_shifts(x)
    return cw_ref[0:1, :] * xp + cw_ref[1:2, :] * x + cw_ref[2:3, :] * xn + cb_ref[...]


def _glu_specs(b, t, f):
    tc = _tile(f, FFN_COLS)
    seq = pl.BlockSpec((1, t, tc), lambda j, bi: (bi, 0, j))
    cw = pl.BlockSpec((3, tc), lambda j, bi: (0, j))
    cb = pl.BlockSpec((1, tc), lambda j, bi: (0, j))
    return tc, seq, cw, cb


def _glu_fwd_call(ug, uv, cwg, cwv, cbg, cbv):
    b, t, f = ug.shape
    tc, seq, cw, cb = _glu_specs(b, t, f)

    def body(ug_ref, uv_ref, cwg_ref, cwv_ref, cbg_ref, cbv_ref, o_ref):
        g = _conv3(ug_ref[0], cwg_ref, cbg_ref)
        o_ref[0] = g * jax.nn.sigmoid(g) * _conv3(uv_ref[0], cwv_ref, cbv_ref)

    return pl.pallas_call(
        body, name='glu_fwd', grid=(f // tc, b), in_specs=[seq, seq, cw, cw, cb, cb], out_specs=seq,
        out_shape=jax.ShapeDtypeStruct((b, t, f), F32), compiler_params=_params("parallel", "parallel"),
    )(ug, uv, cwg, cwv, cbg, cbv)


def _glu_bwd_call(ug, uv, cwg, cwv, cbg, cbv, dact):
    b, t, f = ug.shape
    tc, seq, cw, cb = _glu_specs(b, t, f)

    def body(ug_ref, uv_ref, cwg_ref, cwv_ref, cbg_ref, cbv_ref, da_ref,
             dug_ref, duv_ref, dcwg_ref, dcwv_ref, dcbg_ref, dcbv_ref):
        @pl.when(pl.program_id(1) == 0)
        def _():
            for ref in (dcwg_ref, dcwv_ref, dcbg_ref, dcbv_ref):
                ref[...] = jnp.zeros_like(ref)

        g = _conv3(ug_ref[0], cwg_ref, cbg_ref)
        v = _conv3(uv_ref[0], cwv_ref, cbv_ref)
        sig = jax.nn.sigmoid(g)
        da = da_ref[0]
        dv = da * (g * sig)
        dg = da * v * (sig * (1.0 + g * (1.0 - sig)))

        def conv_bwd(dc, x_ref, cw_ref, dx_ref, dcw_ref, dcb_ref):
            dcp, dcn = _time_shifts(dc)
            dx_ref[0] = cw_ref[0:1, :] * dcn + cw_ref[1:2, :] * dc + cw_ref[2:3, :] * dcp
            x = x_ref[0]
            xp, xn = _time_shifts(x)
            for n, xs in enumerate((xp, x, xn)):
                dcw_ref[n:n + 1, :] += jnp.sum(dc * xs, axis=0, keepdims=True)
            dcb_ref[...] += jnp.sum(dc, axis=0, keepdims=True)

        conv_bwd(dg, ug_ref, cwg_ref, dug_ref, dcwg_ref, dcbg_ref)
        conv_bwd(dv, uv_ref, cwv_ref, duv_ref, dcwv_ref, dcbv_ref)

    big = jax.ShapeDtypeStruct((b, t, f), F32)
    return pl.pallas_call(
        body, name='glu_bwd', grid=(f // tc, b), in_specs=[seq, seq, cw, cw, cb, cb, seq],
        out_specs=[seq, seq, cw, cw, cb, cb],
        out_shape=[big, big, jax.ShapeDtypeStruct((3, f), F32), jax.ShapeDtypeStruct((3, f), F32),
                   jax.ShapeDtypeStruct((1, f), F32), jax.ShapeDtypeStruct((1, f), F32)],
        compiler_params=_params("parallel", "arbitrary"),
    )(ug, uv, cwg, cwv, cbg, cbv, dact)


def _shift_call(z, mu_p, mu_n, dzs=None):
    b, t, c = z.shape
    tc = _tile(c, SHIFT_COLS)
    seq = pl.BlockSpec((1, t, tc), lambda j, bi: (bi, 0, j))
    row = pl.BlockSpec((1, tc), lambda j, bi: (0, j))

    def fwd_body(z_ref, mp_ref, mn_ref, o_ref):
        x = z_ref[0]
        xp, xn = _time_shifts(x)
        o_ref[0] = x + mp_ref[...] * (xp - x) + mn_ref[...] * (xn - x)

    def bwd_body(z_ref, mp_ref, mn_ref, d_ref, dz_ref, dmp_ref, dmn_ref):
        @pl.when(pl.program_id(1) == 0)
        def _():
            dmp_ref[...] = jnp.zeros_like(dmp_ref)
            dmn_ref[...] = jnp.zeros_like(dmn_ref)

        x, d = z_ref[0], d_ref[0]
        xp, xn = _time_shifts(x)
        dp, dn = _time_shifts(d)
        mp, mn = mp_ref[...], mn_ref[...]
        dz_ref[0] = d * (1.0 - mp - mn) + mp * dn + mn * dp
        dmp_ref[...] += jnp.sum(d * (xp - x), axis=0, keepdims=True)
        dmn_ref[...] += jnp.sum(d * (xn - x), axis=0, keepdims=True)

    if dzs is None:
        return pl.pallas_call(
            fwd_body, name='shift_fwd', grid=(c // tc, b), in_specs=[seq, row, row], out_specs=seq,
            out_shape=jax.ShapeDtypeStruct(z.shape, F32), compiler_params=_params("parallel", "parallel"),
        )(z, mu_p, mu_n)
    return pl.pallas_call(
        bwd_body, name='shift_bwd', grid=(c // tc, b), in_specs=[seq, row, row, seq], out_specs=[seq, row, row],
        out_shape=[jax.ShapeDtypeStruct(z.shape, F32), jax.ShapeDtypeStruct(mu_p.shape, F32),
                   jax.ShapeDtypeStruct(mu_n.shape, F32)],
        compiler_params=_params("parallel", "arbitrary"),
    )(z, mu_p, mu_n, dzs)


@jax.custom_vjp
def _token_shift(z, mu_p, mu_n):
    return _shift_call(z, mu_p, mu_n)


_token_shift.defvjp(lambda z, mu_p, mu_n: (_shift_call(z, mu_p, mu_n), (z, mu_p, mu_n)),
                    lambda res, d: tuple(_shift_call(*res, dzs=d)))


@jax.custom_vjp
def _conv_glu(ug, uv, cwg, cwv, cbg, cbv):
    return _glu_fwd_call(ug, uv, cwg, cwv, cbg, cbv)


def _conv_glu_fwd(*args):
    return _glu_fwd_call(*args), args


def _conv_glu_bwd(res, dact):
    return tuple(_glu_bwd_call(*res, dact))


_conv_glu.defvjp(_conv_glu_fwd, _conv_glu_bwd)


def _scores(q_ref, k_ref):
    q16 = (q_ref[0] * MLA_SCALE).astype(BF16)
    return lax.dot_general(q16, k_ref[0].astype(BF16), (((1,), (1,)), ((), ())), preferred_element_type=F32), q16


def _attn_fwd_call(q, k, v, name):
    bh, t, _ = q.shape
    tq = min(ATT_TQ, t)

    def body(q_ref, k_ref, v_ref, o_ref, lse_ref):
        s, _ = _scores(q_ref, k_ref)
        m = jnp.max(s, axis=-1, keepdims=True)
        p = jnp.exp(s - m)
        l = jnp.sum(p, axis=-1, keepdims=True)
        o_ref[0] = jnp.dot(p.astype(BF16), v_ref[0].astype(BF16), preferred_element_type=F32) / l
        lse_ref[0] = m + jnp.log(l)

    return pl.pallas_call(
        body, name=name, grid=(bh, t // tq),
        in_specs=[pl.BlockSpec((1, tq, D_QK), lambda b, i: (b, i, 0)), pl.BlockSpec((1, t, D_QK), lambda b, i: (b, 0, 0)),
                  pl.BlockSpec((1, t, D_V), lambda b, i: (b, 0, 0))],
        out_specs=[pl.BlockSpec((1, tq, D_V), lambda b, i: (b, i, 0)), pl.BlockSpec((1, tq, 1), lambda b, i: (b, i, 0))],
        out_shape=[jax.ShapeDtypeStruct((bh, t, D_V), F32), jax.ShapeDtypeStruct((bh, t, 1), F32)],
        compiler_params=_params("parallel", "parallel"),
    )(q, k, v)


def _attn_bwd_call(q, k, v, o, lse, do, name):
    bh, t, _ = q.shape
    tq = min(ATT_TQ, t)

    def body(q_ref, k_ref, v_ref, o_ref, lse_ref, do_ref, dq_ref, dk_ref, dv_ref):
        @pl.when(pl.program_id(1) == 0)
        def _():
            dk_ref[...] = jnp.zeros_like(dk_ref)
            dv_ref[...] = jnp.zeros_like(dv_ref)

        s, q16 = _scores(q_ref, k_ref)
        p = jnp.exp(s - lse_ref[0])
        dov = do_ref[0]
        do_b = dov.astype(BF16)
        delta = jnp.sum(dov * o_ref[0], axis=-1, keepdims=True)
        dp = lax.dot_general(do_b, v_ref[0].astype(BF16), (((1,), (1,)), ((), ())), preferred_element_type=F32)
        ds = (p * (dp - delta)).astype(BF16)
        dq_ref[0] = jnp.dot(ds, k_ref[0].astype(BF16), preferred_element_type=F32) * MLA_SCALE
        dk_ref[0] += lax.dot_general(ds, q16, (((0,), (0,)), ((), ())), preferred_element_type=F32)
        dv_ref[0] += lax.dot_general(p.astype(BF16), do_b, (((0,), (0,)), ((), ())), preferred_element_type=F32)

    qspec = pl.BlockSpec((1, tq, D_QK), lambda b, i: (b, i, 0))
    kspec = pl.BlockSpec((1, t, D_QK), lambda b, i: (b, 0, 0))
    vspec = pl.BlockSpec((1, t, D_V), lambda b, i: (b, 0, 0))
    ospec = pl.BlockSpec((1, tq, D_V), lambda b, i: (b, i, 0))
    lspec = pl.BlockSpec((1, tq, 1), lambda b, i: (b, i, 0))
    return pl.pallas_call(
        body, name=name, grid=(bh, t // tq),
        in_specs=[qspec, kspec, vspec, ospec, lspec, ospec], out_specs=[qspec, kspec, vspec],
        out_shape=[jax.ShapeDtypeStruct(q.shape, F32), jax.ShapeDtypeStruct(k.shape, F32), jax.ShapeDtypeStruct(v.shape, F32)],
        compiler_params=_params("parallel", "arbitrary"),
    )(q, k, v, o, lse, do)


@jax.custom_vjp
def _attention(q, k, v):
    return _attn_fwd_call(q, k, v, 'attn_fwd')[0]


def _attention_fwd(q, k, v):
    o, lse = _attn_fwd_call(q, k, v, 'attn_fwd')
    return o, (q, k, v, o, lse)


def _attention_bwd(res, do):
    return tuple(_attn_bwd_call(*res, do, 'attn_bwd'))


_attention.defvjp(_attention_fwd, _attention_bwd)


SROWS = N * D_RWKV // SEG


def _seg_ones():
    r = lax.broadcasted_iota(jnp.int32, (SEG, SEG), 0) >> 6
    c = lax.broadcasted_iota(jnp.int32, (SEG, SEG), 1) >> 6
    return (r == c).astype(BF16)


def _eye_mask():
    r = lax.broadcasted_iota(jnp.int32, (SROWS, SEG), 0) & (N - 1)
    c = lax.broadcasted_iota(jnp.int32, (SROWS, SEG), 1) & (N - 1)
    return r == c


def _row2(ref, bi, ti, dtype=F32):
    parts = [jnp.broadcast_to(ref[bi, pl.ds(ti, 1), pl.ds(SEG * q, SEG)].astype(dtype), (N, SEG))
             for q in range(D_RWKV // SEG)]
    return jnp.concatenate(parts, axis=0)


def _split2(x):
    hi = x.astype(BF16)
    return hi, (x - hi.astype(F32)).astype(BF16)


def _col_sum(x):
    return jnp.concatenate([jnp.sum(x[N * q:N * (q + 1)], axis=0, keepdims=True) for q in range(D_RWKV // SEG)], axis=1)


def _scan_specs(b, t, rev):
    nc = t // SCAN_CHUNK
    if rev:
        return (pl.BlockSpec((b, SCAN_CHUNK, D_RWKV), lambda c: (0, nc - 1 - c, 0)),
                pl.BlockSpec((b, SCAN_CHUNK, SROWS, SEG), lambda c: (0, nc - 1 - c, 0, 0)))
    return (pl.BlockSpec((b, SCAN_CHUNK, D_RWKV), lambda c: (0, c, 0)),
            pl.BlockSpec((b, SCAN_CHUNK, SROWS, SEG), lambda c: (0, c, 0, 0)))


def _scan_fwd_call(r, v, kk, wf, kf, qf, wb, kb, qb):
    b, t, _ = r.shape

    last = SCAN_CHUNK - 1

    def body(rf, vf, kkf, wf_, kf_, qf_, rb, vb, kkb, wb_, kb_, qb_, yf, yb, sf, sb, *states):
        @pl.when(pl.program_id(0) == 0)
        def _():
            for st in states:
                st[...] = jnp.zeros_like(st)

        ones, mask = _seg_ones(), _eye_mask()
        zero16 = jnp.zeros((), BF16)
        chains = []
        for bi in range(b):
            chains.append((rf, vf, kkf, wf_, kf_, qf_, yf, sf, states[2 * bi], bi, False))
            chains.append((rb, vb, kkb, wb_, kb_, qb_, yb, sb, states[2 * bi + 1], bi, True))

        def tix(i, rev):
            return last - i if rev else i

        def put_y(y_, bi, ti, ycol):
            y_[bi, pl.ds(ti, 1), :] = _col_sum(jnp.where(mask, ycol, 0.0))

        def steps(i, with_y):
            parts = []
            for (r_, v_, kk_, w_, k_, q_, y_, s_, st, bi, rev) in chains:
                ti = tix(i, rev)
                s = st[...]
                s_[bi, ti] = s
                parts.append((s * _row2(kk_, bi, ti)).astype(BF16))
                parts.append(jnp.where(mask, _row2(v_, bi, ti, BF16), zero16))
                if with_y:
                    parts.append((s * _row2(r_, bi, tix(i - 1, rev))).astype(BF16))
            res = jnp.dot(jnp.concatenate(parts, axis=0), ones, preferred_element_type=F32)
            off = 0
            for (r_, v_, kk_, w_, k_, q_, y_, s_, st, bi, rev) in chains:
                ti = tix(i, rev)
                u = res[off:off + SROWS]
                vcol = res[off + SROWS:off + 2 * SROWS]
                off += 2 * SROWS
                if with_y:
                    put_y(y_, bi, tix(i - 1, rev), res[off:off + SROWS])
                    off += SROWS
                st[...] = st[...] * _row2(w_, bi, ti) - u * _row2(q_, bi, ti) + vcol * _row2(k_, bi, ti)

        steps(0, False)

        def loop(i, carry):
            steps(i, True)
            return carry

        lax.fori_loop(1, SCAN_CHUNK, loop, 0)
        parts = [(c[8][...] * _row2(c[0], c[9], tix(last, c[10]))).astype(BF16) for c in chains]
        res = jnp.dot(jnp.concatenate(parts, axis=0), ones, preferred_element_type=F32)
        for n, c in enumerate(chains):
            put_y(c[6], c[9], tix(last, c[10]), res[n * SROWS:(n + 1) * SROWS])

    fr, fs = _scan_specs(b, t, False)
    br, bs = _scan_specs(b, t, True)
    y_shape = jax.ShapeDtypeStruct((b, t, D_RWKV), F32)
    s_shape = jax.ShapeDtypeStruct((b, t, SROWS, SEG), F32)
    return pl.pallas_call(
        body, name='scan_fwd', grid=(t // SCAN_CHUNK,),
        in_specs=[fr] * 6 + [br] * 6, out_specs=[fr, br, fs, bs], out_shape=[y_shape, y_shape, s_shape, s_shape],
        scratch_shapes=[pltpu.VMEM((SROWS, SEG), F32)] * (2 * b),
        compiler_params=_params("arbitrary"),
    )(r, v, kk, wf, kf, qf, r, v, kk, wb, kb, qb)


def _scan_bwd_call(r, v, kk, wf, kf, qf, wb, kb, qb, sf, sb, dyf, dyb):
    b, t, _ = r.shape

    last = SCAN_CHUNK - 1

    def body(rf, vf, kkf, wf_, kf_, qf_, sf_, dyf_, rb, vb, kkb, wb_, kb_, qb_, sb_, dyb_,
             drf, dvf, dkkf, dwf, dkf, dqf, drb, dvb, dkkb, dwb, dkb, dqb, *scratch):
        @pl.when(pl.program_id(0) == 0)
        def _():
            for n in range(2 * b):
                scratch[4 * n][...] = jnp.zeros_like(scratch[4 * n])

        ones, mask = _seg_ones(), _eye_mask()
        zero16 = jnp.zeros((), BF16)
        chains = []
        for bi in range(b):
            chains.append((rf, vf, kkf, wf_, kf_, qf_, sf_, dyf_, (drf, dvf, dkkf, dwf, dkf, dqf),
                           scratch[8 * bi:8 * bi + 4], bi, True))
            chains.append((rb, vb, kkb, wb_, kb_, qb_, sb_, dyb_, (drb, dvb, dkkb, dwb, dkb, dqb),
                           scratch[8 * bi + 4:8 * bi + 8], bi, False))

        def tix(i, rev):
            return last - i if rev else i

        def state_free_parts(v_, dy_, kk_, s_, bi, ti):
            return [jnp.where(mask, _row2(v_, bi, ti, BF16), zero16), jnp.where(mask, _row2(dy_, bi, ti, BF16), zero16),
                    (s_[bi, ti] * _row2(kk_, bi, ti)).astype(BF16)]

        def keep(scr, res, off):
            for n in range(3):
                scr[1 + n][...] = res[off + n * SROWS:off + (n + 1) * SROWS]
            return off + 3 * SROWS

        def first():
            parts = []
            for (r_, v_, kk_, w_, k_, q_, s_, dy_, outs, scr, bi, rev) in chains:
                parts += state_free_parts(v_, dy_, kk_, s_, bi, tix(0, rev))
            res = jnp.dot(jnp.concatenate(parts, axis=0), ones, preferred_element_type=F32)
            off = 0
            for c in chains:
                off = keep(c[9], res, off)

        def steps(i, has_next, recompute):
            parts = []
            for (r_, v_, kk_, w_, k_, q_, s_, dy_, outs, scr, bi, rev) in chains:
                ti = tix(i, rev)
                gst, vc, dc, uc = scr
                dycol = dc[...]
                if recompute:
                    sc = s_[bi, ti] * _row2(w_, bi, ti) - uc[...] * _row2(q_, bi, ti) + vc[...] * _row2(k_, bi, ti)
                else:
                    sc = s_[bi, tix(i - 1, rev)]
                outs[0][bi, pl.ds(ti, 1), :] = _col_sum(sc * dycol)
                g = gst[...] + dycol * _row2(r_, bi, ti)
                gst[...] = g
                parts.append((g * _row2(q_, bi, ti)).astype(BF16))
                parts.append((g * _row2(k_, bi, ti)).astype(BF16))
                if has_next:
                    parts += state_free_parts(v_, dy_, kk_, s_, bi, tix(i + 1, rev))
            res = jnp.dot(jnp.concatenate(parts, axis=0), ones, preferred_element_type=F32)
            off = 0
            for (r_, v_, kk_, w_, k_, q_, s_, dy_, outs, scr, bi, rev) in chains:
                ti = tix(i, rev)
                gst, vc, dc, uc = scr
                dr_, dv_, dkk_, dw_, dk_, dq_ = outs

                def put(ref, val, sign=1.0):
                    ref[bi, pl.ds(ti, 1), :] = sign * _col_sum(val)

                gq = res[off:off + SROWS]
                put(dv_, jnp.where(mask, res[off + SROWS:off + 2 * SROWS], 0.0))
                off += 2 * SROWS
                g, sp = gst[...], s_[bi, ti]
                put(dk_, g * vc[...])
                put(dw_, g * sp)
                put(dq_, g * uc[...], -1.0)
                put(dkk_, sp * gq, -1.0)
                gst[...] = g * _row2(w_, bi, ti) - gq * _row2(kk_, bi, ti)
                if has_next:
                    off = keep(scr, res, off)

        first()
        steps(0, True, True)

        def loop(i, carry):
            steps(i, True, False)
            return carry

        lax.fori_loop(1, last, loop, 0)
        steps(last, False, False)

    fr, fs = _scan_specs(b, t, True)
    br, bs = _scan_specs(b, t, False)
    y_shape = jax.ShapeDtypeStruct((b, t, D_RWKV), F32)
    return pl.pallas_call(
        body, name='scan_bwd', grid=(t // SCAN_CHUNK,),
        in_specs=[fr] * 6 + [fs, fr] + [br] * 6 + [bs, br],
        out_specs=[fr] * 6 + [br] * 6, out_shape=[y_shape] * 12,
        scratch_shapes=[pltpu.VMEM((SROWS, SEG), F32)] * (8 * b),
        compiler_params=_params("arbitrary"),
    )(r, v, kk, wf, kf, qf, sf, dyf, r, v, kk, wb, kb, qb, sb, dyb)


@jax.custom_vjp
def _wkv_scan(r, v, kk, wf, kf, qf, wb, kb, qb):
    yf, yb, _, _ = _scan_fwd_call(r, v, kk, wf, kf, qf, wb, kb, qb)
    return yf, yb


def _wkv_scan_fwd(r, v, kk, wf, kf, qf, wb, kb, qb):
    yf, yb, sf, sb = _scan_fwd_call(r, v, kk, wf, kf, qf, wb, kb, qb)
    return (yf, yb), (r, v, kk, wf, kf, qf, wb, kb, qb, sf, sb)


def _wkv_scan_bwd(res, dy):
    r, v, kk, wf, kf, qf, wb, kb, qb, sf, sb = res
    (drf, dvf, dkkf, dwf, dkf, dqf, drb, dvb, dkkb, dwb, dkb, dqb) = _scan_bwd_call(
        r, v, kk, wf, kf, qf, wb, kb, qb, sf, sb, dy[0], dy[1])
    return drf + drb, dvf + dvb, dkkf + dkkb, dwf, dkf, dqf, dwb, dkb, dqb


_wkv_scan.defvjp(_wkv_scan_fwd, _wkv_scan_bwd)


def _rope_tables(t):
    inv_freq = jnp.power(ROPE_THETA, -jnp.arange(0, D_ROPE, 2, dtype=F32) / D_ROPE)
    ang = jnp.arange(t, dtype=F32)[:, None] * inv_freq[None, :]
    ang = jnp.concatenate([ang, ang], axis=-1)
    return jnp.cos(ang), jnp.sin(ang)


def _rope(x, cos, sin):
    x1, x2 = jnp.split(x, 2, axis=-1)
    return x * cos + jnp.concatenate([-x2, x1], axis=-1) * sin


@jax.custom_vjp
def _dot16(a, w):
    return jnp.dot(a.astype(BF16), w.astype(BF16), preferred_element_type=F32)


def _dot16_fwd(a, w):
    a16, w16 = a.astype(BF16), w.astype(BF16)
    return jnp.dot(a16, w16, preferred_element_type=F32), (a16, w16)


def _dot16_bwd(res, g):
    a16, w16 = res
    g16 = g.astype(BF16)
    return (lax.dot_general(g16, w16, (((1,), (1,)), ((), ())), preferred_element_type=F32),
            lax.dot_general(a16, g16, (((0,), (0,)), ((), ())), preferred_element_type=F32))


_dot16.defvjp(_dot16_fwd, _dot16_bwd)


def _head_sum_tile(x):
    outs = []
    ones = _seg_ones()
    for q in range(x.shape[1] // SEG):
        hi, lo = _split2(x[:, SEG * q:SEG * (q + 1)])
        outs.append(jnp.dot(hi, ones, preferred_element_type=F32) + jnp.dot(lo, ones, preferred_element_type=F32))
    return jnp.concatenate(outs, axis=1)


@jax.custom_vjp
def _hsum(x):
    return _head_sum_tile(x)


_hsum.defvjp(lambda x: (_head_sum_tile(x), None), lambda _, g: (_head_sum_tile(g),))


def _softplus(x):
    return jnp.maximum(x, 0.0) + jnp.log(1.0 + jnp.exp(-jnp.abs(x)))


def _rwkv_pre_fn(k, wdf, wdb, adf, adb, gd, w0f, w2f, w0b, w2b, a0f, a2f, a0b, a2b, g2, k_k, k_a):
    w_f = jnp.exp(-jnp.exp(-_softplus(-(w0f + _dot16(jnp.tanh(wdf), w2f))) - 0.5))
    w_b = jnp.exp(-jnp.exp(-_softplus(-(w0b + _dot16(jnp.tanh(wdb), w2b))) - 0.5))
    a_f = jax.nn.sigmoid(a0f + _dot16(adf, a2f))
    a_b = jax.nn.sigmoid(a0b + _dot16(adb, a2b))
    gate = _dot16(jax.nn.sigmoid(gd), g2)
    kk = k * k_k
    kk = kk / jnp.maximum(jnp.sqrt(_hsum(kk * kk)), L2_EPS)
    return (kk, w_f, k * (1.0 + (a_f - 1.0) * k_a), kk * a_f, w_b, k * (1.0 + (a_b - 1.0) * k_a), kk * a_b, gate)


def _rwkv_post_fn(y_f, y_b, r, k_f, k_b, v, gate, ln_g, ln_b, r_k):
    y = y_f + y_b
    yc = y - _hsum(y) * (1.0 / N)
    var = _hsum(yc * yc) * (1.0 / N)
    y = yc * lax.rsqrt(var + GN_EPS) * ln_g + ln_b
    return ((y + _hsum(r * (k_f + k_b) * r_k) * v) * gate,)


def _make_rowwise(fn, name, n_rows, tm):
    def specs(arrs, whole):
        if whole:
            return [pl.BlockSpec(a.shape, lambda i: (0, 0)) for a in arrs]
        return [pl.BlockSpec((tm, a.shape[1]), lambda i: (i, 0)) for a in arrs]

    def out_widths(rows, params):
        tiles = [jax.ShapeDtypeStruct((tm, a.shape[1]), F32) for a in rows]
        return [o.shape[1] for o in jax.eval_shape(fn, *tiles, *params)]

    def fwd_call(rows, params):
        m = rows[0].shape[0]
        n_in = len(rows) + len(params)
        outs = [jax.ShapeDtypeStruct((m, d), F32) for d in out_widths(rows, params)]

        def body(*refs):
            for o_ref, o in zip(refs[n_in:], fn(*[ref[...] for ref in refs[:n_in]])):
                o_ref[...] = o

        return pl.pallas_call(
            body, name=name + '_fwd', grid=(m // tm,), in_specs=specs(rows, False) + specs(params, True),
            out_specs=specs(outs, False), out_shape=outs, compiler_params=_params("parallel"),
        )(*rows, *params)

    def bwd_call(rows, params, cts):
        m = rows[0].shape[0]
        n_in = len(rows) + len(params)
        n_all = n_in + len(cts)
        outs = ([jax.ShapeDtypeStruct(a.shape, F32) for a in rows] + [jax.ShapeDtypeStruct(a.shape, F32) for a in params])

        def body(*refs):
            _, vjp = jax.vjp(fn, *[ref[...] for ref in refs[:n_in]])
            grads = vjp(tuple(ref[...] for ref in refs[n_in:n_all]))
            d_rows, d_params = refs[n_all:n_all + len(rows)], refs[n_all + len(rows):]
            for ref, g in zip(d_rows, grads[:len(rows)]):
                ref[...] = g

            @pl.when(pl.program_id(0) == 0)
            def _():
                for ref in d_params:
                    ref[...] = jnp.zeros_like(ref)

            for ref, g in zip(d_params, grads[len(rows):]):
                ref[...] += g

        return pl.pallas_call(
            body, name=name + '_bwd', grid=(m // tm,),
            in_specs=specs(rows, False) + specs(params, True) + specs(cts, False),
            out_specs=specs(rows, False) + specs(params, True), out_shape=outs, compiler_params=_params("arbitrary"),
        )(*rows, *params, *cts)

    @jax.custom_vjp
    def op(*args):
        return tuple(fwd_call(args[:n_rows], args[n_rows:]))

    def op_fwd(*args):
        return tuple(fwd_call(args[:n_rows], args[n_rows:])), args

    def op_bwd(args, cts):
        return tuple(bwd_call(args[:n_rows], args[n_rows:], cts))

    op.defvjp(op_fwd, op_bwd)
    return op


def _rwkv_mixer(z, full, rep):
    b, t, _ = z.shape
    m = b * t
    z = _token_shift(z, rep['shift_mu_prev'], rep['shift_mu_next']).reshape(m, RWKV_COLS)
    r, k, v = z[:, :512], z[:, 512:1024], z[:, 1024:1536]
    lora_in = (z[:, 1536:1600], z[:, 1600:1664], z[:, 1664:1728], z[:, 1728:1792], z[:, 1792:1920])
    kk, w_f, k_f, q_f, w_b, k_b, q_b, gate = _make_rowwise(_rwkv_pre_fn, 'rwkv_pre', 6, _tile(m, ROW_TILE))(
        k, *lora_in, rep['decay_w0_fwd'], full['decay_w2_fwd'], rep['decay_w0_bwd'], full['decay_w2_bwd'],
        rep['iclr_a0_fwd'], full['iclr_a2_fwd'], rep['iclr_a0_bwd'], full['iclr_a2_bwd'], full['gate_g2'],
        rep['k_k'], rep['k_a'])
    seq = lambda a: a.reshape(b, t, D_RWKV)
    y_f, y_b = _wkv_scan(seq(r), seq(v), seq(kk), seq(w_f), seq(k_f), seq(q_f), seq(w_b), seq(k_b), seq(q_b))
    return _make_rowwise(_rwkv_post_fn, 'rwkv_post', 7, _tile(m, ROW_TILE))(
        y_f.reshape(m, D_RWKV), y_b.reshape(m, D_RWKV), r, k_f, k_b, v, gate,
        rep['ln_x_g'], rep['ln_x_b'], rep['r_k'].reshape(1, D_RWKV))[0]


def _mla_mixer(z, full, rep, b, t):
    m = b * t
    c_q, c_kv, k_rope = z[:, :768], z[:, 768:1024], z[:, 1024:1056]
    cos, sin = _rope_tables(t)
    q = _make_mm('mm_uq')(_make_rms('rms_q')(c_q, rep['q_norm_g']), full['w_uq']).reshape(b, t, H, D_QK)
    q = jnp.concatenate([q[..., :D_NOPE], _rope(q[..., D_NOPE:], cos[:, None, :], sin[:, None, :])], axis=-1)
    kv = _make_mm('mm_ukv')(_make_rms('rms_kv')(c_kv, rep['kv_norm_g']), full['w_ukv']).reshape(b, t, H, D_NOPE + D_V)
    k_rope = _rope(k_rope.reshape(b, t, D_ROPE), cos, sin)
    k = jnp.concatenate([kv[..., :D_NOPE], jnp.broadcast_to(k_rope[:, :, None, :], (b, t, H, D_ROPE))], axis=-1)
    heads = lambda a: a.transpose(0, 2, 1, 3).reshape(b * H, t, a.shape[-1])
    o = _attention(heads(q), heads(k), heads(kv[..., D_NOPE:]))
    o = o.reshape(b, H, t, D_V).transpose(0, 2, 1, 3).reshape(m, H * D_V)
    return _make_rms('rms_mla_out')(o, rep['mla_out_g'])


def _local_loss(full, rep, x, target):
    b, t, d = x.shape
    m = b * t
    xf = x.reshape(m, d)
    n1 = _make_rms('rms_mix')(xf, rep['ln_mix_g'])
    d_in = full['w_in'].shape[1]
    d_in_pad = -(-d_in // MM_TILE) * MM_TILE
    z = _make_mm('mm_in')(n1, jnp.pad(full['w_in'], ((0, 0), (0, d_in_pad - d_in))))
    y_rwkv = _rwkv_mixer(z[:, :RWKV_COLS].reshape(b, t, RWKV_COLS), full, rep)
    y_mla = _mla_mixer(z[:, RWKV_COLS:d_in], full, rep, b, t)
    h = xf + _make_mm('mm_out')(jnp.concatenate([y_rwkv, y_mla], axis=-1), full['w_out'])
    n2 = _make_rms('rms_ffn')(h, rep['ln_ffn_g'])
    w_up, cw, cb = full['w_ffn_up'], full['ffn_conv_w'], rep['ffn_conv_b']
    u_gate = _make_mm('mm_up_gate')(n2, w_up[:, :D_FF]).reshape(b, t, D_FF)
    u_val = _make_mm('mm_up_val')(n2, w_up[:, D_FF:]).reshape(b, t, D_FF)
    act = _conv_glu(u_gate, u_val, cw[:, :D_FF], cw[:, D_FF:], cb[:, :D_FF], cb[:, D_FF:]).reshape(m, D_FF)
    h = h + _make_mm('mm_down')(act, full['w_ffn_down'])
    out = _make_rms('rms_final')(h, rep['ln_final_g'])
    err = jnp.square(out - target.reshape(m, d))
    return 0.5 * jnp.sum(jnp.mean(err, axis=-1))


def _mat(a):
    if a.ndim == 1:
        return a.reshape(1, -1)
    if a.ndim == 3:
        return a.reshape(a.shape[1:])
    return a


def _join(shards, name):
    if name in ROW:
        return shards.reshape(-1, shards.shape[-1])
    return shards.transpose(1, 0, 2).reshape(shards.shape[1], -1)


def _cut(whole, name):
    r, c = whole.shape
    if name in ROW:
        return whole.reshape(N_DEV, r // N_DEV, c)
    return whole.reshape(r, N_DEV, c // N_DEV).transpose(1, 0, 2)


def kernel(x, ln_mix_g, w_in, shift_mu_prev, shift_mu_next, decay_w0_fwd, decay_w2_fwd, decay_w0_bwd, decay_w2_bwd, iclr_a0_fwd, iclr_a2_fwd, iclr_a0_bwd, iclr_a2_bwd, gate_g2, k_k, k_a, r_k, ln_x_g, ln_x_b, q_norm_g, w_uq, kv_norm_g, w_ukv, mla_out_g, w_out, ln_ffn_g, w_ffn_up, ffn_conv_w, ffn_conv_b, w_ffn_down, ln_final_g, loss_target, m_ln_mix_g, m_w_in, m_shift_mu_prev, m_shift_mu_next, m_decay_w0_fwd, m_decay_w2_fwd, m_decay_w0_bwd, m_decay_w2_bwd, m_iclr_a0_fwd, m_iclr_a2_fwd, m_iclr_a0_bwd, m_iclr_a2_bwd, m_gate_g2, m_k_k, m_k_a, m_r_k, m_ln_x_g, m_ln_x_b, m_q_norm_g, m_w_uq, m_kv_norm_g, m_w_ukv, m_mla_out_g, m_w_out, m_ln_ffn_g, m_w_ffn_up, m_ffn_conv_w, m_ffn_conv_b, m_w_ffn_down, m_ln_final_g, v_ln_mix_g, v_w_in, v_shift_mu_prev, v_shift_mu_next, v_decay_w0_fwd, v_decay_w2_fwd, v_decay_w0_bwd, v_decay_w2_bwd, v_iclr_a0_fwd, v_iclr_a2_fwd, v_iclr_a0_bwd, v_iclr_a2_bwd, v_gate_g2, v_k_k, v_k_a, v_r_k, v_ln_x_g, v_ln_x_b, v_q_norm_g, v_w_uq, v_kv_norm_g, v_w_ukv, v_mla_out_g, v_w_out, v_ln_ffn_g, v_w_ffn_up, v_ffn_conv_w, v_ffn_conv_b, v_w_ffn_down, v_ln_final_g):
    given = dict(locals())
    w = {n: given[n] for n in WNAMES}
    mom = {n: given['m_' + n] for n in WNAMES}
    var = {n: given['v_' + n] for n in WNAMES}

    wire = [lax.bitcast_convert_type(_mat(w[n]), BF16) if n in EXACT else _mat(w[n]).astype(BF16) for n in SHARDED]
    gathered = _unpack(_all_gather(_pack(wire), 'gather_weights'), [a.shape for a in wire], lead=1)
    gathered = [lax.bitcast_convert_type(s, F32) if n in EXACT else s.astype(F32) for n, s in zip(SHARDED, gathered)]
    full = {n: _join(s, n) for n, s in zip(SHARDED, gathered)}
    rep = {n: _mat(w[n]) for n in REPLICATED}
    rep['r_k'] = w['r_k'].reshape(H, N)

    loss_local, (g_full, g_rep, g_x) = jax.value_and_grad(_local_loss, argnums=(0, 1, 2))(full, rep, x, loss_target)

    g_pack = _pack([_cut(g_full[n], n).astype(BF16) for n in SHARDED], lead=1)
    parts = _grad_exchange(g_pack, 'exchange_grads')
    s_out = _sum_adamw(parts, _pack([w[n] for n in SHARDED]), _pack([mom[n] for n in SHARDED]),
                       _pack([var[n] for n in SHARDED]), 'adamw_sharded')
    s_out = [_unpack(o, [w[n].shape for n in SHARDED]) for o in s_out]

    zero = jnp.zeros((1,), F32)
    r_pack = _pack([g_rep[n] for n in REPLICATED] + [loss_local.reshape(1)])
    r_parts = _all_gather(r_pack, 'gather_small')
    r_out = _sum_adamw(r_parts, _pack([w[n] for n in REPLICATED] + [zero]), _pack([mom[n] for n in REPLICATED] + [zero]),
                       _pack([var[n] for n in REPLICATED] + [zero]), 'adamw_replicated')
    r_out = [_unpack(o, [w[n].shape for n in REPLICATED] + [(1,)]) for o in r_out]

    loss = r_out[0][-1].reshape(())
    outs = [loss, g_x]
    for kind in range(4):
        by_name = dict(zip(SHARDED, s_out[kind]))
        by_name.update(zip(REPLICATED, r_out[kind][:-1]))
        outs += [by_name[n] for n in WNAMES]
    return tuple(outs)
```

```python
import functools

import jax
import jax.numpy as jnp
from jax import lax
from jax.experimental import pallas as pl
from jax.experimental.pallas import tpu as pltpu

F32 = jnp.float32
BF16 = jnp.bfloat16
MESH = pl.DeviceIdType.MESH

N_DEV = 8
LANES = 128
SUBLANES = 8
PACK_TILE = 2 * SUBLANES * LANES
PACK_ROWS = 512
MM_TILE = 512
MM_TILE_WIDE = 1408
MM_K_WHOLE = 2816
VMEM_LIMIT = 48 * 1024 * 1024

H = 8
N = 64
D_RWKV = H * N
D_NOPE, D_ROPE, D_V = 64, 32, 64
D_QK = D_NOPE + D_ROPE
MLA_SCALE = D_QK ** -0.5
ROPE_THETA = 10000.0
RWKV_COLS = 1920
MLA_COLS = 1056
D_FF = 2816
NORM_EPS = 1e-6
GN_EPS = 64e-5
L2_EPS = 1e-12
ADAM_LR, ADAM_B1, ADAM_B2, ADAM_EPS, ADAM_WD, ADAM_STEP = 0.001, 0.9, 0.999, 1e-08, 0.01, 10

SCAN_CHUNK = 16
ATT_TQ = 256
SEG = 256
FFN_COLS = 256
ROW_TILE = 256
SHIFT_COLS = 384

WNAMES = ['ln_mix_g', 'w_in', 'shift_mu_prev', 'shift_mu_next', 'decay_w0_fwd', 'decay_w2_fwd', 'decay_w0_bwd',
          'decay_w2_bwd', 'iclr_a0_fwd', 'iclr_a2_fwd', 'iclr_a0_bwd', 'iclr_a2_bwd', 'gate_g2', 'k_k', 'k_a', 'r_k',
          'ln_x_g', 'ln_x_b', 'q_norm_g', 'w_uq', 'kv_norm_g', 'w_ukv', 'mla_out_g', 'w_out', 'ln_ffn_g', 'w_ffn_up',
          'ffn_conv_w', 'ffn_conv_b', 'w_ffn_down', 'ln_final_g']
COL = ('w_in', 'decay_w2_fwd', 'decay_w2_bwd', 'iclr_a2_fwd', 'iclr_a2_bwd', 'gate_g2', 'w_ukv', 'w_ffn_up', 'ffn_conv_w')
ROW = ('w_uq', 'w_out', 'w_ffn_down')
SHARDED = [n for n in WNAMES if n in COL or n in ROW]
REPLICATED = [n for n in WNAMES if n not in SHARDED]
EXACT = ('ffn_conv_w',)
LATE = ['w_out', 'w_ffn_up', 'ffn_conv_w', 'w_ffn_down']


def _params(*sem):
    return pltpu.CompilerParams(dimension_semantics=sem, vmem_limit_bytes=VMEM_LIMIT)


def _pack(arrs, lead=0):
    parts = []
    for a in arrs:
        head = a.shape[:lead]
        flat = a.reshape(head + (-1,))
        n = flat.shape[-1]
        n_pad = -(-n // PACK_TILE) * PACK_TILE
        flat = jnp.pad(flat, [(0, 0)] * lead + [(0, n_pad - n)])
        parts.append(flat.reshape(head + (n_pad // LANES, LANES)))
    out = jnp.concatenate(parts, axis=lead)
    rows = out.shape[lead]
    rows_pad = -(-rows // PACK_ROWS) * PACK_ROWS
    return jnp.pad(out, [(0, 0)] * lead + [(0, rows_pad - rows), (0, 0)])


def _unpack(packed, shapes, lead=0):
    outs, row = [], 0
    head = packed.shape[:lead]
    for shp in shapes:
        n = 1
        for s in shp:
            n *= s
        rows = -(-n // PACK_TILE) * (PACK_TILE // LANES)
        blk = lax.slice_in_dim(packed, row, row + rows, axis=lead)
        flat = blk.reshape(head + (rows * LANES,))
        outs.append(lax.slice_in_dim(flat, 0, n, axis=lead).reshape(head + tuple(shp)))
        row += rows
    return outs


def _all_gather(x, name):
    rows = x.shape[0]

    def body(x_ref, out_ref, send_sems, recv_sems, local_sem):
        mx, my, mc = lax.axis_index("x"), lax.axis_index("y"), lax.axis_index("c")
        me, sibling = (mx, my, mc), (mx, my, 1 - mc)
        chips = [(1 - mx, my), (mx, 1 - my), (1 - mx, 1 - my)]

        def slot(px, py, pc):
            return out_ref.at[4 * px + 2 * py + pc]

        def copy(k, block, to, src=None):
            return pltpu.make_async_remote_copy(
                src_ref=slot(*block) if src is None else src, dst_ref=slot(*block),
                send_sem=send_sems.at[k], recv_sem=recv_sems.at[k], device_id=to, device_id_type=MESH)

        mine = pltpu.make_async_copy(x_ref, slot(*me), local_sem)
        mine.start()
        first = [copy(0, me, sibling, src=x_ref)]
        first += [copy(1 + j, me, (*chip, mc), src=x_ref) for j, chip in enumerate(chips)]
        for cp in first:
            cp.start()
        passed = [copy(4 + j, (*chip, mc), sibling) for j, chip in enumerate(chips)]
        for j, chip in enumerate(chips):
            copy(1 + j, (*chip, mc), me).wait_recv()
            passed[j].start()
        copy(0, sibling, me).wait_recv()
        for j, chip in enumerate(chips):
            copy(4 + j, (*chip, 1 - mc), me).wait_recv()
        for cp in first + passed:
            cp.wait_send()
        mine.wait()

    return pl.pallas_call(
        body, name=name,
        out_shape=jax.ShapeDtypeStruct((N_DEV, rows, LANES), x.dtype),
        in_specs=[pl.BlockSpec(memory_space=pl.ANY)],
        out_specs=pl.BlockSpec(memory_space=pl.ANY),
        scratch_shapes=[pltpu.SemaphoreType.DMA((7,)), pltpu.SemaphoreType.DMA((7,)), pltpu.SemaphoreType.DMA(())],
    )(x)


EXCHANGE_SEMS = [pltpu.SemaphoreType.DMA((N_DEV - 1,)), pltpu.SemaphoreType.DMA((N_DEV - 1,)), pltpu.SemaphoreType.DMA(())]


def _direct_exchange(src_ref, out_ref, send_sems, recv_sems, local_sem, per_peer):
    mx, my, mc = lax.axis_index("x"), lax.axis_index("y"), lax.axis_index("c")
    me = 4 * mx + 2 * my + mc

    def flip(v, bit):
        return 1 - v if bit else v

    def copies():
        mine = pltpu.make_async_copy(src_ref.at[me] if per_peer else src_ref, out_ref.at[me], local_sem)
        remote = []
        for k in range(1, N_DEV):
            px, py, pc = flip(mx, k & 4), flip(my, k & 2), flip(mc, k & 1)
            remote.append(pltpu.make_async_remote_copy(
                src_ref=src_ref.at[4 * px + 2 * py + pc] if per_peer else src_ref, dst_ref=out_ref.at[me],
                send_sem=send_sems.at[k - 1], recv_sem=recv_sems.at[k - 1],
                device_id=(px, py, pc), device_id_type=MESH))
        return mine, remote

    def start():
        mine, remote = copies()
        mine.start()
        for cp in remote:
            cp.start()

    def wait():
        mine, remote = copies()
        for cp in remote:
            cp.wait_recv()
        for cp in remote:
            cp.wait_send()
        mine.wait()

    return start, wait


def _grad_exchange(g, name):
    rows = g.shape[1]

    def body(g_ref, out_ref, send_sems, recv_sems, local_sem):
        start, wait = _direct_exchange(g_ref, out_ref, send_sems, recv_sems, local_sem, per_peer=True)
        start()
        wait()

    return pl.pallas_call(
        body, name=name,
        out_shape=jax.ShapeDtypeStruct((N_DEV, rows, LANES), g.dtype),
        in_specs=[pl.BlockSpec(memory_space=pl.ANY)],
        out_specs=pl.BlockSpec(memory_space=pl.ANY),
        scratch_shapes=EXCHANGE_SEMS,
    )(g)


def _sum_adamw(parts, w, m, v, name):
    rows = w.shape[0]
    c1 = 1.0 - ADAM_B1 ** ADAM_STEP
    c2 = 1.0 - ADAM_B2 ** ADAM_STEP

    def body(p_ref, w_ref, m_ref, v_ref, g_out, d_out, m_out, v_out):
        g = p_ref[0].astype(F32)
        for q in range(1, N_DEV):
            g = g + p_ref[q].astype(F32)
        m_new = ADAM_B1 * m_ref[...] + (1.0 - ADAM_B1) * g
        v_new = ADAM_B2 * v_ref[...] + (1.0 - ADAM_B2) * (g * g)
        m_hat = m_new / c1
        v_hat = v_new / c2
        g_out[...] = g
        d_out[...] = -ADAM_LR * (m_hat / (jnp.sqrt(v_hat) + ADAM_EPS) + ADAM_WD * w_ref[...])
        m_out[...] = m_new
        v_out[...] = v_new

    blk = pl.BlockSpec((PACK_ROWS, LANES), lambda i: (i, 0))
    out = jax.ShapeDtypeStruct((rows, LANES), F32)
    return pl.pallas_call(
        body, name=name, grid=(rows // PACK_ROWS,),
        in_specs=[pl.BlockSpec((N_DEV, PACK_ROWS, LANES), lambda i: (0, i, 0)), blk, blk, blk],
        out_specs=[blk, blk, blk, blk], out_shape=[out, out, out, out],
        compiler_params=_params("parallel"),
    )(parts, w, m, v)


def _tile(dim, cap=MM_TILE):
    if dim <= cap:
        return dim
    for t in range(cap, LANES - 1, -LANES):
        if dim % t == 0:
            return t
    return dim


def _mm_call(a, b, form, name):
    if form == 'nn':
        (m, k), n = a.shape, b.shape[1]
    elif form == 'nt':
        (m, k), n = a.shape, b.shape[0]
    else:
        (k, m), n = a.shape, b.shape[1]
    tk = k if (form == 'nn' and k <= MM_K_WHOLE) else _tile(k, MM_TILE_WIDE)
    tm = _tile(m, MM_TILE_WIDE if form == 'tn' else MM_TILE)
    tn = _tile(n, MM_TILE_WIDE)
    nk = k // tk
    contract = {'nn': ((1,), (0,)), 'nt': ((1,), (1,)), 'tn': ((0,), (0,))}[form]

    def body(a_ref, b_ref, o_ref):
        part = lax.dot_general(a_ref[...].astype(BF16), b_ref[...].astype(BF16), (contract, ((), ())),
                               preferred_element_type=F32)
        if nk == 1:
            o_ref[...] = part
        else:
            @pl.when(pl.program_id(2) == 0)
            def _():
                o_ref[...] = part

            @pl.when(pl.program_id(2) > 0)
            def _():
                o_ref[...] += part

    a_spec = pl.BlockSpec((tk, tm), lambda j, i, l: (l, i)) if form == 'tn' else pl.BlockSpec((tm, tk), lambda j, i, l: (i, l))
    b_spec = pl.BlockSpec((tn, tk), lambda j, i, l: (j, l)) if form == 'nt' else pl.BlockSpec((tk, tn), lambda j, i, l: (l, j))
    return pl.pallas_call(
        body, name=name, grid=(n // tn, m // tm, nk),
        in_specs=[a_spec, b_spec], out_specs=pl.BlockSpec((tm, tn), lambda j, i, l: (i, j)),
        out_shape=jax.ShapeDtypeStruct((m, n), F32),
        compiler_params=_params("parallel", "parallel", "arbitrary"),
    )(a, b)


def _make_mm(name):
    @jax.custom_vjp
    def mm(a, b):
        return _mm_call(a, b.astype(BF16), 'nn', name + '_fwd')

    def fwd(a, b):
        b16 = b.astype(BF16)
        return _mm_call(a, b16, 'nn', name + '_fwd'), (a, b16)

    def bwd(res, g):
        a, b16 = res
        return _mm_call(g, b16, 'nt', name + '_da'), _mm_call(a, g, 'tn', name + '_db')

    mm.defvjp(fwd, bwd)
    return mm


def _rms_fwd_call(x, g, name):
    m, d = x.shape
    tm = _tile(m)

    def body(x_ref, g_ref, o_ref):
        xv = x_ref[...]
        rinv = lax.rsqrt(jnp.mean(xv * xv, axis=-1, keepdims=True) + NORM_EPS)
        o_ref[...] = xv * rinv * g_ref[...]

    return pl.pallas_call(
        body, name=name, grid=(m // tm,),
        in_specs=[pl.BlockSpec((tm, d), lambda i: (i, 0)), pl.BlockSpec((1, d), lambda i: (0, 0))],
        out_specs=pl.BlockSpec((tm, d), lambda i: (i, 0)), out_shape=jax.ShapeDtypeStruct((m, d), F32),
        compiler_params=_params("parallel"),
    )(x, g)


def _rms_bwd_call(x, g, dy, name):
    m, d = x.shape
    tm = _tile(m)

    def body(x_ref, g_ref, dy_ref, dx_ref, dg_ref):
        @pl.when(pl.program_id(0) == 0)
        def _():
            dg_ref[...] = jnp.zeros_like(dg_ref)

        xv, dyv = x_ref[...], dy_ref[...]
        rinv = lax.rsqrt(jnp.mean(xv * xv, axis=-1, keepdims=True) + NORM_EPS)
        xh = xv * rinv
        dg_ref[...] += jnp.sum(dyv * xh, axis=0, keepdims=True)
        dxh = dyv * g_ref[...]
        dx_ref[...] = rinv * (dxh - xh * jnp.mean(dxh * xh, axis=-1, keepdims=True))

    return pl.pallas_call(
        body, name=name, grid=(m // tm,),
        in_specs=[pl.BlockSpec((tm, d), lambda i: (i, 0)), pl.BlockSpec((1, d), lambda i: (0, 0)),
                  pl.BlockSpec((tm, d), lambda i: (i, 0))],
        out_specs=[pl.BlockSpec((tm, d), lambda i: (i, 0)), pl.BlockSpec((1, d), lambda i: (0, 0))],
        out_shape=[jax.ShapeDtypeStruct((m, d), F32), jax.ShapeDtypeStruct((1, d), F32)],
        compiler_params=_params("arbitrary"),
    )(x, g, dy)


def _make_rms(name):
    @jax.custom_vjp
    def rms(x, g):
        return _rms_fwd_call(x, g, name + '_fwd')

    def fwd(x, g):
        return _rms_fwd_call(x, g, name + '_fwd'), (x, g)

    def bwd(res, dy):
        x, g = res
        dx, dg = _rms_bwd_call(x, g, dy, name + '_bwd')
        return dx, dg

    rms.defvjp(fwd, bwd)
    return rms


def _time_shifts(x):
    t = x.shape[0]
    rows = lax.broadcasted_iota(jnp.int32, x.shape, 0)
    return (jnp.where(rows == 0, 0.0, pltpu.roll(x, 1, 0)), jnp.where(rows == t - 1, 0.0, pltpu.roll(x, t - 1, 0)))


def _conv3(x, cw_ref, cb_ref):
    xp, xn = _time_shifts(x)
    return cw_ref[0:1, :] * xp + cw_ref[1:2, :] * x + cw_ref[2:3, :] * xn + cb_ref[...]


def _glu_specs(b, t, f):
    tc = _tile(f, FFN_COLS)
    seq = pl.BlockSpec((1, t, tc), lambda j, bi: (bi, 0, j))
    cw = pl.BlockSpec((3, tc), lambda j, bi: (0, j))
    cb = pl.BlockSpec((1, tc), lambda j, bi: (0, j))
    return tc, seq, cw, cb


def _glu_fwd_call(ug, uv, cwg, cwv, cbg, cbv):
    b, t, f = ug.shape
    tc, seq, cw, cb = _glu_specs(b, t, f)

    def body(ug_ref, uv_ref, cwg_ref, cwv_ref, cbg_ref, cbv_ref, o_ref):
        g = _conv3(ug_ref[0], cwg_ref, cbg_ref)
        o_ref[0] = g * jax.nn.sigmoid(g) * _conv3(uv_ref[0], cwv_ref, cbv_ref)

    return pl.pallas_call(
        body, name='glu_fwd', grid=(f // tc, b), in_specs=[seq, seq, cw, cw, cb, cb], out_specs=seq,
        out_shape=jax.ShapeDtypeStruct((b, t, f), F32), compiler_params=_params("parallel", "parallel"),
    )(ug, uv, cwg, cwv, cbg, cbv)


def _glu_bwd_call(ug, uv, cwg, cwv, cbg, cbv, dact):
    b, t, f = ug.shape
    tc, seq, cw, cb = _glu_specs(b, t, f)

    def body(ug_ref, uv_ref, cwg_ref, cwv_ref, cbg_ref, cbv_ref, da_ref,
             dug_ref, duv_ref, dcwg_ref, dcwv_ref, dcbg_ref, dcbv_ref):
        @pl.when(pl.program_id(1) == 0)
        def _():
            for ref in (dcwg_ref, dcwv_ref, dcbg_ref, dcbv_ref):
                ref[...] = jnp.zeros_like(ref)

        g = _conv3(ug_ref[0], cwg_ref, cbg_ref)
        v = _conv3(uv_ref[0], cwv_ref, cbv_ref)
        sig = jax.nn.sigmoid(g)
        da = da_ref[0]
        dv = da * (g * sig)
        dg = da * v * (sig * (1.0 + g * (1.0 - sig)))

        def conv_bwd(dc, x_ref, cw_ref, dx_ref, dcw_ref, dcb_ref):
            dcp, dcn = _time_shifts(dc)
            dx_ref[0] = cw_ref[0:1, :] * dcn + cw_ref[1:2, :] * dc + cw_ref[2:3, :] * dcp
            x = x_ref[0]
            xp, xn = _time_shifts(x)
            for n, xs in enumerate((xp, x, xn)):
                dcw_ref[n:n + 1, :] += jnp.sum(dc * xs, axis=0, keepdims=True)
            dcb_ref[...] += jnp.sum(dc, axis=0, keepdims=True)

        conv_bwd(dg, ug_ref, cwg_ref, dug_ref, dcwg_ref, dcbg_ref)
        conv_bwd(dv, uv_ref, cwv_ref, duv_ref, dcwv_ref, dcbv_ref)

    big = jax.ShapeDtypeStruct((b, t, f), F32)
    return pl.pallas_call(
        body, name='glu_bwd', grid=(f // tc, b), in_specs=[seq, seq, cw, cw, cb, cb, seq],
        out_specs=[seq, seq, cw, cw, cb, cb],
        out_shape=[big, big, jax.ShapeDtypeStruct((3, f), F32), jax.ShapeDtypeStruct((3, f), F32),
                   jax.ShapeDtypeStruct((1, f), F32), jax.ShapeDtypeStruct((1, f), F32)],
        compiler_params=_params("parallel", "arbitrary"),
    )(ug, uv, cwg, cwv, cbg, cbv, dact)


def _shift_call(z, mu_p, mu_n, dzs=None):
    b, t, c = z.shape
    tc = _tile(c, SHIFT_COLS)
    seq = pl.BlockSpec((1, t, tc), lambda j, bi: (bi, 0, j))
    row = pl.BlockSpec((1, tc), lambda j, bi: (0, j))

    def fwd_body(z_ref, mp_ref, mn_ref, o_ref):
        x = z_ref[0]
        xp, xn = _time_shifts(x)
        o_ref[0] = x + mp_ref[...] * (xp - x) + mn_ref[...] * (xn - x)

    def bwd_body(z_ref, mp_ref, mn_ref, d_ref, dz_ref, dmp_ref, dmn_ref):
        @pl.when(pl.program_id(1) == 0)
        def _():
            dmp_ref[...] = jnp.zeros_like(dmp_ref)
            dmn_ref[...] = jnp.zeros_like(dmn_ref)

        x, d = z_ref[0], d_ref[0]
        xp, xn = _time_shifts(x)
        dp, dn = _time_shifts(d)
        mp, mn = mp_ref[...], mn_ref[...]
        dz_ref[0] = d * (1.0 - mp - mn) + mp * dn + mn * dp
        dmp_ref[...] += jnp.sum(d * (xp - x), axis=0, keepdims=True)
        dmn_ref[...] += jnp.sum(d * (xn - x), axis=0, keepdims=True)

    if dzs is None:
        return pl.pallas_call(
            fwd_body, name='shift_fwd', grid=(c // tc, b), in_specs=[seq, row, row], out_specs=seq,
            out_shape=jax.ShapeDtypeStruct(z.shape, F32), compiler_params=_params("parallel", "parallel"),
        )(z, mu_p, mu_n)
    return pl.pallas_call(
        bwd_body, name='shift_bwd', grid=(c // tc, b), in_specs=[seq, row, row, seq], out_specs=[seq, row, row],
        out_shape=[jax.ShapeDtypeStruct(z.shape, F32), jax.ShapeDtypeStruct(mu_p.shape, F32),
                   jax.ShapeDtypeStruct(mu_n.shape, F32)],
        compiler_params=_params("parallel", "arbitrary"),
    )(z, mu_p, mu_n, dzs)


@jax.custom_vjp
def _token_shift(z, mu_p, mu_n):
    return _shift_call(z, mu_p, mu_n)


_token_shift.defvjp(lambda z, mu_p, mu_n: (_shift_call(z, mu_p, mu_n), (z, mu_p, mu_n)),
                    lambda res, d: tuple(_shift_call(*res, dzs=d)))


@jax.custom_vjp
def _conv_glu(ug, uv, cwg, cwv, cbg, cbv):
    return _glu_fwd_call(ug, uv, cwg, cwv, cbg, cbv)


def _conv_glu_fwd(*args):
    return _glu_fwd_call(*args), args


def _conv_glu_bwd(res, dact):
    return tuple(_glu_bwd_call(*res, dact))


_conv_glu.defvjp(_conv_glu_fwd, _conv_glu_bwd)


def _scores(q_ref, k_ref):
    q16 = (q_ref[0] * MLA_SCALE).astype(BF16)
    return lax.dot_general(q16, k_ref[0].astype(BF16), (((1,), (1,)), ((), ())), preferred_element_type=F32), q16


def _attn_fwd_call(q, k, v, name):
    bh, t, _ = q.shape
    tq = min(ATT_TQ, t)

    def body(q_ref, k_ref, v_ref, o_ref, lse_ref):
        s, _ = _scores(q_ref, k_ref)
        m = jnp.max(s, axis=-1, keepdims=True)
        p = jnp.exp(s - m)
        l = jnp.sum(p, axis=-1, keepdims=True)
        o_ref[0] = jnp.dot(p.astype(BF16), v_ref[0].astype(BF16), preferred_element_type=F32) / l
        lse_ref[0] = m + jnp.log(l)

    return pl.pallas_call(
        body, name=name, grid=(bh, t // tq),
        in_specs=[pl.BlockSpec((1, tq, D_QK), lambda b, i: (b, i, 0)), pl.BlockSpec((1, t, D_QK), lambda b, i: (b, 0, 0)),
                  pl.BlockSpec((1, t, D_V), lambda b, i: (b, 0, 0))],
        out_specs=[pl.BlockSpec((1, tq, D_V), lambda b, i: (b, i, 0)), pl.BlockSpec((1, tq, 1), lambda b, i: (b, i, 0))],
        out_shape=[jax.ShapeDtypeStruct((bh, t, D_V), F32), jax.ShapeDtypeStruct((bh, t, 1), F32)],
        compiler_params=_params("parallel", "parallel"),
    )(q, k, v)


def _attn_bwd_call(q, k, v, o, lse, do, name):
    bh, t, _ = q.shape
    tq = min(ATT_TQ, t)

    def body(q_ref, k_ref, v_ref, o_ref, lse_ref, do_ref, dq_ref, dk_ref, dv_ref):
        @pl.when(pl.program_id(1) == 0)
        def _():
            dk_ref[...] = jnp.zeros_like(dk_ref)
            dv_ref[...] = jnp.zeros_like(dv_ref)

        s, q16 = _scores(q_ref, k_ref)
        p = jnp.exp(s - lse_ref[0])
        dov = do_ref[0]
        do_b = dov.astype(BF16)
        delta = jnp.sum(dov * o_ref[0], axis=-1, keepdims=True)
        dp = lax.dot_general(do_b, v_ref[0].astype(BF16), (((1,), (1,)), ((), ())), preferred_element_type=F32)
        ds = (p * (dp - delta)).astype(BF16)
        dq_ref[0] = jnp.dot(ds, k_ref[0].astype(BF16), preferred_element_type=F32) * MLA_SCALE
        dk_ref[0] += lax.dot_general(ds, q16, (((0,), (0,)), ((), ())), preferred_element_type=F32)
        dv_ref[0] += lax.dot_general(p.astype(BF16), do_b, (((0,), (0,)), ((), ())), preferred_element_type=F32)

    qspec = pl.BlockSpec((1, tq, D_QK), lambda b, i: (b, i, 0))
    kspec = pl.BlockSpec((1, t, D_QK), lambda b, i: (b, 0, 0))
    vspec = pl.BlockSpec((1, t, D_V), lambda b, i: (b, 0, 0))
    ospec = pl.BlockSpec((1, tq, D_V), lambda b, i: (b, i, 0))
    lspec = pl.BlockSpec((1, tq, 1), lambda b, i: (b, i, 0))
    return pl.pallas_call(
        body, name=name, grid=(bh, t // tq),
        in_specs=[qspec, kspec, vspec, ospec, lspec, ospec], out_specs=[qspec, kspec, vspec],
        out_shape=[jax.ShapeDtypeStruct(q.shape, F32), jax.ShapeDtypeStruct(k.shape, F32), jax.ShapeDtypeStruct(v.shape, F32)],
        compiler_params=_params("parallel", "arbitrary"),
    )(q, k, v, o, lse, do)


@jax.custom_vjp
def _attention(q, k, v):
    return _attn_fwd_call(q, k, v, 'attn_fwd')[0]


def _attention_fwd(q, k, v):
    o, lse = _attn_fwd_call(q, k, v, 'attn_fwd')
    return o, (q, k, v, o, lse)


def _attention_bwd(res, do):
    return tuple(_attn_bwd_call(*res, do, 'attn_bwd'))


_attention.defvjp(_attention_fwd, _attention_bwd)


SROWS = N * D_RWKV // SEG


def _seg_ones():
    r = lax.broadcasted_iota(jnp.int32, (SEG, SEG), 0) >> 6
    c = lax.broadcasted_iota(jnp.int32, (SEG, SEG), 1) >> 6
    return (r == c).astype(BF16)


def _eye_mask():
    r = lax.broadcasted_iota(jnp.int32, (SROWS, SEG), 0) & (N - 1)
    c = lax.broadcasted_iota(jnp.int32, (SROWS, SEG), 1) & (N - 1)
    return r == c


def _row2(ref, bi, ti, dtype=F32):
    parts = [jnp.broadcast_to(ref[bi, pl.ds(ti, 1), pl.ds(SEG * q, SEG)].astype(dtype), (N, SEG))
             for q in range(D_RWKV // SEG)]
    return jnp.concatenate(parts, axis=0)


def _split2(x):
    hi = x.astype(BF16)
    return hi, (x - hi.astype(F32)).astype(BF16)


def _col_sum(x):
    return jnp.concatenate([jnp.sum(x[N * q:N * (q + 1)], axis=0, keepdims=True) for q in range(D_RWKV // SEG)], axis=1)


def _scan_specs(b, t, rev):
    nc = t // SCAN_CHUNK
    if rev:
        return (pl.BlockSpec((b, SCAN_CHUNK, D_RWKV), lambda c: (0, nc - 1 - c, 0)),
                pl.BlockSpec((b, SCAN_CHUNK, SROWS, SEG), lambda c: (0, nc - 1 - c, 0, 0)))
    return (pl.BlockSpec((b, SCAN_CHUNK, D_RWKV), lambda c: (0, c, 0)),
            pl.BlockSpec((b, SCAN_CHUNK, SROWS, SEG), lambda c: (0, c, 0, 0)))


def _scan_fwd_call(r, v, kk, wf, kf, qf, wb, kb, qb, ride):
    b, t, _ = r.shape
    n_chunks = t // SCAN_CHUNK
    last = SCAN_CHUNK - 1

    def body(rf, vf, kkf, wf_, kf_, qf_, rb, vb, kkb, wb_, kb_, qb_, ride_ref, yf, yb, sf, sb, land_ref, *scratch):
        states = scratch[:2 * b]
        send, arrive = _direct_exchange(ride_ref, land_ref, *scratch[2 * b:], per_peer=False)

        @pl.when(pl.program_id(0) == 0)
        def _():
            send()
            for st in states:
                st[...] = jnp.zeros_like(st)

        ones, mask = _seg_ones(), _eye_mask()
        zero16 = jnp.zeros((), BF16)
        chains = []
        for bi in range(b):
            chains.append((rf, vf, kkf, wf_, kf_, qf_, yf, sf, states[2 * bi], bi, False))
            chains.append((rb, vb, kkb, wb_, kb_, qb_, yb, sb, states[2 * bi + 1], bi, True))

        def tix(i, rev):
            return last - i if rev else i

        def put_y(y_, bi, ti, ycol):
            y_[bi, pl.ds(ti, 1), :] = _col_sum(jnp.where(mask, ycol, 0.0))

        def steps(i, with_y):
            parts = []
            for (r_, v_, kk_, w_, k_, q_, y_, s_, st, bi, rev) in chains:
                ti = tix(i, rev)
                s = st[...]
                s_[bi, ti] = s
                parts.append((s * _row2(kk_, bi, ti)).astype(BF16))
                parts.append(jnp.where(mask, _row2(v_, bi, ti, BF16), zero16))
                if with_y:
                    parts.append((s * _row2(r_, bi, tix(i - 1, rev))).astype(BF16))
            res = jnp.dot(jnp.concatenate(parts, axis=0), ones, preferred_element_type=F32)
            off = 0
            for (r_, v_, kk_, w_, k_, q_, y_, s_, st, bi, rev) in chains:
                ti = tix(i, rev)
                u = res[off:off + SROWS]
                vcol = res[off + SROWS:off + 2 * SROWS]
                off += 2 * SROWS
                if with_y:
                    put_y(y_, bi, tix(i - 1, rev), res[off:off + SROWS])
                    off += SROWS
                st[...] = st[...] * _row2(w_, bi, ti) - u * _row2(q_, bi, ti) + vcol * _row2(k_, bi, ti)

        steps(0, False)

        def loop(i, carry):
            steps(i, True)
            return carry

        lax.fori_loop(1, SCAN_CHUNK, loop, 0)
        parts = [(c[8][...] * _row2(c[0], c[9], tix(last, c[10]))).astype(BF16) for c in chains]
        res = jnp.dot(jnp.concatenate(parts, axis=0), ones, preferred_element_type=F32)
        for n, c in enumerate(chains):
            put_y(c[6], c[9], tix(last, c[10]), res[n * SROWS:(n + 1) * SROWS])

        @pl.when(pl.program_id(0) == n_chunks - 1)
        def _():
            arrive()

    fr, fs = _scan_specs(b, t, False)
    br, bs = _scan_specs(b, t, True)
    hbm = pl.BlockSpec(memory_space=pl.ANY)
    y_shape = jax.ShapeDtypeStruct((b, t, D_RWKV), F32)
    s_shape = jax.ShapeDtypeStruct((b, t, SROWS, SEG), F32)
    return pl.pallas_call(
        body, name='scan_fwd', grid=(n_chunks,),
        in_specs=[fr] * 6 + [br] * 6 + [hbm], out_specs=[fr, br, fs, bs, hbm],
        out_shape=[y_shape, y_shape, s_shape, s_shape, jax.ShapeDtypeStruct((N_DEV,) + ride.shape, ride.dtype)],
        scratch_shapes=[pltpu.VMEM((SROWS, SEG), F32)] * (2 * b) + EXCHANGE_SEMS,
        compiler_params=_params("arbitrary"),
    )(r, v, kk, wf, kf, qf, r, v, kk, wb, kb, qb, ride)


def _scan_bwd_call(r, v, kk, wf, kf, qf, wb, kb, qb, sf, sb, dyf, dyb, ride):
    b, t, _ = r.shape
    n_chunks = t // SCAN_CHUNK
    last = SCAN_CHUNK - 1

    def body(rf, vf, kkf, wf_, kf_, qf_, sf_, dyf_, rb, vb, kkb, wb_, kb_, qb_, sb_, dyb_, ride_ref,
             drf, dvf, dkkf, dwf, dkf, dqf, drb, dvb, dkkb, dwb, dkb, dqb, land_ref, *scratch):
        send, arrive = _direct_exchange(ride_ref, land_ref, *scratch[8 * b:], per_peer=True)

        @pl.when(pl.program_id(0) == 0)
        def _():
            send()
            for n in range(2 * b):
                scratch[4 * n][...] = jnp.zeros_like(scratch[4 * n])

        ones, mask = _seg_ones(), _eye_mask()
        zero16 = jnp.zeros((), BF16)
        chains = []
        for bi in range(b):
            chains.append((rf, vf, kkf, wf_, kf_, qf_, sf_, dyf_, (drf, dvf, dkkf, dwf, dkf, dqf),
                           scratch[8 * bi:8 * bi + 4], bi, True))
            chains.append((rb, vb, kkb, wb_, kb_, qb_, sb_, dyb_, (drb, dvb, dkkb, dwb, dkb, dqb),
                           scratch[8 * bi + 4:8 * bi + 8], bi, False))

        def tix(i, rev):
            return last - i if rev else i

        def state_free_parts(v_, dy_, kk_, s_, bi, ti):
            return [jnp.where(mask, _row2(v_, bi, ti, BF16), zero16), jnp.where(mask, _row2(dy_, bi, ti, BF16), zero16),
                    (s_[bi, ti] * _row2(kk_, bi, ti)).astype(BF16)]

        def keep(scr, res, off):
            for n in range(3):
                scr[1 + n][...] = res[off + n * SROWS:off + (n + 1) * SROWS]
            return off + 3 * SROWS

        def first():
            parts = []
            for (r_, v_, kk_, w_, k_, q_, s_, dy_, outs, scr, bi, rev) in chains:
                parts += state_free_parts(v_, dy_, kk_, s_, bi, tix(0, rev))
            res = jnp.dot(jnp.concatenate(parts, axis=0), ones, preferred_element_type=F32)
            off = 0
            for c in chains:
                off = keep(c[9], res, off)

        def steps(i, has_next, recompute):
            parts = []
            for (r_, v_, kk_, w_, k_, q_, s_, dy_, outs, scr, bi, rev) in chains:
                ti = tix(i, rev)
                gst, vc, dc, uc = scr
                dycol = dc[...]
                if recompute:
                    sc = s_[bi, ti] * _row2(w_, bi, ti) - uc[...] * _row2(q_, bi, ti) + vc[...] * _row2(k_, bi, ti)
                else:
                    sc = s_[bi, tix(i - 1, rev)]
                outs[0][bi, pl.ds(ti, 1), :] = _col_sum(sc * dycol)
                g = gst[...] + dycol * _row2(r_, bi, ti)
                gst[...] = g
                parts.append((g * _row2(q_, bi, ti)).astype(BF16))
                parts.append((g * _row2(k_, bi, ti)).astype(BF16))
                if has_next:
                    parts += state_free_parts(v_, dy_, kk_, s_, bi, tix(i + 1, rev))
            res = jnp.dot(jnp.concatenate(parts, axis=0), ones, preferred_element_type=F32)
            off = 0
            for (r_, v_, kk_, w_, k_, q_, s_, dy_, outs, scr, bi, rev) in chains:
                ti = tix(i, rev)
                gst, vc, dc, uc = scr
                dr_, dv_, dkk_, dw_, dk_, dq_ = outs

                def put(ref, val, sign=1.0):
                    ref[bi, pl.ds(ti, 1), :] = sign * _col_sum(val)

                gq = res[off:off + SROWS]
                put(dv_, jnp.where(mask, res[off + SROWS:off + 2 * SROWS], 0.0))
                off += 2 * SROWS
                g, sp = gst[...], s_[bi, ti]
                put(dk_, g * vc[...])
                put(dw_, g * sp)
                put(dq_, g * uc[...], -1.0)
                put(dkk_, sp * gq, -1.0)
                gst[...] = g * _row2(w_, bi, ti) - gq * _row2(kk_, bi, ti)
                if has_next:
                    off = keep(scr, res, off)

        first()
        steps(0, True, True)

        def loop(i, carry):
            steps(i, True, False)
            return carry

        lax.fori_loop(1, last, loop, 0)
        steps(last, False, False)

        @pl.when(pl.program_id(0) == n_chunks - 1)
        def _():
            arrive()

    fr, fs = _scan_specs(b, t, True)
    br, bs = _scan_specs(b, t, False)
    hbm = pl.BlockSpec(memory_space=pl.ANY)
    y_shape = jax.ShapeDtypeStruct((b, t, D_RWKV), F32)
    return pl.pallas_call(
        body, name='scan_bwd', grid=(n_chunks,),
        in_specs=[fr] * 6 + [fs, fr] + [br] * 6 + [bs, br] + [hbm],
        out_specs=[fr] * 6 + [br] * 6 + [hbm], out_shape=[y_shape] * 12 + [jax.ShapeDtypeStruct(ride.shape, ride.dtype)],
        scratch_shapes=[pltpu.VMEM((SROWS, SEG), F32)] * (8 * b) + EXCHANGE_SEMS,
        compiler_params=_params("arbitrary"),
    )(r, v, kk, wf, kf, qf, sf, dyf, r, v, kk, wb, kb, qb, sb, dyb, ride)


def _rope_tables(t):
    inv_freq = jnp.power(ROPE_THETA, -jnp.arange(0, D_ROPE, 2, dtype=F32) / D_ROPE)
    ang = jnp.arange(t, dtype=F32)[:, None] * inv_freq[None, :]
    ang = jnp.concatenate([ang, ang], axis=-1)
    return jnp.cos(ang), jnp.sin(ang)


def _rope(x, cos, sin):
    x1, x2 = jnp.split(x, 2, axis=-1)
    return x * cos + jnp.concatenate([-x2, x1], axis=-1) * sin


@jax.custom_vjp
def _dot16(a, w):
    return jnp.dot(a.astype(BF16), w.astype(BF16), preferred_element_type=F32)


def _dot16_fwd(a, w):
    a16, w16 = a.astype(BF16), w.astype(BF16)
    return jnp.dot(a16, w16, preferred_element_type=F32), (a16, w16)


def _dot16_bwd(res, g):
    a16, w16 = res
    g16 = g.astype(BF16)
    return (lax.dot_general(g16, w16, (((1,), (1,)), ((), ())), preferred_element_type=F32),
            lax.dot_general(a16, g16, (((0,), (0,)), ((), ())), preferred_element_type=F32))


_dot16.defvjp(_dot16_fwd, _dot16_bwd)


def _head_sum_tile(x):
    outs = []
    ones = _seg_ones()
    for q in range(x.shape[1] // SEG):
        hi, lo = _split2(x[:, SEG * q:SEG * (q + 1)])
        outs.append(jnp.dot(hi, ones, preferred_element_type=F32) + jnp.dot(lo, ones, preferred_element_type=F32))
    return jnp.concatenate(outs, axis=1)


@jax.custom_vjp
def _hsum(x):
    return _head_sum_tile(x)


_hsum.defvjp(lambda x: (_head_sum_tile(x), None), lambda _, g: (_head_sum_tile(g),))


def _softplus(x):
    return jnp.maximum(x, 0.0) + jnp.log(1.0 + jnp.exp(-jnp.abs(x)))


def _rwkv_pre_fn(k, wdf, wdb, adf, adb, gd, w0f, w2f, w0b, w2b, a0f, a2f, a0b, a2b, g2, k_k, k_a):
    w_f = jnp.exp(-jnp.exp(-_softplus(-(w0f + _dot16(jnp.tanh(wdf), w2f))) - 0.5))
    w_b = jnp.exp(-jnp.exp(-_softplus(-(w0b + _dot16(jnp.tanh(wdb), w2b))) - 0.5))
    a_f = jax.nn.sigmoid(a0f + _dot16(adf, a2f))
    a_b = jax.nn.sigmoid(a0b + _dot16(adb, a2b))
    gate = _dot16(jax.nn.sigmoid(gd), g2)
    kk = k * k_k
    kk = kk / jnp.maximum(jnp.sqrt(_hsum(kk * kk)), L2_EPS)
    return (kk, w_f, k * (1.0 + (a_f - 1.0) * k_a), kk * a_f, w_b, k * (1.0 + (a_b - 1.0) * k_a), kk * a_b, gate)


def _rwkv_post_fn(y_f, y_b, r, k_f, k_b, v, gate, ln_g, ln_b, r_k):
    y = y_f + y_b
    yc = y - _hsum(y) * (1.0 / N)
    var = _hsum(yc * yc) * (1.0 / N)
    y = yc * lax.rsqrt(var + GN_EPS) * ln_g + ln_b
    return ((y + _hsum(r * (k_f + k_b) * r_k) * v) * gate,)


def _make_rowwise(fn, name, n_rows, tm):
    def specs(arrs, whole):
        if whole:
            return [pl.BlockSpec(a.shape, lambda i: (0, 0)) for a in arrs]
        return [pl.BlockSpec((tm, a.shape[1]), lambda i: (i, 0)) for a in arrs]

    def out_widths(rows, params):
        tiles = [jax.ShapeDtypeStruct((tm, a.shape[1]), F32) for a in rows]
        return [o.shape[1] for o in jax.eval_shape(fn, *tiles, *params)]

    def fwd_call(rows, params):
        m = rows[0].shape[0]
        n_in = len(rows) + len(params)
        outs = [jax.ShapeDtypeStruct((m, d), F32) for d in out_widths(rows, params)]

        def body(*refs):
            for o_ref, o in zip(refs[n_in:], fn(*[ref[...] for ref in refs[:n_in]])):
                o_ref[...] = o

        return pl.pallas_call(
            body, name=name + '_fwd', grid=(m // tm,), in_specs=specs(rows, False) + specs(params, True),
            out_specs=specs(outs, False), out_shape=outs, compiler_params=_params("parallel"),
        )(*rows, *params)

    def bwd_call(rows, params, cts):
        m = rows[0].shape[0]
        n_in = len(rows) + len(params)
        n_all = n_in + len(cts)
        outs = ([jax.ShapeDtypeStruct(a.shape, F32) for a in rows] + [jax.ShapeDtypeStruct(a.shape, F32) for a in params])

        def body(*refs):
            _, vjp = jax.vjp(fn, *[ref[...] for ref in refs[:n_in]])
            grads = vjp(tuple(ref[...] for ref in refs[n_in:n_all]))
            d_rows, d_params = refs[n_all:n_all + len(rows)], refs[n_all + len(rows):]
            for ref, g in zip(d_rows, grads[:len(rows)]):
                ref[...] = g

            @pl.when(pl.program_id(0) == 0)
            def _():
                for ref in d_params:
                    ref[...] = jnp.zeros_like(ref)

            for ref, g in zip(d_params, grads[len(rows):]):
                ref[...] += g

        return pl.pallas_call(
            body, name=name + '_bwd', grid=(m // tm,),
            in_specs=specs(rows, False) + specs(params, True) + specs(cts, False),
            out_specs=specs(rows, False) + specs(params, True), out_shape=outs, compiler_params=_params("arbitrary"),
        )(*rows, *params, *cts)

    @jax.custom_vjp
    def op(*args):
        return tuple(fwd_call(args[:n_rows], args[n_rows:]))

    def op_fwd(*args):
        return tuple(fwd_call(args[:n_rows], args[n_rows:])), args

    def op_bwd(args, cts):
        return tuple(bwd_call(args[:n_rows], args[n_rows:], cts))

    op.defvjp(op_fwd, op_bwd)
    return op


def _rwkv_operands(z, full, rep):
    b, t, _ = z.shape
    m = b * t
    z = _token_shift(z, rep['shift_mu_prev'], rep['shift_mu_next']).reshape(m, RWKV_COLS)
    r, k, v = z[:, :512], z[:, 512:1024], z[:, 1024:1536]
    lora_in = (z[:, 1536:1600], z[:, 1600:1664], z[:, 1664:1728], z[:, 1728:1792], z[:, 1792:1920])
    kk, w_f, k_f, q_f, w_b, k_b, q_b, gate = _make_rowwise(_rwkv_pre_fn, 'rwkv_pre', 6, _tile(m, ROW_TILE))(
        k, *lora_in, rep['decay_w0_fwd'], full['decay_w2_fwd'], rep['decay_w0_bwd'], full['decay_w2_bwd'],
        rep['iclr_a0_fwd'], full['iclr_a2_fwd'], rep['iclr_a0_bwd'], full['iclr_a2_bwd'], full['gate_g2'],
        rep['k_k'], rep['k_a'])
    return r, v, kk, w_f, k_f, q_f, w_b, k_b, q_b, gate


def _mla_mixer(z, full, rep, b, t):
    m = b * t
    c_q, c_kv, k_rope = z[:, :768], z[:, 768:1024], z[:, 1024:1056]
    cos, sin = _rope_tables(t)
    q = _make_mm('mm_uq')(_make_rms('rms_q')(c_q, rep['q_norm_g']), full['w_uq']).reshape(b, t, H, D_QK)
    q = jnp.concatenate([q[..., :D_NOPE], _rope(q[..., D_NOPE:], cos[:, None, :], sin[:, None, :])], axis=-1)
    kv = _make_mm('mm_ukv')(_make_rms('rms_kv')(c_kv, rep['kv_norm_g']), full['w_ukv']).reshape(b, t, H, D_NOPE + D_V)
    k_rope = _rope(k_rope.reshape(b, t, D_ROPE), cos, sin)
    k = jnp.concatenate([kv[..., :D_NOPE], jnp.broadcast_to(k_rope[:, :, None, :], (b, t, H, D_ROPE))], axis=-1)
    heads = lambda a: a.transpose(0, 2, 1, 3).reshape(b * H, t, a.shape[-1])
    o = _attention(heads(q), heads(k), heads(kv[..., D_NOPE:]))
    o = o.reshape(b, H, t, D_V).transpose(0, 2, 1, 3).reshape(m, H * D_V)
    return _make_rms('rms_mla_out')(o, rep['mla_out_g'])


def _before_scan(full, rep, x):
    b, t, d = x.shape
    m = b * t
    n1 = _make_rms('rms_mix')(x.reshape(m, d), rep['ln_mix_g'])
    d_in = full['w_in'].shape[1]
    d_in_pad = -(-d_in // MM_TILE) * MM_TILE
    z = _make_mm('mm_in')(n1, jnp.pad(full['w_in'], ((0, 0), (0, d_in_pad - d_in))))
    return (*_rwkv_operands(z[:, :RWKV_COLS].reshape(b, t, RWKV_COLS), full, rep),
            _mla_mixer(z[:, RWKV_COLS:d_in], full, rep, b, t))


def _after_scan(full, rep, x, target, y_f, y_b, r, k_f, k_b, v, gate, y_mla):
    b, t, d = x.shape
    m = b * t
    xf = x.reshape(m, d)
    y_rwkv = _make_rowwise(_rwkv_post_fn, 'rwkv_post', 7, _tile(m, ROW_TILE))(
        y_f.reshape(m, D_RWKV), y_b.reshape(m, D_RWKV), r, k_f, k_b, v, gate,
        rep['ln_x_g'], rep['ln_x_b'], rep['r_k'].reshape(1, D_RWKV))[0]
    h = xf + _make_mm('mm_out')(jnp.concatenate([y_rwkv, y_mla], axis=-1), full['w_out'])
    n2 = _make_rms('rms_ffn')(h, rep['ln_ffn_g'])
    w_up, cw, cb = full['w_ffn_up'], full['ffn_conv_w'], rep['ffn_conv_b']
    u_gate = _make_mm('mm_up_gate')(n2, w_up[:, :D_FF]).reshape(b, t, D_FF)
    u_val = _make_mm('mm_up_val')(n2, w_up[:, D_FF:]).reshape(b, t, D_FF)
    act = _conv_glu(u_gate, u_val, cw[:, :D_FF], cw[:, D_FF:], cb[:, :D_FF], cb[:, D_FF:]).reshape(m, D_FF)
    h = h + _make_mm('mm_down')(act, full['w_ffn_down'])
    out = _make_rms('rms_final')(h, rep['ln_final_g'])
    err = jnp.square(out - target.reshape(m, d))
    return 0.5 * jnp.sum(jnp.mean(err, axis=-1))


def _mat(a):
    if a.ndim == 1:
        return a.reshape(1, -1)
    if a.ndim == 3:
        return a.reshape(a.shape[1:])
    return a


def _join(shards, name):
    if name in ROW:
        return shards.reshape(-1, shards.shape[-1])
    return shards.transpose(1, 0, 2).reshape(shards.shape[1], -1)


def _cut(whole, name):
    r, c = whole.shape
    if name in ROW:
        return whole.reshape(N_DEV, r // N_DEV, c)
    return whole.reshape(r, N_DEV, c // N_DEV).transpose(1, 0, 2)


def kernel(x, ln_mix_g, w_in, shift_mu_prev, shift_mu_next, decay_w0_fwd, decay_w2_fwd, decay_w0_bwd, decay_w2_bwd, iclr_a0_fwd, iclr_a2_fwd, iclr_a0_bwd, iclr_a2_bwd, gate_g2, k_k, k_a, r_k, ln_x_g, ln_x_b, q_norm_g, w_uq, kv_norm_g, w_ukv, mla_out_g, w_out, ln_ffn_g, w_ffn_up, ffn_conv_w, ffn_conv_b, w_ffn_down, ln_final_g, loss_target, m_ln_mix_g, m_w_in, m_shift_mu_prev, m_shift_mu_next, m_decay_w0_fwd, m_decay_w2_fwd, m_decay_w0_bwd, m_decay_w2_bwd, m_iclr_a0_fwd, m_iclr_a2_fwd, m_iclr_a0_bwd, m_iclr_a2_bwd, m_gate_g2, m_k_k, m_k_a, m_r_k, m_ln_x_g, m_ln_x_b, m_q_norm_g, m_w_uq, m_kv_norm_g, m_w_ukv, m_mla_out_g, m_w_out, m_ln_ffn_g, m_w_ffn_up, m_ffn_conv_w, m_ffn_conv_b, m_w_ffn_down, m_ln_final_g, v_ln_mix_g, v_w_in, v_shift_mu_prev, v_shift_mu_next, v_decay_w0_fwd, v_decay_w2_fwd, v_decay_w0_bwd, v_decay_w2_bwd, v_iclr_a0_fwd, v_iclr_a2_fwd, v_iclr_a0_bwd, v_iclr_a2_bwd, v_gate_g2, v_k_k, v_k_a, v_r_k, v_ln_x_g, v_ln_x_b, v_q_norm_g, v_w_uq, v_kv_norm_g, v_w_ukv, v_mla_out_g, v_w_out, v_ln_ffn_g, v_w_ffn_up, v_ffn_conv_w, v_ffn_conv_b, v_w_ffn_down, v_ln_final_g):
    given = dict(locals())
    w = {n: given[n] for n in WNAMES}
    mom = {n: given['m_' + n] for n in WNAMES}
    var = {n: given['v_' + n] for n in WNAMES}

    def wire_pack(names):
        return _pack([lax.bitcast_convert_type(_mat(w[n]), BF16) if n in EXACT else _mat(w[n]).astype(BF16) for n in names])

    def whole(names, gathered):
        shapes = [_mat(w[n]).shape + ((2,) if n in EXACT else ()) for n in names]
        shards = _unpack(gathered, shapes, lead=1)
        return {n: _join(lax.bitcast_convert_type(s, F32) if n in EXACT else s.astype(F32), n) for n, s in zip(names, shards)}

    def grad_pack(names, grads):
        return _pack([_cut(grads[n], n).astype(BF16) for n in names], lead=1)

    early = [n for n in SHARDED if n not in LATE]
    rep = {n: _mat(w[n]) for n in REPLICATED}
    rep['r_k'] = w['r_k'].reshape(H, N)
    b, t, d = x.shape
    seq = lambda a: a.reshape(b, t, D_RWKV)
    flat = lambda a: a.reshape(b * t, D_RWKV)

    full_early = whole(early, _all_gather(wire_pack(early), 'gather_weights'))
    ops, vjp_before = jax.vjp(_before_scan, full_early, rep, x)
    r, v, kk, w_f, k_f, q_f, w_b, k_b, q_b, gate, y_mla = ops
    scan_in = [seq(a) for a in (r, v, kk, w_f, k_f, q_f, w_b, k_b, q_b)]
    y_f, y_b, s_f, s_b, late_gathered = _scan_fwd_call(*scan_in, wire_pack(LATE))
    full_late = whole(LATE, late_gathered)
    loss_local, vjp_after = jax.vjp(_after_scan, full_late, rep, x, loss_target, y_f, y_b, r, k_f, k_b, v, gate, y_mla)

    g_late, g_rep_after, g_x_after, _, d_yf, d_yb, d_r, d_kf, d_kb, d_v, d_gate, d_ymla = vjp_after(jnp.ones((), F32))
    *d_scan, parts_late = _scan_bwd_call(*scan_in, s_f, s_b, d_yf, d_yb, grad_pack(LATE, g_late))
    drf, dvf, dkkf, dwf, dkf, dqf, drb, dvb, dkkb, dwb, dkb, dqb = [flat(a) for a in d_scan]
    g_early, g_rep_before, g_x_before = vjp_before(
        (drf + drb + d_r, dvf + dvb + d_v, dkkf + dkkb, dwf, dkf + d_kf, dqf, dwb, dkb + d_kb, dqb, d_gate, d_ymla))
    g_rep = {n: g_rep_before[n] + g_rep_after[n] for n in rep}
    g_x = g_x_before + g_x_after
    parts_early = _grad_exchange(grad_pack(early, g_early), 'exchange_grads')

    s_out = [{}, {}, {}, {}]
    for names, parts, tag in ((early, parts_early, 'adamw_early'), (LATE, parts_late, 'adamw_late')):
        res = _sum_adamw(parts, _pack([w[n] for n in names]), _pack([mom[n] for n in names]),
                         _pack([var[n] for n in names]), tag)
        for kind, o in enumerate(res):
            s_out[kind].update(zip(names, _unpack(o, [w[n].shape for n in names])))

    zero = jnp.zeros((1,), F32)
    r_pack = _pack([g_rep[n] for n in REPLICATED] + [loss_local.reshape(1)])
    r_parts = _all_gather(r_pack, 'gather_small')
    r_out = _sum_adamw(r_parts, _pack([w[n] for n in REPLICATED] + [zero]), _pack([mom[n] for n in REPLICATED] + [zero]),
                       _pack([var[n] for n in REPLICATED] + [zero]), 'adamw_replicated')
    r_out = [_unpack(o, [w[n].shape for n in REPLICATED] + [(1,)]) for o in r_out]

    loss = r_out[0][-1].reshape(())
    outs = [loss, g_x]
    for kind in range(4):
        by_name = dict(s_out[kind])
        by_name.update(zip(REPLICATED, r_out[kind][:-1]))
        outs += [by_name[n] for n in WNAMES]
    return tuple(outs)
```

```python
import functools

import jax
import jax.numpy as jnp
from jax import lax
from jax.experimental import pallas as pl
from jax.experimental.pallas import tpu as pltpu

F32 = jnp.float32
BF16 = jnp.bfloat16
MESH = pl.DeviceIdType.MESH

N_DEV = 8
LANES = 128
SUBLANES = 8
PACK_TILE = 2 * SUBLANES * LANES
PACK_ROWS = 512
ADAM_ROWS = 256
MM_TILE = 512
MM_TILE_WIDE = 1408
MM_K_WHOLE = 2816
VMEM_LIMIT = 48 * 1024 * 1024

H = 8
N = 64
D_RWKV = H * N
D_NOPE, D_ROPE, D_V = 64, 32, 64
D_QK = D_NOPE + D_ROPE
MLA_SCALE = D_QK ** -0.5
ROPE_THETA = 10000.0
RWKV_COLS = 1920
MLA_COLS = 1056
D_FF = 2816
NORM_EPS = 1e-6
GN_EPS = 64e-5
L2_EPS = 1e-12
ADAM_LR, ADAM_B1, ADAM_B2, ADAM_EPS, ADAM_WD, ADAM_STEP = 0.001, 0.9, 0.999, 1e-08, 0.01, 10

SCAN_CHUNK = 16
ATT_TQ = 256
SEG = 256
FFN_COLS = 256
ROW_TILE = 256
SHIFT_COLS = 384

WNAMES = ['ln_mix_g', 'w_in', 'shift_mu_prev', 'shift_mu_next', 'decay_w0_fwd', 'decay_w2_fwd', 'decay_w0_bwd',
          'decay_w2_bwd', 'iclr_a0_fwd', 'iclr_a2_fwd', 'iclr_a0_bwd', 'iclr_a2_bwd', 'gate_g2', 'k_k', 'k_a', 'r_k',
          'ln_x_g', 'ln_x_b', 'q_norm_g', 'w_uq', 'kv_norm_g', 'w_ukv', 'mla_out_g', 'w_out', 'ln_ffn_g', 'w_ffn_up',
          'ffn_conv_w', 'ffn_conv_b', 'w_ffn_down', 'ln_final_g']
COL = ('w_in', 'decay_w2_fwd', 'decay_w2_bwd', 'iclr_a2_fwd', 'iclr_a2_bwd', 'gate_g2', 'w_ukv', 'w_ffn_up', 'ffn_conv_w')
ROW = ('w_uq', 'w_out', 'w_ffn_down')
SHARDED = [n for n in WNAMES if n in COL or n in ROW]
REPLICATED = [n for n in WNAMES if n not in SHARDED]
EXACT = ('ffn_conv_w',)
LATE = ['w_out', 'w_ffn_up', 'ffn_conv_w', 'w_ffn_down']
BIG = ('w_in', 'w_uq', 'w_ukv', 'w_out', 'w_ffn_up', 'w_ffn_down')


def _params(*sem):
    return pltpu.CompilerParams(dimension_semantics=sem, vmem_limit_bytes=VMEM_LIMIT)


def _pack(arrs, lead=0):
    parts = []
    for a in arrs:
        head = a.shape[:lead]
        flat = a.reshape(head + (-1,))
        n = flat.shape[-1]
        n_pad = -(-n // PACK_TILE) * PACK_TILE
        flat = jnp.pad(flat, [(0, 0)] * lead + [(0, n_pad - n)])
        parts.append(flat.reshape(head + (n_pad // LANES, LANES)))
    out = jnp.concatenate(parts, axis=lead)
    rows = out.shape[lead]
    rows_pad = -(-rows // PACK_ROWS) * PACK_ROWS
    return jnp.pad(out, [(0, 0)] * lead + [(0, rows_pad - rows), (0, 0)])


def _unpack(packed, shapes, lead=0):
    outs, row = [], 0
    head = packed.shape[:lead]
    for shp in shapes:
        n = 1
        for s in shp:
            n *= s
        rows = -(-n // PACK_TILE) * (PACK_TILE // LANES)
        blk = lax.slice_in_dim(packed, row, row + rows, axis=lead)
        flat = blk.reshape(head + (rows * LANES,))
        outs.append(lax.slice_in_dim(flat, 0, n, axis=lead).reshape(head + tuple(shp)))
        row += rows
    return outs


PEERS = N_DEV - 1
HBM = pl.BlockSpec(memory_space=pl.ANY)


def _comm_sems(n):
    return [pltpu.SemaphoreType.DMA((PEERS * n,)), pltpu.SemaphoreType.DMA((PEERS * n,)), pltpu.SemaphoreType.DMA((n,))]


def _all_gather(xs, name):
    n = len(xs)

    def body(*refs):
        x_refs, out_refs, (send_sems, recv_sems, local_sems) = refs[:n], refs[n:2 * n], refs[2 * n:]
        mx, my, mc = lax.axis_index("x"), lax.axis_index("y"), lax.axis_index("c")
        me, sibling = (mx, my, mc), (mx, my, 1 - mc)
        chips = [(1 - mx, my), (mx, 1 - my), (1 - mx, 1 - my)]

        def slot(a, px, py, pc):
            return out_refs[a].at[4 * px + 2 * py + pc]

        def copy(a, k, block, to, src=None):
            return pltpu.make_async_remote_copy(
                src_ref=slot(a, *block) if src is None else src, dst_ref=slot(a, *block),
                send_sem=send_sems.at[PEERS * a + k], recv_sem=recv_sems.at[PEERS * a + k],
                device_id=to, device_id_type=MESH)

        mine = [pltpu.make_async_copy(x_refs[a], slot(a, *me), local_sems.at[a]) for a in range(n)]
        first, passed = [], []
        for a in range(n):
            mine[a].start()
            first.append(copy(a, 0, me, sibling, src=x_refs[a]))
            first += [copy(a, 1 + j, me, (*chip, mc), src=x_refs[a]) for j, chip in enumerate(chips)]
        for cp in first:
            cp.start()
        for j, chip in enumerate(chips):
            for a in range(n):
                copy(a, 1 + j, (*chip, mc), me).wait_recv()
                passed.append(copy(a, 4 + j, (*chip, mc), sibling))
                passed[-1].start()
        for a in range(n):
            copy(a, 0, sibling, me).wait_recv()
            for j, chip in enumerate(chips):
                copy(a, 4 + j, (*chip, 1 - mc), me).wait_recv()
        for cp in first + passed:
            cp.wait_send()
        for cp in mine:
            cp.wait()

    return pl.pallas_call(
        body, name=name, out_shape=[jax.ShapeDtypeStruct((N_DEV,) + x.shape, x.dtype) for x in xs],
        in_specs=[HBM] * n, out_specs=[HBM] * n, scratch_shapes=_comm_sems(n),
    )(*xs)


def _direct_exchange(src_refs, out_refs, send_sems, recv_sems, local_sems, per_peer):
    mx, my, mc = lax.axis_index("x"), lax.axis_index("y"), lax.axis_index("c")
    me = 4 * mx + 2 * my + mc

    def flip(v, bit):
        return 1 - v if bit else v

    def copies():
        mine, remote = [], []
        for a, (src, out) in enumerate(zip(src_refs, out_refs)):
            mine.append(pltpu.make_async_copy(src.at[me] if per_peer else src, out.at[me], local_sems.at[a]))
            for k in range(1, N_DEV):
                px, py, pc = flip(mx, k & 4), flip(my, k & 2), flip(mc, k & 1)
                remote.append(pltpu.make_async_remote_copy(
                    src_ref=src.at[4 * px + 2 * py + pc] if per_peer else src, dst_ref=out.at[me],
                    send_sem=send_sems.at[PEERS * a + k - 1], recv_sem=recv_sems.at[PEERS * a + k - 1],
                    device_id=(px, py, pc), device_id_type=MESH))
        return mine, remote

    def start():
        mine, remote = copies()
        for cp in mine + remote:
            cp.start()

    def wait():
        mine, remote = copies()
        for cp in remote:
            cp.wait_recv()
        for cp in remote:
            cp.wait_send()
        for cp in mine:
            cp.wait()

    return start, wait


def _grad_exchange(gs, name):
    n = len(gs)

    def body(*refs):
        start, wait = _direct_exchange(refs[:n], refs[n:2 * n], *refs[2 * n:], per_peer=True)
        start()
        wait()

    return pl.pallas_call(
        body, name=name, out_shape=[jax.ShapeDtypeStruct(g.shape, g.dtype) for g in gs],
        in_specs=[HBM] * n, out_specs=[HBM] * n, scratch_shapes=_comm_sems(n),
    )(*gs)


def _sum_adamw(parts, w, m, v, name):
    rows, cols = w.shape
    tr = next((t for t in range(ADAM_ROWS, 15, -16) if rows % t == 0), rows)
    c1 = 1.0 - ADAM_B1 ** ADAM_STEP
    c2 = 1.0 - ADAM_B2 ** ADAM_STEP

    def body(p_ref, w_ref, m_ref, v_ref, g_out, d_out, m_out, v_out):
        g = p_ref[0].astype(F32)
        for q in range(1, N_DEV):
            g = g + p_ref[q].astype(F32)
        m_new = ADAM_B1 * m_ref[...] + (1.0 - ADAM_B1) * g
        v_new = ADAM_B2 * v_ref[...] + (1.0 - ADAM_B2) * (g * g)
        m_hat = m_new / c1
        v_hat = v_new / c2
        g_out[...] = g
        d_out[...] = -ADAM_LR * (m_hat / (jnp.sqrt(v_hat) + ADAM_EPS) + ADAM_WD * w_ref[...])
        m_out[...] = m_new
        v_out[...] = v_new

    blk = pl.BlockSpec((tr, cols), lambda i: (i, 0))
    out = jax.ShapeDtypeStruct((rows, cols), F32)
    return pl.pallas_call(
        body, name=name, grid=(rows // tr,),
        in_specs=[pl.BlockSpec((N_DEV, tr, cols), lambda i: (0, i, 0)), blk, blk, blk],
        out_specs=[blk, blk, blk, blk], out_shape=[out, out, out, out],
        compiler_params=_params("parallel"),
    )(parts, w, m, v)


def _tile(dim, cap=MM_TILE):
    if dim <= cap:
        return dim
    for t in range(cap, LANES - 1, -LANES):
        if dim % t == 0:
            return t
    return dim


def _mm_call(a, b, form, name, out_dtype=F32):
    if form == 'nn':
        (m, k), n = a.shape, b.shape[1]
    elif form == 'nt':
        (m, k), n = a.shape, b.shape[0]
    else:
        (k, m), n = a.shape, b.shape[1]
    tk = k if (form == 'nn' and k <= MM_K_WHOLE) else _tile(k, MM_TILE_WIDE)
    tm = _tile(m, MM_TILE_WIDE if form == 'tn' else MM_TILE)
    tn = _tile(n, MM_TILE_WIDE)
    nk = k // tk
    contract = {'nn': ((1,), (0,)), 'nt': ((1,), (1,)), 'tn': ((0,), (0,))}[form]

    acc_in_out = nk == 1 or out_dtype == F32

    def body(a_ref, b_ref, o_ref, *acc):
        part = lax.dot_general(a_ref[...].astype(BF16), b_ref[...].astype(BF16), (contract, ((), ())),
                               preferred_element_type=F32)
        if nk == 1:
            o_ref[...] = part.astype(out_dtype)
            return
        acc_ref = o_ref if acc_in_out else acc[0]

        @pl.when(pl.program_id(2) == 0)
        def _():
            acc_ref[...] = part

        @pl.when(pl.program_id(2) > 0)
        def _():
            acc_ref[...] += part

        if not acc_in_out:
            @pl.when(pl.program_id(2) == nk - 1)
            def _():
                o_ref[...] = acc_ref[...].astype(out_dtype)

    a_spec = pl.BlockSpec((tk, tm), lambda j, i, l: (l, i)) if form == 'tn' else pl.BlockSpec((tm, tk), lambda j, i, l: (i, l))
    b_spec = pl.BlockSpec((tn, tk), lambda j, i, l: (j, l)) if form == 'nt' else pl.BlockSpec((tk, tn), lambda j, i, l: (l, j))
    return pl.pallas_call(
        body, name=name, grid=(n // tn, m // tm, nk),
        in_specs=[a_spec, b_spec], out_specs=pl.BlockSpec((tm, tn), lambda j, i, l: (i, j)),
        out_shape=jax.ShapeDtypeStruct((m, n), out_dtype),
        scratch_shapes=[] if acc_in_out else [pltpu.VMEM((tm, tn), F32)],
        compiler_params=_params("parallel", "parallel", "arbitrary"),
    )(a, b)


def _make_mm(name):
    @jax.custom_vjp
    def mm(a, b):
        return _mm_call(a, b, 'nn', name + '_fwd')

    def fwd(a, b):
        return _mm_call(a, b, 'nn', name + '_fwd'), (a, b)

    def bwd(res, g):
        a, b = res
        return _mm_call(g, b, 'nt', name + '_da'), _mm_call(a, g, 'tn', name + '_db', out_dtype=BF16)

    mm.defvjp(fwd, bwd)
    return mm


def _rms_fwd_call(x, g, name):
    m, d = x.shape
    tm = _tile(m)

    def body(x_ref, g_ref, o_ref):
        xv = x_ref[...]
        rinv = lax.rsqrt(jnp.mean(xv * xv, axis=-1, keepdims=True) + NORM_EPS)
        o_ref[...] = xv * rinv * g_ref[...]

    return pl.pallas_call(
        body, name=name, grid=(m // tm,),
        in_specs=[pl.BlockSpec((tm, d), lambda i: (i, 0)), pl.BlockSpec((1, d), lambda i: (0, 0))],
        out_specs=pl.BlockSpec((tm, d), lambda i: (i, 0)), out_shape=jax.ShapeDtypeStruct((m, d), F32),
        compiler_params=_params("parallel"),
    )(x, g)


def _rms_bwd_call(x, g, dy, name):
    m, d = x.shape
    tm = _tile(m)

    def body(x_ref, g_ref, dy_ref, dx_ref, dg_ref):
        @pl.when(pl.program_id(0) == 0)
        def _():
            dg_ref[...] = jnp.zeros_like(dg_ref)

        xv, dyv = x_ref[...], dy_ref[...]
        rinv = lax.rsqrt(jnp.mean(xv * xv, axis=-1, keepdims=True) + NORM_EPS)
        xh = xv * rinv
        dg_ref[...] += jnp.sum(dyv * xh, axis=0, keepdims=True)
        dxh = dyv * g_ref[...]
        dx_ref[...] = rinv * (dxh - xh * jnp.mean(dxh * xh, axis=-1, keepdims=True))

    return pl.pallas_call(
        body, name=name, grid=(m // tm,),
        in_specs=[pl.BlockSpec((tm, d), lambda i: (i, 0)), pl.BlockSpec((1, d), lambda i: (0, 0)),
                  pl.BlockSpec((tm, d), lambda i: (i, 0))],
        out_specs=[pl.BlockSpec((tm, d), lambda i: (i, 0)), pl.BlockSpec((1, d), lambda i: (0, 0))],
        out_shape=[jax.ShapeDtypeStruct((m, d), F32), jax.ShapeDtypeStruct((1, d), F32)],
        compiler_params=_params("arbitrary"),
    )(x, g, dy)


def _make_rms(name):
    @jax.custom_vjp
    def rms(x, g):
        return _rms_fwd_call(x, g, name + '_fwd')

    def fwd(x, g):
        return _rms_fwd_call(x, g, name + '_fwd'), (x, g)

    def bwd(res, dy):
        x, g = res
        dx, dg = _rms_bwd_call(x, g, dy, name + '_bwd')
        return dx, dg

    rms.defvjp(fwd, bwd)
    return rms


def _time_shifts(x):
    t = x.shape[0]
    rows = lax.broadcasted_iota(jnp.int32, x.shape, 0)
    return (jnp.where(rows == 0, 0.0, pltpu.roll(x, 1, 0)), jnp.where(rows == t - 1, 0.0, pltpu.roll(x, t - 1, 0)))


def _conv3(x, cw_ref, cb_ref):
    xp, xn = _time_shifts(x)
    return cw_ref[0:1, :] * xp + cw_ref[1:2, :] * x + cw_ref[2:3, :] * xn + cb_ref[...]


def _glu_specs(b, t, f):
    tc = _tile(f, FFN_COLS)
    seq = pl.BlockSpec((1, t, tc), lambda j, bi: (bi, 0, j))
    cw = pl.BlockSpec((3, tc), lambda j, bi: (0, j))
    cb = pl.BlockSpec((1, tc), lambda j, bi: (0, j))
    return tc, seq, cw, cb


def _glu_fwd_call(ug, uv, cwg, cwv, cbg, cbv):
    b, t, f = ug.shape
    tc, seq, cw, cb = _glu_specs(b, t, f)

    def body(ug_ref, uv_ref, cwg_ref, cwv_ref, cbg_ref, cbv_ref, o_ref):
        g = _conv3(ug_ref[0], cwg_ref, cbg_ref)
        o_ref[0] = g * jax.nn.sigmoid(g) * _conv3(uv_ref[0], cwv_ref, cbv_ref)

    return pl.pallas_call(
        body, name='glu_fwd', grid=(f // tc, b), in_specs=[seq, seq, cw, cw, cb, cb], out_specs=seq,
        out_shape=jax.ShapeDtypeStruct((b, t, f), F32), compiler_params=_params("parallel", "parallel"),
    )(ug, uv, cwg, cwv, cbg, cbv)


def _glu_bwd_call(ug, uv, cwg, cwv, cbg, cbv, dact):
    b, t, f = ug.shape
    tc, seq, cw, cb = _glu_specs(b, t, f)

    def body(ug_ref, uv_ref, cwg_ref, cwv_ref, cbg_ref, cbv_ref, da_ref,
             dug_ref, duv_ref, dcwg_ref, dcwv_ref, dcbg_ref, dcbv_ref):
        @pl.when(pl.program_id(1) == 0)
        def _():
            for ref in (dcwg_ref, dcwv_ref, dcbg_ref, dcbv_ref):
                ref[...] = jnp.zeros_like(ref)

        g = _conv3(ug_ref[0], cwg_ref, cbg_ref)
        v = _conv3(uv_ref[0], cwv_ref, cbv_ref)
        sig = jax.nn.sigmoid(g)
        da = da_ref[0]
        dv = da * (g * sig)
        dg = da * v * (sig * (1.0 + g * (1.0 - sig)))

        def conv_bwd(dc, x_ref, cw_ref, dx_ref, dcw_ref, dcb_ref):
            dcp, dcn = _time_shifts(dc)
            dx_ref[0] = cw_ref[0:1, :] * dcn + cw_ref[1:2, :] * dc + cw_ref[2:3, :] * dcp
            x = x_ref[0]
            xp, xn = _time_shifts(x)
            for n, xs in enumerate((xp, x, xn)):
                dcw_ref[n:n + 1, :] += jnp.sum(dc * xs, axis=0, keepdims=True)
            dcb_ref[...] += jnp.sum(dc, axis=0, keepdims=True)

        conv_bwd(dg, ug_ref, cwg_ref, dug_ref, dcwg_ref, dcbg_ref)
        conv_bwd(dv, uv_ref, cwv_ref, duv_ref, dcwv_ref, dcbv_ref)

    big = jax.ShapeDtypeStruct((b, t, f), F32)
    return pl.pallas_call(
        body, name='glu_bwd', grid=(f // tc, b), in_specs=[seq, seq, cw, cw, cb, cb, seq],
        out_specs=[seq, seq, cw, cw, cb, cb],
        out_shape=[big, big, jax.ShapeDtypeStruct((3, f), F32), jax.ShapeDtypeStruct((3, f), F32),
                   jax.ShapeDtypeStruct((1, f), F32), jax.ShapeDtypeStruct((1, f), F32)],
        compiler_params=_params("parallel", "arbitrary"),
    )(ug, uv, cwg, cwv, cbg, cbv, dact)


def _shift_call(z, mu_p, mu_n, dzs=None):
    b, t, c = z.shape
    tc = _tile(c, SHIFT_COLS)
    seq = pl.BlockSpec((1, t, tc), lambda j, bi: (bi, 0, j))
    row = pl.BlockSpec((1, tc), lambda j, bi: (0, j))

    def fwd_body(z_ref, mp_ref, mn_ref, o_ref):
        x = z_ref[0]
        xp, xn = _time_shifts(x)
        o_ref[0] = x + mp_ref[...] * (xp - x) + mn_ref[...] * (xn - x)

    def bwd_body(z_ref, mp_ref, mn_ref, d_ref, dz_ref, dmp_ref, dmn_ref):
        @pl.when(pl.program_id(1) == 0)
        def _():
            dmp_ref[...] = jnp.zeros_like(dmp_ref)
            dmn_ref[...] = jnp.zeros_like(dmn_ref)

        x, d = z_ref[0], d_ref[0]
        xp, xn = _time_shifts(x)
        dp, dn = _time_shifts(d)
        mp, mn = mp_ref[...], mn_ref[...]
        dz_ref[0] = d * (1.0 - mp - mn) + mp * dn + mn * dp
        dmp_ref[...] += jnp.sum(d * (xp - x), axis=0, keepdims=True)
        dmn_ref[...] += jnp.sum(d * (xn - x), axis=0, keepdims=True)

    if dzs is None:
        return pl.pallas_call(
            fwd_body, name='shift_fwd', grid=(c // tc, b), in_specs=[seq, row, row], out_specs=seq,
            out_shape=jax.ShapeDtypeStruct(z.shape, F32), compiler_params=_params("parallel", "parallel"),
        )(z, mu_p, mu_n)
    return pl.pallas_call(
        bwd_body, name='shift_bwd', grid=(c // tc, b), in_specs=[seq, row, row, seq], out_specs=[seq, row, row],
        out_shape=[jax.ShapeDtypeStruct(z.shape, F32), jax.ShapeDtypeStruct(mu_p.shape, F32),
                   jax.ShapeDtypeStruct(mu_n.shape, F32)],
        compiler_params=_params("parallel", "arbitrary"),
    )(z, mu_p, mu_n, dzs)


@jax.custom_vjp
def _token_shift(z, mu_p, mu_n):
    return _shift_call(z, mu_p, mu_n)


_token_shift.defvjp(lambda z, mu_p, mu_n: (_shift_call(z, mu_p, mu_n), (z, mu_p, mu_n)),
                    lambda res, d: tuple(_shift_call(*res, dzs=d)))


@jax.custom_vjp
def _conv_glu(ug, uv, cwg, cwv, cbg, cbv):
    return _glu_fwd_call(ug, uv, cwg, cwv, cbg, cbv)


def _conv_glu_fwd(*args):
    return _glu_fwd_call(*args), args


def _conv_glu_bwd(res, dact):
    return tuple(_glu_bwd_call(*res, dact))


_conv_glu.defvjp(_conv_glu_fwd, _conv_glu_bwd)


def _scores(q_ref, k_ref):
    q16 = (q_ref[0] * MLA_SCALE).astype(BF16)
    return lax.dot_general(q16, k_ref[0].astype(BF16), (((1,), (1,)), ((), ())), preferred_element_type=F32), q16


def _attn_fwd_call(q, k, v, name):
    bh, t, _ = q.shape
    tq = min(ATT_TQ, t)

    def body(q_ref, k_ref, v_ref, o_ref, lse_ref):
        s, _ = _scores(q_ref, k_ref)
        m = jnp.max(s, axis=-1, keepdims=True)
        p = jnp.exp(s - m)
        l = jnp.sum(p, axis=-1, keepdims=True)
        o_ref[0] = jnp.dot(p.astype(BF16), v_ref[0].astype(BF16), preferred_element_type=F32) / l
        lse_ref[0] = m + jnp.log(l)

    return pl.pallas_call(
        body, name=name, grid=(bh, t // tq),
        in_specs=[pl.BlockSpec((1, tq, D_QK), lambda b, i: (b, i, 0)), pl.BlockSpec((1, t, D_QK), lambda b, i: (b, 0, 0)),
                  pl.BlockSpec((1, t, D_V), lambda b, i: (b, 0, 0))],
        out_specs=[pl.BlockSpec((1, tq, D_V), lambda b, i: (b, i, 0)), pl.BlockSpec((1, tq, 1), lambda b, i: (b, i, 0))],
        out_shape=[jax.ShapeDtypeStruct((bh, t, D_V), F32), jax.ShapeDtypeStruct((bh, t, 1), F32)],
        compiler_params=_params("parallel", "parallel"),
    )(q, k, v)


def _attn_bwd_call(q, k, v, o, lse, do, name):
    bh, t, _ = q.shape
    tq = min(ATT_TQ, t)

    def body(q_ref, k_ref, v_ref, o_ref, lse_ref, do_ref, dq_ref, dk_ref, dv_ref):
        @pl.when(pl.program_id(1) == 0)
        def _():
            dk_ref[...] = jnp.zeros_like(dk_ref)
            dv_ref[...] = jnp.zeros_like(dv_ref)

        s, q16 = _scores(q_ref, k_ref)
        p = jnp.exp(s - lse_ref[0])
        dov = do_ref[0]
        do_b = dov.astype(BF16)
        delta = jnp.sum(dov * o_ref[0], axis=-1, keepdims=True)
        dp = lax.dot_general(do_b, v_ref[0].astype(BF16), (((1,), (1,)), ((), ())), preferred_element_type=F32)
        ds = (p * (dp - delta)).astype(BF16)
        dq_ref[0] = jnp.dot(ds, k_ref[0].astype(BF16), preferred_element_type=F32) * MLA_SCALE
        dk_ref[0] += lax.dot_general(ds, q16, (((0,), (0,)), ((), ())), preferred_element_type=F32)
        dv_ref[0] += lax.dot_general(p.astype(BF16), do_b, (((0,), (0,)), ((), ())), preferred_element_type=F32)

    qspec = pl.BlockSpec((1, tq, D_QK), lambda b, i: (b, i, 0))
    kspec = pl.BlockSpec((1, t, D_QK), lambda b, i: (b, 0, 0))
    vspec = pl.BlockSpec((1, t, D_V), lambda b, i: (b, 0, 0))
    ospec = pl.BlockSpec((1, tq, D_V), lambda b, i: (b, i, 0))
    lspec = pl.BlockSpec((1, tq, 1), lambda b, i: (b, i, 0))
    return pl.pallas_call(
        body, name=name, grid=(bh, t // tq),
        in_specs=[qspec, kspec, vspec, ospec, lspec, ospec], out_specs=[qspec, kspec, vspec],
        out_shape=[jax.ShapeDtypeStruct(q.shape, F32), jax.ShapeDtypeStruct(k.shape, F32), jax.ShapeDtypeStruct(v.shape, F32)],
        compiler_params=_params("parallel", "arbitrary"),
    )(q, k, v, o, lse, do)


@jax.custom_vjp
def _attention(q, k, v):
    return _attn_fwd_call(q, k, v, 'attn_fwd')[0]


def _attention_fwd(q, k, v):
    o, lse = _attn_fwd_call(q, k, v, 'attn_fwd')
    return o, (q, k, v, o, lse)


def _attention_bwd(res, do):
    return tuple(_attn_bwd_call(*res, do, 'attn_bwd'))


_attention.defvjp(_attention_fwd, _attention_bwd)


SROWS = N * D_RWKV // SEG


def _seg_ones():
    r = lax.broadcasted_iota(jnp.int32, (SEG, SEG), 0) >> 6
    c = lax.broadcasted_iota(jnp.int32, (SEG, SEG), 1) >> 6
    return (r == c).astype(BF16)


def _eye_mask():
    r = lax.broadcasted_iota(jnp.int32, (SROWS, SEG), 0) & (N - 1)
    c = lax.broadcasted_iota(jnp.int32, (SROWS, SEG), 1) & (N - 1)
    return r == c


def _row2(ref, bi, ti, dtype=F32):
    parts = [jnp.broadcast_to(ref[bi, pl.ds(ti, 1), pl.ds(SEG * q, SEG)].astype(dtype), (N, SEG))
             for q in range(D_RWKV // SEG)]
    return jnp.concatenate(parts, axis=0)


def _split2(x):
    hi = x.astype(BF16)
    return hi, (x - hi.astype(F32)).astype(BF16)


def _col_sum(x):
    return jnp.concatenate([jnp.sum(x[N * q:N * (q + 1)], axis=0, keepdims=True) for q in range(D_RWKV // SEG)], axis=1)


def _scan_specs(b, t, rev):
    nc = t // SCAN_CHUNK
    if rev:
        return (pl.BlockSpec((b, SCAN_CHUNK, D_RWKV), lambda c: (0, nc - 1 - c, 0)),
                pl.BlockSpec((b, SCAN_CHUNK, SROWS, SEG), lambda c: (0, nc - 1 - c, 0, 0)))
    return (pl.BlockSpec((b, SCAN_CHUNK, D_RWKV), lambda c: (0, c, 0)),
            pl.BlockSpec((b, SCAN_CHUNK, SROWS, SEG), lambda c: (0, c, 0, 0)))


def _scan_fwd_call(r, v, kk, wf, kf, qf, wb, kb, qb, ride):
    b, t, _ = r.shape
    n_chunks = t // SCAN_CHUNK
    last = SCAN_CHUNK - 1
    nr = len(ride)

    def body(rf, vf, kkf, wf_, kf_, qf_, rb, vb, kkb, wb_, kb_, qb_, *rest):
        (yf, yb, sf, sb), scratch = rest[nr:nr + 4], rest[2 * nr + 4:]
        states = scratch[:2 * b]
        send, arrive = _direct_exchange(rest[:nr], rest[nr + 4:2 * nr + 4], *scratch[2 * b:], per_peer=False)

        @pl.when(pl.program_id(0) == 0)
        def _():
            send()
            for st in states:
                st[...] = jnp.zeros_like(st)

        ones, mask = _seg_ones(), _eye_mask()
        zero16 = jnp.zeros((), BF16)
        chains = []
        for bi in range(b):
            chains.append((rf, vf, kkf, wf_, kf_, qf_, yf, sf, states[2 * bi], bi, False))
            chains.append((rb, vb, kkb, wb_, kb_, qb_, yb, sb, states[2 * bi + 1], bi, True))

        def tix(i, rev):
            return last - i if rev else i

        def put_y(y_, bi, ti, ycol):
            y_[bi, pl.ds(ti, 1), :] = _col_sum(jnp.where(mask, ycol, 0.0))

        def steps(i, with_y):
            parts = []
            for (r_, v_, kk_, w_, k_, q_, y_, s_, st, bi, rev) in chains:
                ti = tix(i, rev)
                s = st[...]
                s_[bi, ti] = s
                parts.append((s * _row2(kk_, bi, ti)).astype(BF16))
                parts.append(jnp.where(mask, _row2(v_, bi, ti, BF16), zero16))
                if with_y:
                    parts.append((s * _row2(r_, bi, tix(i - 1, rev))).astype(BF16))
            res = jnp.dot(jnp.concatenate(parts, axis=0), ones, preferred_element_type=F32)
            off = 0
            for (r_, v_, kk_, w_, k_, q_, y_, s_, st, bi, rev) in chains:
                ti = tix(i, rev)
                u = res[off:off + SROWS]
                vcol = res[off + SROWS:off + 2 * SROWS]
                off += 2 * SROWS
                if with_y:
                    put_y(y_, bi, tix(i - 1, rev), res[off:off + SROWS])
                    off += SROWS
                st[...] = st[...] * _row2(w_, bi, ti) - u * _row2(q_, bi, ti) + vcol * _row2(k_, bi, ti)

        steps(0, False)

        def loop(i, carry):
            steps(i, True)
            return carry

        lax.fori_loop(1, SCAN_CHUNK, loop, 0, unroll=5)
        parts = [(c[8][...] * _row2(c[0], c[9], tix(last, c[10]))).astype(BF16) for c in chains]
        res = jnp.dot(jnp.concatenate(parts, axis=0), ones, preferred_element_type=F32)
        for n, c in enumerate(chains):
            put_y(c[6], c[9], tix(last, c[10]), res[n * SROWS:(n + 1) * SROWS])

        @pl.when(pl.program_id(0) == n_chunks - 1)
        def _():
            arrive()

    fr, fs = _scan_specs(b, t, False)
    br, bs = _scan_specs(b, t, True)
    y_shape = jax.ShapeDtypeStruct((b, t, D_RWKV), F32)
    s_shape = jax.ShapeDtypeStruct((b, t, SROWS, SEG), F32)
    return pl.pallas_call(
        body, name='scan_fwd', grid=(n_chunks,),
        in_specs=[fr] * 6 + [br] * 6 + [HBM] * nr, out_specs=[fr, br, fs, bs] + [HBM] * nr,
        out_shape=[y_shape, y_shape, s_shape, s_shape] + [jax.ShapeDtypeStruct((N_DEV,) + a.shape, a.dtype) for a in ride],
        scratch_shapes=[pltpu.VMEM((SROWS, SEG), F32)] * (2 * b) + _comm_sems(nr),
        compiler_params=_params("arbitrary"),
    )(r, v, kk, wf, kf, qf, r, v, kk, wb, kb, qb, *ride)


def _scan_bwd_call(r, v, kk, wf, kf, qf, wb, kb, qb, sf, sb, dyf, dyb, ride):
    b, t, _ = r.shape
    n_chunks = t // SCAN_CHUNK
    last = SCAN_CHUNK - 1
    nr = len(ride)

    def body(rf, vf, kkf, wf_, kf_, qf_, sf_, dyf_, rb, vb, kkb, wb_, kb_, qb_, sb_, dyb_, *rest):
        drf, dvf, dkkf, dwf, dkf, dqf, drb, dvb, dkkb, dwb, dkb, dqb = rest[nr:nr + 12]
        scratch = rest[2 * nr + 12:]
        send, arrive = _direct_exchange(rest[:nr], rest[nr + 12:2 * nr + 12], *scratch[8 * b:], per_peer=True)

        @pl.when(pl.program_id(0) == 0)
        def _():
            send()
            for n in range(2 * b):
                scratch[4 * n][...] = jnp.zeros_like(scratch[4 * n])

        ones, mask = _seg_ones(), _eye_mask()
        zero16 = jnp.zeros((), BF16)
        chains = []
        for bi in range(b):
            chains.append((rf, vf, kkf, wf_, kf_, qf_, sf_, dyf_, (drf, dvf, dkkf, dwf, dkf, dqf),
                           scratch[8 * bi:8 * bi + 4], bi, True))
            chains.append((rb, vb, kkb, wb_, kb_, qb_, sb_, dyb_, (drb, dvb, dkkb, dwb, dkb, dqb),
                           scratch[8 * bi + 4:8 * bi + 8], bi, False))

        def tix(i, rev):
            return last - i if rev else i

        def state_free_parts(v_, dy_, kk_, s_, bi, ti):
            return [jnp.where(mask, _row2(v_, bi, ti, BF16), zero16), jnp.where(mask, _row2(dy_, bi, ti, BF16), zero16),
                    (s_[bi, ti] * _row2(kk_, bi, ti)).astype(BF16)]

        def keep(scr, res, off):
            for n in range(3):
                scr[1 + n][...] = res[off + n * SROWS:off + (n + 1) * SROWS]
            return off + 3 * SROWS

        def first():
            parts = []
            for (r_, v_, kk_, w_, k_, q_, s_, dy_, outs, scr, bi, rev) in chains:
                parts += state_free_parts(v_, dy_, kk_, s_, bi, tix(0, rev))
            res = jnp.dot(jnp.concatenate(parts, axis=0), ones, preferred_element_type=F32)
            off = 0
            for c in chains:
                off = keep(c[9], res, off)

        def steps(i, has_next, recompute):
            parts = []
            for (r_, v_, kk_, w_, k_, q_, s_, dy_, outs, scr, bi, rev) in chains:
                ti = tix(i, rev)
                gst, vc, dc, uc = scr
                dycol = dc[...]
                if recompute:
                    sc = s_[bi, ti] * _row2(w_, bi, ti) - uc[...] * _row2(q_, bi, ti) + vc[...] * _row2(k_, bi, ti)
                else:
                    sc = s_[bi, tix(i - 1, rev)]
                outs[0][bi, pl.ds(ti, 1), :] = _col_sum(sc * dycol)
                g = gst[...] + dycol * _row2(r_, bi, ti)
                gst[...] = g
                parts.append((g * _row2(q_, bi, ti)).astype(BF16))
                parts.append((g * _row2(k_, bi, ti)).astype(BF16))
                if has_next:
                    parts += state_free_parts(v_, dy_, kk_, s_, bi, tix(i + 1, rev))
            res = jnp.dot(jnp.concatenate(parts, axis=0), ones, preferred_element_type=F32)
            off = 0
            for (r_, v_, kk_, w_, k_, q_, s_, dy_, outs, scr, bi, rev) in chains:
                ti = tix(i, rev)
                gst, vc, dc, uc = scr
                dr_, dv_, dkk_, dw_, dk_, dq_ = outs

                def put(ref, val, sign=1.0):
                    ref[bi, pl.ds(ti, 1), :] = sign * _col_sum(val)

                gq = res[off:off + SROWS]
                put(dv_, jnp.where(mask, res[off + SROWS:off + 2 * SROWS], 0.0))
                off += 2 * SROWS
                g, sp = gst[...], s_[bi, ti]
                put(dk_, g * vc[...])
                put(dw_, g * sp)
                put(dq_, g * uc[...], -1.0)
                put(dkk_, sp * gq, -1.0)
                gst[...] = g * _row2(w_, bi, ti) - gq * _row2(kk_, bi, ti)
                if has_next:
                    off = keep(scr, res, off)

        first()
        steps(0, True, True)

        def loop(i, carry):
            steps(i, True, False)
            return carry

        lax.fori_loop(1, last, loop, 0)
        steps(last, False, False)

        @pl.when(pl.program_id(0) == n_chunks - 1)
        def _():
            arrive()

    fr, fs = _scan_specs(b, t, True)
    br, bs = _scan_specs(b, t, False)
    y_shape = jax.ShapeDtypeStruct((b, t, D_RWKV), F32)
    return pl.pallas_call(
        body, name='scan_bwd', grid=(n_chunks,),
        in_specs=[fr] * 6 + [fs, fr] + [br] * 6 + [bs, br] + [HBM] * nr,
        out_specs=[fr] * 6 + [br] * 6 + [HBM] * nr,
        out_shape=[y_shape] * 12 + [jax.ShapeDtypeStruct(a.shape, a.dtype) for a in ride],
        scratch_shapes=[pltpu.VMEM((SROWS, SEG), F32)] * (8 * b) + _comm_sems(nr),
        compiler_params=_params("arbitrary"),
    )(r, v, kk, wf, kf, qf, sf, dyf, r, v, kk, wb, kb, qb, sb, dyb, *ride)


def _rope_tables(t):
    inv_freq = jnp.power(ROPE_THETA, -jnp.arange(0, D_ROPE, 2, dtype=F32) / D_ROPE)
    ang = jnp.arange(t, dtype=F32)[:, None] * inv_freq[None, :]
    ang = jnp.concatenate([ang, ang], axis=-1)
    return jnp.cos(ang), jnp.sin(ang)


def _rope(x, cos, sin):
    x1, x2 = jnp.split(x, 2, axis=-1)
    return x * cos + jnp.concatenate([-x2, x1], axis=-1) * sin


@jax.custom_vjp
def _dot16(a, w):
    return jnp.dot(a.astype(BF16), w.astype(BF16), preferred_element_type=F32)


def _dot16_fwd(a, w):
    a16, w16 = a.astype(BF16), w.astype(BF16)
    return jnp.dot(a16, w16, preferred_element_type=F32), (a16, w16)


def _dot16_bwd(res, g):
    a16, w16 = res
    g16 = g.astype(BF16)
    return (lax.dot_general(g16, w16, (((1,), (1,)), ((), ())), preferred_element_type=F32),
            lax.dot_general(a16, g16, (((0,), (0,)), ((), ())), preferred_element_type=F32))


_dot16.defvjp(_dot16_fwd, _dot16_bwd)


def _head_sum_tile(x):
    outs = []
    ones = _seg_ones()
    for q in range(x.shape[1] // SEG):
        hi, lo = _split2(x[:, SEG * q:SEG * (q + 1)])
        outs.append(jnp.dot(hi, ones, preferred_element_type=F32) + jnp.dot(lo, ones, preferred_element_type=F32))
    return jnp.concatenate(outs, axis=1)


@jax.custom_vjp
def _hsum(x):
    return _head_sum_tile(x)


_hsum.defvjp(lambda x: (_head_sum_tile(x), None), lambda _, g: (_head_sum_tile(g),))


def _softplus(x):
    return jnp.maximum(x, 0.0) + jnp.log(1.0 + jnp.exp(-jnp.abs(x)))


def _rwkv_pre_fn(k, wdf, wdb, adf, adb, gd, w0f, w2f, w0b, w2b, a0f, a2f, a0b, a2b, g2, k_k, k_a):
    w_f = jnp.exp(-jnp.exp(-_softplus(-(w0f + _dot16(jnp.tanh(wdf), w2f))) - 0.5))
    w_b = jnp.exp(-jnp.exp(-_softplus(-(w0b + _dot16(jnp.tanh(wdb), w2b))) - 0.5))
    a_f = jax.nn.sigmoid(a0f + _dot16(adf, a2f))
    a_b = jax.nn.sigmoid(a0b + _dot16(adb, a2b))
    gate = _dot16(jax.nn.sigmoid(gd), g2)
    kk = k * k_k
    kk = kk / jnp.maximum(jnp.sqrt(_hsum(kk * kk)), L2_EPS)
    return (kk, w_f, k * (1.0 + (a_f - 1.0) * k_a), kk * a_f, w_b, k * (1.0 + (a_b - 1.0) * k_a), kk * a_b, gate)


def _rwkv_post_fn(y_f, y_b, r, k_f, k_b, v, gate, ln_g, ln_b, r_k):
    y = y_f + y_b
    yc = y - _hsum(y) * (1.0 / N)
    var = _hsum(yc * yc) * (1.0 / N)
    y = yc * lax.rsqrt(var + GN_EPS) * ln_g + ln_b
    return ((y + _hsum(r * (k_f + k_b) * r_k) * v) * gate,)


def _make_rowwise(fn, name, n_rows, tm):
    def specs(arrs, whole):
        if whole:
            return [pl.BlockSpec(a.shape, lambda i: (0, 0)) for a in arrs]
        return [pl.BlockSpec((tm, a.shape[1]), lambda i: (i, 0)) for a in arrs]

    def out_widths(rows, params):
        tiles = [jax.ShapeDtypeStruct((tm, a.shape[1]), F32) for a in rows]
        return [o.shape[1] for o in jax.eval_shape(fn, *tiles, *params)]

    def fwd_call(rows, params):
        m = rows[0].shape[0]
        n_in = len(rows) + len(params)
        outs = [jax.ShapeDtypeStruct((m, d), F32) for d in out_widths(rows, params)]

        def body(*refs):
            for o_ref, o in zip(refs[n_in:], fn(*[ref[...] for ref in refs[:n_in]])):
                o_ref[...] = o

        return pl.pallas_call(
            body, name=name + '_fwd', grid=(m // tm,), in_specs=specs(rows, False) + specs(params, True),
            out_specs=specs(outs, False), out_shape=outs, compiler_params=_params("parallel"),
        )(*rows, *params)

    def bwd_call(rows, params, cts):
        m = rows[0].shape[0]
        n_in = len(rows) + len(params)
        n_all = n_in + len(cts)
        outs = ([jax.ShapeDtypeStruct(a.shape, F32) for a in rows] + [jax.ShapeDtypeStruct(a.shape, F32) for a in params])

        def body(*refs):
            _, vjp = jax.vjp(fn, *[ref[...] for ref in refs[:n_in]])
            grads = vjp(tuple(ref[...] for ref in refs[n_in:n_all]))
            d_rows, d_params = refs[n_all:n_all + len(rows)], refs[n_all + len(rows):]
            for ref, g in zip(d_rows, grads[:len(rows)]):
                ref[...] = g

            @pl.when(pl.program_id(0) == 0)
            def _():
                for ref in d_params:
                    ref[...] = jnp.zeros_like(ref)

            for ref, g in zip(d_params, grads[len(rows):]):
                ref[...] += g

        return pl.pallas_call(
            body, name=name + '_bwd', grid=(m // tm,),
            in_specs=specs(rows, False) + specs(params, True) + specs(cts, False),
            out_specs=specs(rows, False) + specs(params, True), out_shape=outs, compiler_params=_params("arbitrary"),
        )(*rows, *params, *cts)

    @jax.custom_vjp
    def op(*args):
        return tuple(fwd_call(args[:n_rows], args[n_rows:]))

    def op_fwd(*args):
        return tuple(fwd_call(args[:n_rows], args[n_rows:])), args

    def op_bwd(args, cts):
        return tuple(bwd_call(args[:n_rows], args[n_rows:], cts))

    op.defvjp(op_fwd, op_bwd)
    return op


def _rwkv_operands(z, full, rep):
    b, t, _ = z.shape
    m = b * t
    z = _token_shift(z, rep['shift_mu_prev'], rep['shift_mu_next']).reshape(m, RWKV_COLS)
    r, k, v = z[:, :512], z[:, 512:1024], z[:, 1024:1536]
    lora_in = (z[:, 1536:1600], z[:, 1600:1664], z[:, 1664:1728], z[:, 1728:1792], z[:, 1792:1920])
    kk, w_f, k_f, q_f, w_b, k_b, q_b, gate = _make_rowwise(_rwkv_pre_fn, 'rwkv_pre', 6, _tile(m, ROW_TILE))(
        k, *lora_in, rep['decay_w0_fwd'], full['decay_w2_fwd'], rep['decay_w0_bwd'], full['decay_w2_bwd'],
        rep['iclr_a0_fwd'], full['iclr_a2_fwd'], rep['iclr_a0_bwd'], full['iclr_a2_bwd'], full['gate_g2'],
        rep['k_k'], rep['k_a'])
    return r, v, kk, w_f, k_f, q_f, w_b, k_b, q_b, gate


def _mla_mixer(z, full, rep, b, t):
    m = b * t
    c_q, c_kv, k_rope = z[:, :768], z[:, 768:1024], z[:, 1024:1056]
    cos, sin = _rope_tables(t)
    q = _make_mm('mm_uq')(_make_rms('rms_q')(c_q, rep['q_norm_g']), full['w_uq']).reshape(b, t, H, D_QK)
    q = jnp.concatenate([q[..., :D_NOPE], _rope(q[..., D_NOPE:], cos[:, None, :], sin[:, None, :])], axis=-1)
    kv = _make_mm('mm_ukv')(_make_rms('rms_kv')(c_kv, rep['kv_norm_g']), full['w_ukv']).reshape(b, t, H, D_NOPE + D_V)
    k_rope = _rope(k_rope.reshape(b, t, D_ROPE), cos, sin)
    k = jnp.concatenate([kv[..., :D_NOPE], jnp.broadcast_to(k_rope[:, :, None, :], (b, t, H, D_ROPE))], axis=-1)
    heads = lambda a: a.transpose(0, 2, 1, 3).reshape(b * H, t, a.shape[-1])
    o = _attention(heads(q), heads(k), heads(kv[..., D_NOPE:]))
    o = o.reshape(b, H, t, D_V).transpose(0, 2, 1, 3).reshape(m, H * D_V)
    return _make_rms('rms_mla_out')(o, rep['mla_out_g'])


def _before_scan(full, rep, x):
    b, t, d = x.shape
    m = b * t
    n1 = _make_rms('rms_mix')(x.reshape(m, d), rep['ln_mix_g'])
    d_in = full['w_in'].shape[1]
    d_in_pad = -(-d_in // MM_TILE) * MM_TILE
    z = _make_mm('mm_in')(n1, jnp.pad(full['w_in'], ((0, 0), (0, d_in_pad - d_in))))
    return (*_rwkv_operands(z[:, :RWKV_COLS].reshape(b, t, RWKV_COLS), full, rep),
            _mla_mixer(z[:, RWKV_COLS:d_in], full, rep, b, t))


def _after_scan(full, rep, x, target, y_f, y_b, r, k_f, k_b, v, gate, y_mla):
    b, t, d = x.shape
    m = b * t
    xf = x.reshape(m, d)
    y_rwkv = _make_rowwise(_rwkv_post_fn, 'rwkv_post', 7, _tile(m, ROW_TILE))(
        y_f.reshape(m, D_RWKV), y_b.reshape(m, D_RWKV), r, k_f, k_b, v, gate,
        rep['ln_x_g'], rep['ln_x_b'], rep['r_k'].reshape(1, D_RWKV))[0]
    h = xf + _make_mm('mm_out')(jnp.concatenate([y_rwkv, y_mla], axis=-1), full['w_out'])
    n2 = _make_rms('rms_ffn')(h, rep['ln_ffn_g'])
    w_up, cw, cb = full['w_ffn_up'], full['ffn_conv_w'], rep['ffn_conv_b']
    u_gate = _make_mm('mm_up_gate')(n2, w_up[:, :D_FF]).reshape(b, t, D_FF)
    u_val = _make_mm('mm_up_val')(n2, w_up[:, D_FF:]).reshape(b, t, D_FF)
    act = _conv_glu(u_gate, u_val, cw[:, :D_FF], cw[:, D_FF:], cb[:, :D_FF], cb[:, D_FF:]).reshape(m, D_FF)
    h = h + _make_mm('mm_down')(act, full['w_ffn_down'])
    out = _make_rms('rms_final')(h, rep['ln_final_g'])
    err = jnp.square(out - target.reshape(m, d))
    return 0.5 * jnp.sum(jnp.mean(err, axis=-1))


def _mat(a):
    if a.ndim == 1:
        return a.reshape(1, -1)
    if a.ndim == 3:
        return a.reshape(a.shape[1:])
    return a


def _join(shards, name):
    if name in ROW:
        return shards.reshape(-1, shards.shape[-1])
    return shards.transpose(1, 0, 2).reshape(shards.shape[1], -1)


def _cut(whole, name):
    r, c = whole.shape
    if name in ROW:
        return whole.reshape(N_DEV, r // N_DEV, c)
    return whole.reshape(r, N_DEV, c // N_DEV).transpose(1, 0, 2)


def kernel(x, ln_mix_g, w_in, shift_mu_prev, shift_mu_next, decay_w0_fwd, decay_w2_fwd, decay_w0_bwd, decay_w2_bwd, iclr_a0_fwd, iclr_a2_fwd, iclr_a0_bwd, iclr_a2_bwd, gate_g2, k_k, k_a, r_k, ln_x_g, ln_x_b, q_norm_g, w_uq, kv_norm_g, w_ukv, mla_out_g, w_out, ln_ffn_g, w_ffn_up, ffn_conv_w, ffn_conv_b, w_ffn_down, ln_final_g, loss_target, m_ln_mix_g, m_w_in, m_shift_mu_prev, m_shift_mu_next, m_decay_w0_fwd, m_decay_w2_fwd, m_decay_w0_bwd, m_decay_w2_bwd, m_iclr_a0_fwd, m_iclr_a2_fwd, m_iclr_a0_bwd, m_iclr_a2_bwd, m_gate_g2, m_k_k, m_k_a, m_r_k, m_ln_x_g, m_ln_x_b, m_q_norm_g, m_w_uq, m_kv_norm_g, m_w_ukv, m_mla_out_g, m_w_out, m_ln_ffn_g, m_w_ffn_up, m_ffn_conv_w, m_ffn_conv_b, m_w_ffn_down, m_ln_final_g, v_ln_mix_g, v_w_in, v_shift_mu_prev, v_shift_mu_next, v_decay_w0_fwd, v_decay_w2_fwd, v_decay_w0_bwd, v_decay_w2_bwd, v_iclr_a0_fwd, v_iclr_a2_fwd, v_iclr_a0_bwd, v_iclr_a2_bwd, v_gate_g2, v_k_k, v_k_a, v_r_k, v_ln_x_g, v_ln_x_b, v_q_norm_g, v_w_uq, v_kv_norm_g, v_w_ukv, v_mla_out_g, v_w_out, v_ln_ffn_g, v_w_ffn_up, v_ffn_conv_w, v_ffn_conv_b, v_w_ffn_down, v_ln_final_g):
    given = dict(locals())
    w = {n: given[n] for n in WNAMES}
    mom = {n: given['m_' + n] for n in WNAMES}
    var = {n: given['v_' + n] for n in WNAMES}

    def split(names):
        return [n for n in names if n in BIG], [n for n in names if n not in BIG]

    def wire(names):
        big, small = split(names)
        pack = _pack([lax.bitcast_convert_type(_mat(w[n]), BF16) if n in EXACT else _mat(w[n]).astype(BF16) for n in small])
        return [_mat(w[n]).astype(BF16) for n in big] + [pack]

    def whole(names, gathered):
        big, small = split(names)
        out = {n: _join(g, n) for n, g in zip(big, gathered)}
        shapes = [_mat(w[n]).shape + ((2,) if n in EXACT else ()) for n in small]
        for n, s in zip(small, _unpack(gathered[-1], shapes, lead=1)):
            out[n] = _join(lax.bitcast_convert_type(s, F32) if n in EXACT else s.astype(F32), n)
        return out

    def grad_wire(names, grads):
        big, small = split(names)
        return [_cut(grads[n], n) for n in big] + [_pack([_cut(grads[n], n).astype(BF16) for n in small], lead=1)]

    early = [n for n in SHARDED if n not in LATE]
    rep = {n: _mat(w[n]) for n in REPLICATED}
    rep['r_k'] = w['r_k'].reshape(H, N)
    b, t, d = x.shape
    seq = lambda a: a.reshape(b, t, D_RWKV)
    flat = lambda a: a.reshape(b * t, D_RWKV)

    full_early = whole(early, _all_gather(wire(early), 'gather_weights'))
    ops, vjp_before = jax.vjp(_before_scan, full_early, rep, x)
    r, v, kk, w_f, k_f, q_f, w_b, k_b, q_b, gate, y_mla = ops
    scan_in = [seq(a) for a in (r, v, kk, w_f, k_f, q_f, w_b, k_b, q_b)]
    y_f, y_b, s_f, s_b, *late_gathered = _scan_fwd_call(*scan_in, wire(LATE))
    full_late = whole(LATE, late_gathered)
    loss_local, vjp_after = jax.vjp(_after_scan, full_late, rep, x, loss_target, y_f, y_b, r, k_f, k_b, v, gate, y_mla)

    g_late, g_rep_after, g_x_after, _, d_yf, d_yb, d_r, d_kf, d_kb, d_v, d_gate, d_ymla = vjp_after(jnp.ones((), F32))
    scan_out = _scan_bwd_call(*scan_in, s_f, s_b, d_yf, d_yb, grad_wire(LATE, g_late))
    parts_late = scan_out[12:]
    drf, dvf, dkkf, dwf, dkf, dqf, drb, dvb, dkkb, dwb, dkb, dqb = [flat(a) for a in scan_out[:12]]
    g_early, g_rep_before, g_x_before = vjp_before(
        (drf + drb + d_r, dvf + dvb + d_v, dkkf + dkkb, dwf, dkf + d_kf, dqf, dwb, dkb + d_kb, dqb, d_gate, d_ymla))
    g_rep = {n: g_rep_before[n] + g_rep_after[n] for n in rep}
    g_x = g_x_before + g_x_after
    parts_early = _grad_exchange(grad_wire(early, g_early), 'exchange_grads')

    s_out = [{}, {}, {}, {}]
    for names, parts, tag in ((early, parts_early, 'early'), (LATE, parts_late, 'late')):
        big, small = split(names)
        for n, p in zip(big, parts):
            res = _sum_adamw(p, _mat(w[n]), _mat(mom[n]), _mat(var[n]), 'adamw_' + n)
            for kind, o in enumerate(res):
                s_out[kind][n] = o.reshape(w[n].shape)
        res = _sum_adamw(parts[-1], _pack([w[n] for n in small]), _pack([mom[n] for n in small]),
                         _pack([var[n] for n in small]), 'adamw_small_' + tag)
        for kind, o in enumerate(res):
            s_out[kind].update(zip(small, _unpack(o, [w[n].shape for n in small])))

    zero = jnp.zeros((1,), F32)
    r_pack = _pack([g_rep[n] for n in REPLICATED] + [loss_local.reshape(1)])
    r_parts = _all_gather([r_pack], 'gather_small')[0]
    r_out = _sum_adamw(r_parts,_pack([w[n] for n in REPLICATED] + [zero]), _pack([mom[n] for n in REPLICATED] + [zero]),
                       _pack([var[n] for n in REPLICATED] + [zero]), 'adamw_replicated')
    r_out = [_unpack(o, [w[n].shape for n in REPLICATED] + [(1,)]) for o in r_out]

    loss = r_out[0][-1].reshape(())
    outs = [loss, g_x]
    for kind in range(4):
        by_name = dict(s_out[kind])
        by_name.update(zip(REPLICATED, r_out[kind][:-1]))
        outs += [by_name[n] for n in WNAMES]
    return tuple(outs)
```

```python
import functools

import jax
import jax.numpy as jnp
from jax import lax
from jax.experimental import pallas as pl
from jax.experimental.pallas import tpu as pltpu

F32 = jnp.float32
BF16 = jnp.bfloat16
MESH = pl.DeviceIdType.MESH

N_DEV = 8
LANES = 128
SUBLANES = 8
PACK_TILE = 2 * SUBLANES * LANES
PACK_ROWS = 512
ADAM_ROWS = 256
MM_TILE = 512
MM_TILE_WIDE = 1408
MM_K_WHOLE = 2816
VMEM_LIMIT = 48 * 1024 * 1024

H = 8
N = 64
D_RWKV = H * N
D_NOPE, D_ROPE, D_V = 64, 32, 64
D_QK = D_NOPE + D_ROPE
MLA_SCALE = D_QK ** -0.5
ROPE_THETA = 10000.0
RWKV_COLS = 1920
MLA_COLS = 1056
D_FF = 2816
NORM_EPS = 1e-6
GN_EPS = 64e-5
L2_EPS = 1e-12
ADAM_LR, ADAM_B1, ADAM_B2, ADAM_EPS, ADAM_WD, ADAM_STEP = 0.001, 0.9, 0.999, 1e-08, 0.01, 10

SCAN_CHUNK = 16
ATT_TQ = 256
SEG = 256
FFN_COLS = 256
ROW_TILE = 256
SHIFT_COLS = 384

WNAMES = ['ln_mix_g', 'w_in', 'shift_mu_prev', 'shift_mu_next', 'decay_w0_fwd', 'decay_w2_fwd', 'decay_w0_bwd',
          'decay_w2_bwd', 'iclr_a0_fwd', 'iclr_a2_fwd', 'iclr_a0_bwd', 'iclr_a2_bwd', 'gate_g2', 'k_k', 'k_a', 'r_k',
          'ln_x_g', 'ln_x_b', 'q_norm_g', 'w_uq', 'kv_norm_g', 'w_ukv', 'mla_out_g', 'w_out', 'ln_ffn_g', 'w_ffn_up',
          'ffn_conv_w', 'ffn_conv_b', 'w_ffn_down', 'ln_final_g']
COL = ('w_in', 'decay_w2_fwd', 'decay_w2_bwd', 'iclr_a2_fwd', 'iclr_a2_bwd', 'gate_g2', 'w_ukv', 'w_ffn_up', 'ffn_conv_w')
ROW = ('w_uq', 'w_out', 'w_ffn_down')
SHARDED = [n for n in WNAMES if n in COL or n in ROW]
REPLICATED = [n for n in WNAMES if n not in SHARDED]
EXACT = ('ffn_conv_w',)
LATE = ['w_out', 'w_ffn_up', 'ffn_conv_w', 'w_ffn_down']
BIG = ('w_in', 'w_uq', 'w_ukv', 'w_out', 'w_ffn_up', 'w_ffn_down')


def _params(*sem):
    return pltpu.CompilerParams(dimension_semantics=sem, vmem_limit_bytes=VMEM_LIMIT)


def _pack(arrs, lead=0):
    parts = []
    for a in arrs:
        head = a.shape[:lead]
        flat = a.reshape(head + (-1,))
        n = flat.shape[-1]
        n_pad = -(-n // PACK_TILE) * PACK_TILE
        flat = jnp.pad(flat, [(0, 0)] * lead + [(0, n_pad - n)])
        parts.append(flat.reshape(head + (n_pad // LANES, LANES)))
    out = jnp.concatenate(parts, axis=lead)
    rows = out.shape[lead]
    rows_pad = -(-rows // PACK_ROWS) * PACK_ROWS
    return jnp.pad(out, [(0, 0)] * lead + [(0, rows_pad - rows), (0, 0)])


def _unpack(packed, shapes, lead=0):
    outs, row = [], 0
    head = packed.shape[:lead]
    for shp in shapes:
        n = 1
        for s in shp:
            n *= s
        rows = -(-n // PACK_TILE) * (PACK_TILE // LANES)
        blk = lax.slice_in_dim(packed, row, row + rows, axis=lead)
        flat = blk.reshape(head + (rows * LANES,))
        outs.append(lax.slice_in_dim(flat, 0, n, axis=lead).reshape(head + tuple(shp)))
        row += rows
    return outs


PEERS = N_DEV - 1
HBM = pl.BlockSpec(memory_space=pl.ANY)


def _comm_sems(n):
    return [pltpu.SemaphoreType.DMA((PEERS * n,)), pltpu.SemaphoreType.DMA((PEERS * n,)), pltpu.SemaphoreType.DMA((n,))]


def _all_gather(xs, name):
    n = len(xs)

    def body(*refs):
        x_refs, out_refs, (send_sems, recv_sems, local_sems) = refs[:n], refs[n:2 * n], refs[2 * n:]
        mx, my, mc = lax.axis_index("x"), lax.axis_index("y"), lax.axis_index("c")
        me, sibling = (mx, my, mc), (mx, my, 1 - mc)
        chips = [(1 - mx, my), (mx, 1 - my), (1 - mx, 1 - my)]

        def slot(a, px, py, pc):
            return out_refs[a].at[4 * px + 2 * py + pc]

        def copy(a, k, block, to, src=None):
            return pltpu.make_async_remote_copy(
                src_ref=slot(a, *block) if src is None else src, dst_ref=slot(a, *block),
                send_sem=send_sems.at[PEERS * a + k], recv_sem=recv_sems.at[PEERS * a + k],
                device_id=to, device_id_type=MESH)

        mine = [pltpu.make_async_copy(x_refs[a], slot(a, *me), local_sems.at[a]) for a in range(n)]
        first, passed = [], []
        for a in range(n):
            mine[a].start()
            first.append(copy(a, 0, me, sibling, src=x_refs[a]))
            first += [copy(a, 1 + j, me, (*chip, mc), src=x_refs[a]) for j, chip in enumerate(chips)]
        for cp in first:
            cp.start()
        for j, chip in enumerate(chips):
            for a in range(n):
                copy(a, 1 + j, (*chip, mc), me).wait_recv()
                passed.append(copy(a, 4 + j, (*chip, mc), sibling))
                passed[-1].start()
        for a in range(n):
            copy(a, 0, sibling, me).wait_recv()
            for j, chip in enumerate(chips):
                copy(a, 4 + j, (*chip, 1 - mc), me).wait_recv()
        for cp in first + passed:
            cp.wait_send()
        for cp in mine:
            cp.wait()

    return pl.pallas_call(
        body, name=name, out_shape=[jax.ShapeDtypeStruct((N_DEV,) + x.shape, x.dtype) for x in xs],
        in_specs=[HBM] * n, out_specs=[HBM] * n, scratch_shapes=_comm_sems(n),
    )(*xs)


def _direct_exchange(src_refs, out_refs, send_sems, recv_sems, local_sems, per_peer):
    mx, my, mc = lax.axis_index("x"), lax.axis_index("y"), lax.axis_index("c")
    me = 4 * mx + 2 * my + mc

    def flip(v, bit):
        return 1 - v if bit else v

    def copies():
        mine, remote = [], []
        for a, (src, out) in enumerate(zip(src_refs, out_refs)):
            mine.append(pltpu.make_async_copy(src.at[me] if per_peer else src, out.at[me], local_sems.at[a]))
            for k in range(1, N_DEV):
                px, py, pc = flip(mx, k & 4), flip(my, k & 2), flip(mc, k & 1)
                remote.append(pltpu.make_async_remote_copy(
                    src_ref=src.at[4 * px + 2 * py + pc] if per_peer else src, dst_ref=out.at[me],
                    send_sem=send_sems.at[PEERS * a + k - 1], recv_sem=recv_sems.at[PEERS * a + k - 1],
                    device_id=(px, py, pc), device_id_type=MESH))
        return mine, remote

    def start():
        mine, remote = copies()
        for cp in mine + remote:
            cp.start()

    def wait():
        mine, remote = copies()
        for cp in remote:
            cp.wait_recv()
        for cp in remote:
            cp.wait_send()
        for cp in mine:
            cp.wait()

    return start, wait


def _grad_exchange(gs, name):
    n = len(gs)

    def body(*refs):
        start, wait = _direct_exchange(refs[:n], refs[n:2 * n], *refs[2 * n:], per_peer=True)
        start()
        wait()

    return pl.pallas_call(
        body, name=name, out_shape=[jax.ShapeDtypeStruct(g.shape, g.dtype) for g in gs],
        in_specs=[HBM] * n, out_specs=[HBM] * n, scratch_shapes=_comm_sems(n),
    )(*gs)


def _sum_adamw(parts, w, m, v, name):
    rows, cols = w.shape
    tr = next((t for t in range(ADAM_ROWS, 15, -16) if rows % t == 0), rows)
    c1 = 1.0 - ADAM_B1 ** ADAM_STEP
    c2 = 1.0 - ADAM_B2 ** ADAM_STEP

    def body(p_ref, w_ref, m_ref, v_ref, g_out, d_out, m_out, v_out):
        g = p_ref[0].astype(F32)
        for q in range(1, N_DEV):
            g = g + p_ref[q].astype(F32)
        m_new = ADAM_B1 * m_ref[...] + (1.0 - ADAM_B1) * g
        v_new = ADAM_B2 * v_ref[...] + (1.0 - ADAM_B2) * (g * g)
        m_hat = m_new / c1
        v_hat = v_new / c2
        g_out[...] = g
        d_out[...] = -ADAM_LR * (m_hat / (jnp.sqrt(v_hat) + ADAM_EPS) + ADAM_WD * w_ref[...])
        m_out[...] = m_new
        v_out[...] = v_new

    blk = pl.BlockSpec((tr, cols), lambda i: (i, 0))
    out = jax.ShapeDtypeStruct((rows, cols), F32)
    return pl.pallas_call(
        body, name=name, grid=(rows // tr,),
        in_specs=[pl.BlockSpec((N_DEV, tr, cols), lambda i: (0, i, 0)), blk, blk, blk],
        out_specs=[blk, blk, blk, blk], out_shape=[out, out, out, out],
        compiler_params=_params("parallel"),
    )(parts, w, m, v)


def _tile(dim, cap=MM_TILE):
    if dim <= cap:
        return dim
    for t in range(cap, LANES - 1, -LANES):
        if dim % t == 0:
            return t
    return dim


def _mm_call(a, b, form, name, out_dtype=F32, base=None):
    if form == 'nn':
        (m, k), n = a.shape, b.shape[1]
    elif form == 'nt':
        (m, k), n = a.shape, b.shape[0]
    else:
        (k, m), n = a.shape, b.shape[1]
    tk = k if (form == 'nn' and k <= MM_K_WHOLE) else _tile(k, MM_TILE_WIDE)
    tm = _tile(m, MM_TILE_WIDE if form == 'tn' else MM_TILE)
    tn = _tile(n, MM_TILE_WIDE)
    nk = k // tk
    contract = {'nn': ((1,), (0,)), 'nt': ((1,), (1,)), 'tn': ((0,), (0,))}[form]

    acc_in_out = nk == 1 or out_dtype == F32
    assert base is None or nk == 1
    extra = [] if base is None else [base]

    def body(a_ref, b_ref, *rest):
        o_ref, acc = rest[len(extra)], rest[len(extra) + 1:]
        part = lax.dot_general(a_ref[...].astype(BF16), b_ref[...].astype(BF16), (contract, ((), ())),
                               preferred_element_type=F32)
        if nk == 1:
            o_ref[...] = (part + rest[0][...] if extra else part).astype(out_dtype)
            return
        acc_ref = o_ref if acc_in_out else acc[0]

        @pl.when(pl.program_id(2) == 0)
        def _():
            acc_ref[...] = part

        @pl.when(pl.program_id(2) > 0)
        def _():
            acc_ref[...] += part

        if not acc_in_out:
            @pl.when(pl.program_id(2) == nk - 1)
            def _():
                o_ref[...] = acc_ref[...].astype(out_dtype)

    a_spec = pl.BlockSpec((tk, tm), lambda j, i, l: (l, i)) if form == 'tn' else pl.BlockSpec((tm, tk), lambda j, i, l: (i, l))
    b_spec = pl.BlockSpec((tn, tk), lambda j, i, l: (j, l)) if form == 'nt' else pl.BlockSpec((tk, tn), lambda j, i, l: (l, j))
    o_spec = pl.BlockSpec((tm, tn), lambda j, i, l: (i, j))
    return pl.pallas_call(
        body, name=name, grid=(n // tn, m // tm, nk),
        in_specs=[a_spec, b_spec] + [o_spec] * len(extra), out_specs=o_spec,
        out_shape=jax.ShapeDtypeStruct((m, n), out_dtype),
        scratch_shapes=[] if acc_in_out else [pltpu.VMEM((tm, tn), F32)],
        compiler_params=_params("parallel", "parallel", "arbitrary"),
    )(a, b, *extra)


def _make_mm(name):
    @jax.custom_vjp
    def mm(a, b):
        return _mm_call(a, b, 'nn', name + '_fwd')

    def fwd(a, b):
        return _mm_call(a, b, 'nn', name + '_fwd'), (a, b)

    def bwd(res, g):
        a, b = res
        return _mm_call(g, b, 'nt', name + '_da'), _mm_call(a, g, 'tn', name + '_db', out_dtype=BF16)

    mm.defvjp(fwd, bwd)
    return mm


def _make_mm_add(name):
    @jax.custom_vjp
    def mm(base, a, b):
        return _mm_call(a, b, 'nn', name + '_fwd', base=base)

    def fwd(base, a, b):
        return _mm_call(a, b, 'nn', name + '_fwd', base=base), (a, b)

    def bwd(res, g):
        a, b = res
        return g, _mm_call(g, b, 'nt', name + '_da'), _mm_call(a, g, 'tn', name + '_db', out_dtype=BF16)

    mm.defvjp(fwd, bwd)
    return mm


def _rms_fwd_call(x, g, name):
    m, d = x.shape
    tm = _tile(m)

    def body(x_ref, g_ref, o_ref):
        xv = x_ref[...]
        rinv = lax.rsqrt(jnp.mean(xv * xv, axis=-1, keepdims=True) + NORM_EPS)
        o_ref[...] = xv * rinv * g_ref[...]

    return pl.pallas_call(
        body, name=name, grid=(m // tm,),
        in_specs=[pl.BlockSpec((tm, d), lambda i: (i, 0)), pl.BlockSpec((1, d), lambda i: (0, 0))],
        out_specs=pl.BlockSpec((tm, d), lambda i: (i, 0)), out_shape=jax.ShapeDtypeStruct((m, d), F32),
        compiler_params=_params("parallel"),
    )(x, g)


def _rms_bwd_call(x, g, dy, name, d_skip=None):
    m, d = x.shape
    tm = _tile(m)
    extra = [] if d_skip is None else [d_skip]

    def body(x_ref, g_ref, dy_ref, *rest):
        dx_ref, dg_ref = rest[len(extra):]

        @pl.when(pl.program_id(0) == 0)
        def _():
            dg_ref[...] = jnp.zeros_like(dg_ref)

        xv, dyv = x_ref[...], dy_ref[...]
        rinv = lax.rsqrt(jnp.mean(xv * xv, axis=-1, keepdims=True) + NORM_EPS)
        xh = xv * rinv
        dg_ref[...] += jnp.sum(dyv * xh, axis=0, keepdims=True)
        dxh = dyv * g_ref[...]
        dx = rinv * (dxh - xh * jnp.mean(dxh * xh, axis=-1, keepdims=True))
        dx_ref[...] = dx + rest[0][...] if extra else dx

    row = pl.BlockSpec((tm, d), lambda i: (i, 0))
    vec = pl.BlockSpec((1, d), lambda i: (0, 0))
    return pl.pallas_call(
        body, name=name, grid=(m // tm,), in_specs=[row, vec, row] + [row] * len(extra), out_specs=[row, vec],
        out_shape=[jax.ShapeDtypeStruct((m, d), F32), jax.ShapeDtypeStruct((1, d), F32)],
        compiler_params=_params("arbitrary"),
    )(x, g, dy, *extra)


def _make_rms(name):
    @jax.custom_vjp
    def rms(x, g):
        return _rms_fwd_call(x, g, name + '_fwd')

    def fwd(x, g):
        return _rms_fwd_call(x, g, name + '_fwd'), (x, g)

    def bwd(res, dy):
        x, g = res
        dx, dg = _rms_bwd_call(x, g, dy, name + '_bwd')
        return dx, dg

    rms.defvjp(fwd, bwd)
    return rms


def _make_rms_skip(name):
    @jax.custom_vjp
    def rms(x, g):
        return _rms_fwd_call(x, g, name + '_fwd'), x

    def fwd(x, g):
        return (_rms_fwd_call(x, g, name + '_fwd'), x), (x, g)

    def bwd(res, cts):
        x, g = res
        dx, dg = _rms_bwd_call(x, g, cts[0], name + '_bwd', d_skip=cts[1])
        return dx, dg

    rms.defvjp(fwd, bwd)
    return rms


def _time_shifts(x):
    t = x.shape[0]
    rows = lax.broadcasted_iota(jnp.int32, x.shape, 0)
    return (jnp.where(rows == 0, 0.0, pltpu.roll(x, 1, 0)), jnp.where(rows == t - 1, 0.0, pltpu.roll(x, t - 1, 0)))


def _conv3(x, cw_ref, cb_ref):
    xp, xn = _time_shifts(x)
    return cw_ref[0:1, :] * xp + cw_ref[1:2, :] * x + cw_ref[2:3, :] * xn + cb_ref[...]


def _glu_specs(b, t, f):
    tc = _tile(f, FFN_COLS)
    seq = pl.BlockSpec((1, t, tc), lambda j, bi: (bi, 0, j))
    cw = pl.BlockSpec((3, tc), lambda j, bi: (0, j))
    cb = pl.BlockSpec((1, tc), lambda j, bi: (0, j))
    return tc, seq, cw, cb


def _glu_fwd_call(ug, uv, cwg, cwv, cbg, cbv):
    b, t, f = ug.shape
    tc, seq, cw, cb = _glu_specs(b, t, f)

    def body(ug_ref, uv_ref, cwg_ref, cwv_ref, cbg_ref, cbv_ref, o_ref):
        g = _conv3(ug_ref[0], cwg_ref, cbg_ref)
        o_ref[0] = g * jax.nn.sigmoid(g) * _conv3(uv_ref[0], cwv_ref, cbv_ref)

    return pl.pallas_call(
        body, name='glu_fwd', grid=(f // tc, b), in_specs=[seq, seq, cw, cw, cb, cb], out_specs=seq,
        out_shape=jax.ShapeDtypeStruct((b, t, f), F32), compiler_params=_params("parallel", "parallel"),
    )(ug, uv, cwg, cwv, cbg, cbv)


def _glu_bwd_call(ug, uv, cwg, cwv, cbg, cbv, dact):
    b, t, f = ug.shape
    tc, seq, cw, cb = _glu_specs(b, t, f)

    def body(ug_ref, uv_ref, cwg_ref, cwv_ref, cbg_ref, cbv_ref, da_ref,
             dug_ref, duv_ref, dcwg_ref, dcwv_ref, dcbg_ref, dcbv_ref):
        @pl.when(pl.program_id(1) == 0)
        def _():
            for ref in (dcwg_ref, dcwv_ref, dcbg_ref, dcbv_ref):
                ref[...] = jnp.zeros_like(ref)

        g = _conv3(ug_ref[0], cwg_ref, cbg_ref)
        v = _conv3(uv_ref[0], cwv_ref, cbv_ref)
        sig = jax.nn.sigmoid(g)
        da = da_ref[0]
        dv = da * (g * sig)
        dg = da * v * (sig * (1.0 + g * (1.0 - sig)))

        def conv_bwd(dc, x_ref, cw_ref, dx_ref, dcw_ref, dcb_ref):
            dcp, dcn = _time_shifts(dc)
            dx_ref[0] = cw_ref[0:1, :] * dcn + cw_ref[1:2, :] * dc + cw_ref[2:3, :] * dcp
            x = x_ref[0]
            xp, xn = _time_shifts(x)
            for n, xs in enumerate((xp, x, xn)):
                dcw_ref[n:n + 1, :] += jnp.sum(dc * xs, axis=0, keepdims=True)
            dcb_ref[...] += jnp.sum(dc, axis=0, keepdims=True)

        conv_bwd(dg, ug_ref, cwg_ref, dug_ref, dcwg_ref, dcbg_ref)
        conv_bwd(dv, uv_ref, cwv_ref, duv_ref, dcwv_ref, dcbv_ref)

    big = jax.ShapeDtypeStruct((b, t, f), F32)
    return pl.pallas_call(
        body, name='glu_bwd', grid=(f // tc, b), in_specs=[seq, seq, cw, cw, cb, cb, seq],
        out_specs=[seq, seq, cw, cw, cb, cb],
        out_shape=[big, big, jax.ShapeDtypeStruct((3, f), F32), jax.ShapeDtypeStruct((3, f), F32),
                   jax.ShapeDtypeStruct((1, f), F32), jax.ShapeDtypeStruct((1, f), F32)],
        compiler_params=_params("parallel", "arbitrary"),
    )(ug, uv, cwg, cwv, cbg, cbv, dact)


def _shift_call(z, mu_p, mu_n, dzs=None):
    b, t, c = z.shape
    tc = _tile(c, SHIFT_COLS)
    seq = pl.BlockSpec((1, t, tc), lambda j, bi: (bi, 0, j))
    row = pl.BlockSpec((1, tc), lambda j, bi: (0, j))

    def fwd_body(z_ref, mp_ref, mn_ref, o_ref):
        x = z_ref[0]
        xp, xn = _time_shifts(x)
        o_ref[0] = x + mp_ref[...] * (xp - x) + mn_ref[...] * (xn - x)

    def bwd_body(z_ref, mp_ref, mn_ref, d_ref, dz_ref, dmp_ref, dmn_ref):
        @pl.when(pl.program_id(1) == 0)
        def _():
            dmp_ref[...] = jnp.zeros_like(dmp_ref)
            dmn_ref[...] = jnp.zeros_like(dmn_ref)

        x, d = z_ref[0], d_ref[0]
        xp, xn = _time_shifts(x)
        dp, dn = _time_shifts(d)
        mp, mn = mp_ref[...], mn_ref[...]
        dz_ref[0] = d * (1.0 - mp - mn) + mp * dn + mn * dp
        dmp_ref[...] += jnp.sum(d * (xp - x), axis=0, keepdims=True)
        dmn_ref[...] += jnp.sum(d * (xn - x), axis=0, keepdims=True)

    if dzs is None:
        return pl.pallas_call(
            fwd_body, name='shift_fwd', grid=(c // tc, b), in_specs=[seq, row, row], out_specs=seq,
            out_shape=jax.ShapeDtypeStruct(z.shape, F32), compiler_params=_params("parallel", "parallel"),
        )(z, mu_p, mu_n)
    return pl.pallas_call(
        bwd_body, name='shift_bwd', grid=(c // tc, b), in_specs=[seq, row, row, seq], out_specs=[seq, row, row],
        out_shape=[jax.ShapeDtypeStruct(z.shape, F32), jax.ShapeDtypeStruct(mu_p.shape, F32),
                   jax.ShapeDtypeStruct(mu_n.shape, F32)],
        compiler_params=_params("parallel", "arbitrary"),
    )(z, mu_p, mu_n, dzs)


@jax.custom_vjp
def _token_shift(z, mu_p, mu_n):
    return _shift_call(z, mu_p, mu_n)


_token_shift.defvjp(lambda z, mu_p, mu_n: (_shift_call(z, mu_p, mu_n), (z, mu_p, mu_n)),
                    lambda res, d: tuple(_shift_call(*res, dzs=d)))


@jax.custom_vjp
def _conv_glu(ug, uv, cwg, cwv, cbg, cbv):
    return _glu_fwd_call(ug, uv, cwg, cwv, cbg, cbv)


def _conv_glu_fwd(*args):
    return _glu_fwd_call(*args), args


def _conv_glu_bwd(res, dact):
    return tuple(_glu_bwd_call(*res, dact))


_conv_glu.defvjp(_conv_glu_fwd, _conv_glu_bwd)


def _scores(q_ref, k_ref):
    q16 = (q_ref[0] * MLA_SCALE).astype(BF16)
    return lax.dot_general(q16, k_ref[0].astype(BF16), (((1,), (1,)), ((), ())), preferred_element_type=F32), q16


def _attn_fwd_call(q, k, v, name):
    bh, t, _ = q.shape
    tq = min(ATT_TQ, t)

    def body(q_ref, k_ref, v_ref, o_ref, lse_ref):
        s, _ = _scores(q_ref, k_ref)
        m = jnp.max(s, axis=-1, keepdims=True)
        p = jnp.exp(s - m)
        l = jnp.sum(p, axis=-1, keepdims=True)
        o_ref[0] = jnp.dot(p.astype(BF16), v_ref[0].astype(BF16), preferred_element_type=F32) / l
        lse_ref[0] = m + jnp.log(l)

    return pl.pallas_call(
        body, name=name, grid=(bh, t // tq),
        in_specs=[pl.BlockSpec((1, tq, D_QK), lambda b, i: (b, i, 0)), pl.BlockSpec((1, t, D_QK), lambda b, i: (b, 0, 0)),
                  pl.BlockSpec((1, t, D_V), lambda b, i: (b, 0, 0))],
        out_specs=[pl.BlockSpec((1, tq, D_V), lambda b, i: (b, i, 0)), pl.BlockSpec((1, tq, 1), lambda b, i: (b, i, 0))],
        out_shape=[jax.ShapeDtypeStruct((bh, t, D_V), F32), jax.ShapeDtypeStruct((bh, t, 1), F32)],
        compiler_params=_params("parallel", "parallel"),
    )(q, k, v)


def _attn_bwd_call(q, k, v, o, lse, do, name):
    bh, t, _ = q.shape
    tq = min(ATT_TQ, t)

    def body(q_ref, k_ref, v_ref, o_ref, lse_ref, do_ref, dq_ref, dk_ref, dv_ref):
        @pl.when(pl.program_id(1) == 0)
        def _():
            dk_ref[...] = jnp.zeros_like(dk_ref)
            dv_ref[...] = jnp.zeros_like(dv_ref)

        s, q16 = _scores(q_ref, k_ref)
        p = jnp.exp(s - lse_ref[0])
        dov = do_ref[0]
        do_b = dov.astype(BF16)
        delta = jnp.sum(dov * o_ref[0], axis=-1, keepdims=True)
        dp = lax.dot_general(do_b, v_ref[0].astype(BF16), (((1,), (1,)), ((), ())), preferred_element_type=F32)
        ds = (p * (dp - delta)).astype(BF16)
        dq_ref[0] = jnp.dot(ds, k_ref[0].astype(BF16), preferred_element_type=F32) * MLA_SCALE
        dk_ref[0] += lax.dot_general(ds, q16, (((0,), (0,)), ((), ())), preferred_element_type=F32)
        dv_ref[0] += lax.dot_general(p.astype(BF16), do_b, (((0,), (0,)), ((), ())), preferred_element_type=F32)

    qspec = pl.BlockSpec((1, tq, D_QK), lambda b, i: (b, i, 0))
    kspec = pl.BlockSpec((1, t, D_QK), lambda b, i: (b, 0, 0))
    vspec = pl.BlockSpec((1, t, D_V), lambda b, i: (b, 0, 0))
    ospec = pl.BlockSpec((1, tq, D_V), lambda b, i: (b, i, 0))
    lspec = pl.BlockSpec((1, tq, 1), lambda b, i: (b, i, 0))
    return pl.pallas_call(
        body, name=name, grid=(bh, t // tq),
        in_specs=[qspec, kspec, vspec, ospec, lspec, ospec], out_specs=[qspec, kspec, vspec],
        out_shape=[jax.ShapeDtypeStruct(q.shape, F32), jax.ShapeDtypeStruct(k.shape, F32), jax.ShapeDtypeStruct(v.shape, F32)],
        compiler_params=_params("parallel", "arbitrary"),
    )(q, k, v, o, lse, do)


@jax.custom_vjp
def _attention(q, k, v):
    return _attn_fwd_call(q, k, v, 'attn_fwd')[0]


def _attention_fwd(q, k, v):
    o, lse = _attn_fwd_call(q, k, v, 'attn_fwd')
    return o, (q, k, v, o, lse)


def _attention_bwd(res, do):
    return tuple(_attn_bwd_call(*res, do, 'attn_bwd'))


_attention.defvjp(_attention_fwd, _attention_bwd)


SROWS = N * D_RWKV // SEG


def _seg_ones():
    r = lax.broadcasted_iota(jnp.int32, (SEG, SEG), 0) >> 6
    c = lax.broadcasted_iota(jnp.int32, (SEG, SEG), 1) >> 6
    return (r == c).astype(BF16)


def _eye_mask():
    r = lax.broadcasted_iota(jnp.int32, (SROWS, SEG), 0) & (N - 1)
    c = lax.broadcasted_iota(jnp.int32, (SROWS, SEG), 1) & (N - 1)
    return r == c


def _row2(ref, bi, ti, dtype=F32):
    parts = [jnp.broadcast_to(ref[bi, pl.ds(ti, 1), pl.ds(SEG * q, SEG)].astype(dtype), (N, SEG))
             for q in range(D_RWKV // SEG)]
    return jnp.concatenate(parts, axis=0)


def _split2(x):
    hi = x.astype(BF16)
    return hi, (x - hi.astype(F32)).astype(BF16)


def _col_sum(x):
    return jnp.concatenate([jnp.sum(x[N * q:N * (q + 1)], axis=0, keepdims=True) for q in range(D_RWKV // SEG)], axis=1)


def _scan_specs(b, t, rev):
    nc = t // SCAN_CHUNK
    if rev:
        return (pl.BlockSpec((b, SCAN_CHUNK, D_RWKV), lambda c: (0, nc - 1 - c, 0)),
                pl.BlockSpec((b, SCAN_CHUNK, SROWS, SEG), lambda c: (0, nc - 1 - c, 0, 0)))
    return (pl.BlockSpec((b, SCAN_CHUNK, D_RWKV), lambda c: (0, c, 0)),
            pl.BlockSpec((b, SCAN_CHUNK, SROWS, SEG), lambda c: (0, c, 0, 0)))


def _scan_fwd_call(r, v, kk, wf, kf, qf, wb, kb, qb, ride):
    b, t, _ = r.shape
    n_chunks = t // SCAN_CHUNK
    last = SCAN_CHUNK - 1
    nr = len(ride)

    def body(rf, vf, kkf, wf_, kf_, qf_, rb, vb, kkb, wb_, kb_, qb_, *rest):
        (yf, yb, sf, sb), scratch = rest[nr:nr + 4], rest[2 * nr + 4:]
        states = scratch[:2 * b]
        send, arrive = _direct_exchange(rest[:nr], rest[nr + 4:2 * nr + 4], *scratch[2 * b:], per_peer=False)

        @pl.when(pl.program_id(0) == 0)
        def _():
            send()
            for st in states:
                st[...] = jnp.zeros_like(st)

        ones, mask = _seg_ones(), _eye_mask()
        zero16 = jnp.zeros((), BF16)
        chains = []
        for bi in range(b):
            chains.append((rf, vf, kkf, wf_, kf_, qf_, yf, sf, states[2 * bi], bi, False))
            chains.append((rb, vb, kkb, wb_, kb_, qb_, yb, sb, states[2 * bi + 1], bi, True))

        def tix(i, rev):
            return last - i if rev else i

        def put_y(y_, bi, ti, ycol):
            y_[bi, pl.ds(ti, 1), :] = _col_sum(jnp.where(mask, ycol, 0.0))

        def steps(i, with_y):
            parts = []
            for (r_, v_, kk_, w_, k_, q_, y_, s_, st, bi, rev) in chains:
                ti = tix(i, rev)
                s = st[...]
                s_[bi, ti] = s
                parts.append((s * _row2(kk_, bi, ti)).astype(BF16))
                parts.append(jnp.where(mask, _row2(v_, bi, ti, BF16), zero16))
                if with_y:
                    parts.append((s * _row2(r_, bi, tix(i - 1, rev))).astype(BF16))
            res = jnp.dot(jnp.concatenate(parts, axis=0), ones, preferred_element_type=F32)
            off = 0
            for (r_, v_, kk_, w_, k_, q_, y_, s_, st, bi, rev) in chains:
                ti = tix(i, rev)
                u = res[off:off + SROWS]
                vcol = res[off + SROWS:off + 2 * SROWS]
                off += 2 * SROWS
                if with_y:
                    put_y(y_, bi, tix(i - 1, rev), res[off:off + SROWS])
                    off += SROWS
                st[...] = st[...] * _row2(w_, bi, ti) - u * _row2(q_, bi, ti) + vcol * _row2(k_, bi, ti)

        steps(0, False)

        def loop(i, carry):
            steps(i, True)
            return carry

        lax.fori_loop(1, SCAN_CHUNK, loop, 0, unroll=5)
        parts = [(c[8][...] * _row2(c[0], c[9], tix(last, c[10]))).astype(BF16) for c in chains]
        res = jnp.dot(jnp.concatenate(parts, axis=0), ones, preferred_element_type=F32)
        for n, c in enumerate(chains):
            put_y(c[6], c[9], tix(last, c[10]), res[n * SROWS:(n + 1) * SROWS])

        @pl.when(pl.program_id(0) == n_chunks - 1)
        def _():
            arrive()

    fr, fs = _scan_specs(b, t, False)
    br, bs = _scan_specs(b, t, True)
    y_shape = jax.ShapeDtypeStruct((b, t, D_RWKV), F32)
    s_shape = jax.ShapeDtypeStruct((b, t, SROWS, SEG), F32)
    return pl.pallas_call(
        body, name='scan_fwd', grid=(n_chunks,),
        in_specs=[fr] * 6 + [br] * 6 + [HBM] * nr, out_specs=[fr, br, fs, bs] + [HBM] * nr,
        out_shape=[y_shape, y_shape, s_shape, s_shape] + [jax.ShapeDtypeStruct((N_DEV,) + a.shape, a.dtype) for a in ride],
        scratch_shapes=[pltpu.VMEM((SROWS, SEG), F32)] * (2 * b) + _comm_sems(nr),
        compiler_params=_params("arbitrary"),
    )(r, v, kk, wf, kf, qf, r, v, kk, wb, kb, qb, *ride)


def _scan_bwd_call(r, v, kk, wf, kf, qf, wb, kb, qb, sf, sb, dyf, dyb, ride):
    b, t, _ = r.shape
    n_chunks = t // SCAN_CHUNK
    last = SCAN_CHUNK - 1
    nr = len(ride)

    def body(rf, vf, kkf, wf_, kf_, qf_, sf_, dyf_, rb, vb, kkb, wb_, kb_, qb_, sb_, dyb_, *rest):
        drf, dvf, dkkf, dwf, dkf, dqf, drb, dvb, dkkb, dwb, dkb, dqb = rest[nr:nr + 12]
        scratch = rest[2 * nr + 12:]
        send, arrive = _direct_exchange(rest[:nr], rest[nr + 12:2 * nr + 12], *scratch[8 * b:], per_peer=True)

        @pl.when(pl.program_id(0) == 0)
        def _():
            send()
            for n in range(2 * b):
                scratch[4 * n][...] = jnp.zeros_like(scratch[4 * n])

        ones, mask = _seg_ones(), _eye_mask()
        zero16 = jnp.zeros((), BF16)
        chains = []
        for bi in range(b):
            chains.append((rf, vf, kkf, wf_, kf_, qf_, sf_, dyf_, (drf, dvf, dkkf, dwf, dkf, dqf),
                           scratch[8 * bi:8 * bi + 4], bi, True))
            chains.append((rb, vb, kkb, wb_, kb_, qb_, sb_, dyb_, (drb, dvb, dkkb, dwb, dkb, dqb),
                           scratch[8 * bi + 4:8 * bi + 8], bi, False))

        def tix(i, rev):
            return last - i if rev else i

        def state_free_parts(v_, dy_, kk_, s_, bi, ti):
            return [jnp.where(mask, _row2(v_, bi, ti, BF16), zero16), jnp.where(mask, _row2(dy_, bi, ti, BF16), zero16),
                    (s_[bi, ti] * _row2(kk_, bi, ti)).astype(BF16)]

        def keep(scr, res, off):
            for n in range(3):
                scr[1 + n][...] = res[off + n * SROWS:off + (n + 1) * SROWS]
            return off + 3 * SROWS

        def first():
            parts = []
            for (r_, v_, kk_, w_, k_, q_, s_, dy_, outs, scr, bi, rev) in chains:
                parts += state_free_parts(v_, dy_, kk_, s_, bi, tix(0, rev))
            res = jnp.dot(jnp.concatenate(parts, axis=0), ones, preferred_element_type=F32)
            off = 0
            for c in chains:
                off = keep(c[9], res, off)

        def steps(i, has_next, recompute):
            parts = []
            for (r_, v_, kk_, w_, k_, q_, s_, dy_, outs, scr, bi, rev) in chains:
                ti = tix(i, rev)
                gst, vc, dc, uc = scr
                dycol = dc[...]
                if recompute:
                    sc = s_[bi, ti] * _row2(w_, bi, ti) - uc[...] * _row2(q_, bi, ti) + vc[...] * _row2(k_, bi, ti)
                else:
                    sc = s_[bi, tix(i - 1, rev)]
                outs[0][bi, pl.ds(ti, 1), :] = _col_sum(sc * dycol)
                g = gst[...] + dycol * _row2(r_, bi, ti)
                gst[...] = g
                parts.append((g * _row2(q_, bi, ti)).astype(BF16))
                parts.append((g * _row2(k_, bi, ti)).astype(BF16))
                if has_next:
                    parts += state_free_parts(v_, dy_, kk_, s_, bi, tix(i + 1, rev))
            res = jnp.dot(jnp.concatenate(parts, axis=0), ones, preferred_element_type=F32)
            off = 0
            for (r_, v_, kk_, w_, k_, q_, s_, dy_, outs, scr, bi, rev) in chains:
                ti = tix(i, rev)
                gst, vc, dc, uc = scr
                dr_, dv_, dkk_, dw_, dk_, dq_ = outs

                def put(ref, val, sign=1.0):
                    ref[bi, pl.ds(ti, 1), :] = sign * _col_sum(val)

                gq = res[off:off + SROWS]
                put(dv_, jnp.where(mask, res[off + SROWS:off + 2 * SROWS], 0.0))
                off += 2 * SROWS
                g, sp = gst[...], s_[bi, ti]
                put(dk_, g * vc[...])
                put(dw_, g * sp)
                put(dq_, g * uc[...], -1.0)
                put(dkk_, sp * gq, -1.0)
                gst[...] = g * _row2(w_, bi, ti) - gq * _row2(kk_, bi, ti)
                if has_next:
                    off = keep(scr, res, off)

        first()
        steps(0, True, True)

        def loop(i, carry):
            steps(i, True, False)
            return carry

        lax.fori_loop(1, last, loop, 0)
        steps(last, False, False)

        @pl.when(pl.program_id(0) == n_chunks - 1)
        def _():
            arrive()

    fr, fs = _scan_specs(b, t, True)
    br, bs = _scan_specs(b, t, False)
    y_shape = jax.ShapeDtypeStruct((b, t, D_RWKV), F32)
    return pl.pallas_call(
        body, name='scan_bwd', grid=(n_chunks,),
        in_specs=[fr] * 6 + [fs, fr] + [br] * 6 + [bs, br] + [HBM] * nr,
        out_specs=[fr] * 6 + [br] * 6 + [HBM] * nr,
        out_shape=[y_shape] * 12 + [jax.ShapeDtypeStruct(a.shape, a.dtype) for a in ride],
        scratch_shapes=[pltpu.VMEM((SROWS, SEG), F32)] * (8 * b) + _comm_sems(nr),
        compiler_params=_params("arbitrary"),
    )(r, v, kk, wf, kf, qf, sf, dyf, r, v, kk, wb, kb, qb, sb, dyb, *ride)


def _rope_tables(t):
    inv_freq = jnp.power(ROPE_THETA, -jnp.arange(0, D_ROPE, 2, dtype=F32) / D_ROPE)
    ang = jnp.arange(t, dtype=F32)[:, None] * inv_freq[None, :]
    ang = jnp.concatenate([ang, ang], axis=-1)
    return jnp.cos(ang), jnp.sin(ang)


def _rope(x, cos, sin):
    x1, x2 = jnp.split(x, 2, axis=-1)
    return x * cos + jnp.concatenate([-x2, x1], axis=-1) * sin


@jax.custom_vjp
def _dot16(a, w):
    return jnp.dot(a.astype(BF16), w.astype(BF16), preferred_element_type=F32)


def _dot16_fwd(a, w):
    a16, w16 = a.astype(BF16), w.astype(BF16)
    return jnp.dot(a16, w16, preferred_element_type=F32), (a16, w16)


def _dot16_bwd(res, g):
    a16, w16 = res
    g16 = g.astype(BF16)
    return (lax.dot_general(g16, w16, (((1,), (1,)), ((), ())), preferred_element_type=F32),
            lax.dot_general(a16, g16, (((0,), (0,)), ((), ())), preferred_element_type=F32))


_dot16.defvjp(_dot16_fwd, _dot16_bwd)


def _head_sum_tile(x):
    outs = []
    ones = _seg_ones()
    for q in range(x.shape[1] // SEG):
        hi, lo = _split2(x[:, SEG * q:SEG * (q + 1)])
        outs.append(jnp.dot(hi, ones, preferred_element_type=F32) + jnp.dot(lo, ones, preferred_element_type=F32))
    return jnp.concatenate(outs, axis=1)


@jax.custom_vjp
def _hsum(x):
    return _head_sum_tile(x)


_hsum.defvjp(lambda x: (_head_sum_tile(x), None), lambda _, g: (_head_sum_tile(g),))


def _softplus(x):
    return jnp.maximum(x, 0.0) + jnp.log(1.0 + jnp.exp(-jnp.abs(x)))


def _rwkv_pre_fn(k, wdf, wdb, adf, adb, gd, w0f, w2f, w0b, w2b, a0f, a2f, a0b, a2b, g2, k_k, k_a):
    w_f = jnp.exp(-jnp.exp(-_softplus(-(w0f + _dot16(jnp.tanh(wdf), w2f))) - 0.5))
    w_b = jnp.exp(-jnp.exp(-_softplus(-(w0b + _dot16(jnp.tanh(wdb), w2b))) - 0.5))
    a_f = jax.nn.sigmoid(a0f + _dot16(adf, a2f))
    a_b = jax.nn.sigmoid(a0b + _dot16(adb, a2b))
    gate = _dot16(jax.nn.sigmoid(gd), g2)
    kk = k * k_k
    kk = kk / jnp.maximum(jnp.sqrt(_hsum(kk * kk)), L2_EPS)
    return (kk, w_f, k * (1.0 + (a_f - 1.0) * k_a), kk * a_f, w_b, k * (1.0 + (a_b - 1.0) * k_a), kk * a_b, gate)


def _rwkv_post_fn(y_f, y_b, r, k_f, k_b, v, gate, ln_g, ln_b, r_k):
    y = y_f + y_b
    yc = y - _hsum(y) * (1.0 / N)
    var = _hsum(yc * yc) * (1.0 / N)
    y = yc * lax.rsqrt(var + GN_EPS) * ln_g + ln_b
    return ((y + _hsum(r * (k_f + k_b) * r_k) * v) * gate,)


def _make_rowwise(fn, name, n_rows, tm):
    def specs(arrs, whole):
        if whole:
            return [pl.BlockSpec(a.shape, lambda i: (0, 0)) for a in arrs]
        return [pl.BlockSpec((tm, a.shape[1]), lambda i: (i, 0)) for a in arrs]

    def out_widths(rows, params):
        tiles = [jax.ShapeDtypeStruct((tm, a.shape[1]), F32) for a in rows]
        return [o.shape[1] for o in jax.eval_shape(fn, *tiles, *params)]

    def fwd_call(rows, params):
        m = rows[0].shape[0]
        n_in = len(rows) + len(params)
        outs = [jax.ShapeDtypeStruct((m, d), F32) for d in out_widths(rows, params)]

        def body(*refs):
            for o_ref, o in zip(refs[n_in:], fn(*[ref[...] for ref in refs[:n_in]])):
                o_ref[...] = o

        return pl.pallas_call(
            body, name=name + '_fwd', grid=(m // tm,), in_specs=specs(rows, False) + specs(params, True),
            out_specs=specs(outs, False), out_shape=outs, compiler_params=_params("parallel"),
        )(*rows, *params)

    def bwd_call(rows, params, cts):
        m = rows[0].shape[0]
        n_in = len(rows) + len(params)
        n_all = n_in + len(cts)
        outs = ([jax.ShapeDtypeStruct(a.shape, F32) for a in rows] + [jax.ShapeDtypeStruct(a.shape, F32) for a in params])

        def body(*refs):
            _, vjp = jax.vjp(fn, *[ref[...] for ref in refs[:n_in]])
            grads = vjp(tuple(ref[...] for ref in refs[n_in:n_all]))
            d_rows, d_params = refs[n_all:n_all + len(rows)], refs[n_all + len(rows):]
            for ref, g in zip(d_rows, grads[:len(rows)]):
                ref[...] = g

            @pl.when(pl.program_id(0) == 0)
            def _():
                for ref in d_params:
                    ref[...] = jnp.zeros_like(ref)

            for ref, g in zip(d_params, grads[len(rows):]):
                ref[...] += g

        return pl.pallas_call(
            body, name=name + '_bwd', grid=(m // tm,),
            in_specs=specs(rows, False) + specs(params, True) + specs(cts, False),
            out_specs=specs(rows, False) + specs(params, True), out_shape=outs, compiler_params=_params("arbitrary"),
        )(*rows, *params, *cts)

    @jax.custom_vjp
    def op(*args):
        return tuple(fwd_call(args[:n_rows], args[n_rows:]))

    def op_fwd(*args):
        return tuple(fwd_call(args[:n_rows], args[n_rows:])), args

    def op_bwd(args, cts):
        return tuple(bwd_call(args[:n_rows], args[n_rows:], cts))

    op.defvjp(op_fwd, op_bwd)
    return op


def _rwkv_operands(z, full, rep):
    b, t, _ = z.shape
    m = b * t
    z = _token_shift(z, rep['shift_mu_prev'], rep['shift_mu_next']).reshape(m, RWKV_COLS)
    r, k, v = z[:, :512], z[:, 512:1024], z[:, 1024:1536]
    lora_in = (z[:, 1536:1600], z[:, 1600:1664], z[:, 1664:1728], z[:, 1728:1792], z[:, 1792:1920])
    kk, w_f, k_f, q_f, w_b, k_b, q_b, gate = _make_rowwise(_rwkv_pre_fn, 'rwkv_pre', 6, _tile(m, ROW_TILE))(
        k, *lora_in, rep['decay_w0_fwd'], full['decay_w2_fwd'], rep['decay_w0_bwd'], full['decay_w2_bwd'],
        rep['iclr_a0_fwd'], full['iclr_a2_fwd'], rep['iclr_a0_bwd'], full['iclr_a2_bwd'], full['gate_g2'],
        rep['k_k'], rep['k_a'])
    return r, v, kk, w_f, k_f, q_f, w_b, k_b, q_b, gate


def _mla_mixer(z, full, rep, b, t):
    m = b * t
    c_q, c_kv, k_rope = z[:, :768], z[:, 768:1024], z[:, 1024:1056]
    cos, sin = _rope_tables(t)
    q = _make_mm('mm_uq')(_make_rms('rms_q')(c_q, rep['q_norm_g']), full['w_uq']).reshape(b, t, H, D_QK)
    q = jnp.concatenate([q[..., :D_NOPE], _rope(q[..., D_NOPE:], cos[:, None, :], sin[:, None, :])], axis=-1)
    kv = _make_mm('mm_ukv')(_make_rms('rms_kv')(c_kv, rep['kv_norm_g']), full['w_ukv']).reshape(b, t, H, D_NOPE + D_V)
    k_rope = _rope(k_rope.reshape(b, t, D_ROPE), cos, sin)
    k = jnp.concatenate([kv[..., :D_NOPE], jnp.broadcast_to(k_rope[:, :, None, :], (b, t, H, D_ROPE))], axis=-1)
    heads = lambda a: a.transpose(0, 2, 1, 3).reshape(b * H, t, a.shape[-1])
    o = _attention(heads(q), heads(k), heads(kv[..., D_NOPE:]))
    o = o.reshape(b, H, t, D_V).transpose(0, 2, 1, 3).reshape(m, H * D_V)
    return _make_rms('rms_mla_out')(o, rep['mla_out_g'])


def _before_scan(full, rep, x):
    b, t, d = x.shape
    m = b * t
    n1 = _make_rms('rms_mix')(x.reshape(m, d), rep['ln_mix_g'])
    d_in = full['w_in'].shape[1]
    d_in_pad = -(-d_in // MM_TILE) * MM_TILE
    z = _make_mm('mm_in')(n1, jnp.pad(full['w_in'], ((0, 0), (0, d_in_pad - d_in))))
    return (*_rwkv_operands(z[:, :RWKV_COLS].reshape(b, t, RWKV_COLS), full, rep),
            _mla_mixer(z[:, RWKV_COLS:d_in], full, rep, b, t))


def _after_scan(full, rep, x, target, y_f, y_b, r, k_f, k_b, v, gate, y_mla):
    b, t, d = x.shape
    m = b * t
    xf = x.reshape(m, d)
    y_rwkv = _make_rowwise(_rwkv_post_fn, 'rwkv_post', 7, _tile(m, ROW_TILE))(
        y_f.reshape(m, D_RWKV), y_b.reshape(m, D_RWKV), r, k_f, k_b, v, gate,
        rep['ln_x_g'], rep['ln_x_b'], rep['r_k'].reshape(1, D_RWKV))[0]
    w_out = full['w_out']
    h = _make_mm_add('mm_out_rwkv')(xf, y_rwkv, w_out[:D_RWKV])
    h = _make_mm_add('mm_out_mla')(h, y_mla, w_out[D_RWKV:])
    n2, h = _make_rms_skip('rms_ffn')(h, rep['ln_ffn_g'])
    w_up, cw, cb = full['w_ffn_up'], full['ffn_conv_w'], rep['ffn_conv_b']
    u_gate = _make_mm('mm_up_gate')(n2, w_up[:, :D_FF]).reshape(b, t, D_FF)
    u_val = _make_mm('mm_up_val')(n2, w_up[:, D_FF:]).reshape(b, t, D_FF)
    act = _conv_glu(u_gate, u_val, cw[:, :D_FF], cw[:, D_FF:], cb[:, :D_FF], cb[:, D_FF:]).reshape(m, D_FF)
    h = _make_mm_add('mm_down')(h, act, full['w_ffn_down'])
    out = _make_rms('rms_final')(h, rep['ln_final_g'])
    err = jnp.square(out - target.reshape(m, d))
    return 0.5 * jnp.sum(jnp.mean(err, axis=-1))


def _mat(a):
    if a.ndim == 1:
        return a.reshape(1, -1)
    if a.ndim == 3:
        return a.reshape(a.shape[1:])
    return a


def _join(shards, name):
    if name in ROW:
        return shards.reshape(-1, shards.shape[-1])
    return shards.transpose(1, 0, 2).reshape(shards.shape[1], -1)


def _cut(whole, name):
    r, c = whole.shape
    if name in ROW:
        return whole.reshape(N_DEV, r // N_DEV, c)
    return whole.reshape(r, N_DEV, c // N_DEV).transpose(1, 0, 2)


def kernel(x, ln_mix_g, w_in, shift_mu_prev, shift_mu_next, decay_w0_fwd, decay_w2_fwd, decay_w0_bwd, decay_w2_bwd, iclr_a0_fwd, iclr_a2_fwd, iclr_a0_bwd, iclr_a2_bwd, gate_g2, k_k, k_a, r_k, ln_x_g, ln_x_b, q_norm_g, w_uq, kv_norm_g, w_ukv, mla_out_g, w_out, ln_ffn_g, w_ffn_up, ffn_conv_w, ffn_conv_b, w_ffn_down, ln_final_g, loss_target, m_ln_mix_g, m_w_in, m_shift_mu_prev, m_shift_mu_next, m_decay_w0_fwd, m_decay_w2_fwd, m_decay_w0_bwd, m_decay_w2_bwd, m_iclr_a0_fwd, m_iclr_a2_fwd, m_iclr_a0_bwd, m_iclr_a2_bwd, m_gate_g2, m_k_k, m_k_a, m_r_k, m_ln_x_g, m_ln_x_b, m_q_norm_g, m_w_uq, m_kv_norm_g, m_w_ukv, m_mla_out_g, m_w_out, m_ln_ffn_g, m_w_ffn_up, m_ffn_conv_w, m_ffn_conv_b, m_w_ffn_down, m_ln_final_g, v_ln_mix_g, v_w_in, v_shift_mu_prev, v_shift_mu_next, v_decay_w0_fwd, v_decay_w2_fwd, v_decay_w0_bwd, v_decay_w2_bwd, v_iclr_a0_fwd, v_iclr_a2_fwd, v_iclr_a0_bwd, v_iclr_a2_bwd, v_gate_g2, v_k_k, v_k_a, v_r_k, v_ln_x_g, v_ln_x_b, v_q_norm_g, v_w_uq, v_kv_norm_g, v_w_ukv, v_mla_out_g, v_w_out, v_ln_ffn_g, v_w_ffn_up, v_ffn_conv_w, v_ffn_conv_b, v_w_ffn_down, v_ln_final_g):
    given = dict(locals())
    w = {n: given[n] for n in WNAMES}
    mom = {n: given['m_' + n] for n in WNAMES}
    var = {n: given['v_' + n] for n in WNAMES}

    def split(names):
        return [n for n in names if n in BIG], [n for n in names if n not in BIG]

    def wire(names):
        big, small = split(names)
        pack = _pack([lax.bitcast_convert_type(_mat(w[n]), BF16) if n in EXACT else _mat(w[n]).astype(BF16) for n in small])
        return [_mat(w[n]).astype(BF16) for n in big] + [pack]

    def whole(names, gathered):
        big, small = split(names)
        out = {n: _join(g, n) for n, g in zip(big, gathered)}
        shapes = [_mat(w[n]).shape + ((2,) if n in EXACT else ()) for n in small]
        for n, s in zip(small, _unpack(gathered[-1], shapes, lead=1)):
            out[n] = _join(lax.bitcast_convert_type(s, F32) if n in EXACT else s.astype(F32), n)
        return out

    def grad_wire(names, grads):
        big, small = split(names)
        return [_cut(grads[n], n) for n in big] + [_pack([_cut(grads[n], n).astype(BF16) for n in small], lead=1)]

    early = [n for n in SHARDED if n not in LATE]
    rep = {n: _mat(w[n]) for n in REPLICATED}
    rep['r_k'] = w['r_k'].reshape(H, N)
    b, t, d = x.shape
    seq = lambda a: a.reshape(b, t, D_RWKV)
    flat = lambda a: a.reshape(b * t, D_RWKV)

    full_early = whole(early, _all_gather(wire(early), 'gather_weights'))
    ops, vjp_before = jax.vjp(_before_scan, full_early, rep, x)
    r, v, kk, w_f, k_f, q_f, w_b, k_b, q_b, gate, y_mla = ops
    scan_in = [seq(a) for a in (r, v, kk, w_f, k_f, q_f, w_b, k_b, q_b)]
    y_f, y_b, s_f, s_b, *late_gathered = _scan_fwd_call(*scan_in, wire(LATE))
    full_late = whole(LATE, late_gathered)
    loss_local, vjp_after = jax.vjp(_after_scan, full_late, rep, x, loss_target, y_f, y_b, r, k_f, k_b, v, gate, y_mla)

    g_late, g_rep_after, g_x_after, _, d_yf, d_yb, d_r, d_kf, d_kb, d_v, d_gate, d_ymla = vjp_after(jnp.ones((), F32))
    scan_out = _scan_bwd_call(*scan_in, s_f, s_b, d_yf, d_yb, grad_wire(LATE, g_late))
    parts_late = scan_out[12:]
    drf, dvf, dkkf, dwf, dkf, dqf, drb, dvb, dkkb, dwb, dkb, dqb = [flat(a) for a in scan_out[:12]]
    g_early, g_rep_before, g_x_before = vjp_before(
        (drf + drb + d_r, dvf + dvb + d_v, dkkf + dkkb, dwf, dkf + d_kf, dqf, dwb, dkb + d_kb, dqb, d_gate, d_ymla))
    g_rep = {n: g_rep_before[n] + g_rep_after[n] for n in rep}
    g_x = g_x_before + g_x_after
    parts_early = _grad_exchange(grad_wire(early, g_early), 'exchange_grads')

    s_out = [{}, {}, {}, {}]
    for names, parts, tag in ((early, parts_early, 'early'), (LATE, parts_late, 'late')):
        big, small = split(names)
        for n, p in zip(big, parts):
            res = _sum_adamw(p, _mat(w[n]), _mat(mom[n]), _mat(var[n]), 'adamw_' + n)
            for kind, o in enumerate(res):
                s_out[kind][n] = o.reshape(w[n].shape)
        res = _sum_adamw(parts[-1], _pack([w[n] for n in small]), _pack([mom[n] for n in small]),
                         _pack([var[n] for n in small]), 'adamw_small_' + tag)
        for kind, o in enumerate(res):
            s_out[kind].update(zip(small, _unpack(o, [w[n].shape for n in small])))

    zero = jnp.zeros((1,), F32)
    r_pack = _pack([g_rep[n] for n in REPLICATED] + [loss_local.reshape(1)])
    r_parts = _all_gather([r_pack], 'gather_small')[0]
    r_out = _sum_adamw(r_parts,_pack([w[n] for n in REPLICATED] + [zero]), _pack([mom[n] for n in REPLICATED] + [zero]),
                       _pack([var[n] for n in REPLICATED] + [zero]), 'adamw_replicated')
    r_out = [_unpack(o, [w[n].shape for n in REPLICATED] + [(1,)]) for o in r_out]

    loss = r_out[0][-1].reshape(())
    outs = [loss, g_x]
    for kind in range(4):
        by_name = dict(s_out[kind])
        by_name.update(zip(REPLICATED, r_out[kind][:-1]))
        outs += [by_name[n] for n in WNAMES]
    return tuple(outs)
```

```python
import functools

import jax
import jax.numpy as jnp
from jax import lax
from jax.experimental import pallas as pl
from jax.experimental.pallas import tpu as pltpu

F32 = jnp.float32
BF16 = jnp.bfloat16
MESH = pl.DeviceIdType.MESH

N_DEV = 8
LANES = 128
SUBLANES = 8
PACK_TILE = 2 * SUBLANES * LANES
PACK_ROWS = 512
ADAM_ROWS = 256
MM_TILE = 512
MM_TILE_WIDE = 1408
MM_K_WHOLE = 2816
VMEM_LIMIT = 48 * 1024 * 1024

H = 8
N = 64
D_RWKV = H * N
D_NOPE, D_ROPE, D_V = 64, 32, 64
D_QK = D_NOPE + D_ROPE
MLA_SCALE = D_QK ** -0.5
ROPE_THETA = 10000.0
RWKV_COLS = 1920
MLA_COLS = 1056
Q_LORA, KV_LORA = 768, 256
D_FF = 2816
NORM_EPS = 1e-6
GN_EPS = 64e-5
L2_EPS = 1e-12
ADAM_LR, ADAM_B1, ADAM_B2, ADAM_EPS, ADAM_WD, ADAM_STEP = 0.001, 0.9, 0.999, 1e-08, 0.01, 10

SCAN_CHUNK = 16
ATT_TQ = 256
SEG = 256
FFN_COLS = 256
ROW_TILE = 256
SHIFT_COLS = 384

WNAMES = ['ln_mix_g', 'w_in', 'shift_mu_prev', 'shift_mu_next', 'decay_w0_fwd', 'decay_w2_fwd', 'decay_w0_bwd',
          'decay_w2_bwd', 'iclr_a0_fwd', 'iclr_a2_fwd', 'iclr_a0_bwd', 'iclr_a2_bwd', 'gate_g2', 'k_k', 'k_a', 'r_k',
          'ln_x_g', 'ln_x_b', 'q_norm_g', 'w_uq', 'kv_norm_g', 'w_ukv', 'mla_out_g', 'w_out', 'ln_ffn_g', 'w_ffn_up',
          'ffn_conv_w', 'ffn_conv_b', 'w_ffn_down', 'ln_final_g']
COL = ('w_in', 'decay_w2_fwd', 'decay_w2_bwd', 'iclr_a2_fwd', 'iclr_a2_bwd', 'gate_g2', 'w_ukv', 'w_ffn_up', 'ffn_conv_w')
ROW = ('w_uq', 'w_out', 'w_ffn_down')
SHARDED = [n for n in WNAMES if n in COL or n in ROW]
REPLICATED = [n for n in WNAMES if n not in SHARDED]
EXACT = ('ffn_conv_w',)
LATE = ['w_out', 'w_ffn_up', 'ffn_conv_w', 'w_ffn_down']
BIG = ('w_in', 'w_uq', 'w_ukv', 'w_out', 'w_ffn_up', 'w_ffn_down')


def _params(*sem):
    return pltpu.CompilerParams(dimension_semantics=sem, vmem_limit_bytes=VMEM_LIMIT)


def _pack(arrs, lead=0):
    parts = []
    for a in arrs:
        head = a.shape[:lead]
        flat = a.reshape(head + (-1,))
        n = flat.shape[-1]
        n_pad = -(-n // PACK_TILE) * PACK_TILE
        flat = jnp.pad(flat, [(0, 0)] * lead + [(0, n_pad - n)])
        parts.append(flat.reshape(head + (n_pad // LANES, LANES)))
    out = jnp.concatenate(parts, axis=lead)
    rows = out.shape[lead]
    rows_pad = -(-rows // PACK_ROWS) * PACK_ROWS
    return jnp.pad(out, [(0, 0)] * lead + [(0, rows_pad - rows), (0, 0)])


def _unpack(packed, shapes, lead=0):
    outs, row = [], 0
    head = packed.shape[:lead]
    for shp in shapes:
        n = 1
        for s in shp:
            n *= s
        rows = -(-n // PACK_TILE) * (PACK_TILE // LANES)
        blk = lax.slice_in_dim(packed, row, row + rows, axis=lead)
        flat = blk.reshape(head + (rows * LANES,))
        outs.append(lax.slice_in_dim(flat, 0, n, axis=lead).reshape(head + tuple(shp)))
        row += rows
    return outs


PEERS = N_DEV - 1
HBM = pl.BlockSpec(memory_space=pl.ANY)


def _comm_sems(n):
    return [pltpu.SemaphoreType.DMA((PEERS * n,)), pltpu.SemaphoreType.DMA((PEERS * n,)), pltpu.SemaphoreType.DMA((n,))]


def _all_gather(xs, name):
    n = len(xs)

    def body(*refs):
        x_refs, out_refs, (send_sems, recv_sems, local_sems) = refs[:n], refs[n:2 * n], refs[2 * n:]
        mx, my, mc = lax.axis_index("x"), lax.axis_index("y"), lax.axis_index("c")
        me, sibling = (mx, my, mc), (mx, my, 1 - mc)
        chips = [(1 - mx, my), (mx, 1 - my), (1 - mx, 1 - my)]

        def slot(a, px, py, pc):
            return out_refs[a].at[4 * px + 2 * py + pc]

        def copy(a, k, block, to, src=None):
            return pltpu.make_async_remote_copy(
                src_ref=slot(a, *block) if src is None else src, dst_ref=slot(a, *block),
                send_sem=send_sems.at[PEERS * a + k], recv_sem=recv_sems.at[PEERS * a + k],
                device_id=to, device_id_type=MESH)

        mine = [pltpu.make_async_copy(x_refs[a], slot(a, *me), local_sems.at[a]) for a in range(n)]
        first, passed = [], []
        for a in range(n):
            mine[a].start()
            first.append(copy(a, 0, me, sibling, src=x_refs[a]))
            first += [copy(a, 1 + j, me, (*chip, mc), src=x_refs[a]) for j, chip in enumerate(chips)]
        for cp in first:
            cp.start()
        for j, chip in enumerate(chips):
            for a in range(n):
                copy(a, 1 + j, (*chip, mc), me).wait_recv()
                passed.append(copy(a, 4 + j, (*chip, mc), sibling))
                passed[-1].start()
        for a in range(n):
            copy(a, 0, sibling, me).wait_recv()
            for j, chip in enumerate(chips):
                copy(a, 4 + j, (*chip, 1 - mc), me).wait_recv()
        for cp in first + passed:
            cp.wait_send()
        for cp in mine:
            cp.wait()

    return pl.pallas_call(
        body, name=name, out_shape=[jax.ShapeDtypeStruct((N_DEV,) + x.shape, x.dtype) for x in xs],
        in_specs=[HBM] * n, out_specs=[HBM] * n, scratch_shapes=_comm_sems(n),
    )(*xs)


def _direct_exchange(src_refs, out_refs, send_sems, recv_sems, local_sems, per_peer):
    mx, my, mc = lax.axis_index("x"), lax.axis_index("y"), lax.axis_index("c")
    me = 4 * mx + 2 * my + mc

    def flip(v, bit):
        return 1 - v if bit else v

    def copies():
        mine, remote = [], []
        for a, (src, out) in enumerate(zip(src_refs, out_refs)):
            mine.append(pltpu.make_async_copy(src.at[me] if per_peer else src, out.at[me], local_sems.at[a]))
            for k in range(1, N_DEV):
                px, py, pc = flip(mx, k & 4), flip(my, k & 2), flip(mc, k & 1)
                remote.append(pltpu.make_async_remote_copy(
                    src_ref=src.at[4 * px + 2 * py + pc] if per_peer else src, dst_ref=out.at[me],
                    send_sem=send_sems.at[PEERS * a + k - 1], recv_sem=recv_sems.at[PEERS * a + k - 1],
                    device_id=(px, py, pc), device_id_type=MESH))
        return mine, remote

    def start():
        mine, remote = copies()
        for cp in mine + remote:
            cp.start()

    def wait():
        mine, remote = copies()
        for cp in remote:
            cp.wait_recv()
        for cp in remote:
            cp.wait_send()
        for cp in mine:
            cp.wait()

    return start, wait


def _grad_exchange(gs, name):
    n = len(gs)

    def body(*refs):
        start, wait = _direct_exchange(refs[:n], refs[n:2 * n], *refs[2 * n:], per_peer=True)
        start()
        wait()

    return pl.pallas_call(
        body, name=name, out_shape=[jax.ShapeDtypeStruct(g.shape, g.dtype) for g in gs],
        in_specs=[HBM] * n, out_specs=[HBM] * n, scratch_shapes=_comm_sems(n),
    )(*gs)


def _sum_adamw(parts, w, m, v, name):
    rows, cols = w.shape
    tr = next((t for t in range(ADAM_ROWS, 15, -16) if rows % t == 0), rows)
    c1 = 1.0 - ADAM_B1 ** ADAM_STEP
    c2 = 1.0 - ADAM_B2 ** ADAM_STEP

    def body(p_ref, w_ref, m_ref, v_ref, g_out, d_out, m_out, v_out):
        g = p_ref[0].astype(F32)
        for q in range(1, N_DEV):
            g = g + p_ref[q].astype(F32)
        m_new = ADAM_B1 * m_ref[...] + (1.0 - ADAM_B1) * g
        v_new = ADAM_B2 * v_ref[...] + (1.0 - ADAM_B2) * (g * g)
        m_hat = m_new / c1
        v_hat = v_new / c2
        g_out[...] = g
        d_out[...] = -ADAM_LR * (m_hat / (jnp.sqrt(v_hat) + ADAM_EPS) + ADAM_WD * w_ref[...])
        m_out[...] = m_new
        v_out[...] = v_new

    blk = pl.BlockSpec((tr, cols), lambda i: (i, 0))
    out = jax.ShapeDtypeStruct((rows, cols), F32)
    return pl.pallas_call(
        body, name=name, grid=(rows // tr,),
        in_specs=[pl.BlockSpec((N_DEV, tr, cols), lambda i: (0, i, 0)), blk, blk, blk],
        out_specs=[blk, blk, blk, blk], out_shape=[out, out, out, out],
        compiler_params=_params("parallel"),
    )(parts, w, m, v)


def _tile(dim, cap=MM_TILE):
    if dim <= cap:
        return dim
    for t in range(cap, LANES - 1, -LANES):
        if dim % t == 0:
            return t
    return dim


def _mm_call(a, b, form, name, out_dtype=F32, base=None):
    if form == 'nn':
        (m, k), n = a.shape, b.shape[1]
    elif form == 'nt':
        (m, k), n = a.shape, b.shape[0]
    else:
        (k, m), n = a.shape, b.shape[1]
    tk = k if (form == 'nn' and k <= MM_K_WHOLE) else _tile(k, MM_TILE_WIDE)
    tm = _tile(m, MM_TILE_WIDE if form == 'tn' else MM_TILE)
    tn = _tile(n, MM_TILE_WIDE)
    nk = k // tk
    contract = {'nn': ((1,), (0,)), 'nt': ((1,), (1,)), 'tn': ((0,), (0,))}[form]

    acc_in_out = nk == 1 or out_dtype == F32
    assert base is None or nk == 1
    extra = [] if base is None else [base]

    def body(a_ref, b_ref, *rest):
        o_ref, acc = rest[len(extra)], rest[len(extra) + 1:]
        part = lax.dot_general(a_ref[...].astype(BF16), b_ref[...].astype(BF16), (contract, ((), ())),
                               preferred_element_type=F32)
        if nk == 1:
            o_ref[...] = (part + rest[0][...] if extra else part).astype(out_dtype)
            return
        acc_ref = o_ref if acc_in_out else acc[0]

        @pl.when(pl.program_id(2) == 0)
        def _():
            acc_ref[...] = part

        @pl.when(pl.program_id(2) > 0)
        def _():
            acc_ref[...] += part

        if not acc_in_out:
            @pl.when(pl.program_id(2) == nk - 1)
            def _():
                o_ref[...] = acc_ref[...].astype(out_dtype)

    a_spec = pl.BlockSpec((tk, tm), lambda j, i, l: (l, i)) if form == 'tn' else pl.BlockSpec((tm, tk), lambda j, i, l: (i, l))
    b_spec = pl.BlockSpec((tn, tk), lambda j, i, l: (j, l)) if form == 'nt' else pl.BlockSpec((tk, tn), lambda j, i, l: (l, j))
    o_spec = pl.BlockSpec((tm, tn), lambda j, i, l: (i, j))
    return pl.pallas_call(
        body, name=name, grid=(n // tn, m // tm, nk),
        in_specs=[a_spec, b_spec] + [o_spec] * len(extra), out_specs=o_spec,
        out_shape=jax.ShapeDtypeStruct((m, n), out_dtype),
        scratch_shapes=[] if acc_in_out else [pltpu.VMEM((tm, tn), F32)],
        compiler_params=_params("parallel", "parallel", "arbitrary"),
    )(a, b, *extra)


def _make_mm(name):
    @jax.custom_vjp
    def mm(a, b):
        return _mm_call(a, b, 'nn', name + '_fwd')

    def fwd(a, b):
        return _mm_call(a, b, 'nn', name + '_fwd'), (a, b)

    def bwd(res, g):
        a, b = res
        return _mm_call(g, b, 'nt', name + '_da'), _mm_call(a, g, 'tn', name + '_db', out_dtype=BF16)

    mm.defvjp(fwd, bwd)
    return mm


def _make_mm_add(name):
    @jax.custom_vjp
    def mm(base, a, b):
        return _mm_call(a, b, 'nn', name + '_fwd', base=base)

    def fwd(base, a, b):
        return _mm_call(a, b, 'nn', name + '_fwd', base=base), (a, b)

    def bwd(res, g):
        a, b = res
        return g, _mm_call(g, b, 'nt', name + '_da'), _mm_call(a, g, 'tn', name + '_db', out_dtype=BF16)

    mm.defvjp(fwd, bwd)
    return mm


def _rms_fwd_call(x, g, name):
    m, d = x.shape
    tm = _tile(m)

    def body(x_ref, g_ref, o_ref):
        xv = x_ref[...]
        rinv = lax.rsqrt(jnp.mean(xv * xv, axis=-1, keepdims=True) + NORM_EPS)
        o_ref[...] = xv * rinv * g_ref[...]

    return pl.pallas_call(
        body, name=name, grid=(m // tm,),
        in_specs=[pl.BlockSpec((tm, d), lambda i: (i, 0)), pl.BlockSpec((1, d), lambda i: (0, 0))],
        out_specs=pl.BlockSpec((tm, d), lambda i: (i, 0)), out_shape=jax.ShapeDtypeStruct((m, d), F32),
        compiler_params=_params("parallel"),
    )(x, g)


def _rms_bwd_call(x, g, dy, name, d_skip=None):
    m, d = x.shape
    tm = _tile(m)
    extra = [] if d_skip is None else [d_skip]

    def body(x_ref, g_ref, dy_ref, *rest):
        dx_ref, dg_ref = rest[len(extra):]

        @pl.when(pl.program_id(0) == 0)
        def _():
            dg_ref[...] = jnp.zeros_like(dg_ref)

        xv, dyv = x_ref[...], dy_ref[...]
        rinv = lax.rsqrt(jnp.mean(xv * xv, axis=-1, keepdims=True) + NORM_EPS)
        xh = xv * rinv
        dg_ref[...] += jnp.sum(dyv * xh, axis=0, keepdims=True)
        dxh = dyv * g_ref[...]
        dx = rinv * (dxh - xh * jnp.mean(dxh * xh, axis=-1, keepdims=True))
        dx_ref[...] = dx + rest[0][...] if extra else dx

    row = pl.BlockSpec((tm, d), lambda i: (i, 0))
    vec = pl.BlockSpec((1, d), lambda i: (0, 0))
    return pl.pallas_call(
        body, name=name, grid=(m // tm,), in_specs=[row, vec, row] + [row] * len(extra), out_specs=[row, vec],
        out_shape=[jax.ShapeDtypeStruct((m, d), F32), jax.ShapeDtypeStruct((1, d), F32)],
        compiler_params=_params("arbitrary"),
    )(x, g, dy, *extra)


def _make_rms(name):
    @jax.custom_vjp
    def rms(x, g):
        return _rms_fwd_call(x, g, name + '_fwd')

    def fwd(x, g):
        return _rms_fwd_call(x, g, name + '_fwd'), (x, g)

    def bwd(res, dy):
        x, g = res
        dx, dg = _rms_bwd_call(x, g, dy, name + '_bwd')
        return dx, dg

    rms.defvjp(fwd, bwd)
    return rms


def _make_rms_skip(name):
    @jax.custom_vjp
    def rms(x, g):
        return _rms_fwd_call(x, g, name + '_fwd'), x

    def fwd(x, g):
        return (_rms_fwd_call(x, g, name + '_fwd'), x), (x, g)

    def bwd(res, cts):
        x, g = res
        dx, dg = _rms_bwd_call(x, g, cts[0], name + '_bwd', d_skip=cts[1])
        return dx, dg

    rms.defvjp(fwd, bwd)
    return rms


def _time_shifts(x):
    t = x.shape[0]
    rows = lax.broadcasted_iota(jnp.int32, x.shape, 0)
    return (jnp.where(rows == 0, 0.0, pltpu.roll(x, 1, 0)), jnp.where(rows == t - 1, 0.0, pltpu.roll(x, t - 1, 0)))


def _conv3(x, cw_ref, cb_ref):
    xp, xn = _time_shifts(x)
    return cw_ref[0:1, :] * xp + cw_ref[1:2, :] * x + cw_ref[2:3, :] * xn + cb_ref[...]


def _glu_specs(b, t, f):
    tc = _tile(f, FFN_COLS)
    seq = pl.BlockSpec((1, t, tc), lambda j, bi: (bi, 0, j))
    cw = pl.BlockSpec((3, tc), lambda j, bi: (0, j))
    cb = pl.BlockSpec((1, tc), lambda j, bi: (0, j))
    return tc, seq, cw, cb


def _glu_fwd_call(ug, uv, cwg, cwv, cbg, cbv):
    b, t, f = ug.shape
    tc, seq, cw, cb = _glu_specs(b, t, f)

    def body(ug_ref, uv_ref, cwg_ref, cwv_ref, cbg_ref, cbv_ref, o_ref):
        g = _conv3(ug_ref[0], cwg_ref, cbg_ref)
        o_ref[0] = g * jax.nn.sigmoid(g) * _conv3(uv_ref[0], cwv_ref, cbv_ref)

    return pl.pallas_call(
        body, name='glu_fwd', grid=(f // tc, b), in_specs=[seq, seq, cw, cw, cb, cb], out_specs=seq,
        out_shape=jax.ShapeDtypeStruct((b, t, f), F32), compiler_params=_params("parallel", "parallel"),
    )(ug, uv, cwg, cwv, cbg, cbv)


def _glu_bwd_call(ug, uv, cwg, cwv, cbg, cbv, dact):
    b, t, f = ug.shape
    tc, seq, cw, cb = _glu_specs(b, t, f)

    def body(ug_ref, uv_ref, cwg_ref, cwv_ref, cbg_ref, cbv_ref, da_ref,
             dug_ref, duv_ref, dcwg_ref, dcwv_ref, dcbg_ref, dcbv_ref):
        @pl.when(pl.program_id(1) == 0)
        def _():
            for ref in (dcwg_ref, dcwv_ref, dcbg_ref, dcbv_ref):
                ref[...] = jnp.zeros_like(ref)

        g = _conv3(ug_ref[0], cwg_ref, cbg_ref)
        v = _conv3(uv_ref[0], cwv_ref, cbv_ref)
        sig = jax.nn.sigmoid(g)
        da = da_ref[0]
        dv = da * (g * sig)
        dg = da * v * (sig * (1.0 + g * (1.0 - sig)))

        def conv_bwd(dc, x_ref, cw_ref, dx_ref, dcw_ref, dcb_ref):
            dcp, dcn = _time_shifts(dc)
            dx_ref[0] = cw_ref[0:1, :] * dcn + cw_ref[1:2, :] * dc + cw_ref[2:3, :] * dcp
            x = x_ref[0]
            xp, xn = _time_shifts(x)
            for n, xs in enumerate((xp, x, xn)):
                dcw_ref[n:n + 1, :] += jnp.sum(dc * xs, axis=0, keepdims=True)
            dcb_ref[...] += jnp.sum(dc, axis=0, keepdims=True)

        conv_bwd(dg, ug_ref, cwg_ref, dug_ref, dcwg_ref, dcbg_ref)
        conv_bwd(dv, uv_ref, cwv_ref, duv_ref, dcwv_ref, dcbv_ref)

    big = jax.ShapeDtypeStruct((b, t, f), F32)
    return pl.pallas_call(
        body, name='glu_bwd', grid=(f // tc, b), in_specs=[seq, seq, cw, cw, cb, cb, seq],
        out_specs=[seq, seq, cw, cw, cb, cb],
        out_shape=[big, big, jax.ShapeDtypeStruct((3, f), F32), jax.ShapeDtypeStruct((3, f), F32),
                   jax.ShapeDtypeStruct((1, f), F32), jax.ShapeDtypeStruct((1, f), F32)],
        compiler_params=_params("parallel", "arbitrary"),
    )(ug, uv, cwg, cwv, cbg, cbv, dact)


def _shift_call(z, mu_p, mu_n, dzs=None):
    b, t, c = z.shape
    tc = _tile(c, SHIFT_COLS)
    seq = pl.BlockSpec((1, t, tc), lambda j, bi: (bi, 0, j))
    row = pl.BlockSpec((1, tc), lambda j, bi: (0, j))

    def fwd_body(z_ref, mp_ref, mn_ref, o_ref):
        x = z_ref[0]
        xp, xn = _time_shifts(x)
        o_ref[0] = x + mp_ref[...] * (xp - x) + mn_ref[...] * (xn - x)

    def bwd_body(z_ref, mp_ref, mn_ref, d_ref, dz_ref, dmp_ref, dmn_ref):
        @pl.when(pl.program_id(1) == 0)
        def _():
            dmp_ref[...] = jnp.zeros_like(dmp_ref)
            dmn_ref[...] = jnp.zeros_like(dmn_ref)

        x, d = z_ref[0], d_ref[0]
        xp, xn = _time_shifts(x)
        dp, dn = _time_shifts(d)
        mp, mn = mp_ref[...], mn_ref[...]
        dz_ref[0] = d * (1.0 - mp - mn) + mp * dn + mn * dp
        dmp_ref[...] += jnp.sum(d * (xp - x), axis=0, keepdims=True)
        dmn_ref[...] += jnp.sum(d * (xn - x), axis=0, keepdims=True)

    if dzs is None:
        return pl.pallas_call(
            fwd_body, name='shift_fwd', grid=(c // tc, b), in_specs=[seq, row, row], out_specs=seq,
            out_shape=jax.ShapeDtypeStruct(z.shape, F32), compiler_params=_params("parallel", "parallel"),
        )(z, mu_p, mu_n)
    return pl.pallas_call(
        bwd_body, name='shift_bwd', grid=(c // tc, b), in_specs=[seq, row, row, seq], out_specs=[seq, row, row],
        out_shape=[jax.ShapeDtypeStruct(z.shape, F32), jax.ShapeDtypeStruct(mu_p.shape, F32),
                   jax.ShapeDtypeStruct(mu_n.shape, F32)],
        compiler_params=_params("parallel", "arbitrary"),
    )(z, mu_p, mu_n, dzs)


@jax.custom_vjp
def _token_shift(z, mu_p, mu_n):
    return _shift_call(z, mu_p, mu_n)


_token_shift.defvjp(lambda z, mu_p, mu_n: (_shift_call(z, mu_p, mu_n), (z, mu_p, mu_n)),
                    lambda res, d: tuple(_shift_call(*res, dzs=d)))


@jax.custom_vjp
def _conv_glu(ug, uv, cwg, cwv, cbg, cbv):
    return _glu_fwd_call(ug, uv, cwg, cwv, cbg, cbv)


def _conv_glu_fwd(*args):
    return _glu_fwd_call(*args), args


def _conv_glu_bwd(res, dact):
    return tuple(_glu_bwd_call(*res, dact))


_conv_glu.defvjp(_conv_glu_fwd, _conv_glu_bwd)


HEAD_LANES = 2 * D_NOPE
PAIR = 2


def _lane(shape):
    return lax.broadcasted_iota(jnp.int32, shape, len(shape) - 1)


def _rope(x, c, s1, s2):
    return x * c + pltpu.roll(x, HEAD_LANES - D_ROPE // 2, 1) * s1 + pltpu.roll(x, D_ROPE // 2, 1) * s2


def _rope_t(g, c, s1, s2):
    return g * c + pltpu.roll(g * s1, D_ROPE // 2, 1) + pltpu.roll(g * s2, HEAD_LANES - D_ROPE // 2, 1)


def _attn_setup(kv_ref, kr_ref, tabs, k2, v16):
    c, s1, s2 = (tb[...] for tb in tabs)
    krr = _rope(kr_ref[0], c, s1, s2)
    v16[...] = kv_ref[0].astype(BF16)
    for hh in range(PAIR):
        slab = kv_ref[0, :, HEAD_LANES * hh:HEAD_LANES * (hh + 1)]
        k2[hh] = jnp.where(_lane(slab.shape) < D_NOPE, slab, krr).astype(BF16)


def _attn_queries(q_ref, tabs, rows, hh):
    c, s1, s2 = (tb[rows, :] for tb in tabs)
    return (_rope(q_ref[0, :, HEAD_LANES * hh:HEAD_LANES * (hh + 1)], c, s1, s2) * MLA_SCALE).astype(BF16), (c, s1, s2)


def _attn_specs(b, t, tq):
    qspec = pl.BlockSpec((1, tq, PAIR * HEAD_LANES), lambda bi, p, i: (bi, i, p))
    kvspec = pl.BlockSpec((1, t, PAIR * HEAD_LANES), lambda bi, p, i: (bi, 0, p))
    krspec = pl.BlockSpec((1, t, HEAD_LANES), lambda bi, p, i: (bi, 0, 0))
    tab = pl.BlockSpec((t, HEAD_LANES), lambda bi, p, i: (0, 0))
    ospec = pl.BlockSpec((1, tq, PAIR * D_V), lambda bi, p, i: (bi, i, p))
    lspec = pl.BlockSpec((1, 1, tq, HEAD_LANES), lambda bi, p, i: (bi, p, i, 0))
    return qspec, kvspec, krspec, tab, ospec, lspec


def _attn_fwd_call(q, kv, kr, tabs):
    b, t, _ = q.shape
    tq = min(ATT_TQ, t)
    qspec, kvspec, krspec, tab, ospec, lspec = _attn_specs(b, t, tq)

    def body(q_ref, kv_ref, kr_ref, c_ref, s1_ref, s2_ref, o_ref, lse_ref, k2, v16):
        tabs = (c_ref, s1_ref, s2_ref)

        @pl.when(pl.program_id(2) == 0)
        def _():
            _attn_setup(kv_ref, kr_ref, tabs, k2, v16)

        rows = pl.ds(pl.multiple_of(pl.program_id(2) * tq, tq), tq)
        outs, lses = [], []
        for hh in range(PAIR):
            qh, _ = _attn_queries(q_ref, tabs, rows, hh)
            s = lax.dot_general(qh, k2[hh], (((1,), (1,)), ((), ())), preferred_element_type=F32)
            m = jnp.max(s, axis=-1, keepdims=True)
            p = jnp.exp(s - m)
            l = jnp.sum(p, axis=-1, keepdims=True)
            slab16 = v16[:, HEAD_LANES * hh:HEAD_LANES * (hh + 1)]
            outs.append(jnp.dot(p.astype(BF16), slab16, preferred_element_type=F32) / l)
            lses.append(m + jnp.log(l))
        low = _lane(outs[0].shape) < D_V
        o_ref[0] = jnp.where(low, pltpu.roll(outs[0], D_V, 1), outs[1])
        lse_ref[0, 0] = jnp.where(low, lses[0], lses[1])

    return pl.pallas_call(
        body, name='attn_fwd', grid=(b, H // PAIR, t // tq),
        in_specs=[qspec, kvspec, krspec, tab, tab, tab], out_specs=[ospec, lspec],
        out_shape=[jax.ShapeDtypeStruct((b, t, H * D_V), F32), jax.ShapeDtypeStruct((b, H // PAIR, t, HEAD_LANES), F32)],
        scratch_shapes=[pltpu.VMEM((PAIR, t, HEAD_LANES), BF16), pltpu.VMEM((t, PAIR * HEAD_LANES), BF16)],
        compiler_params=_params("parallel", "parallel", "arbitrary"),
    )(q, kv, kr, *tabs)


def _attn_bwd_call(q, kv, kr, tabs, o, lse, do):
    b, t, _ = q.shape
    tq = min(ATT_TQ, t)
    n_q = t // tq
    qspec, kvspec, krspec, tab, ospec, lspec = _attn_specs(b, t, tq)

    def body(q_ref, kv_ref, kr_ref, c_ref, s1_ref, s2_ref, o_ref, lse_ref, do_ref, dq_ref, dkv_ref, dkr_ref, k2, v16, dk2):
        tabs = (c_ref, s1_ref, s2_ref)
        pair, step = pl.program_id(1), pl.program_id(2)

        @pl.when(step == 0)
        def _():
            _attn_setup(kv_ref, kr_ref, tabs, k2, v16)
            dk2[...] = jnp.zeros_like(dk2)
            dkv_ref[...] = jnp.zeros_like(dkv_ref)

        @pl.when((step == 0) & (pair == 0))
        def _():
            dkr_ref[...] = jnp.zeros_like(dkr_ref)

        rows = pl.ds(pl.multiple_of(step * tq, tq), tq)
        dov, ov = do_ref[0], o_ref[0]
        lane = _lane(dov.shape)
        upper = lane >= D_V
        for hh in range(PAIR):
            qh, qtabs = _attn_queries(q_ref, tabs, rows, hh)
            s = lax.dot_general(qh, k2[hh], (((1,), (1,)), ((), ())), preferred_element_type=F32)
            p = jnp.exp(s - lse_ref[0, 0, :, D_V * hh:D_V * hh + 1])
            mine = upper if hh else ~upper
            delta = jnp.sum(jnp.where(mine, dov * ov, 0.0), axis=-1, keepdims=True)
            do_h = jnp.where(upper, dov if hh else pltpu.roll(dov, D_V, 1), 0.0).astype(BF16)
            slab16 = v16[:, HEAD_LANES * hh:HEAD_LANES * (hh + 1)]
            dp = lax.dot_general(do_h, slab16, (((1,), (1,)), ((), ())), preferred_element_type=F32)
            ds = (p * (dp - delta)).astype(BF16)
            dqh = jnp.dot(ds, k2[hh], preferred_element_type=F32) * MLA_SCALE
            dq_ref[0, :, HEAD_LANES * hh:HEAD_LANES * (hh + 1)] = _rope_t(dqh, *qtabs)
            dk2[hh] += lax.dot_general(ds, qh, (((0,), (0,)), ((), ())), preferred_element_type=F32)
            dkv_ref[0, :, HEAD_LANES * hh:HEAD_LANES * (hh + 1)] += lax.dot_general(
                p.astype(BF16), do_h, (((0,), (0,)), ((), ())), preferred_element_type=F32)

        @pl.when(step == n_q - 1)
        def _():
            c, s1, s2 = (tb[...] for tb in tabs)
            for hh in range(PAIR):
                g = dk2[hh]
                key_lane = _lane(g.shape)
                dkv_ref[0, :, HEAD_LANES * hh:HEAD_LANES * (hh + 1)] += jnp.where(key_lane < D_NOPE, g, 0.0)
                dkr_ref[0] += _rope_t(jnp.where(key_lane >= D_NOPE, g, 0.0), c, s1, s2)

    return pl.pallas_call(
        body, name='attn_bwd', grid=(b, H // PAIR, n_q),
        in_specs=[qspec, kvspec, krspec, tab, tab, tab, ospec, lspec, ospec], out_specs=[qspec, kvspec, krspec],
        out_shape=[jax.ShapeDtypeStruct(q.shape, F32), jax.ShapeDtypeStruct(kv.shape, F32), jax.ShapeDtypeStruct(kr.shape, F32)],
        scratch_shapes=[pltpu.VMEM((PAIR, t, HEAD_LANES), BF16), pltpu.VMEM((t, PAIR * HEAD_LANES), BF16),
                        pltpu.VMEM((PAIR, t, HEAD_LANES), F32)],
        compiler_params=_params("parallel", "arbitrary", "arbitrary"),
    )(q, kv, kr, *tabs, o, lse, do)


@jax.custom_vjp
def _attention(q, kv, kr, tabs):
    return _attn_fwd_call(q, kv, kr, tabs)[0]


def _attention_fwd(q, kv, kr, tabs):
    o, lse = _attn_fwd_call(q, kv, kr, tabs)
    return o, (q, kv, kr, tabs, o, lse)


def _attention_bwd(res, do):
    q, kv, kr, tabs, o, lse = res
    return (*_attn_bwd_call(q, kv, kr, tabs, o, lse, do), tuple(jnp.zeros_like(tb) for tb in tabs))


_attention.defvjp(_attention_fwd, _attention_bwd)


SROWS = N * D_RWKV // SEG


def _seg_ones():
    r = lax.broadcasted_iota(jnp.int32, (SEG, SEG), 0) >> 6
    c = lax.broadcasted_iota(jnp.int32, (SEG, SEG), 1) >> 6
    return (r == c).astype(BF16)


def _eye_mask():
    r = lax.broadcasted_iota(jnp.int32, (SROWS, SEG), 0) & (N - 1)
    c = lax.broadcasted_iota(jnp.int32, (SROWS, SEG), 1) & (N - 1)
    return r == c


def _row2(ref, bi, ti, dtype=F32):
    parts = [jnp.broadcast_to(ref[bi, pl.ds(ti, 1), pl.ds(SEG * q, SEG)].astype(dtype), (N, SEG))
             for q in range(D_RWKV // SEG)]
    return jnp.concatenate(parts, axis=0)


def _split2(x):
    hi = x.astype(BF16)
    return hi, (x - hi.astype(F32)).astype(BF16)


def _col_sum(x):
    return jnp.concatenate([jnp.sum(x[N * q:N * (q + 1)], axis=0, keepdims=True) for q in range(D_RWKV // SEG)], axis=1)


def _scan_specs(b, t, rev):
    nc = t // SCAN_CHUNK
    if rev:
        return (pl.BlockSpec((b, SCAN_CHUNK, D_RWKV), lambda c: (0, nc - 1 - c, 0)),
                pl.BlockSpec((b, SCAN_CHUNK, SROWS, SEG), lambda c: (0, nc - 1 - c, 0, 0)))
    return (pl.BlockSpec((b, SCAN_CHUNK, D_RWKV), lambda c: (0, c, 0)),
            pl.BlockSpec((b, SCAN_CHUNK, SROWS, SEG), lambda c: (0, c, 0, 0)))


def _scan_fwd_call(r, v, kk, wf, kf, qf, wb, kb, qb, ride):
    b, t, _ = r.shape
    n_chunks = t // SCAN_CHUNK
    last = SCAN_CHUNK - 1
    nr = len(ride)

    def body(rf, vf, kkf, wf_, kf_, qf_, rb, vb, kkb, wb_, kb_, qb_, *rest):
        (yf, yb, sf, sb), scratch = rest[nr:nr + 4], rest[2 * nr + 4:]
        states = scratch[:2 * b]
        send, arrive = _direct_exchange(rest[:nr], rest[nr + 4:2 * nr + 4], *scratch[2 * b:], per_peer=False)

        @pl.when(pl.program_id(0) == 0)
        def _():
            send()
            for st in states:
                st[...] = jnp.zeros_like(st)

        ones, mask = _seg_ones(), _eye_mask()
        zero16 = jnp.zeros((), BF16)
        chains = []
        for bi in range(b):
            chains.append((rf, vf, kkf, wf_, kf_, qf_, yf, sf, states[2 * bi], bi, False))
            chains.append((rb, vb, kkb, wb_, kb_, qb_, yb, sb, states[2 * bi + 1], bi, True))

        def tix(i, rev):
            return last - i if rev else i

        def put_y(y_, bi, ti, ycol):
            y_[bi, pl.ds(ti, 1), :] = _col_sum(jnp.where(mask, ycol, 0.0))

        def steps(i, with_y):
            parts = []
            for (r_, v_, kk_, w_, k_, q_, y_, s_, st, bi, rev) in chains:
                ti = tix(i, rev)
                s = st[...]
                s_[bi, ti] = s
                parts.append((s * _row2(kk_, bi, ti)).astype(BF16))
                parts.append(jnp.where(mask, _row2(v_, bi, ti, BF16), zero16))
                if with_y:
                    parts.append((s * _row2(r_, bi, tix(i - 1, rev))).astype(BF16))
            res = jnp.dot(jnp.concatenate(parts, axis=0), ones, preferred_element_type=F32)
            off = 0
            for (r_, v_, kk_, w_, k_, q_, y_, s_, st, bi, rev) in chains:
                ti = tix(i, rev)
                u = res[off:off + SROWS]
                vcol = res[off + SROWS:off + 2 * SROWS]
                off += 2 * SROWS
                if with_y:
                    put_y(y_, bi, tix(i - 1, rev), res[off:off + SROWS])
                    off += SROWS
                st[...] = st[...] * _row2(w_, bi, ti) - u * _row2(q_, bi, ti) + vcol * _row2(k_, bi, ti)

        steps(0, False)

        def loop(i, carry):
            steps(i, True)
            return carry

        lax.fori_loop(1, SCAN_CHUNK, loop, 0, unroll=5)
        parts = [(c[8][...] * _row2(c[0], c[9], tix(last, c[10]))).astype(BF16) for c in chains]
        res = jnp.dot(jnp.concatenate(parts, axis=0), ones, preferred_element_type=F32)
        for n, c in enumerate(chains):
            put_y(c[6], c[9], tix(last, c[10]), res[n * SROWS:(n + 1) * SROWS])

        @pl.when(pl.program_id(0) == n_chunks - 1)
        def _():
            arrive()

    fr, fs = _scan_specs(b, t, False)
    br, bs = _scan_specs(b, t, True)
    y_shape = jax.ShapeDtypeStruct((b, t, D_RWKV), F32)
    s_shape = jax.ShapeDtypeStruct((b, t, SROWS, SEG), F32)
    return pl.pallas_call(
        body, name='scan_fwd', grid=(n_chunks,),
        in_specs=[fr] * 6 + [br] * 6 + [HBM] * nr, out_specs=[fr, br, fs, bs] + [HBM] * nr,
        out_shape=[y_shape, y_shape, s_shape, s_shape] + [jax.ShapeDtypeStruct((N_DEV,) + a.shape, a.dtype) for a in ride],
        scratch_shapes=[pltpu.VMEM((SROWS, SEG), F32)] * (2 * b) + _comm_sems(nr),
        compiler_params=_params("arbitrary"),
    )(r, v, kk, wf, kf, qf, r, v, kk, wb, kb, qb, *ride)


def _scan_bwd_call(r, v, kk, wf, kf, qf, wb, kb, qb, sf, sb, dyf, dyb, ride):
    b, t, _ = r.shape
    n_chunks = t // SCAN_CHUNK
    last = SCAN_CHUNK - 1
    nr = len(ride)

    def body(rf, vf, kkf, wf_, kf_, qf_, sf_, dyf_, rb, vb, kkb, wb_, kb_, qb_, sb_, dyb_, *rest):
        drf, dvf, dkkf, dwf, dkf, dqf, drb, dvb, dkkb, dwb, dkb, dqb = rest[nr:nr + 12]
        scratch = rest[2 * nr + 12:]
        send, arrive = _direct_exchange(rest[:nr], rest[nr + 12:2 * nr + 12], *scratch[8 * b:], per_peer=True)

        @pl.when(pl.program_id(0) == 0)
        def _():
            send()
            for n in range(2 * b):
                scratch[4 * n][...] = jnp.zeros_like(scratch[4 * n])

        ones, mask = _seg_ones(), _eye_mask()
        zero16 = jnp.zeros((), BF16)
        chains = []
        for bi in range(b):
            chains.append((rf, vf, kkf, wf_, kf_, qf_, sf_, dyf_, (drf, dvf, dkkf, dwf, dkf, dqf),
                           scratch[8 * bi:8 * bi + 4], bi, True))
            chains.append((rb, vb, kkb, wb_, kb_, qb_, sb_, dyb_, (drb, dvb, dkkb, dwb, dkb, dqb),
                           scratch[8 * bi + 4:8 * bi + 8], bi, False))

        def tix(i, rev):
            return last - i if rev else i

        def state_free_parts(v_, dy_, kk_, s_, bi, ti):
            return [jnp.where(mask, _row2(v_, bi, ti, BF16), zero16), jnp.where(mask, _row2(dy_, bi, ti, BF16), zero16),
                    (s_[bi, ti] * _row2(kk_, bi, ti)).astype(BF16)]

        def keep(scr, res, off):
            for n in range(3):
                scr[1 + n][...] = res[off + n * SROWS:off + (n + 1) * SROWS]
            return off + 3 * SROWS

        def first():
            parts = []
            for (r_, v_, kk_, w_, k_, q_, s_, dy_, outs, scr, bi, rev) in chains:
                parts += state_free_parts(v_, dy_, kk_, s_, bi, tix(0, rev))
            res = jnp.dot(jnp.concatenate(parts, axis=0), ones, preferred_element_type=F32)
            off = 0
            for c in chains:
                off = keep(c[9], res, off)

        def steps(i, has_next, recompute):
            parts = []
            for (r_, v_, kk_, w_, k_, q_, s_, dy_, outs, scr, bi, rev) in chains:
                ti = tix(i, rev)
                gst, vc, dc, uc = scr
                dycol = dc[...]
                if recompute:
                    sc = s_[bi, ti] * _row2(w_, bi, ti) - uc[...] * _row2(q_, bi, ti) + vc[...] * _row2(k_, bi, ti)
                else:
                    sc = s_[bi, tix(i - 1, rev)]
                outs[0][bi, pl.ds(ti, 1), :] = _col_sum(sc * dycol)
                g = gst[...] + dycol * _row2(r_, bi, ti)
                gst[...] = g
                parts.append((g * _row2(q_, bi, ti)).astype(BF16))
                parts.append((g * _row2(k_, bi, ti)).astype(BF16))
                if has_next:
                    parts += state_free_parts(v_, dy_, kk_, s_, bi, tix(i + 1, rev))
            res = jnp.dot(jnp.concatenate(parts, axis=0), ones, preferred_element_type=F32)
            off = 0
            for (r_, v_, kk_, w_, k_, q_, s_, dy_, outs, scr, bi, rev) in chains:
                ti = tix(i, rev)
                gst, vc, dc, uc = scr
                dr_, dv_, dkk_, dw_, dk_, dq_ = outs

                def put(ref, val, sign=1.0):
                    ref[bi, pl.ds(ti, 1), :] = sign * _col_sum(val)

                gq = res[off:off + SROWS]
                put(dv_, jnp.where(mask, res[off + SROWS:off + 2 * SROWS], 0.0))
                off += 2 * SROWS
                g, sp = gst[...], s_[bi, ti]
                put(dk_, g * vc[...])
                put(dw_, g * sp)
                put(dq_, g * uc[...], -1.0)
                put(dkk_, sp * gq, -1.0)
                gst[...] = g * _row2(w_, bi, ti) - gq * _row2(kk_, bi, ti)
                if has_next:
                    off = keep(scr, res, off)

        first()
        steps(0, True, True)

        def loop(i, carry):
            steps(i, True, False)
            return carry

        lax.fori_loop(1, last, loop, 0)
        steps(last, False, False)

        @pl.when(pl.program_id(0) == n_chunks - 1)
        def _():
            arrive()

    fr, fs = _scan_specs(b, t, True)
    br, bs = _scan_specs(b, t, False)
    y_shape = jax.ShapeDtypeStruct((b, t, D_RWKV), F32)
    return pl.pallas_call(
        body, name='scan_bwd', grid=(n_chunks,),
        in_specs=[fr] * 6 + [fs, fr] + [br] * 6 + [bs, br] + [HBM] * nr,
        out_specs=[fr] * 6 + [br] * 6 + [HBM] * nr,
        out_shape=[y_shape] * 12 + [jax.ShapeDtypeStruct(a.shape, a.dtype) for a in ride],
        scratch_shapes=[pltpu.VMEM((SROWS, SEG), F32)] * (8 * b) + _comm_sems(nr),
        compiler_params=_params("arbitrary"),
    )(r, v, kk, wf, kf, qf, sf, dyf, r, v, kk, wb, kb, qb, sb, dyb, *ride)


def _rope_tables(t):
    half = D_ROPE // 2
    inv_freq = jnp.power(ROPE_THETA, -jnp.arange(0, D_ROPE, 2, dtype=F32) / D_ROPE)
    ang = jnp.arange(t, dtype=F32)[:, None] * inv_freq[None, :]
    cos, sin, zero = jnp.cos(ang), jnp.sin(ang), jnp.zeros((t, half), F32)
    tail = HEAD_LANES - D_QK
    c = jnp.concatenate([jnp.ones((t, D_NOPE), F32), cos, cos, jnp.ones((t, tail), F32)], axis=1)
    s1 = jnp.concatenate([jnp.zeros((t, D_NOPE), F32), -sin, zero, jnp.zeros((t, tail), F32)], axis=1)
    s2 = jnp.concatenate([jnp.zeros((t, D_NOPE), F32), zero, sin, jnp.zeros((t, tail), F32)], axis=1)
    return c, s1, s2


@jax.custom_vjp
def _dot16(a, w):
    return jnp.dot(a.astype(BF16), w.astype(BF16), preferred_element_type=F32)


def _dot16_fwd(a, w):
    a16, w16 = a.astype(BF16), w.astype(BF16)
    return jnp.dot(a16, w16, preferred_element_type=F32), (a16, w16)


def _dot16_bwd(res, g):
    a16, w16 = res
    g16 = g.astype(BF16)
    return (lax.dot_general(g16, w16, (((1,), (1,)), ((), ())), preferred_element_type=F32),
            lax.dot_general(a16, g16, (((0,), (0,)), ((), ())), preferred_element_type=F32))


_dot16.defvjp(_dot16_fwd, _dot16_bwd)


def _head_sum_tile(x):
    outs = []
    ones = _seg_ones()
    for q in range(x.shape[1] // SEG):
        hi, lo = _split2(x[:, SEG * q:SEG * (q + 1)])
        outs.append(jnp.dot(hi, ones, preferred_element_type=F32) + jnp.dot(lo, ones, preferred_element_type=F32))
    return jnp.concatenate(outs, axis=1)


@jax.custom_vjp
def _hsum(x):
    return _head_sum_tile(x)


_hsum.defvjp(lambda x: (_head_sum_tile(x), None), lambda _, g: (_head_sum_tile(g),))


def _softplus(x):
    return jnp.maximum(x, 0.0) + jnp.log(1.0 + jnp.exp(-jnp.abs(x)))


def _rwkv_pre_fn(k, wdf, wdb, adf, adb, gd, w0f, w2f, w0b, w2b, a0f, a2f, a0b, a2b, g2, k_k, k_a):
    w_f = jnp.exp(-jnp.exp(-_softplus(-(w0f + _dot16(jnp.tanh(wdf), w2f))) - 0.5))
    w_b = jnp.exp(-jnp.exp(-_softplus(-(w0b + _dot16(jnp.tanh(wdb), w2b))) - 0.5))
    a_f = jax.nn.sigmoid(a0f + _dot16(adf, a2f))
    a_b = jax.nn.sigmoid(a0b + _dot16(adb, a2b))
    gate = _dot16(jax.nn.sigmoid(gd), g2)
    kk = k * k_k
    kk = kk / jnp.maximum(jnp.sqrt(_hsum(kk * kk)), L2_EPS)
    return (kk, w_f, k * (1.0 + (a_f - 1.0) * k_a), kk * a_f, w_b, k * (1.0 + (a_b - 1.0) * k_a), kk * a_b, gate)


def _rwkv_post_fn(y_f, y_b, r, k_f, k_b, v, gate, ln_g, ln_b, r_k):
    y = y_f + y_b
    yc = y - _hsum(y) * (1.0 / N)
    var = _hsum(yc * yc) * (1.0 / N)
    y = yc * lax.rsqrt(var + GN_EPS) * ln_g + ln_b
    return ((y + _hsum(r * (k_f + k_b) * r_k) * v) * gate,)


def _make_rowwise(fn, name, n_rows, tm):
    def specs(arrs, whole):
        if whole:
            return [pl.BlockSpec(a.shape, lambda i: (0, 0)) for a in arrs]
        return [pl.BlockSpec((tm, a.shape[1]), lambda i: (i, 0)) for a in arrs]

    def out_widths(rows, params):
        tiles = [jax.ShapeDtypeStruct((tm, a.shape[1]), F32) for a in rows]
        return [o.shape[1] for o in jax.eval_shape(fn, *tiles, *params)]

    def fwd_call(rows, params):
        m = rows[0].shape[0]
        n_in = len(rows) + len(params)
        outs = [jax.ShapeDtypeStruct((m, d), F32) for d in out_widths(rows, params)]

        def body(*refs):
            for o_ref, o in zip(refs[n_in:], fn(*[ref[...] for ref in refs[:n_in]])):
                o_ref[...] = o

        return pl.pallas_call(
            body, name=name + '_fwd', grid=(m // tm,), in_specs=specs(rows, False) + specs(params, True),
            out_specs=specs(outs, False), out_shape=outs, compiler_params=_params("parallel"),
        )(*rows, *params)

    def bwd_call(rows, params, cts):
        m = rows[0].shape[0]
        n_in = len(rows) + len(params)
        n_all = n_in + len(cts)
        outs = ([jax.ShapeDtypeStruct(a.shape, F32) for a in rows] + [jax.ShapeDtypeStruct(a.shape, F32) for a in params])

        def body(*refs):
            _, vjp = jax.vjp(fn, *[ref[...] for ref in refs[:n_in]])
            grads = vjp(tuple(ref[...] for ref in refs[n_in:n_all]))
            d_rows, d_params = refs[n_all:n_all + len(rows)], refs[n_all + len(rows):]
            for ref, g in zip(d_rows, grads[:len(rows)]):
                ref[...] = g

            @pl.when(pl.program_id(0) == 0)
            def _():
                for ref in d_params:
                    ref[...] = jnp.zeros_like(ref)

            for ref, g in zip(d_params, grads[len(rows):]):
                ref[...] += g

        return pl.pallas_call(
            body, name=name + '_bwd', grid=(m // tm,),
            in_specs=specs(rows, False) + specs(params, True) + specs(cts, False),
            out_specs=specs(rows, False) + specs(params, True), out_shape=outs, compiler_params=_params("arbitrary"),
        )(*rows, *params, *cts)

    @jax.custom_vjp
    def op(*args):
        return tuple(fwd_call(args[:n_rows], args[n_rows:]))

    def op_fwd(*args):
        return tuple(fwd_call(args[:n_rows], args[n_rows:])), args

    def op_bwd(args, cts):
        return tuple(bwd_call(args[:n_rows], args[n_rows:], cts))

    op.defvjp(op_fwd, op_bwd)
    return op


def _rwkv_operands(z, full, rep):
    b, t, _ = z.shape
    m = b * t
    z = _token_shift(z, rep['shift_mu_prev'], rep['shift_mu_next']).reshape(m, RWKV_COLS)
    r, k, v = z[:, :512], z[:, 512:1024], z[:, 1024:1536]
    lora_in = (z[:, 1536:1600], z[:, 1600:1664], z[:, 1664:1728], z[:, 1728:1792], z[:, 1792:1920])
    kk, w_f, k_f, q_f, w_b, k_b, q_b, gate = _make_rowwise(_rwkv_pre_fn, 'rwkv_pre', 6, _tile(m, ROW_TILE))(
        k, *lora_in, rep['decay_w0_fwd'], full['decay_w2_fwd'], rep['decay_w0_bwd'], full['decay_w2_bwd'],
        rep['iclr_a0_fwd'], full['iclr_a2_fwd'], rep['iclr_a0_bwd'], full['iclr_a2_bwd'], full['gate_g2'],
        rep['k_k'], rep['k_a'])
    return r, v, kk, w_f, k_f, q_f, w_b, k_b, q_b, gate


def _mla_mixer(z, full, rep, b, t):
    m = b * t
    c_q, c_kv, k_rope = z[:, :Q_LORA], z[:, Q_LORA:Q_LORA + KV_LORA], z[:, Q_LORA + KV_LORA:]
    w_uq = jnp.pad(full['w_uq'].reshape(Q_LORA, H, D_QK), ((0, 0), (0, 0), (0, HEAD_LANES - D_QK))).reshape(Q_LORA, H * HEAD_LANES)
    q = _make_mm('mm_uq')(_make_rms('rms_q')(c_q, rep['q_norm_g']), w_uq)
    kv = _make_mm('mm_ukv')(_make_rms('rms_kv')(c_kv, rep['kv_norm_g']), full['w_ukv'])
    kr = jnp.pad(k_rope, ((0, 0), (D_NOPE, HEAD_LANES - D_QK)))
    o = _attention(q.reshape(b, t, -1), kv.reshape(b, t, -1), kr.reshape(b, t, HEAD_LANES), _rope_tables(t))
    return _make_rms('rms_mla_out')(o.reshape(m, H * D_V), rep['mla_out_g'])


def _before_scan(full, rep, x):
    b, t, d = x.shape
    m = b * t
    n1 = _make_rms('rms_mix')(x.reshape(m, d), rep['ln_mix_g'])
    d_in = full['w_in'].shape[1]
    d_in_pad = -(-d_in // MM_TILE) * MM_TILE
    z = _make_mm('mm_in')(n1, jnp.pad(full['w_in'], ((0, 0), (0, d_in_pad - d_in))))
    return (*_rwkv_operands(z[:, :RWKV_COLS].reshape(b, t, RWKV_COLS), full, rep),
            _mla_mixer(z[:, RWKV_COLS:d_in], full, rep, b, t))


def _after_scan(full, rep, x, target, y_f, y_b, r, k_f, k_b, v, gate, y_mla):
    b, t, d = x.shape
    m = b * t
    xf = x.reshape(m, d)
    y_rwkv = _make_rowwise(_rwkv_post_fn, 'rwkv_post', 7, _tile(m, ROW_TILE))(
        y_f.reshape(m, D_RWKV), y_b.reshape(m, D_RWKV), r, k_f, k_b, v, gate,
        rep['ln_x_g'], rep['ln_x_b'], rep['r_k'].reshape(1, D_RWKV))[0]
    w_out = full['w_out']
    h = _make_mm_add('mm_out_rwkv')(xf, y_rwkv, w_out[:D_RWKV])
    h = _make_mm_add('mm_out_mla')(h, y_mla, w_out[D_RWKV:])
    n2, h = _make_rms_skip('rms_ffn')(h, rep['ln_ffn_g'])
    w_up, cw, cb = full['w_ffn_up'], full['ffn_conv_w'], rep['ffn_conv_b']
    u_gate = _make_mm('mm_up_gate')(n2, w_up[:, :D_FF]).reshape(b, t, D_FF)
    u_val = _make_mm('mm_up_val')(n2, w_up[:, D_FF:]).reshape(b, t, D_FF)
    act = _conv_glu(u_gate, u_val, cw[:, :D_FF], cw[:, D_FF:], cb[:, :D_FF], cb[:, D_FF:]).reshape(m, D_FF)
    h = _make_mm_add('mm_down')(h, act, full['w_ffn_down'])
    out = _make_rms('rms_final')(h, rep['ln_final_g'])
    err = jnp.square(out - target.reshape(m, d))
    return 0.5 * jnp.sum(jnp.mean(err, axis=-1))


def _mat(a):
    if a.ndim == 1:
        return a.reshape(1, -1)
    if a.ndim == 3:
        return a.reshape(a.shape[1:])
    return a


def _join(shards, name):
    if name in ROW:
        return shards.reshape(-1, shards.shape[-1])
    return shards.transpose(1, 0, 2).reshape(shards.shape[1], -1)


def _cut(whole, name):
    r, c = whole.shape
    if name in ROW:
        return whole.reshape(N_DEV, r // N_DEV, c)
    return whole.reshape(r, N_DEV, c // N_DEV).transpose(1, 0, 2)


def kernel(x, ln_mix_g, w_in, shift_mu_prev, shift_mu_next, decay_w0_fwd, decay_w2_fwd, decay_w0_bwd, decay_w2_bwd, iclr_a0_fwd, iclr_a2_fwd, iclr_a0_bwd, iclr_a2_bwd, gate_g2, k_k, k_a, r_k, ln_x_g, ln_x_b, q_norm_g, w_uq, kv_norm_g, w_ukv, mla_out_g, w_out, ln_ffn_g, w_ffn_up, ffn_conv_w, ffn_conv_b, w_ffn_down, ln_final_g, loss_target, m_ln_mix_g, m_w_in, m_shift_mu_prev, m_shift_mu_next, m_decay_w0_fwd, m_decay_w2_fwd, m_decay_w0_bwd, m_decay_w2_bwd, m_iclr_a0_fwd, m_iclr_a2_fwd, m_iclr_a0_bwd, m_iclr_a2_bwd, m_gate_g2, m_k_k, m_k_a, m_r_k, m_ln_x_g, m_ln_x_b, m_q_norm_g, m_w_uq, m_kv_norm_g, m_w_ukv, m_mla_out_g, m_w_out, m_ln_ffn_g, m_w_ffn_up, m_ffn_conv_w, m_ffn_conv_b, m_w_ffn_down, m_ln_final_g, v_ln_mix_g, v_w_in, v_shift_mu_prev, v_shift_mu_next, v_decay_w0_fwd, v_decay_w2_fwd, v_decay_w0_bwd, v_decay_w2_bwd, v_iclr_a0_fwd, v_iclr_a2_fwd, v_iclr_a0_bwd, v_iclr_a2_bwd, v_gate_g2, v_k_k, v_k_a, v_r_k, v_ln_x_g, v_ln_x_b, v_q_norm_g, v_w_uq, v_kv_norm_g, v_w_ukv, v_mla_out_g, v_w_out, v_ln_ffn_g, v_w_ffn_up, v_ffn_conv_w, v_ffn_conv_b, v_w_ffn_down, v_ln_final_g):
    given = dict(locals())
    w = {n: given[n] for n in WNAMES}
    mom = {n: given['m_' + n] for n in WNAMES}
    var = {n: given['v_' + n] for n in WNAMES}

    def split(names):
        return [n for n in names if n in BIG], [n for n in names if n not in BIG]

    def wire(names):
        big, small = split(names)
        pack = _pack([lax.bitcast_convert_type(_mat(w[n]), BF16) if n in EXACT else _mat(w[n]).astype(BF16) for n in small])
        return [_mat(w[n]).astype(BF16) for n in big] + [pack]

    def whole(names, gathered):
        big, small = split(names)
        out = {n: _join(g, n) for n, g in zip(big, gathered)}
        shapes = [_mat(w[n]).shape + ((2,) if n in EXACT else ()) for n in small]
        for n, s in zip(small, _unpack(gathered[-1], shapes, lead=1)):
            out[n] = _join(lax.bitcast_convert_type(s, F32) if n in EXACT else s.astype(F32), n)
        return out

    def grad_wire(names, grads):
        big, small = split(names)
        return [_cut(grads[n], n) for n in big] + [_pack([_cut(grads[n], n).astype(BF16) for n in small], lead=1)]

    early = [n for n in SHARDED if n not in LATE]
    rep = {n: _mat(w[n]) for n in REPLICATED}
    rep['r_k'] = w['r_k'].reshape(H, N)
    b, t, d = x.shape
    seq = lambda a: a.reshape(b, t, D_RWKV)
    flat = lambda a: a.reshape(b * t, D_RWKV)

    full_early = whole(early, _all_gather(wire(early), 'gather_weights'))
    ops, vjp_before = jax.vjp(_before_scan, full_early, rep, x)
    r, v, kk, w_f, k_f, q_f, w_b, k_b, q_b, gate, y_mla = ops
    scan_in = [seq(a) for a in (r, v, kk, w_f, k_f, q_f, w_b, k_b, q_b)]
    y_f, y_b, s_f, s_b, *late_gathered = _scan_fwd_call(*scan_in, wire(LATE))
    full_late = whole(LATE, late_gathered)
    loss_local, vjp_after = jax.vjp(_after_scan, full_late, rep, x, loss_target, y_f, y_b, r, k_f, k_b, v, gate, y_mla)

    g_late, g_rep_after, g_x_after, _, d_yf, d_yb, d_r, d_kf, d_kb, d_v, d_gate, d_ymla = vjp_after(jnp.ones((), F32))
    scan_out = _scan_bwd_call(*scan_in, s_f, s_b, d_yf, d_yb, grad_wire(LATE, g_late))
    parts_late = scan_out[12:]
    drf, dvf, dkkf, dwf, dkf, dqf, drb, dvb, dkkb, dwb, dkb, dqb = [flat(a) for a in scan_out[:12]]
    g_early, g_rep_before, g_x_before = vjp_before(
        (drf + drb + d_r, dvf + dvb + d_v, dkkf + dkkb, dwf, dkf + d_kf, dqf, dwb, dkb + d_kb, dqb, d_gate, d_ymla))
    g_rep = {n: g_rep_before[n] + g_rep_after[n] for n in rep}
    g_x = g_x_before + g_x_after
    parts_early = _grad_exchange(grad_wire(early, g_early), 'exchange_grads')

    s_out = [{}, {}, {}, {}]
    for names, parts, tag in ((early, parts_early, 'early'), (LATE, parts_late, 'late')):
        big, small = split(names)
        for n, p in zip(big, parts):
            res = _sum_adamw(p, _mat(w[n]), _mat(mom[n]), _mat(var[n]), 'adamw_' + n)
            for kind, o in enumerate(res):
                s_out[kind][n] = o.reshape(w[n].shape)
        res = _sum_adamw(parts[-1], _pack([w[n] for n in small]), _pack([mom[n] for n in small]),
                         _pack([var[n] for n in small]), 'adamw_small_' + tag)
        for kind, o in enumerate(res):
            s_out[kind].update(zip(small, _unpack(o, [w[n].shape for n in small])))

    zero = jnp.zeros((1,), F32)
    r_pack = _pack([g_rep[n] for n in REPLICATED] + [loss_local.reshape(1)])
    r_parts = _all_gather([r_pack], 'gather_small')[0]
    r_out = _sum_adamw(r_parts,_pack([w[n] for n in REPLICATED] + [zero]), _pack([mom[n] for n in REPLICATED] + [zero]),
                       _pack([var[n] for n in REPLICATED] + [zero]), 'adamw_replicated')
    r_out = [_unpack(o, [w[n].shape for n in REPLICATED] + [(1,)]) for o in r_out]

    loss = r_out[0][-1].reshape(())
    outs = [loss, g_x]
    for kind in range(4):
        by_name = dict(s_out[kind])
        by_name.update(zip(REPLICATED, r_out[kind][:-1]))
        outs += [by_name[n] for n in WNAMES]
    return tuple(outs)
```

```python
import functools

import jax
import jax.numpy as jnp
from jax import lax
from jax.experimental import pallas as pl
from jax.experimental.pallas import tpu as pltpu

F32 = jnp.float32
BF16 = jnp.bfloat16
MESH = pl.DeviceIdType.MESH

N_DEV = 8
LANES = 128
SUBLANES = 8
PACK_TILE = 2 * SUBLANES * LANES
PACK_ROWS = 512
ADAM_ROWS = 256
MM_TILE = 512
MM_TILE_WIDE = 1408
MM_K_WHOLE = 2816
VMEM_LIMIT = 56 * 1024 * 1024

H = 8
N = 64
D_RWKV = H * N
D_NOPE, D_ROPE, D_V = 64, 32, 64
D_QK = D_NOPE + D_ROPE
MLA_SCALE = D_QK ** -0.5
ROPE_THETA = 10000.0
RWKV_SPLITS = (D_RWKV, D_RWKV, D_RWKV, 64, 64, 64, 64, 128)
RWKV_COLS = sum(RWKV_SPLITS)
Q_LORA, KV_LORA = 768, 256
D_FF = 2816
NORM_EPS = 1e-6
GN_EPS = 64e-5
L2_EPS = 1e-12
ADAM_LR, ADAM_B1, ADAM_B2, ADAM_EPS, ADAM_WD, ADAM_STEP = 0.001, 0.9, 0.999, 1e-08, 0.01, 10

SCAN_CHUNK = 32
ATT_TQ = 256
SEG = 256
FFN_COLS = 256
ROW_TILE = 256
SHIFT_COLS = 384

WNAMES = ['ln_mix_g', 'w_in', 'shift_mu_prev', 'shift_mu_next', 'decay_w0_fwd', 'decay_w2_fwd', 'decay_w0_bwd',
          'decay_w2_bwd', 'iclr_a0_fwd', 'iclr_a2_fwd', 'iclr_a0_bwd', 'iclr_a2_bwd', 'gate_g2', 'k_k', 'k_a', 'r_k',
          'ln_x_g', 'ln_x_b', 'q_norm_g', 'w_uq', 'kv_norm_g', 'w_ukv', 'mla_out_g', 'w_out', 'ln_ffn_g', 'w_ffn_up',
          'ffn_conv_w', 'ffn_conv_b', 'w_ffn_down', 'ln_final_g']
COL = ('w_in', 'decay_w2_fwd', 'decay_w2_bwd', 'iclr_a2_fwd', 'iclr_a2_bwd', 'gate_g2', 'w_ukv', 'w_ffn_up', 'ffn_conv_w')
ROW = ('w_uq', 'w_out', 'w_ffn_down')
SHARDED = [n for n in WNAMES if n in COL or n in ROW]
REPLICATED = [n for n in WNAMES if n not in SHARDED]
EXACT = ('ffn_conv_w',)
LATE = ['w_out', 'w_ffn_up', 'ffn_conv_w', 'w_ffn_down']
BIG = ('w_in', 'w_uq', 'w_ukv', 'w_out', 'w_ffn_up', 'w_ffn_down')


def _params(*sem):
    return pltpu.CompilerParams(dimension_semantics=sem, vmem_limit_bytes=VMEM_LIMIT)


def _pack(arrs, lead=0):
    parts = []
    for a in arrs:
        head = a.shape[:lead]
        flat = a.reshape(head + (-1,))
        n = flat.shape[-1]
        n_pad = -(-n // PACK_TILE) * PACK_TILE
        flat = jnp.pad(flat, [(0, 0)] * lead + [(0, n_pad - n)])
        parts.append(flat.reshape(head + (n_pad // LANES, LANES)))
    out = jnp.concatenate(parts, axis=lead)
    rows = out.shape[lead]
    rows_pad = -(-rows // PACK_ROWS) * PACK_ROWS
    return jnp.pad(out, [(0, 0)] * lead + [(0, rows_pad - rows), (0, 0)])


def _unpack(packed, shapes, lead=0):
    outs, row = [], 0
    head = packed.shape[:lead]
    for shp in shapes:
        n = 1
        for s in shp:
            n *= s
        rows = -(-n // PACK_TILE) * (PACK_TILE // LANES)
        blk = lax.slice_in_dim(packed, row, row + rows, axis=lead)
        flat = blk.reshape(head + (rows * LANES,))
        outs.append(lax.slice_in_dim(flat, 0, n, axis=lead).reshape(head + tuple(shp)))
        row += rows
    return outs


PEERS = N_DEV - 1
HBM = pl.BlockSpec(memory_space=pl.ANY)


def _comm_sems(n):
    return [pltpu.SemaphoreType.DMA((PEERS * n,)), pltpu.SemaphoreType.DMA((PEERS * n,)), pltpu.SemaphoreType.DMA((n,))]


def _all_gather(xs, name):
    n = len(xs)

    def body(*refs):
        x_refs, out_refs, (send_sems, recv_sems, local_sems) = refs[:n], refs[n:2 * n], refs[2 * n:]
        mx, my, mc = lax.axis_index("x"), lax.axis_index("y"), lax.axis_index("c")
        me, sibling = (mx, my, mc), (mx, my, 1 - mc)
        chips = [(1 - mx, my), (mx, 1 - my), (1 - mx, 1 - my)]

        def slot(a, px, py, pc):
            return out_refs[a].at[4 * px + 2 * py + pc]

        def copy(a, k, block, to, src=None):
            return pltpu.make_async_remote_copy(
                src_ref=slot(a, *block) if src is None else src, dst_ref=slot(a, *block),
                send_sem=send_sems.at[PEERS * a + k], recv_sem=recv_sems.at[PEERS * a + k],
                device_id=to, device_id_type=MESH)

        mine = [pltpu.make_async_copy(x_refs[a], slot(a, *me), local_sems.at[a]) for a in range(n)]
        first, passed = [], []
        for a in range(n):
            mine[a].start()
            first.append(copy(a, 0, me, sibling, src=x_refs[a]))
            first += [copy(a, 1 + j, me, (*chip, mc), src=x_refs[a]) for j, chip in enumerate(chips)]
        for cp in first:
            cp.start()
        for j, chip in enumerate(chips):
            for a in range(n):
                copy(a, 1 + j, (*chip, mc), me).wait_recv()
                passed.append(copy(a, 4 + j, (*chip, mc), sibling))
                passed[-1].start()
        for a in range(n):
            copy(a, 0, sibling, me).wait_recv()
            for j, chip in enumerate(chips):
                copy(a, 4 + j, (*chip, 1 - mc), me).wait_recv()
        for cp in first + passed:
            cp.wait_send()
        for cp in mine:
            cp.wait()

    return pl.pallas_call(
        body, name=name, out_shape=[jax.ShapeDtypeStruct((N_DEV,) + x.shape, x.dtype) for x in xs],
        in_specs=[HBM] * n, out_specs=[HBM] * n, scratch_shapes=_comm_sems(n),
    )(*xs)


def _direct_exchange(src_refs, out_refs, send_sems, recv_sems, local_sems, per_peer):
    mx, my, mc = lax.axis_index("x"), lax.axis_index("y"), lax.axis_index("c")
    me = 4 * mx + 2 * my + mc

    def flip(v, bit):
        return 1 - v if bit else v

    def copies():
        mine, remote = [], []
        for a, (src, out) in enumerate(zip(src_refs, out_refs)):
            mine.append(pltpu.make_async_copy(src.at[me] if per_peer else src, out.at[me], local_sems.at[a]))
            for k in range(1, N_DEV):
                px, py, pc = flip(mx, k & 4), flip(my, k & 2), flip(mc, k & 1)
                remote.append(pltpu.make_async_remote_copy(
                    src_ref=src.at[4 * px + 2 * py + pc] if per_peer else src, dst_ref=out.at[me],
                    send_sem=send_sems.at[PEERS * a + k - 1], recv_sem=recv_sems.at[PEERS * a + k - 1],
                    device_id=(px, py, pc), device_id_type=MESH))
        return mine, remote

    def start():
        mine, remote = copies()
        for cp in mine + remote:
            cp.start()

    def wait():
        mine, remote = copies()
        for cp in remote:
            cp.wait_recv()
        for cp in remote:
            cp.wait_send()
        for cp in mine:
            cp.wait()

    return start, wait


def _grad_exchange(gs, name):
    n = len(gs)

    def body(*refs):
        start, wait = _direct_exchange(refs[:n], refs[n:2 * n], *refs[2 * n:], per_peer=True)
        start()
        wait()

    return pl.pallas_call(
        body, name=name, out_shape=[jax.ShapeDtypeStruct(g.shape, g.dtype) for g in gs],
        in_specs=[HBM] * n, out_specs=[HBM] * n, scratch_shapes=_comm_sems(n),
    )(*gs)


def _sum_adamw(parts, w, m, v, name):
    rows, cols = w.shape
    tr = next((t for t in range(ADAM_ROWS, 15, -16) if rows % t == 0), rows)
    c1 = 1.0 - ADAM_B1 ** ADAM_STEP
    c2 = 1.0 - ADAM_B2 ** ADAM_STEP

    def body(p_ref, w_ref, m_ref, v_ref, g_out, d_out, m_out, v_out):
        g = p_ref[0].astype(F32)
        for q in range(1, N_DEV):
            g = g + p_ref[q].astype(F32)
        m_new = ADAM_B1 * m_ref[...] + (1.0 - ADAM_B1) * g
        v_new = ADAM_B2 * v_ref[...] + (1.0 - ADAM_B2) * (g * g)
        m_hat = m_new / c1
        v_hat = v_new / c2
        g_out[...] = g
        d_out[...] = -ADAM_LR * (m_hat / (jnp.sqrt(v_hat) + ADAM_EPS) + ADAM_WD * w_ref[...])
        m_out[...] = m_new
        v_out[...] = v_new

    blk = pl.BlockSpec((tr, cols), lambda i: (i, 0))
    out = jax.ShapeDtypeStruct((rows, cols), F32)
    return pl.pallas_call(
        body, name=name, grid=(rows // tr,),
        in_specs=[pl.BlockSpec((N_DEV, tr, cols), lambda i: (0, i, 0)), blk, blk, blk],
        out_specs=[blk, blk, blk, blk], out_shape=[out, out, out, out],
        compiler_params=_params("parallel"),
    )(parts, w, m, v)


def _tile(dim, cap=MM_TILE):
    if dim <= cap:
        return dim
    for t in range(cap, LANES - 1, -LANES):
        if dim % t == 0:
            return t
    return dim


def _mm_call(a, b, form, name, out_dtype=F32, base=None):
    if form == 'nn':
        (m, k), n = a.shape, b.shape[1]
    elif form == 'nt':
        (m, k), n = a.shape, b.shape[0]
    else:
        (k, m), n = a.shape, b.shape[1]
    tk = k if (form == 'nn' and k <= MM_K_WHOLE) else _tile(k, MM_TILE_WIDE)
    tm = _tile(m, MM_TILE_WIDE if form == 'tn' else MM_TILE)
    tn = _tile(n, MM_TILE_WIDE)
    nk = k // tk
    contract = {'nn': ((1,), (0,)), 'nt': ((1,), (1,)), 'tn': ((0,), (0,))}[form]

    acc_in_out = nk == 1 or out_dtype == F32
    assert base is None or nk == 1
    extra = [] if base is None else [base]

    def body(a_ref, b_ref, *rest):
        o_ref, acc = rest[len(extra)], rest[len(extra) + 1:]
        part = lax.dot_general(a_ref[...].astype(BF16), b_ref[...].astype(BF16), (contract, ((), ())),
                               preferred_element_type=F32)
        if nk == 1:
            o_ref[...] = (part + rest[0][...] if extra else part).astype(out_dtype)
            return
        acc_ref = o_ref if acc_in_out else acc[0]

        @pl.when(pl.program_id(2) == 0)
        def _():
            acc_ref[...] = part

        @pl.when(pl.program_id(2) > 0)
        def _():
            acc_ref[...] += part

        if not acc_in_out:
            @pl.when(pl.program_id(2) == nk - 1)
            def _():
                o_ref[...] = acc_ref[...].astype(out_dtype)

    a_spec = pl.BlockSpec((tk, tm), lambda j, i, l: (l, i)) if form == 'tn' else pl.BlockSpec((tm, tk), lambda j, i, l: (i, l))
    b_spec = pl.BlockSpec((tn, tk), lambda j, i, l: (j, l)) if form == 'nt' else pl.BlockSpec((tk, tn), lambda j, i, l: (l, j))
    o_spec = pl.BlockSpec((tm, tn), lambda j, i, l: (i, j))
    return pl.pallas_call(
        body, name=name, grid=(n // tn, m // tm, nk),
        in_specs=[a_spec, b_spec] + [o_spec] * len(extra), out_specs=o_spec,
        out_shape=jax.ShapeDtypeStruct((m, n), out_dtype),
        scratch_shapes=[] if acc_in_out else [pltpu.VMEM((tm, tn), F32)],
        compiler_params=_params("parallel", "parallel", "arbitrary"),
    )(a, b, *extra)


def _make_mm(name):
    @jax.custom_vjp
    def mm(a, b):
        return _mm_call(a, b, 'nn', name + '_fwd')

    def fwd(a, b):
        return _mm_call(a, b, 'nn', name + '_fwd'), (a, b)

    def bwd(res, g):
        a, b = res
        return _mm_call(g, b, 'nt', name + '_da'), _mm_call(a, g, 'tn', name + '_db', out_dtype=BF16)

    mm.defvjp(fwd, bwd)
    return mm


def _make_mm_add(name):
    @jax.custom_vjp
    def mm(base, a, b):
        return _mm_call(a, b, 'nn', name + '_fwd', base=base)

    def fwd(base, a, b):
        return _mm_call(a, b, 'nn', name + '_fwd', base=base), (a, b)

    def bwd(res, g):
        a, b = res
        return g, _mm_call(g, b, 'nt', name + '_da'), _mm_call(a, g, 'tn', name + '_db', out_dtype=BF16)

    mm.defvjp(fwd, bwd)
    return mm


def _rms_fwd_call(x, g, name):
    m, d = x.shape
    tm = _tile(m)

    def body(x_ref, g_ref, o_ref):
        xv = x_ref[...]
        rinv = lax.rsqrt(jnp.mean(xv * xv, axis=-1, keepdims=True) + NORM_EPS)
        o_ref[...] = xv * rinv * g_ref[...]

    return pl.pallas_call(
        body, name=name, grid=(m // tm,),
        in_specs=[pl.BlockSpec((tm, d), lambda i: (i, 0)), pl.BlockSpec((1, d), lambda i: (0, 0))],
        out_specs=pl.BlockSpec((tm, d), lambda i: (i, 0)), out_shape=jax.ShapeDtypeStruct((m, d), F32),
        compiler_params=_params("parallel"),
    )(x, g)


def _rms_bwd_call(x, g, dy, name, d_skip=None):
    m, d = x.shape
    tm = _tile(m)
    extra = [] if d_skip is None else [d_skip]

    def body(x_ref, g_ref, dy_ref, *rest):
        dx_ref, dg_ref = rest[len(extra):]

        @pl.when(pl.program_id(0) == 0)
        def _():
            dg_ref[...] = jnp.zeros_like(dg_ref)

        xv, dyv = x_ref[...], dy_ref[...]
        rinv = lax.rsqrt(jnp.mean(xv * xv, axis=-1, keepdims=True) + NORM_EPS)
        xh = xv * rinv
        dg_ref[...] += jnp.sum(dyv * xh, axis=0, keepdims=True)
        dxh = dyv * g_ref[...]
        dx = rinv * (dxh - xh * jnp.mean(dxh * xh, axis=-1, keepdims=True))
        dx_ref[...] = dx + rest[0][...] if extra else dx

    row = pl.BlockSpec((tm, d), lambda i: (i, 0))
    vec = pl.BlockSpec((1, d), lambda i: (0, 0))
    return pl.pallas_call(
        body, name=name, grid=(m // tm,), in_specs=[row, vec, row] + [row] * len(extra), out_specs=[row, vec],
        out_shape=[jax.ShapeDtypeStruct((m, d), F32), jax.ShapeDtypeStruct((1, d), F32)],
        compiler_params=_params("arbitrary"),
    )(x, g, dy, *extra)


def _make_rms(name):
    @jax.custom_vjp
    def rms(x, g):
        return _rms_fwd_call(x, g, name + '_fwd')

    def fwd(x, g):
        return _rms_fwd_call(x, g, name + '_fwd'), (x, g)

    def bwd(res, dy):
        x, g = res
        dx, dg = _rms_bwd_call(x, g, dy, name + '_bwd')
        return dx, dg

    rms.defvjp(fwd, bwd)
    return rms


def _final_loss_call(x, g, target, ct=None):
    m, d = x.shape
    tm = _tile(m)
    row = pl.BlockSpec((tm, d), lambda i: (i, 0))
    vec = pl.BlockSpec((1, d), lambda i: (0, 0))
    acc = pl.BlockSpec((1, LANES), lambda i: (0, 0))

    def normed(x_ref, g_ref):
        xv = x_ref[...]
        rinv = lax.rsqrt(jnp.mean(xv * xv, axis=-1, keepdims=True) + NORM_EPS)
        return rinv, xv * rinv

    def fwd_body(x_ref, g_ref, t_ref, loss_ref):
        @pl.when(pl.program_id(0) == 0)
        def _():
            loss_ref[...] = jnp.zeros_like(loss_ref)

        _, xh = normed(x_ref, g_ref)
        err = xh * g_ref[...] - t_ref[...]
        loss_ref[...] += 0.5 * jnp.sum(jnp.mean(err * err, axis=-1, keepdims=True), axis=0, keepdims=True)

    def bwd_body(x_ref, g_ref, t_ref, ct_ref, dx_ref, dg_ref):
        @pl.when(pl.program_id(0) == 0)
        def _():
            dg_ref[...] = jnp.zeros_like(dg_ref)

        rinv, xh = normed(x_ref, g_ref)
        dyv = (xh * g_ref[...] - t_ref[...]) * (ct_ref[0:1, 0:1] * (1.0 / d))
        dg_ref[...] += jnp.sum(dyv * xh, axis=0, keepdims=True)
        dxh = dyv * g_ref[...]
        dx_ref[...] = rinv * (dxh - xh * jnp.mean(dxh * xh, axis=-1, keepdims=True))

    if ct is None:
        return pl.pallas_call(
            fwd_body, name='final_loss_fwd', grid=(m // tm,), in_specs=[row, vec, row], out_specs=acc,
            out_shape=jax.ShapeDtypeStruct((1, LANES), F32), compiler_params=_params("arbitrary"),
        )(x, g, target)
    return pl.pallas_call(
        bwd_body, name='final_loss_bwd', grid=(m // tm,), in_specs=[row, vec, row, acc], out_specs=[row, vec],
        out_shape=[jax.ShapeDtypeStruct((m, d), F32), jax.ShapeDtypeStruct((1, d), F32)],
        compiler_params=_params("arbitrary"),
    )(x, g, target, ct)


@jax.custom_vjp
def _final_loss(x, g, target):
    return _final_loss_call(x, g, target)[0, 0]


def _final_loss_bwd(res, ct):
    x, g, target = res
    dx, dg = _final_loss_call(x, g, target, ct=jnp.full((1, LANES), ct, F32))
    return dx, dg, jnp.zeros_like(target)


_final_loss.defvjp(lambda x, g, target: (_final_loss_call(x, g, target)[0, 0], (x, g, target)), _final_loss_bwd)


def _make_rms_skip(name):
    @jax.custom_vjp
    def rms(x, g):
        return _rms_fwd_call(x, g, name + '_fwd'), x

    def fwd(x, g):
        return (_rms_fwd_call(x, g, name + '_fwd'), x), (x, g)

    def bwd(res, cts):
        x, g = res
        dx, dg = _rms_bwd_call(x, g, cts[0], name + '_bwd', d_skip=cts[1])
        return dx, dg

    rms.defvjp(fwd, bwd)
    return rms


def _time_shifts(x):
    t = x.shape[0]
    rows = lax.broadcasted_iota(jnp.int32, x.shape, 0)
    return (jnp.where(rows == 0, 0.0, pltpu.roll(x, 1, 0)), jnp.where(rows == t - 1, 0.0, pltpu.roll(x, t - 1, 0)))


def _conv3(x, cw_ref, cb_ref):
    xp, xn = _time_shifts(x)
    return cw_ref[0:1, :] * xp + cw_ref[1:2, :] * x + cw_ref[2:3, :] * xn + cb_ref[...]


def _glu_specs(b, t, f):
    tc = _tile(f, FFN_COLS)
    seq = pl.BlockSpec((1, t, tc), lambda j, bi: (bi, 0, j))
    cw = pl.BlockSpec((3, tc), lambda j, bi: (0, j))
    cb = pl.BlockSpec((1, tc), lambda j, bi: (0, j))
    return tc, seq, cw, cb


def _glu_fwd_call(ug, uv, cwg, cwv, cbg, cbv):
    b, t, f = ug.shape
    tc, seq, cw, cb = _glu_specs(b, t, f)

    def body(ug_ref, uv_ref, cwg_ref, cwv_ref, cbg_ref, cbv_ref, o_ref):
        g = _conv3(ug_ref[0], cwg_ref, cbg_ref)
        o_ref[0] = g * jax.nn.sigmoid(g) * _conv3(uv_ref[0], cwv_ref, cbv_ref)

    return pl.pallas_call(
        body, name='glu_fwd', grid=(f // tc, b), in_specs=[seq, seq, cw, cw, cb, cb], out_specs=seq,
        out_shape=jax.ShapeDtypeStruct((b, t, f), F32), compiler_params=_params("parallel", "parallel"),
    )(ug, uv, cwg, cwv, cbg, cbv)


def _glu_bwd_call(ug, uv, cwg, cwv, cbg, cbv, dact):
    b, t, f = ug.shape
    tc, seq, cw, cb = _glu_specs(b, t, f)

    def body(ug_ref, uv_ref, cwg_ref, cwv_ref, cbg_ref, cbv_ref, da_ref,
             dug_ref, duv_ref, dcwg_ref, dcwv_ref, dcbg_ref, dcbv_ref):
        @pl.when(pl.program_id(1) == 0)
        def _():
            for ref in (dcwg_ref, dcwv_ref, dcbg_ref, dcbv_ref):
                ref[...] = jnp.zeros_like(ref)

        g = _conv3(ug_ref[0], cwg_ref, cbg_ref)
        v = _conv3(uv_ref[0], cwv_ref, cbv_ref)
        sig = jax.nn.sigmoid(g)
        da = da_ref[0]
        dv = da * (g * sig)
        dg = da * v * (sig * (1.0 + g * (1.0 - sig)))

        def conv_bwd(dc, x_ref, cw_ref, dx_ref, dcw_ref, dcb_ref):
            dcp, dcn = _time_shifts(dc)
            dx_ref[0] = cw_ref[0:1, :] * dcn + cw_ref[1:2, :] * dc + cw_ref[2:3, :] * dcp
            x = x_ref[0]
            xp, xn = _time_shifts(x)
            for n, xs in enumerate((xp, x, xn)):
                dcw_ref[n:n + 1, :] += jnp.sum(dc * xs, axis=0, keepdims=True)
            dcb_ref[...] += jnp.sum(dc, axis=0, keepdims=True)

        conv_bwd(dg, ug_ref, cwg_ref, dug_ref, dcwg_ref, dcbg_ref)
        conv_bwd(dv, uv_ref, cwv_ref, duv_ref, dcwv_ref, dcbv_ref)

    big = jax.ShapeDtypeStruct((b, t, f), F32)
    return pl.pallas_call(
        body, name='glu_bwd', grid=(f // tc, b), in_specs=[seq, seq, cw, cw, cb, cb, seq],
        out_specs=[seq, seq, cw, cw, cb, cb],
        out_shape=[big, big, jax.ShapeDtypeStruct((3, f), F32), jax.ShapeDtypeStruct((3, f), F32),
                   jax.ShapeDtypeStruct((1, f), F32), jax.ShapeDtypeStruct((1, f), F32)],
        compiler_params=_params("parallel", "arbitrary"),
    )(ug, uv, cwg, cwv, cbg, cbv, dact)


def _shift_call(z, mu_p, mu_n, dzs=None):
    b, t, c = z.shape
    tc = _tile(c, SHIFT_COLS)
    seq = pl.BlockSpec((1, t, tc), lambda j, bi: (bi, 0, j))
    row = pl.BlockSpec((1, tc), lambda j, bi: (0, j))

    def fwd_body(z_ref, mp_ref, mn_ref, o_ref):
        x = z_ref[0]
        xp, xn = _time_shifts(x)
        o_ref[0] = x + mp_ref[...] * (xp - x) + mn_ref[...] * (xn - x)

    def bwd_body(z_ref, mp_ref, mn_ref, d_ref, dz_ref, dmp_ref, dmn_ref):
        @pl.when(pl.program_id(1) == 0)
        def _():
            dmp_ref[...] = jnp.zeros_like(dmp_ref)
            dmn_ref[...] = jnp.zeros_like(dmn_ref)

        x, d = z_ref[0], d_ref[0]
        xp, xn = _time_shifts(x)
        dp, dn = _time_shifts(d)
        mp, mn = mp_ref[...], mn_ref[...]
        dz_ref[0] = d * (1.0 - mp - mn) + mp * dn + mn * dp
        dmp_ref[...] += jnp.sum(d * (xp - x), axis=0, keepdims=True)
        dmn_ref[...] += jnp.sum(d * (xn - x), axis=0, keepdims=True)

    if dzs is None:
        return pl.pallas_call(
            fwd_body, name='shift_fwd', grid=(c // tc, b), in_specs=[seq, row, row], out_specs=seq,
            out_shape=jax.ShapeDtypeStruct(z.shape, F32), compiler_params=_params("parallel", "parallel"),
        )(z, mu_p, mu_n)
    return pl.pallas_call(
        bwd_body, name='shift_bwd', grid=(c // tc, b), in_specs=[seq, row, row, seq], out_specs=[seq, row, row],
        out_shape=[jax.ShapeDtypeStruct(z.shape, F32), jax.ShapeDtypeStruct(mu_p.shape, F32),
                   jax.ShapeDtypeStruct(mu_n.shape, F32)],
        compiler_params=_params("parallel", "arbitrary"),
    )(z, mu_p, mu_n, dzs)


@jax.custom_vjp
def _token_shift(z, mu_p, mu_n):
    return _shift_call(z, mu_p, mu_n)


_token_shift.defvjp(lambda z, mu_p, mu_n: (_shift_call(z, mu_p, mu_n), (z, mu_p, mu_n)),
                    lambda res, d: tuple(_shift_call(*res, dzs=d)))


@jax.custom_vjp
def _conv_glu(ug, uv, cwg, cwv, cbg, cbv):
    return _glu_fwd_call(ug, uv, cwg, cwv, cbg, cbv)


def _conv_glu_fwd(*args):
    return _glu_fwd_call(*args), args


def _conv_glu_bwd(res, dact):
    return tuple(_glu_bwd_call(*res, dact))


_conv_glu.defvjp(_conv_glu_fwd, _conv_glu_bwd)


HEAD_LANES = 2 * D_NOPE
PAIR = 2


def _lane(shape):
    return lax.broadcasted_iota(jnp.int32, shape, len(shape) - 1)


def _rope(x, c, s1, s2):
    return x * c + pltpu.roll(x, HEAD_LANES - D_ROPE // 2, 1) * s1 + pltpu.roll(x, D_ROPE // 2, 1) * s2


def _rope_t(g, c, s1, s2):
    return g * c + pltpu.roll(g * s1, D_ROPE // 2, 1) + pltpu.roll(g * s2, HEAD_LANES - D_ROPE // 2, 1)


def _attn_setup(kv_ref, kr_ref, tabs, k2, v16):
    c, s1, s2 = (tb[...] for tb in tabs)
    krr = _rope(kr_ref[0], c, s1, s2)
    v16[...] = kv_ref[0].astype(BF16)
    for hh in range(PAIR):
        slab = kv_ref[0, :, HEAD_LANES * hh:HEAD_LANES * (hh + 1)]
        k2[hh] = jnp.where(_lane(slab.shape) < D_NOPE, slab, krr).astype(BF16)


def _attn_queries(q_ref, tabs, rows, hh):
    c, s1, s2 = (tb[rows, :] for tb in tabs)
    return (_rope(q_ref[0, :, HEAD_LANES * hh:HEAD_LANES * (hh + 1)], c, s1, s2) * MLA_SCALE).astype(BF16), (c, s1, s2)


def _attn_specs(b, t, tq):
    qspec = pl.BlockSpec((1, tq, PAIR * HEAD_LANES), lambda bi, p, i: (bi, i, p))
    kvspec = pl.BlockSpec((1, t, PAIR * HEAD_LANES), lambda bi, p, i: (bi, 0, p))
    krspec = pl.BlockSpec((1, t, HEAD_LANES), lambda bi, p, i: (bi, 0, 0))
    tab = pl.BlockSpec((t, HEAD_LANES), lambda bi, p, i: (0, 0))
    ospec = pl.BlockSpec((1, tq, PAIR * D_V), lambda bi, p, i: (bi, i, p))
    lspec = pl.BlockSpec((1, 1, tq, HEAD_LANES), lambda bi, p, i: (bi, p, i, 0))
    return qspec, kvspec, krspec, tab, ospec, lspec


def _attn_fwd_call(q, kv, kr, tabs):
    b, t, _ = q.shape
    tq = min(ATT_TQ, t)
    qspec, kvspec, krspec, tab, ospec, lspec = _attn_specs(b, t, tq)

    def body(q_ref, kv_ref, kr_ref, c_ref, s1_ref, s2_ref, o_ref, lse_ref, k2, v16):
        tabs = (c_ref, s1_ref, s2_ref)

        @pl.when(pl.program_id(2) == 0)
        def _():
            _attn_setup(kv_ref, kr_ref, tabs, k2, v16)

        rows = pl.ds(pl.multiple_of(pl.program_id(2) * tq, tq), tq)
        outs, lses = [], []
        for hh in range(PAIR):
            qh, _ = _attn_queries(q_ref, tabs, rows, hh)
            s = lax.dot_general(qh, k2[hh], (((1,), (1,)), ((), ())), preferred_element_type=F32)
            m = jnp.max(s, axis=-1, keepdims=True)
            p = jnp.exp(s - m)
            l = jnp.sum(p, axis=-1, keepdims=True)
            slab16 = v16[:, HEAD_LANES * hh:HEAD_LANES * (hh + 1)]
            outs.append(jnp.dot(p.astype(BF16), slab16, preferred_element_type=F32) / l)
            lses.append(m + jnp.log(l))
        low = _lane(outs[0].shape) < D_V
        o_ref[0] = jnp.where(low, pltpu.roll(outs[0], D_V, 1), outs[1])
        lse_ref[0, 0] = jnp.where(low, lses[0], lses[1])

    return pl.pallas_call(
        body, name='attn_fwd', grid=(b, H // PAIR, t // tq),
        in_specs=[qspec, kvspec, krspec, tab, tab, tab], out_specs=[ospec, lspec],
        out_shape=[jax.ShapeDtypeStruct((b, t, H * D_V), F32), jax.ShapeDtypeStruct((b, H // PAIR, t, HEAD_LANES), F32)],
        scratch_shapes=[pltpu.VMEM((PAIR, t, HEAD_LANES), BF16), pltpu.VMEM((t, PAIR * HEAD_LANES), BF16)],
        compiler_params=_params("parallel", "parallel", "arbitrary"),
    )(q, kv, kr, *tabs)


def _attn_bwd_call(q, kv, kr, tabs, o, lse, do):
    b, t, _ = q.shape
    tq = min(ATT_TQ, t)
    n_q = t // tq
    qspec, kvspec, krspec, tab, ospec, lspec = _attn_specs(b, t, tq)

    def body(q_ref, kv_ref, kr_ref, c_ref, s1_ref, s2_ref, o_ref, lse_ref, do_ref, dq_ref, dkv_ref, dkr_ref, k2, v16, dk2):
        tabs = (c_ref, s1_ref, s2_ref)
        pair, step = pl.program_id(1), pl.program_id(2)

        @pl.when(step == 0)
        def _():
            _attn_setup(kv_ref, kr_ref, tabs, k2, v16)
            dk2[...] = jnp.zeros_like(dk2)
            dkv_ref[...] = jnp.zeros_like(dkv_ref)

        @pl.when((step == 0) & (pair == 0))
        def _():
            dkr_ref[...] = jnp.zeros_like(dkr_ref)

        rows = pl.ds(pl.multiple_of(step * tq, tq), tq)
        dov, ov = do_ref[0], o_ref[0]
        lane = _lane(dov.shape)
        upper = lane >= D_V
        for hh in range(PAIR):
            qh, qtabs = _attn_queries(q_ref, tabs, rows, hh)
            s = lax.dot_general(qh, k2[hh], (((1,), (1,)), ((), ())), preferred_element_type=F32)
            p = jnp.exp(s - lse_ref[0, 0, :, D_V * hh:D_V * hh + 1])
            mine = upper if hh else ~upper
            delta = jnp.sum(jnp.where(mine, dov * ov, 0.0), axis=-1, keepdims=True)
            do_h = jnp.where(upper, dov if hh else pltpu.roll(dov, D_V, 1), 0.0).astype(BF16)
            slab16 = v16[:, HEAD_LANES * hh:HEAD_LANES * (hh + 1)]
            dp = lax.dot_general(do_h, slab16, (((1,), (1,)), ((), ())), preferred_element_type=F32)
            ds = (p * (dp - delta)).astype(BF16)
            dqh = jnp.dot(ds, k2[hh], preferred_element_type=F32) * MLA_SCALE
            dq_ref[0, :, HEAD_LANES * hh:HEAD_LANES * (hh + 1)] = _rope_t(dqh, *qtabs)
            dk2[hh] += lax.dot_general(ds, qh, (((0,), (0,)), ((), ())), preferred_element_type=F32)
            dkv_ref[0, :, HEAD_LANES * hh:HEAD_LANES * (hh + 1)] += lax.dot_general(
                p.astype(BF16), do_h, (((0,), (0,)), ((), ())), preferred_element_type=F32)

        @pl.when(step == n_q - 1)
        def _():
            c, s1, s2 = (tb[...] for tb in tabs)
            for hh in range(PAIR):
                g = dk2[hh]
                key_lane = _lane(g.shape)
                dkv_ref[0, :, HEAD_LANES * hh:HEAD_LANES * (hh + 1)] += jnp.where(key_lane < D_NOPE, g, 0.0)
                dkr_ref[0] += _rope_t(jnp.where(key_lane >= D_NOPE, g, 0.0), c, s1, s2)

    return pl.pallas_call(
        body, name='attn_bwd', grid=(b, H // PAIR, n_q),
        in_specs=[qspec, kvspec, krspec, tab, tab, tab, ospec, lspec, ospec], out_specs=[qspec, kvspec, krspec],
        out_shape=[jax.ShapeDtypeStruct(q.shape, F32), jax.ShapeDtypeStruct(kv.shape, F32), jax.ShapeDtypeStruct(kr.shape, F32)],
        scratch_shapes=[pltpu.VMEM((PAIR, t, HEAD_LANES), BF16), pltpu.VMEM((t, PAIR * HEAD_LANES), BF16),
                        pltpu.VMEM((PAIR, t, HEAD_LANES), F32)],
        compiler_params=_params("parallel", "arbitrary", "arbitrary"),
    )(q, kv, kr, *tabs, o, lse, do)


@jax.custom_vjp
def _attention(q, kv, kr, tabs):
    return _attn_fwd_call(q, kv, kr, tabs)[0]


def _attention_fwd(q, kv, kr, tabs):
    o, lse = _attn_fwd_call(q, kv, kr, tabs)
    return o, (q, kv, kr, tabs, o, lse)


def _attention_bwd(res, do):
    q, kv, kr, tabs, o, lse = res
    return (*_attn_bwd_call(q, kv, kr, tabs, o, lse, do), tuple(jnp.zeros_like(tb) for tb in tabs))


_attention.defvjp(_attention_fwd, _attention_bwd)


SROWS = N * D_RWKV // SEG


def _seg_ones():
    r = lax.broadcasted_iota(jnp.int32, (SEG, SEG), 0) // N
    c = lax.broadcasted_iota(jnp.int32, (SEG, SEG), 1) // N
    return (r == c).astype(BF16)


def _eye_mask():
    r = lax.broadcasted_iota(jnp.int32, (SROWS, SEG), 0) & (N - 1)
    c = lax.broadcasted_iota(jnp.int32, (SROWS, SEG), 1) & (N - 1)
    return r == c


def _row2(ref, bi, ti, dtype=F32):
    parts = [jnp.broadcast_to(ref[bi, pl.ds(ti, 1), pl.ds(SEG * q, SEG)].astype(dtype), (N, SEG))
             for q in range(D_RWKV // SEG)]
    return jnp.concatenate(parts, axis=0)


def _split2(x):
    hi = x.astype(BF16)
    return hi, (x - hi.astype(F32)).astype(BF16)


def _col_sum(x):
    return jnp.concatenate([jnp.sum(x[N * q:N * (q + 1)], axis=0, keepdims=True) for q in range(D_RWKV // SEG)], axis=1)


def _scan_specs(b, t, rev):
    nc = t // SCAN_CHUNK
    if rev:
        return (pl.BlockSpec((b, SCAN_CHUNK, D_RWKV), lambda c: (0, nc - 1 - c, 0)),
                pl.BlockSpec((b, SCAN_CHUNK, SROWS, SEG), lambda c: (0, nc - 1 - c, 0, 0)))
    return (pl.BlockSpec((b, SCAN_CHUNK, D_RWKV), lambda c: (0, c, 0)),
            pl.BlockSpec((b, SCAN_CHUNK, SROWS, SEG), lambda c: (0, c, 0, 0)))


def _scan_fwd_call(r, v, kk, wf, kf, qf, wb, kb, qb, ride):
    b, t, _ = r.shape
    n_chunks = t // SCAN_CHUNK
    last = SCAN_CHUNK - 1
    nr = len(ride)

    def body(rf, vf, kkf, wf_, kf_, qf_, rb, vb, kkb, wb_, kb_, qb_, *rest):
        (yf, yb, sf, sb), scratch = rest[nr:nr + 4], rest[2 * nr + 4:]
        states = scratch[:2 * b]
        send, arrive = _direct_exchange(rest[:nr], rest[nr + 4:2 * nr + 4], *scratch[2 * b:], per_peer=False)

        @pl.when(pl.program_id(0) == 0)
        def _():
            send()
            for st in states:
                st[...] = jnp.zeros_like(st)

        ones, mask = _seg_ones(), _eye_mask()
        zero16 = jnp.zeros((), BF16)
        chains = []
        for bi in range(b):
            chains.append((rf, vf, kkf, wf_, kf_, qf_, yf, sf, states[2 * bi], bi, False))
            chains.append((rb, vb, kkb, wb_, kb_, qb_, yb, sb, states[2 * bi + 1], bi, True))

        def tix(i, rev):
            return last - i if rev else i

        def put_y(y_, bi, ti, ycol):
            y_[bi, pl.ds(ti, 1), :] = _col_sum(jnp.where(mask, ycol, 0.0))

        def steps(i, with_y):
            parts = []
            for (r_, v_, kk_, w_, k_, q_, y_, s_, st, bi, rev) in chains:
                ti = tix(i, rev)
                s = st[...]
                s_[bi, ti] = s
                parts.append((s * _row2(kk_, bi, ti)).astype(BF16))
                parts.append(jnp.where(mask, _row2(v_, bi, ti, BF16), zero16))
                if with_y:
                    parts.append((s * _row2(r_, bi, tix(i - 1, rev))).astype(BF16))
            res = jnp.dot(jnp.concatenate(parts, axis=0), ones, preferred_element_type=F32)
            off = 0
            for (r_, v_, kk_, w_, k_, q_, y_, s_, st, bi, rev) in chains:
                ti = tix(i, rev)
                u = res[off:off + SROWS]
                vcol = res[off + SROWS:off + 2 * SROWS]
                off += 2 * SROWS
                if with_y:
                    put_y(y_, bi, tix(i - 1, rev), res[off:off + SROWS])
                    off += SROWS
                st[...] = st[...] * _row2(w_, bi, ti) - u * _row2(q_, bi, ti) + vcol * _row2(k_, bi, ti)

        steps(0, False)

        def loop(i, carry):
            steps(i, True)
            return carry

        lax.fori_loop(1, SCAN_CHUNK, loop, 0, unroll=5)
        parts = [(c[8][...] * _row2(c[0], c[9], tix(last, c[10]))).astype(BF16) for c in chains]
        res = jnp.dot(jnp.concatenate(parts, axis=0), ones, preferred_element_type=F32)
        for n, c in enumerate(chains):
            put_y(c[6], c[9], tix(last, c[10]), res[n * SROWS:(n + 1) * SROWS])

        @pl.when(pl.program_id(0) == n_chunks - 1)
        def _():
            arrive()

    fr, fs = _scan_specs(b, t, False)
    br, bs = _scan_specs(b, t, True)
    y_shape = jax.ShapeDtypeStruct((b, t, D_RWKV), F32)
    s_shape = jax.ShapeDtypeStruct((b, t, SROWS, SEG), F32)
    return pl.pallas_call(
        body, name='scan_fwd', grid=(n_chunks,),
        in_specs=[fr] * 6 + [br] * 6 + [HBM] * nr, out_specs=[fr, br, fs, bs] + [HBM] * nr,
        out_shape=[y_shape, y_shape, s_shape, s_shape] + [jax.ShapeDtypeStruct((N_DEV,) + a.shape, a.dtype) for a in ride],
        scratch_shapes=[pltpu.VMEM((SROWS, SEG), F32)] * (2 * b) + _comm_sems(nr),
        compiler_params=_params("arbitrary"),
    )(r, v, kk, wf, kf, qf, r, v, kk, wb, kb, qb, *ride)


def _scan_bwd_call(r, v, kk, wf, kf, qf, wb, kb, qb, sf, sb, dyf, dyb, ride):
    b, t, _ = r.shape
    n_chunks = t // SCAN_CHUNK
    last = SCAN_CHUNK - 1
    nr = len(ride)

    def body(rf, vf, kkf, wf_, kf_, qf_, sf_, dyf_, rb, vb, kkb, wb_, kb_, qb_, sb_, dyb_, *rest):
        drf, dvf, dkkf, dwf, dkf, dqf, drb, dvb, dkkb, dwb, dkb, dqb = rest[nr:nr + 12]
        scratch = rest[2 * nr + 12:]
        send, arrive = _direct_exchange(rest[:nr], rest[nr + 12:2 * nr + 12], *scratch[8 * b:], per_peer=True)

        @pl.when(pl.program_id(0) == 0)
        def _():
            send()
            for n in range(2 * b):
                scratch[4 * n][...] = jnp.zeros_like(scratch[4 * n])

        ones, mask = _seg_ones(), _eye_mask()
        zero16 = jnp.zeros((), BF16)
        chains = []
        for bi in range(b):
            chains.append((rf, vf, kkf, wf_, kf_, qf_, sf_, dyf_, (drf, dvf, dkkf, dwf, dkf, dqf),
                           scratch[8 * bi:8 * bi + 4], bi, True))
            chains.append((rb, vb, kkb, wb_, kb_, qb_, sb_, dyb_, (drb, dvb, dkkb, dwb, dkb, dqb),
                           scratch[8 * bi + 4:8 * bi + 8], bi, False))

        def tix(i, rev):
            return last - i if rev else i

        def state_free_parts(v_, dy_, kk_, s_, bi, ti):
            return [jnp.where(mask, _row2(v_, bi, ti, BF16), zero16), jnp.where(mask, _row2(dy_, bi, ti, BF16), zero16),
                    (s_[bi, ti] * _row2(kk_, bi, ti)).astype(BF16)]

        def keep(scr, res, off):
            for n in range(3):
                scr[1 + n][...] = res[off + n * SROWS:off + (n + 1) * SROWS]
            return off + 3 * SROWS

        def first():
            parts = []
            for (r_, v_, kk_, w_, k_, q_, s_, dy_, outs, scr, bi, rev) in chains:
                parts += state_free_parts(v_, dy_, kk_, s_, bi, tix(0, rev))
            res = jnp.dot(jnp.concatenate(parts, axis=0), ones, preferred_element_type=F32)
            off = 0
            for c in chains:
                off = keep(c[9], res, off)

        def steps(i, has_next, recompute):
            parts = []
            for (r_, v_, kk_, w_, k_, q_, s_, dy_, outs, scr, bi, rev) in chains:
                ti = tix(i, rev)
                gst, vc, dc, uc = scr
                dycol = dc[...]
                if recompute:
                    sc = s_[bi, ti] * _row2(w_, bi, ti) - uc[...] * _row2(q_, bi, ti) + vc[...] * _row2(k_, bi, ti)
                else:
                    sc = s_[bi, tix(i - 1, rev)]
                outs[0][bi, pl.ds(ti, 1), :] = _col_sum(sc * dycol)
                g = gst[...] + dycol * _row2(r_, bi, ti)
                gst[...] = g
                parts.append((g * _row2(q_, bi, ti)).astype(BF16))
                parts.append((g * _row2(k_, bi, ti)).astype(BF16))
                if has_next:
                    parts += state_free_parts(v_, dy_, kk_, s_, bi, tix(i + 1, rev))
            res = jnp.dot(jnp.concatenate(parts, axis=0), ones, preferred_element_type=F32)
            off = 0
            for (r_, v_, kk_, w_, k_, q_, s_, dy_, outs, scr, bi, rev) in chains:
                ti = tix(i, rev)
                gst, vc, dc, uc = scr
                dr_, dv_, dkk_, dw_, dk_, dq_ = outs

                def put(ref, val, sign=1.0):
                    ref[bi, pl.ds(ti, 1), :] = sign * _col_sum(val)

                gq = res[off:off + SROWS]
                put(dv_, jnp.where(mask, res[off + SROWS:off + 2 * SROWS], 0.0))
                off += 2 * SROWS
                g, sp = gst[...], s_[bi, ti]
                put(dk_, g * vc[...])
                put(dw_, g * sp)
                put(dq_, g * uc[...], -1.0)
                put(dkk_, sp * gq, -1.0)
                gst[...] = g * _row2(w_, bi, ti) - gq * _row2(kk_, bi, ti)
                if has_next:
                    off = keep(scr, res, off)

        first()
        steps(0, True, True)

        def loop(i, carry):
            steps(i, True, False)
            return carry

        lax.fori_loop(1, last, loop, 0)
        steps(last, False, False)

        @pl.when(pl.program_id(0) == n_chunks - 1)
        def _():
            arrive()

    fr, fs = _scan_specs(b, t, True)
    br, bs = _scan_specs(b, t, False)
    y_shape = jax.ShapeDtypeStruct((b, t, D_RWKV), F32)
    return pl.pallas_call(
        body, name='scan_bwd', grid=(n_chunks,),
        in_specs=[fr] * 6 + [fs, fr] + [br] * 6 + [bs, br] + [HBM] * nr,
        out_specs=[fr] * 6 + [br] * 6 + [HBM] * nr,
        out_shape=[y_shape] * 12 + [jax.ShapeDtypeStruct(a.shape, a.dtype) for a in ride],
        scratch_shapes=[pltpu.VMEM((SROWS, SEG), F32)] * (8 * b) + _comm_sems(nr),
        compiler_params=_params("arbitrary"),
    )(r, v, kk, wf, kf, qf, sf, dyf, r, v, kk, wb, kb, qb, sb, dyb, *ride)


def _rope_tables(t):
    half = D_ROPE // 2
    inv_freq = jnp.power(ROPE_THETA, -jnp.arange(0, D_ROPE, 2, dtype=F32) / D_ROPE)
    ang = jnp.arange(t, dtype=F32)[:, None] * inv_freq[None, :]
    cos, sin, zero = jnp.cos(ang), jnp.sin(ang), jnp.zeros((t, half), F32)
    tail = HEAD_LANES - D_QK
    c = jnp.concatenate([jnp.ones((t, D_NOPE), F32), cos, cos, jnp.ones((t, tail), F32)], axis=1)
    s1 = jnp.concatenate([jnp.zeros((t, D_NOPE), F32), -sin, zero, jnp.zeros((t, tail), F32)], axis=1)
    s2 = jnp.concatenate([jnp.zeros((t, D_NOPE), F32), zero, sin, jnp.zeros((t, tail), F32)], axis=1)
    return c, s1, s2


@jax.custom_vjp
def _dot16(a, w):
    return jnp.dot(a.astype(BF16), w.astype(BF16), preferred_element_type=F32)


def _dot16_fwd(a, w):
    a16, w16 = a.astype(BF16), w.astype(BF16)
    return jnp.dot(a16, w16, preferred_element_type=F32), (a16, w16)


def _dot16_bwd(res, g):
    a16, w16 = res
    g16 = g.astype(BF16)
    return (lax.dot_general(g16, w16, (((1,), (1,)), ((), ())), preferred_element_type=F32),
            lax.dot_general(a16, g16, (((0,), (0,)), ((), ())), preferred_element_type=F32))


_dot16.defvjp(_dot16_fwd, _dot16_bwd)


def _head_sum_tile(x):
    outs = []
    ones = _seg_ones()
    for q in range(x.shape[1] // SEG):
        hi, lo = _split2(x[:, SEG * q:SEG * (q + 1)])
        outs.append(jnp.dot(hi, ones, preferred_element_type=F32) + jnp.dot(lo, ones, preferred_element_type=F32))
    return jnp.concatenate(outs, axis=1)


@jax.custom_vjp
def _hsum(x):
    return _head_sum_tile(x)


_hsum.defvjp(lambda x: (_head_sum_tile(x), None), lambda _, g: (_head_sum_tile(g),))


def _softplus(x):
    return jnp.maximum(x, 0.0) + jnp.log(1.0 + jnp.exp(-jnp.abs(x)))


def _rwkv_pre_fn(k, wdf, wdb, adf, adb, gd, w0f, w2f, w0b, w2b, a0f, a2f, a0b, a2b, g2, k_k, k_a):
    w_f = jnp.exp(-jnp.exp(-_softplus(-(w0f + _dot16(jnp.tanh(wdf), w2f))) - 0.5))
    w_b = jnp.exp(-jnp.exp(-_softplus(-(w0b + _dot16(jnp.tanh(wdb), w2b))) - 0.5))
    a_f = jax.nn.sigmoid(a0f + _dot16(adf, a2f))
    a_b = jax.nn.sigmoid(a0b + _dot16(adb, a2b))
    gate = _dot16(jax.nn.sigmoid(gd), g2)
    kk = k * k_k
    kk = kk / jnp.maximum(jnp.sqrt(_hsum(kk * kk)), L2_EPS)
    return (kk, w_f, k * (1.0 + (a_f - 1.0) * k_a), kk * a_f, w_b, k * (1.0 + (a_b - 1.0) * k_a), kk * a_b, gate)


def _rwkv_post_fn(y_f, y_b, r, k_f, k_b, v, gate, ln_g, ln_b, r_k):
    y = y_f + y_b
    yc = y - _hsum(y) * (1.0 / N)
    var = _hsum(yc * yc) * (1.0 / N)
    y = yc * lax.rsqrt(var + GN_EPS) * ln_g + ln_b
    return ((y + _hsum(r * (k_f + k_b) * r_k) * v) * gate,)


def _make_rowwise(fn, name, n_rows, tm):
    def specs(arrs, whole):
        if whole:
            return [pl.BlockSpec(a.shape, lambda i: (0, 0)) for a in arrs]
        return [pl.BlockSpec((tm, a.shape[1]), lambda i: (i, 0)) for a in arrs]

    def out_widths(rows, params):
        tiles = [jax.ShapeDtypeStruct((tm, a.shape[1]), F32) for a in rows]
        return [o.shape[1] for o in jax.eval_shape(fn, *tiles, *params)]

    def fwd_call(rows, params):
        m = rows[0].shape[0]
        n_in = len(rows) + len(params)
        outs = [jax.ShapeDtypeStruct((m, d), F32) for d in out_widths(rows, params)]

        def body(*refs):
            for o_ref, o in zip(refs[n_in:], fn(*[ref[...] for ref in refs[:n_in]])):
                o_ref[...] = o

        return pl.pallas_call(
            body, name=name + '_fwd', grid=(m // tm,), in_specs=specs(rows, False) + specs(params, True),
            out_specs=specs(outs, False), out_shape=outs, compiler_params=_params("parallel"),
        )(*rows, *params)

    def bwd_call(rows, params, cts):
        m = rows[0].shape[0]
        n_in = len(rows) + len(params)
        n_all = n_in + len(cts)
        outs = ([jax.ShapeDtypeStruct(a.shape, F32) for a in rows] + [jax.ShapeDtypeStruct(a.shape, F32) for a in params])

        def body(*refs):
            _, vjp = jax.vjp(fn, *[ref[...] for ref in refs[:n_in]])
            grads = vjp(tuple(ref[...] for ref in refs[n_in:n_all]))
            d_rows, d_params = refs[n_all:n_all + len(rows)], refs[n_all + len(rows):]
            for ref, g in zip(d_rows, grads[:len(rows)]):
                ref[...] = g

            @pl.when(pl.program_id(0) == 0)
            def _():
                for ref in d_params:
                    ref[...] = jnp.zeros_like(ref)

            for ref, g in zip(d_params, grads[len(rows):]):
                ref[...] += g

        return pl.pallas_call(
            body, name=name + '_bwd', grid=(m // tm,),
            in_specs=specs(rows, False) + specs(params, True) + specs(cts, False),
            out_specs=specs(rows, False) + specs(params, True), out_shape=outs, compiler_params=_params("arbitrary"),
        )(*rows, *params, *cts)

    @jax.custom_vjp
    def op(*args):
        return tuple(fwd_call(args[:n_rows], args[n_rows:]))

    def op_fwd(*args):
        return tuple(fwd_call(args[:n_rows], args[n_rows:])), args

    def op_bwd(args, cts):
        return tuple(bwd_call(args[:n_rows], args[n_rows:], cts))

    op.defvjp(op_fwd, op_bwd)
    return op


def _rwkv_operands(z, full, rep):
    b, t, _ = z.shape
    m = b * t
    z = _token_shift(z, rep['shift_mu_prev'], rep['shift_mu_next']).reshape(m, RWKV_COLS)
    cols, at = [], 0
    for width in RWKV_SPLITS:
        cols.append(z[:, at:at + width])
        at += width
    r, k, v, *lora_in = cols
    kk, w_f, k_f, q_f, w_b, k_b, q_b, gate = _make_rowwise(_rwkv_pre_fn, 'rwkv_pre', 6, _tile(m, ROW_TILE))(
        k, *lora_in, rep['decay_w0_fwd'], full['decay_w2_fwd'], rep['decay_w0_bwd'], full['decay_w2_bwd'],
        rep['iclr_a0_fwd'], full['iclr_a2_fwd'], rep['iclr_a0_bwd'], full['iclr_a2_bwd'], full['gate_g2'],
        rep['k_k'], rep['k_a'])
    return r, v, kk, w_f, k_f, q_f, w_b, k_b, q_b, gate


def _mla_mixer(z, full, rep, b, t):
    m = b * t
    c_q, c_kv, k_rope = z[:, :Q_LORA], z[:, Q_LORA:Q_LORA + KV_LORA], z[:, Q_LORA + KV_LORA:]
    w_uq = jnp.pad(full['w_uq'].reshape(Q_LORA, H, D_QK), ((0, 0), (0, 0), (0, HEAD_LANES - D_QK))).reshape(Q_LORA, H * HEAD_LANES)
    q = _make_mm('mm_uq')(_make_rms('rms_q')(c_q, rep['q_norm_g']), w_uq)
    kv = _make_mm('mm_ukv')(_make_rms('rms_kv')(c_kv, rep['kv_norm_g']), full['w_ukv'])
    kr = jnp.pad(k_rope, ((0, 0), (D_NOPE, HEAD_LANES - D_QK)))
    o = _attention(q.reshape(b, t, -1), kv.reshape(b, t, -1), kr.reshape(b, t, HEAD_LANES), _rope_tables(t))
    return _make_rms('rms_mla_out')(o.reshape(m, H * D_V), rep['mla_out_g'])


def _before_scan(full, rep, x):
    b, t, d = x.shape
    m = b * t
    n1 = _make_rms('rms_mix')(x.reshape(m, d), rep['ln_mix_g'])
    d_in = full['w_in'].shape[1]
    d_in_pad = -(-d_in // MM_TILE) * MM_TILE
    z = _make_mm('mm_in')(n1, jnp.pad(full['w_in'], ((0, 0), (0, d_in_pad - d_in))))
    return (*_rwkv_operands(z[:, :RWKV_COLS].reshape(b, t, RWKV_COLS), full, rep),
            _mla_mixer(z[:, RWKV_COLS:d_in], full, rep, b, t))


def _after_scan(full, rep, x, target, y_f, y_b, r, k_f, k_b, v, gate, y_mla):
    b, t, d = x.shape
    m = b * t
    xf = x.reshape(m, d)
    y_rwkv = _make_rowwise(_rwkv_post_fn, 'rwkv_post', 7, _tile(m, ROW_TILE))(
        y_f.reshape(m, D_RWKV), y_b.reshape(m, D_RWKV), r, k_f, k_b, v, gate,
        rep['ln_x_g'], rep['ln_x_b'], rep['r_k'].reshape(1, D_RWKV))[0]
    w_out = full['w_out']
    h = _make_mm_add('mm_out_rwkv')(xf, y_rwkv, w_out[:D_RWKV])
    h = _make_mm_add('mm_out_mla')(h, y_mla, w_out[D_RWKV:])
    n2, h = _make_rms_skip('rms_ffn')(h, rep['ln_ffn_g'])
    w_up, cw, cb = full['w_ffn_up'], full['ffn_conv_w'], rep['ffn_conv_b']
    u_gate = _make_mm('mm_up_gate')(n2, w_up[:, :D_FF]).reshape(b, t, D_FF)
    u_val = _make_mm('mm_up_val')(n2, w_up[:, D_FF:]).reshape(b, t, D_FF)
    act = _conv_glu(u_gate, u_val, cw[:, :D_FF], cw[:, D_FF:], cb[:, :D_FF], cb[:, D_FF:]).reshape(m, D_FF)
    h = _make_mm_add('mm_down')(h, act, full['w_ffn_down'])
    return _final_loss(h, rep['ln_final_g'], target.reshape(m, d))


def _mat(a):
    if a.ndim == 1:
        return a.reshape(1, -1)
    if a.ndim == 3:
        return a.reshape(a.shape[1:])
    return a


def _join(shards, name):
    if name in ROW:
        return shards.reshape(-1, shards.shape[-1])
    return shards.transpose(1, 0, 2).reshape(shards.shape[1], -1)


def _cut(whole, name):
    r, c = whole.shape
    if name in ROW:
        return whole.reshape(N_DEV, r // N_DEV, c)
    return whole.reshape(r, N_DEV, c // N_DEV).transpose(1, 0, 2)


def kernel(x, ln_mix_g, w_in, shift_mu_prev, shift_mu_next, decay_w0_fwd, decay_w2_fwd, decay_w0_bwd, decay_w2_bwd, iclr_a0_fwd, iclr_a2_fwd, iclr_a0_bwd, iclr_a2_bwd, gate_g2, k_k, k_a, r_k, ln_x_g, ln_x_b, q_norm_g, w_uq, kv_norm_g, w_ukv, mla_out_g, w_out, ln_ffn_g, w_ffn_up, ffn_conv_w, ffn_conv_b, w_ffn_down, ln_final_g, loss_target, m_ln_mix_g, m_w_in, m_shift_mu_prev, m_shift_mu_next, m_decay_w0_fwd, m_decay_w2_fwd, m_decay_w0_bwd, m_decay_w2_bwd, m_iclr_a0_fwd, m_iclr_a2_fwd, m_iclr_a0_bwd, m_iclr_a2_bwd, m_gate_g2, m_k_k, m_k_a, m_r_k, m_ln_x_g, m_ln_x_b, m_q_norm_g, m_w_uq, m_kv_norm_g, m_w_ukv, m_mla_out_g, m_w_out, m_ln_ffn_g, m_w_ffn_up, m_ffn_conv_w, m_ffn_conv_b, m_w_ffn_down, m_ln_final_g, v_ln_mix_g, v_w_in, v_shift_mu_prev, v_shift_mu_next, v_decay_w0_fwd, v_decay_w2_fwd, v_decay_w0_bwd, v_decay_w2_bwd, v_iclr_a0_fwd, v_iclr_a2_fwd, v_iclr_a0_bwd, v_iclr_a2_bwd, v_gate_g2, v_k_k, v_k_a, v_r_k, v_ln_x_g, v_ln_x_b, v_q_norm_g, v_w_uq, v_kv_norm_g, v_w_ukv, v_mla_out_g, v_w_out, v_ln_ffn_g, v_w_ffn_up, v_ffn_conv_w, v_ffn_conv_b, v_w_ffn_down, v_ln_final_g):
    given = dict(locals())
    w = {n: given[n] for n in WNAMES}
    mom = {n: given['m_' + n] for n in WNAMES}
    var = {n: given['v_' + n] for n in WNAMES}

    def split(names):
        return [n for n in names if n in BIG], [n for n in names if n not in BIG]

    def wire(names):
        big, small = split(names)
        pack = _pack([lax.bitcast_convert_type(_mat(w[n]), BF16) if n in EXACT else _mat(w[n]).astype(BF16) for n in small])
        return [_mat(w[n]).astype(BF16) for n in big] + [pack]

    def whole(names, gathered):
        big, small = split(names)
        out = {n: _join(g, n) for n, g in zip(big, gathered)}
        shapes = [_mat(w[n]).shape + ((2,) if n in EXACT else ()) for n in small]
        for n, s in zip(small, _unpack(gathered[-1], shapes, lead=1)):
            out[n] = _join(lax.bitcast_convert_type(s, F32) if n in EXACT else s.astype(F32), n)
        return out

    def grad_wire(names, grads):
        big, small = split(names)
        return [_cut(grads[n], n) for n in big] + [_pack([_cut(grads[n], n).astype(BF16) for n in small], lead=1)]

    early = [n for n in SHARDED if n not in LATE]
    rep = {n: _mat(w[n]) for n in REPLICATED}
    rep['r_k'] = w['r_k'].reshape(H, N)
    b, t, d = x.shape
    seq = lambda a: a.reshape(b, t, D_RWKV)
    flat = lambda a: a.reshape(b * t, D_RWKV)

    full_early = whole(early, _all_gather(wire(early), 'gather_weights'))
    ops, vjp_before = jax.vjp(_before_scan, full_early, rep, x)
    r, v, kk, w_f, k_f, q_f, w_b, k_b, q_b, gate, y_mla = ops
    scan_in = [seq(a) for a in (r, v, kk, w_f, k_f, q_f, w_b, k_b, q_b)]
    y_f, y_b, s_f, s_b, *late_gathered = _scan_fwd_call(*scan_in, wire(LATE))
    full_late = whole(LATE, late_gathered)
    loss_local, vjp_after = jax.vjp(_after_scan, full_late, rep, x, loss_target, y_f, y_b, r, k_f, k_b, v, gate, y_mla)

    g_late, g_rep_after, g_x_after, _, d_yf, d_yb, d_r, d_kf, d_kb, d_v, d_gate, d_ymla = vjp_after(jnp.ones((), F32))
    scan_out = _scan_bwd_call(*scan_in, s_f, s_b, d_yf, d_yb, grad_wire(LATE, g_late))
    parts_late = scan_out[12:]
    drf, dvf, dkkf, dwf, dkf, dqf, drb, dvb, dkkb, dwb, dkb, dqb = [flat(a) for a in scan_out[:12]]
    g_early, g_rep_before, g_x_before = vjp_before(
        (drf + drb + d_r, dvf + dvb + d_v, dkkf + dkkb, dwf, dkf + d_kf, dqf, dwb, dkb + d_kb, dqb, d_gate, d_ymla))
    g_rep = {n: g_rep_before[n] + g_rep_after[n] for n in rep}
    g_x = g_x_before + g_x_after
    parts_early = _grad_exchange(grad_wire(early, g_early), 'exchange_grads')

    s_out = [{}, {}, {}, {}]
    for names, parts, tag in ((early, parts_early, 'early'), (LATE, parts_late, 'late')):
        big, small = split(names)
        for n, p in zip(big, parts):
            res = _sum_adamw(p, _mat(w[n]), _mat(mom[n]), _mat(var[n]), 'adamw_' + n)
            for kind, o in enumerate(res):
                s_out[kind][n] = o.reshape(w[n].shape)
        res = _sum_adamw(parts[-1], _pack([w[n] for n in small]), _pack([mom[n] for n in small]),
                         _pack([var[n] for n in small]), 'adamw_small_' + tag)
        for kind, o in enumerate(res):
            s_out[kind].update(zip(small, _unpack(o, [w[n].shape for n in small])))

    zero = jnp.zeros((1,), F32)
    r_pack = _pack([g_rep[n] for n in REPLICATED] + [loss_local.reshape(1)])
    r_parts = _all_gather([r_pack], 'gather_small')[0]
    r_out = _sum_adamw(r_parts,_pack([w[n] for n in REPLICATED] + [zero]), _pack([mom[n] for n in REPLICATED] + [zero]),
                       _pack([var[n] for n in REPLICATED] + [zero]), 'adamw_replicated')
    r_out = [_unpack(o, [w[n].shape for n in REPLICATED] + [(1,)]) for o in r_out]

    loss = r_out[0][-1].reshape(())
    outs = [loss, g_x]
    for kind in range(4):
        by_name = dict(s_out[kind])
        by_name.update(zip(REPLICATED, r_out[kind][:-1]))
        outs += [by_name[n] for n in WNAMES]
    return tuple(outs)
```

```python
import functools

import jax
import jax.numpy as jnp
from jax import lax
from jax.experimental import pallas as pl
from jax.experimental.pallas import tpu as pltpu

F32 = jnp.float32
BF16 = jnp.bfloat16
MESH = pl.DeviceIdType.MESH

N_DEV = 8
LANES = 128
SUBLANES = 8
PACK_TILE = 2 * SUBLANES * LANES
PACK_ROWS = 512
ADAM_ROWS = 256
MM_TILE = 512
MM_TILE_WIDE = 1408
MM_K_WHOLE = 2816
VMEM_LIMIT = 56 * 1024 * 1024

H = 8
N = 64
D_RWKV = H * N
D_NOPE, D_ROPE, D_V = 64, 32, 64
D_QK = D_NOPE + D_ROPE
MLA_SCALE = D_QK ** -0.5
ROPE_THETA = 10000.0
RWKV_SPLITS = (D_RWKV, D_RWKV, D_RWKV, 64, 64, 64, 64, 128)
RWKV_COLS = sum(RWKV_SPLITS)
Q_LORA, KV_LORA = 768, 256
D_FF = 2816
NORM_EPS = 1e-6
GN_EPS = 64e-5
L2_EPS = 1e-12
ADAM_LR, ADAM_B1, ADAM_B2, ADAM_EPS, ADAM_WD, ADAM_STEP = 0.001, 0.9, 0.999, 1e-08, 0.01, 10

SCAN_CHUNK = 32
ATT_TQ = 256
SEG = 256
FFN_COLS = 256
ROW_TILE = 256
SHIFT_COLS = 384

WNAMES = ['ln_mix_g', 'w_in', 'shift_mu_prev', 'shift_mu_next', 'decay_w0_fwd', 'decay_w2_fwd', 'decay_w0_bwd',
          'decay_w2_bwd', 'iclr_a0_fwd', 'iclr_a2_fwd', 'iclr_a0_bwd', 'iclr_a2_bwd', 'gate_g2', 'k_k', 'k_a', 'r_k',
          'ln_x_g', 'ln_x_b', 'q_norm_g', 'w_uq', 'kv_norm_g', 'w_ukv', 'mla_out_g', 'w_out', 'ln_ffn_g', 'w_ffn_up',
          'ffn_conv_w', 'ffn_conv_b', 'w_ffn_down', 'ln_final_g']
COL = ('w_in', 'decay_w2_fwd', 'decay_w2_bwd', 'iclr_a2_fwd', 'iclr_a2_bwd', 'gate_g2', 'w_ukv', 'w_ffn_up', 'ffn_conv_w')
ROW = ('w_uq', 'w_out', 'w_ffn_down')
SHARDED = [n for n in WNAMES if n in COL or n in ROW]
REPLICATED = [n for n in WNAMES if n not in SHARDED]
EXACT = ('ffn_conv_w',)
LATE = ['w_out', 'w_ffn_up', 'ffn_conv_w', 'w_ffn_down']
BIG = ('w_in', 'w_uq', 'w_ukv', 'w_out', 'w_ffn_up', 'w_ffn_down')


def _params(*sem):
    return pltpu.CompilerParams(dimension_semantics=sem, vmem_limit_bytes=VMEM_LIMIT)


def _pack(arrs, lead=0):
    parts = []
    for a in arrs:
        head = a.shape[:lead]
        flat = a.reshape(head + (-1,))
        n = flat.shape[-1]
        n_pad = -(-n // PACK_TILE) * PACK_TILE
        flat = jnp.pad(flat, [(0, 0)] * lead + [(0, n_pad - n)])
        parts.append(flat.reshape(head + (n_pad // LANES, LANES)))
    out = jnp.concatenate(parts, axis=lead)
    rows = out.shape[lead]
    rows_pad = -(-rows // PACK_ROWS) * PACK_ROWS
    return jnp.pad(out, [(0, 0)] * lead + [(0, rows_pad - rows), (0, 0)])


def _unpack(packed, shapes, lead=0):
    outs, row = [], 0
    head = packed.shape[:lead]
    for shp in shapes:
        n = 1
        for s in shp:
            n *= s
        rows = -(-n // PACK_TILE) * (PACK_TILE // LANES)
        blk = lax.slice_in_dim(packed, row, row + rows, axis=lead)
        flat = blk.reshape(head + (rows * LANES,))
        outs.append(lax.slice_in_dim(flat, 0, n, axis=lead).reshape(head + tuple(shp)))
        row += rows
    return outs


PEERS = N_DEV - 1
HBM = pl.BlockSpec(memory_space=pl.ANY)


def _comm_sems(n):
    return [pltpu.SemaphoreType.DMA((PEERS * n,)), pltpu.SemaphoreType.DMA((PEERS * n,)), pltpu.SemaphoreType.DMA((n,))]


def _all_gather(xs, name):
    n = len(xs)

    def body(*refs):
        x_refs, out_refs, (send_sems, recv_sems, local_sems) = refs[:n], refs[n:2 * n], refs[2 * n:]
        mx, my, mc = lax.axis_index("x"), lax.axis_index("y"), lax.axis_index("c")
        me, sibling = (mx, my, mc), (mx, my, 1 - mc)
        chips = [(1 - mx, my), (mx, 1 - my), (1 - mx, 1 - my)]

        def slot(a, px, py, pc):
            return out_refs[a].at[4 * px + 2 * py + pc]

        def copy(a, k, block, to, src=None):
            return pltpu.make_async_remote_copy(
                src_ref=slot(a, *block) if src is None else src, dst_ref=slot(a, *block),
                send_sem=send_sems.at[PEERS * a + k], recv_sem=recv_sems.at[PEERS * a + k],
                device_id=to, device_id_type=MESH)

        mine = [pltpu.make_async_copy(x_refs[a], slot(a, *me), local_sems.at[a]) for a in range(n)]
        first, passed = [], []
        for a in range(n):
            mine[a].start()
            first.append(copy(a, 0, me, sibling, src=x_refs[a]))
            first += [copy(a, 1 + j, me, (*chip, mc), src=x_refs[a]) for j, chip in enumerate(chips)]
        for cp in first:
            cp.start()
        for j, chip in enumerate(chips):
            for a in range(n):
                copy(a, 1 + j, (*chip, mc), me).wait_recv()
                passed.append(copy(a, 4 + j, (*chip, mc), sibling))
                passed[-1].start()
        for a in range(n):
            copy(a, 0, sibling, me).wait_recv()
            for j, chip in enumerate(chips):
                copy(a, 4 + j, (*chip, 1 - mc), me).wait_recv()
        for cp in first + passed:
            cp.wait_send()
        for cp in mine:
            cp.wait()

    return pl.pallas_call(
        body, name=name, out_shape=[jax.ShapeDtypeStruct((N_DEV,) + x.shape, x.dtype) for x in xs],
        in_specs=[HBM] * n, out_specs=[HBM] * n, scratch_shapes=_comm_sems(n),
    )(*xs)


def _direct_exchange(src_refs, out_refs, send_sems, recv_sems, local_sems, per_peer):
    mx, my, mc = lax.axis_index("x"), lax.axis_index("y"), lax.axis_index("c")
    me = 4 * mx + 2 * my + mc

    def flip(v, bit):
        return 1 - v if bit else v

    def copies():
        mine, remote = [], []
        for a, (src, out) in enumerate(zip(src_refs, out_refs)):
            mine.append(pltpu.make_async_copy(src.at[me] if per_peer else src, out.at[me], local_sems.at[a]))
            for k in range(1, N_DEV):
                px, py, pc = flip(mx, k & 4), flip(my, k & 2), flip(mc, k & 1)
                remote.append(pltpu.make_async_remote_copy(
                    src_ref=src.at[4 * px + 2 * py + pc] if per_peer else src, dst_ref=out.at[me],
                    send_sem=send_sems.at[PEERS * a + k - 1], recv_sem=recv_sems.at[PEERS * a + k - 1],
                    device_id=(px, py, pc), device_id_type=MESH))
        return mine, remote

    def start():
        mine, remote = copies()
        for cp in mine + remote:
            cp.start()

    def wait():
        mine, remote = copies()
        for cp in remote:
            cp.wait_recv()
        for cp in remote:
            cp.wait_send()
        for cp in mine:
            cp.wait()

    return start, wait


def _grad_exchange(gs, name):
    n = len(gs)

    def body(*refs):
        start, wait = _direct_exchange(refs[:n], refs[n:2 * n], *refs[2 * n:], per_peer=True)
        start()
        wait()

    return pl.pallas_call(
        body, name=name, out_shape=[jax.ShapeDtypeStruct(g.shape, g.dtype) for g in gs],
        in_specs=[HBM] * n, out_specs=[HBM] * n, scratch_shapes=_comm_sems(n),
    )(*gs)


def _sum_adamw(parts, w, m, v, name):
    rows, cols = w.shape
    tr = next((t for t in range(ADAM_ROWS, 15, -16) if rows % t == 0), rows)
    c1 = 1.0 - ADAM_B1 ** ADAM_STEP
    c2 = 1.0 - ADAM_B2 ** ADAM_STEP

    def body(p_ref, w_ref, m_ref, v_ref, g_out, d_out, m_out, v_out):
        g = p_ref[0].astype(F32)
        for q in range(1, N_DEV):
            g = g + p_ref[q].astype(F32)
        m_new = ADAM_B1 * m_ref[...] + (1.0 - ADAM_B1) * g
        v_new = ADAM_B2 * v_ref[...] + (1.0 - ADAM_B2) * (g * g)
        m_hat = m_new / c1
        v_hat = v_new / c2
        g_out[...] = g
        d_out[...] = -ADAM_LR * (m_hat / (jnp.sqrt(v_hat) + ADAM_EPS) + ADAM_WD * w_ref[...])
        m_out[...] = m_new
        v_out[...] = v_new

    blk = pl.BlockSpec((tr, cols), lambda i: (i, 0))
    out = jax.ShapeDtypeStruct((rows, cols), F32)
    return pl.pallas_call(
        body, name=name, grid=(rows // tr,),
        in_specs=[pl.BlockSpec((N_DEV, tr, cols), lambda i: (0, i, 0)), blk, blk, blk],
        out_specs=[blk, blk, blk, blk], out_shape=[out, out, out, out],
        compiler_params=_params("parallel"),
    )(parts, w, m, v)


def _tile(dim, cap=MM_TILE):
    if dim <= cap:
        return dim
    for t in range(cap, LANES - 1, -LANES):
        if dim % t == 0:
            return t
    return dim


def _mm_call(a, b, form, name, out_dtype=F32, base=None):
    if form == 'nn':
        (m, k), n = a.shape, b.shape[1]
    elif form == 'nt':
        (m, k), n = a.shape, b.shape[0]
    else:
        (k, m), n = a.shape, b.shape[1]
    tk = k if (form == 'nn' and k <= MM_K_WHOLE) else _tile(k, MM_TILE_WIDE)
    tm = _tile(m, MM_TILE_WIDE if form == 'tn' else MM_TILE)
    tn = _tile(n, MM_TILE_WIDE)
    nk = k // tk
    contract = {'nn': ((1,), (0,)), 'nt': ((1,), (1,)), 'tn': ((0,), (0,))}[form]

    acc_in_out = nk == 1 or out_dtype == F32
    assert base is None or out_dtype == F32
    extra = [] if base is None else [base]

    def body(a_ref, b_ref, *rest):
        o_ref, acc = rest[len(extra)], rest[len(extra) + 1:]
        part = lax.dot_general(a_ref[...].astype(BF16), b_ref[...].astype(BF16), (contract, ((), ())),
                               preferred_element_type=F32)
        if nk == 1:
            o_ref[...] = (part + rest[0][...] if extra else part).astype(out_dtype)
            return
        acc_ref = o_ref if acc_in_out else acc[0]

        @pl.when(pl.program_id(2) == 0)
        def _():
            acc_ref[...] = part + rest[0][...] if extra else part

        @pl.when(pl.program_id(2) > 0)
        def _():
            acc_ref[...] += part

        if not acc_in_out:
            @pl.when(pl.program_id(2) == nk - 1)
            def _():
                o_ref[...] = acc_ref[...].astype(out_dtype)

    a_spec = pl.BlockSpec((tk, tm), lambda j, i, l: (l, i)) if form == 'tn' else pl.BlockSpec((tm, tk), lambda j, i, l: (i, l))
    b_spec = pl.BlockSpec((tn, tk), lambda j, i, l: (j, l)) if form == 'nt' else pl.BlockSpec((tk, tn), lambda j, i, l: (l, j))
    o_spec = pl.BlockSpec((tm, tn), lambda j, i, l: (i, j))
    return pl.pallas_call(
        body, name=name, grid=(n // tn, m // tm, nk),
        in_specs=[a_spec, b_spec] + [o_spec] * len(extra), out_specs=o_spec,
        out_shape=jax.ShapeDtypeStruct((m, n), out_dtype),
        scratch_shapes=[] if acc_in_out else [pltpu.VMEM((tm, tn), F32)],
        compiler_params=_params("parallel", "parallel", "arbitrary"),
    )(a, b, *extra)


def _make_mm(name):
    @jax.custom_vjp
    def mm(a, b):
        return _mm_call(a, b, 'nn', name + '_fwd')

    def fwd(a, b):
        return _mm_call(a, b, 'nn', name + '_fwd'), (a, b)

    def bwd(res, g):
        a, b = res
        return _mm_call(g, b, 'nt', name + '_da'), _mm_call(a, g, 'tn', name + '_db', out_dtype=BF16)

    mm.defvjp(fwd, bwd)
    return mm


def _make_mm_pair(name):
    def both(a, b1, b2):
        return _mm_call(a, b1, 'nn', name + '_1_fwd'), _mm_call(a, b2, 'nn', name + '_2_fwd')

    mm = jax.custom_vjp(both)

    def fwd(a, b1, b2):
        return both(a, b1, b2), (a, b1, b2)

    def bwd(res, gs):
        a, b1, b2 = res
        da = _mm_call(gs[1], b2, 'nt', name + '_2_da', base=_mm_call(gs[0], b1, 'nt', name + '_1_da'))
        return (da, _mm_call(a, gs[0], 'tn', name + '_1_db', out_dtype=BF16),
                _mm_call(a, gs[1], 'tn', name + '_2_db', out_dtype=BF16))

    mm.defvjp(fwd, bwd)
    return mm


def _make_mm_add(name):
    @jax.custom_vjp
    def mm(base, a, b):
        return _mm_call(a, b, 'nn', name + '_fwd', base=base)

    def fwd(base, a, b):
        return _mm_call(a, b, 'nn', name + '_fwd', base=base), (a, b)

    def bwd(res, g):
        a, b = res
        return g, _mm_call(g, b, 'nt', name + '_da'), _mm_call(a, g, 'tn', name + '_db', out_dtype=BF16)

    mm.defvjp(fwd, bwd)
    return mm


def _rms_fwd_call(x, g, name):
    m, d = x.shape
    tm = _tile(m)

    def body(x_ref, g_ref, o_ref):
        xv = x_ref[...]
        rinv = lax.rsqrt(jnp.mean(xv * xv, axis=-1, keepdims=True) + NORM_EPS)
        o_ref[...] = xv * rinv * g_ref[...]

    return pl.pallas_call(
        body, name=name, grid=(m // tm,),
        in_specs=[pl.BlockSpec((tm, d), lambda i: (i, 0)), pl.BlockSpec((1, d), lambda i: (0, 0))],
        out_specs=pl.BlockSpec((tm, d), lambda i: (i, 0)), out_shape=jax.ShapeDtypeStruct((m, d), F32),
        compiler_params=_params("parallel"),
    )(x, g)


def _rms_bwd_call(x, g, dy, name, d_skip=None):
    m, d = x.shape
    tm = _tile(m)
    extra = [] if d_skip is None else [d_skip]

    def body(x_ref, g_ref, dy_ref, *rest):
        dx_ref, dg_ref = rest[len(extra):]

        @pl.when(pl.program_id(0) == 0)
        def _():
            dg_ref[...] = jnp.zeros_like(dg_ref)

        xv, dyv = x_ref[...], dy_ref[...]
        rinv = lax.rsqrt(jnp.mean(xv * xv, axis=-1, keepdims=True) + NORM_EPS)
        xh = xv * rinv
        dg_ref[...] += jnp.sum(dyv * xh, axis=0, keepdims=True)
        dxh = dyv * g_ref[...]
        dx = rinv * (dxh - xh * jnp.mean(dxh * xh, axis=-1, keepdims=True))
        dx_ref[...] = dx + rest[0][...] if extra else dx

    row = pl.BlockSpec((tm, d), lambda i: (i, 0))
    vec = pl.BlockSpec((1, d), lambda i: (0, 0))
    return pl.pallas_call(
        body, name=name, grid=(m // tm,), in_specs=[row, vec, row] + [row] * len(extra), out_specs=[row, vec],
        out_shape=[jax.ShapeDtypeStruct((m, d), F32), jax.ShapeDtypeStruct((1, d), F32)],
        compiler_params=_params("arbitrary"),
    )(x, g, dy, *extra)


def _make_rms(name):
    @jax.custom_vjp
    def rms(x, g):
        return _rms_fwd_call(x, g, name + '_fwd')

    def fwd(x, g):
        return _rms_fwd_call(x, g, name + '_fwd'), (x, g)

    def bwd(res, dy):
        x, g = res
        dx, dg = _rms_bwd_call(x, g, dy, name + '_bwd')
        return dx, dg

    rms.defvjp(fwd, bwd)
    return rms


def _final_loss_call(x, g, target, ct=None):
    m, d = x.shape
    tm = _tile(m)
    row = pl.BlockSpec((tm, d), lambda i: (i, 0))
    vec = pl.BlockSpec((1, d), lambda i: (0, 0))
    acc = pl.BlockSpec((1, LANES), lambda i: (0, 0))

    def normed(x_ref, g_ref):
        xv = x_ref[...]
        rinv = lax.rsqrt(jnp.mean(xv * xv, axis=-1, keepdims=True) + NORM_EPS)
        return rinv, xv * rinv

    def fwd_body(x_ref, g_ref, t_ref, loss_ref):
        @pl.when(pl.program_id(0) == 0)
        def _():
            loss_ref[...] = jnp.zeros_like(loss_ref)

        _, xh = normed(x_ref, g_ref)
        err = xh * g_ref[...] - t_ref[...]
        loss_ref[...] += 0.5 * jnp.sum(jnp.mean(err * err, axis=-1, keepdims=True), axis=0, keepdims=True)

    def bwd_body(x_ref, g_ref, t_ref, ct_ref, dx_ref, dg_ref):
        @pl.when(pl.program_id(0) == 0)
        def _():
            dg_ref[...] = jnp.zeros_like(dg_ref)

        rinv, xh = normed(x_ref, g_ref)
        dyv = (xh * g_ref[...] - t_ref[...]) * (ct_ref[0:1, 0:1] * (1.0 / d))
        dg_ref[...] += jnp.sum(dyv * xh, axis=0, keepdims=True)
        dxh = dyv * g_ref[...]
        dx_ref[...] = rinv * (dxh - xh * jnp.mean(dxh * xh, axis=-1, keepdims=True))

    if ct is None:
        return pl.pallas_call(
            fwd_body, name='final_loss_fwd', grid=(m // tm,), in_specs=[row, vec, row], out_specs=acc,
            out_shape=jax.ShapeDtypeStruct((1, LANES), F32), compiler_params=_params("arbitrary"),
        )(x, g, target)
    return pl.pallas_call(
        bwd_body, name='final_loss_bwd', grid=(m // tm,), in_specs=[row, vec, row, acc], out_specs=[row, vec],
        out_shape=[jax.ShapeDtypeStruct((m, d), F32), jax.ShapeDtypeStruct((1, d), F32)],
        compiler_params=_params("arbitrary"),
    )(x, g, target, ct)


@jax.custom_vjp
def _final_loss(x, g, target):
    return _final_loss_call(x, g, target)[0, 0]


def _final_loss_bwd(res, ct):
    x, g, target = res
    dx, dg = _final_loss_call(x, g, target, ct=jnp.full((1, LANES), ct, F32))
    return dx, dg, jnp.zeros_like(target)


_final_loss.defvjp(lambda x, g, target: (_final_loss_call(x, g, target)[0, 0], (x, g, target)), _final_loss_bwd)


def _make_rms_skip(name):
    @jax.custom_vjp
    def rms(x, g):
        return _rms_fwd_call(x, g, name + '_fwd'), x

    def fwd(x, g):
        return (_rms_fwd_call(x, g, name + '_fwd'), x), (x, g)

    def bwd(res, cts):
        x, g = res
        dx, dg = _rms_bwd_call(x, g, cts[0], name + '_bwd', d_skip=cts[1])
        return dx, dg

    rms.defvjp(fwd, bwd)
    return rms


def _time_shifts(x):
    t = x.shape[0]
    rows = lax.broadcasted_iota(jnp.int32, x.shape, 0)
    return (jnp.where(rows == 0, 0.0, pltpu.roll(x, 1, 0)), jnp.where(rows == t - 1, 0.0, pltpu.roll(x, t - 1, 0)))


def _conv3(x, cw_ref, cb_ref):
    xp, xn = _time_shifts(x)
    return cw_ref[0:1, :] * xp + cw_ref[1:2, :] * x + cw_ref[2:3, :] * xn + cb_ref[...]


def _glu_specs(b, t, f):
    tc = _tile(f, FFN_COLS)
    seq = pl.BlockSpec((1, t, tc), lambda j, bi: (bi, 0, j))
    cw = pl.BlockSpec((3, tc), lambda j, bi: (0, j))
    cb = pl.BlockSpec((1, tc), lambda j, bi: (0, j))
    return tc, seq, cw, cb


def _glu_fwd_call(ug, uv, cwg, cwv, cbg, cbv):
    b, t, f = ug.shape
    tc, seq, cw, cb = _glu_specs(b, t, f)

    def body(ug_ref, uv_ref, cwg_ref, cwv_ref, cbg_ref, cbv_ref, o_ref):
        g = _conv3(ug_ref[0], cwg_ref, cbg_ref)
        o_ref[0] = g * jax.nn.sigmoid(g) * _conv3(uv_ref[0], cwv_ref, cbv_ref)

    return pl.pallas_call(
        body, name='glu_fwd', grid=(f // tc, b), in_specs=[seq, seq, cw, cw, cb, cb], out_specs=seq,
        out_shape=jax.ShapeDtypeStruct((b, t, f), F32), compiler_params=_params("parallel", "parallel"),
    )(ug, uv, cwg, cwv, cbg, cbv)


def _glu_bwd_call(ug, uv, cwg, cwv, cbg, cbv, dact):
    b, t, f = ug.shape
    tc, seq, cw, cb = _glu_specs(b, t, f)

    def body(ug_ref, uv_ref, cwg_ref, cwv_ref, cbg_ref, cbv_ref, da_ref,
             dug_ref, duv_ref, dcwg_ref, dcwv_ref, dcbg_ref, dcbv_ref):
        @pl.when(pl.program_id(1) == 0)
        def _():
            for ref in (dcwg_ref, dcwv_ref, dcbg_ref, dcbv_ref):
                ref[...] = jnp.zeros_like(ref)

        g = _conv3(ug_ref[0], cwg_ref, cbg_ref)
        v = _conv3(uv_ref[0], cwv_ref, cbv_ref)
        sig = jax.nn.sigmoid(g)
        da = da_ref[0]
        dv = da * (g * sig)
        dg = da * v * (sig * (1.0 + g * (1.0 - sig)))

        def conv_bwd(dc, x_ref, cw_ref, dx_ref, dcw_ref, dcb_ref):
            dcp, dcn = _time_shifts(dc)
            dx_ref[0] = cw_ref[0:1, :] * dcn + cw_ref[1:2, :] * dc + cw_ref[2:3, :] * dcp
            x = x_ref[0]
            for n, ds in enumerate((dcn, dc, dcp)):
                dcw_ref[n:n + 1, :] += jnp.sum(ds * x, axis=0, keepdims=True)
            dcb_ref[...] += jnp.sum(dc, axis=0, keepdims=True)

        conv_bwd(dg, ug_ref, cwg_ref, dug_ref, dcwg_ref, dcbg_ref)
        conv_bwd(dv, uv_ref, cwv_ref, duv_ref, dcwv_ref, dcbv_ref)

    big = jax.ShapeDtypeStruct((b, t, f), F32)
    return pl.pallas_call(
        body, name='glu_bwd', grid=(f // tc, b), in_specs=[seq, seq, cw, cw, cb, cb, seq],
        out_specs=[seq, seq, cw, cw, cb, cb],
        out_shape=[big, big, jax.ShapeDtypeStruct((3, f), F32), jax.ShapeDtypeStruct((3, f), F32),
                   jax.ShapeDtypeStruct((1, f), F32), jax.ShapeDtypeStruct((1, f), F32)],
        compiler_params=_params("parallel", "arbitrary"),
    )(ug, uv, cwg, cwv, cbg, cbv, dact)


def _shift_call(z, mu_p, mu_n, dzs=None):
    b, t, c = z.shape
    tc = _tile(c, SHIFT_COLS)
    seq = pl.BlockSpec((1, t, tc), lambda j, bi: (bi, 0, j))
    row = pl.BlockSpec((1, tc), lambda j, bi: (0, j))

    def fwd_body(z_ref, mp_ref, mn_ref, o_ref):
        x = z_ref[0]
        xp, xn = _time_shifts(x)
        o_ref[0] = x + mp_ref[...] * (xp - x) + mn_ref[...] * (xn - x)

    def bwd_body(z_ref, mp_ref, mn_ref, d_ref, dz_ref, dmp_ref, dmn_ref):
        @pl.when(pl.program_id(1) == 0)
        def _():
            dmp_ref[...] = jnp.zeros_like(dmp_ref)
            dmn_ref[...] = jnp.zeros_like(dmn_ref)

        x, d = z_ref[0], d_ref[0]
        xp, xn = _time_shifts(x)
        dp, dn = _time_shifts(d)
        mp, mn = mp_ref[...], mn_ref[...]
        dz_ref[0] = d * (1.0 - mp - mn) + mp * dn + mn * dp
        dmp_ref[...] += jnp.sum(d * (xp - x), axis=0, keepdims=True)
        dmn_ref[...] += jnp.sum(d * (xn - x), axis=0, keepdims=True)

    if dzs is None:
        return pl.pallas_call(
            fwd_body, name='shift_fwd', grid=(c // tc, b), in_specs=[seq, row, row], out_specs=seq,
            out_shape=jax.ShapeDtypeStruct(z.shape, F32), compiler_params=_params("parallel", "parallel"),
        )(z, mu_p, mu_n)
    return pl.pallas_call(
        bwd_body, name='shift_bwd', grid=(c // tc, b), in_specs=[seq, row, row, seq], out_specs=[seq, row, row],
        out_shape=[jax.ShapeDtypeStruct(z.shape, F32), jax.ShapeDtypeStruct(mu_p.shape, F32),
                   jax.ShapeDtypeStruct(mu_n.shape, F32)],
        compiler_params=_params("parallel", "arbitrary"),
    )(z, mu_p, mu_n, dzs)


@jax.custom_vjp
def _token_shift(z, mu_p, mu_n):
    return _shift_call(z, mu_p, mu_n)


_token_shift.defvjp(lambda z, mu_p, mu_n: (_shift_call(z, mu_p, mu_n), (z, mu_p, mu_n)),
                    lambda res, d: tuple(_shift_call(*res, dzs=d)))


@jax.custom_vjp
def _conv_glu(ug, uv, cwg, cwv, cbg, cbv):
    return _glu_fwd_call(ug, uv, cwg, cwv, cbg, cbv)


def _conv_glu_fwd(*args):
    return _glu_fwd_call(*args), args


def _conv_glu_bwd(res, dact):
    return tuple(_glu_bwd_call(*res, dact))


_conv_glu.defvjp(_conv_glu_fwd, _conv_glu_bwd)


HEAD_LANES = 2 * D_NOPE
PAIR = 2


def _lane(shape):
    return lax.broadcasted_iota(jnp.int32, shape, len(shape) - 1)


def _rope(x, c, s1, s2):
    return x * c + pltpu.roll(x, HEAD_LANES - D_ROPE // 2, 1) * s1 + pltpu.roll(x, D_ROPE // 2, 1) * s2


def _rope_t(g, c, s1, s2):
    return g * c + pltpu.roll(g * s1, D_ROPE // 2, 1) + pltpu.roll(g * s2, HEAD_LANES - D_ROPE // 2, 1)


def _attn_setup(kv_ref, kr_ref, tabs, k2, v16):
    c, s1, s2 = (tb[...] for tb in tabs)
    krr = _rope(kr_ref[0], c, s1, s2)
    v16[...] = kv_ref[0].astype(BF16)
    for hh in range(PAIR):
        slab = kv_ref[0, :, HEAD_LANES * hh:HEAD_LANES * (hh + 1)]
        k2[hh] = jnp.where(_lane(slab.shape) < D_NOPE, slab, krr).astype(BF16)


def _attn_queries(q_ref, tabs, rows, hh):
    c, s1, s2 = (tb[rows, :] for tb in tabs)
    return (_rope(q_ref[0, :, HEAD_LANES * hh:HEAD_LANES * (hh + 1)], c, s1, s2) * MLA_SCALE).astype(BF16), (c, s1, s2)


def _attn_specs(b, t, tq):
    qspec = pl.BlockSpec((1, tq, PAIR * HEAD_LANES), lambda bi, p, i: (bi, i, p))
    kvspec = pl.BlockSpec((1, t, PAIR * HEAD_LANES), lambda bi, p, i: (bi, 0, p))
    krspec = pl.BlockSpec((1, t, HEAD_LANES), lambda bi, p, i: (bi, 0, 0))
    tab = pl.BlockSpec((t, HEAD_LANES), lambda bi, p, i: (0, 0))
    ospec = pl.BlockSpec((1, tq, PAIR * D_V), lambda bi, p, i: (bi, i, p))
    lspec = pl.BlockSpec((1, 1, tq, HEAD_LANES), lambda bi, p, i: (bi, p, i, 0))
    return qspec, kvspec, krspec, tab, ospec, lspec


def _attn_fwd_call(q, kv, kr, tabs):
    b, t, _ = q.shape
    tq = min(ATT_TQ, t)
    qspec, kvspec, krspec, tab, ospec, lspec = _attn_specs(b, t, tq)

    def body(q_ref, kv_ref, kr_ref, c_ref, s1_ref, s2_ref, o_ref, lse_ref, k2, v16):
        tabs = (c_ref, s1_ref, s2_ref)

        @pl.when(pl.program_id(2) == 0)
        def _():
            _attn_setup(kv_ref, kr_ref, tabs, k2, v16)

        rows = pl.ds(pl.multiple_of(pl.program_id(2) * tq, tq), tq)
        outs, lses = [], []
        for hh in range(PAIR):
            qh, _ = _attn_queries(q_ref, tabs, rows, hh)
            s = lax.dot_general(qh, k2[hh], (((1,), (1,)), ((), ())), preferred_element_type=F32)
            m = jnp.max(s, axis=-1, keepdims=True)
            p = jnp.exp(s - m)
            l = jnp.sum(p, axis=-1, keepdims=True)
            slab16 = v16[:, HEAD_LANES * hh:HEAD_LANES * (hh + 1)]
            outs.append(jnp.dot(p.astype(BF16), slab16, preferred_element_type=F32) / l)
            lses.append(m + jnp.log(l))
        low = _lane(outs[0].shape) < D_V
        o_ref[0] = jnp.where(low, pltpu.roll(outs[0], D_V, 1), outs[1])
        lse_ref[0, 0] = jnp.where(low, lses[0], lses[1])

    return pl.pallas_call(
        body, name='attn_fwd', grid=(b, H // PAIR, t // tq),
        in_specs=[qspec, kvspec, krspec, tab, tab, tab], out_specs=[ospec, lspec],
        out_shape=[jax.ShapeDtypeStruct((b, t, H * D_V), F32), jax.ShapeDtypeStruct((b, H // PAIR, t, HEAD_LANES), F32)],
        scratch_shapes=[pltpu.VMEM((PAIR, t, HEAD_LANES), BF16), pltpu.VMEM((t, PAIR * HEAD_LANES), BF16)],
        compiler_params=_params("parallel", "parallel", "arbitrary"),
    )(q, kv, kr, *tabs)


def _attn_bwd_call(q, kv, kr, tabs, o, lse, do):
    b, t, _ = q.shape
    tq = min(ATT_TQ, t)
    n_q = t // tq
    qspec, kvspec, krspec, tab, ospec, lspec = _attn_specs(b, t, tq)

    def body(q_ref, kv_ref, kr_ref, c_ref, s1_ref, s2_ref, o_ref, lse_ref, do_ref, dq_ref, dkv_ref, dkr_ref, k2, v16, dk2):
        tabs = (c_ref, s1_ref, s2_ref)
        pair, step = pl.program_id(1), pl.program_id(2)

        @pl.when(step == 0)
        def _():
            _attn_setup(kv_ref, kr_ref, tabs, k2, v16)
            dk2[...] = jnp.zeros_like(dk2)
            dkv_ref[...] = jnp.zeros_like(dkv_ref)

        @pl.when((step == 0) & (pair == 0))
        def _():
            dkr_ref[...] = jnp.zeros_like(dkr_ref)

        rows = pl.ds(pl.multiple_of(step * tq, tq), tq)
        dov, ov = do_ref[0], o_ref[0]
        lane = _lane(dov.shape)
        upper = lane >= D_V
        for hh in range(PAIR):
            qh, qtabs = _attn_queries(q_ref, tabs, rows, hh)
            s = lax.dot_general(qh, k2[hh], (((1,), (1,)), ((), ())), preferred_element_type=F32)
            p = jnp.exp(s - lse_ref[0, 0, :, D_V * hh:D_V * hh + 1])
            mine = upper if hh else ~upper
            delta = jnp.sum(jnp.where(mine, dov * ov, 0.0), axis=-1, keepdims=True)
            do_h = jnp.where(upper, dov if hh else pltpu.roll(dov, D_V, 1), 0.0).astype(BF16)
            slab16 = v16[:, HEAD_LANES * hh:HEAD_LANES * (hh + 1)]
            dp = lax.dot_general(do_h, slab16, (((1,), (1,)), ((), ())), preferred_element_type=F32)
            ds = (p * (dp - delta)).astype(BF16)
            dqh = jnp.dot(ds, k2[hh], preferred_element_type=F32) * MLA_SCALE
            dq_ref[0, :, HEAD_LANES * hh:HEAD_LANES * (hh + 1)] = _rope_t(dqh, *qtabs)
            dk2[hh] += lax.dot_general(ds, qh, (((0,), (0,)), ((), ())), preferred_element_type=F32)
            dkv_ref[0, :, HEAD_LANES * hh:HEAD_LANES * (hh + 1)] += lax.dot_general(
                p.astype(BF16), do_h, (((0,), (0,)), ((), ())), preferred_element_type=F32)

        @pl.when(step == n_q - 1)
        def _():
            c, s1, s2 = (tb[...] for tb in tabs)
            for hh in range(PAIR):
                g = dk2[hh]
                key_lane = _lane(g.shape)
                dkv_ref[0, :, HEAD_LANES * hh:HEAD_LANES * (hh + 1)] += jnp.where(key_lane < D_NOPE, g, 0.0)
                dkr_ref[0] += _rope_t(jnp.where(key_lane >= D_NOPE, g, 0.0), c, s1, s2)

    return pl.pallas_call(
        body, name='attn_bwd', grid=(b, H // PAIR, n_q),
        in_specs=[qspec, kvspec, krspec, tab, tab, tab, ospec, lspec, ospec], out_specs=[qspec, kvspec, krspec],
        out_shape=[jax.ShapeDtypeStruct(q.shape, F32), jax.ShapeDtypeStruct(kv.shape, F32), jax.ShapeDtypeStruct(kr.shape, F32)],
        scratch_shapes=[pltpu.VMEM((PAIR, t, HEAD_LANES), BF16), pltpu.VMEM((t, PAIR * HEAD_LANES), BF16),
                        pltpu.VMEM((PAIR, t, HEAD_LANES), F32)],
        compiler_params=_params("parallel", "arbitrary", "arbitrary"),
    )(q, kv, kr, *tabs, o, lse, do)


@jax.custom_vjp
def _attention(q, kv, kr, tabs):
    return _attn_fwd_call(q, kv, kr, tabs)[0]


def _attention_fwd(q, kv, kr, tabs):
    o, lse = _attn_fwd_call(q, kv, kr, tabs)
    return o, (q, kv, kr, tabs, o, lse)


def _attention_bwd(res, do):
    q, kv, kr, tabs, o, lse = res
    return (*_attn_bwd_call(q, kv, kr, tabs, o, lse, do), tuple(jnp.zeros_like(tb) for tb in tabs))


_attention.defvjp(_attention_fwd, _attention_bwd)


SROWS = N * D_RWKV // SEG


def _seg_ones():
    r = lax.broadcasted_iota(jnp.int32, (SEG, SEG), 0) // N
    c = lax.broadcasted_iota(jnp.int32, (SEG, SEG), 1) // N
    return (r == c).astype(BF16)


def _eye_mask():
    r = lax.broadcasted_iota(jnp.int32, (SROWS, SEG), 0) & (N - 1)
    c = lax.broadcasted_iota(jnp.int32, (SROWS, SEG), 1) & (N - 1)
    return r == c


def _row2(ref, bi, ti, dtype=F32):
    parts = [jnp.broadcast_to(ref[bi, pl.ds(ti, 1), pl.ds(SEG * q, SEG)].astype(dtype), (N, SEG))
             for q in range(D_RWKV // SEG)]
    return jnp.concatenate(parts, axis=0)


def _split2(x):
    hi = x.astype(BF16)
    return hi, (x - hi.astype(F32)).astype(BF16)


def _col_sum(x):
    return jnp.concatenate([jnp.sum(x[N * q:N * (q + 1)], axis=0, keepdims=True) for q in range(D_RWKV // SEG)], axis=1)


def _scan_specs(b, t, rev):
    nc = t // SCAN_CHUNK
    if rev:
        return (pl.BlockSpec((b, SCAN_CHUNK, D_RWKV), lambda c: (0, nc - 1 - c, 0)),
                pl.BlockSpec((b, SCAN_CHUNK, SROWS, SEG), lambda c: (0, nc - 1 - c, 0, 0)))
    return (pl.BlockSpec((b, SCAN_CHUNK, D_RWKV), lambda c: (0, c, 0)),
            pl.BlockSpec((b, SCAN_CHUNK, SROWS, SEG), lambda c: (0, c, 0, 0)))


def _scan_fwd_call(r, v, kk, wf, kf, qf, wb, kb, qb, ride):
    b, t, _ = r.shape
    n_chunks = t // SCAN_CHUNK
    last = SCAN_CHUNK - 1
    nr = len(ride)

    def body(rf, vf, kkf, wf_, kf_, qf_, rb, vb, kkb, wb_, kb_, qb_, *rest):
        (yf, yb, sf, sb), scratch = rest[nr:nr + 4], rest[2 * nr + 4:]
        states = scratch[:2 * b]
        send, arrive = _direct_exchange(rest[:nr], rest[nr + 4:2 * nr + 4], *scratch[2 * b:], per_peer=False)

        @pl.when(pl.program_id(0) == 0)
        def _():
            send()
            for st in states:
                st[...] = jnp.zeros_like(st)

        ones, mask = _seg_ones(), _eye_mask()
        zero16 = jnp.zeros((), BF16)
        chains = []
        for bi in range(b):
            chains.append((rf, vf, kkf, wf_, kf_, qf_, yf, sf, states[2 * bi], bi, False))
            chains.append((rb, vb, kkb, wb_, kb_, qb_, yb, sb, states[2 * bi + 1], bi, True))

        def tix(i, rev):
            return last - i if rev else i

        def put_y(y_, bi, ti, ycol):
            y_[bi, pl.ds(ti, 1), :] = _col_sum(jnp.where(mask, ycol, 0.0))

        def steps(i, with_y):
            parts = []
            for (r_, v_, kk_, w_, k_, q_, y_, s_, st, bi, rev) in chains:
                ti = tix(i, rev)
                s = st[...]
                s_[bi, ti] = s
                parts.append((s * _row2(kk_, bi, ti)).astype(BF16))
                parts.append(jnp.where(mask, _row2(v_, bi, ti, BF16), zero16))
                if with_y:
                    parts.append((s * _row2(r_, bi, tix(i - 1, rev))).astype(BF16))
            res = jnp.dot(jnp.concatenate(parts, axis=0), ones, preferred_element_type=F32)
            off = 0
            for (r_, v_, kk_, w_, k_, q_, y_, s_, st, bi, rev) in chains:
                ti = tix(i, rev)
                u = res[off:off + SROWS]
                vcol = res[off + SROWS:off + 2 * SROWS]
                off += 2 * SROWS
                if with_y:
                    put_y(y_, bi, tix(i - 1, rev), res[off:off + SROWS])
                    off += SROWS
                st[...] = st[...] * _row2(w_, bi, ti) - u * _row2(q_, bi, ti) + vcol * _row2(k_, bi, ti)

        steps(0, False)

        def loop(i, carry):
            steps(i, True)
            return carry

        lax.fori_loop(1, SCAN_CHUNK, loop, 0, unroll=5)
        parts = [(c[8][...] * _row2(c[0], c[9], tix(last, c[10]))).astype(BF16) for c in chains]
        res = jnp.dot(jnp.concatenate(parts, axis=0), ones, preferred_element_type=F32)
        for n, c in enumerate(chains):
            put_y(c[6], c[9], tix(last, c[10]), res[n * SROWS:(n + 1) * SROWS])

        @pl.when(pl.program_id(0) == n_chunks - 1)
        def _():
            arrive()

    fr, fs = _scan_specs(b, t, False)
    br, bs = _scan_specs(b, t, True)
    y_shape = jax.ShapeDtypeStruct((b, t, D_RWKV), F32)
    s_shape = jax.ShapeDtypeStruct((b, t, SROWS, SEG), F32)
    return pl.pallas_call(
        body, name='scan_fwd', grid=(n_chunks,),
        in_specs=[fr] * 6 + [br] * 6 + [HBM] * nr, out_specs=[fr, br, fs, bs] + [HBM] * nr,
        out_shape=[y_shape, y_shape, s_shape, s_shape] + [jax.ShapeDtypeStruct((N_DEV,) + a.shape, a.dtype) for a in ride],
        scratch_shapes=[pltpu.VMEM((SROWS, SEG), F32)] * (2 * b) + _comm_sems(nr),
        compiler_params=_params("arbitrary"),
    )(r, v, kk, wf, kf, qf, r, v, kk, wb, kb, qb, *ride)


def _scan_bwd_call(r, v, kk, wf, kf, qf, wb, kb, qb, sf, sb, dyf, dyb, ride):
    b, t, _ = r.shape
    n_chunks = t // SCAN_CHUNK
    last = SCAN_CHUNK - 1
    nr = len(ride)

    def body(rf, vf, kkf, wf_, kf_, qf_, sf_, dyf_, rb, vb, kkb, wb_, kb_, qb_, sb_, dyb_, *rest):
        drf, dvf, dkkf, dwf, dkf, dqf, drb, dvb, dkkb, dwb, dkb, dqb = rest[nr:nr + 12]
        scratch = rest[2 * nr + 12:]
        send, arrive = _direct_exchange(rest[:nr], rest[nr + 12:2 * nr + 12], *scratch[8 * b:], per_peer=True)

        @pl.when(pl.program_id(0) == 0)
        def _():
            send()
            for n in range(2 * b):
                scratch[4 * n][...] = jnp.zeros_like(scratch[4 * n])

        ones, mask = _seg_ones(), _eye_mask()
        zero16 = jnp.zeros((), BF16)
        chains = []
        for bi in range(b):
            chains.append((rf, vf, kkf, wf_, kf_, qf_, sf_, dyf_, (drf, dvf, dkkf, dwf, dkf, dqf),
                           scratch[8 * bi:8 * bi + 4], bi, True))
            chains.append((rb, vb, kkb, wb_, kb_, qb_, sb_, dyb_, (drb, dvb, dkkb, dwb, dkb, dqb),
                           scratch[8 * bi + 4:8 * bi + 8], bi, False))

        def tix(i, rev):
            return last - i if rev else i

        def state_free_parts(v_, dy_, kk_, s_, bi, ti):
            return [jnp.where(mask, _row2(v_, bi, ti, BF16), zero16), jnp.where(mask, _row2(dy_, bi, ti, BF16), zero16),
                    (s_[bi, ti] * _row2(kk_, bi, ti)).astype(BF16)]

        def keep(scr, res, off):
            for n in range(3):
                scr[1 + n][...] = res[off + n * SROWS:off + (n + 1) * SROWS]
            return off + 3 * SROWS

        def first():
            parts = []
            for (r_, v_, kk_, w_, k_, q_, s_, dy_, outs, scr, bi, rev) in chains:
                parts += state_free_parts(v_, dy_, kk_, s_, bi, tix(0, rev))
            res = jnp.dot(jnp.concatenate(parts, axis=0), ones, preferred_element_type=F32)
            off = 0
            for c in chains:
                off = keep(c[9], res, off)

        def steps(i, has_next, recompute):
            parts = []
            for (r_, v_, kk_, w_, k_, q_, s_, dy_, outs, scr, bi, rev) in chains:
                ti = tix(i, rev)
                gst, vc, dc, uc = scr
                dycol = dc[...]
                if recompute:
                    sc = s_[bi, ti] * _row2(w_, bi, ti) - uc[...] * _row2(q_, bi, ti) + vc[...] * _row2(k_, bi, ti)
                else:
                    sc = s_[bi, tix(i - 1, rev)]
                outs[0][bi, pl.ds(ti, 1), :] = _col_sum(sc * dycol)
                g = gst[...] + dycol * _row2(r_, bi, ti)
                gst[...] = g
                parts.append((g * _row2(q_, bi, ti)).astype(BF16))
                parts.append((g * _row2(k_, bi, ti)).astype(BF16))
                if has_next:
                    parts += state_free_parts(v_, dy_, kk_, s_, bi, tix(i + 1, rev))
            res = jnp.dot(jnp.concatenate(parts, axis=0), ones, preferred_element_type=F32)
            off = 0
            for (r_, v_, kk_, w_, k_, q_, s_, dy_, outs, scr, bi, rev) in chains:
                ti = tix(i, rev)
                gst, vc, dc, uc = scr
                dr_, dv_, dkk_, dw_, dk_, dq_ = outs

                def put(ref, val, sign=1.0):
                    ref[bi, pl.ds(ti, 1), :] = sign * _col_sum(val)

                gq = res[off:off + SROWS]
                put(dv_, jnp.where(mask, res[off + SROWS:off + 2 * SROWS], 0.0))
                off += 2 * SROWS
                g, sp = gst[...], s_[bi, ti]
                put(dk_, g * vc[...])
                put(dw_, g * sp)
                put(dq_, g * uc[...], -1.0)
                put(dkk_, sp * gq, -1.0)
                gst[...] = g * _row2(w_, bi, ti) - gq * _row2(kk_, bi, ti)
                if has_next:
                    off = keep(scr, res, off)

        first()
        steps(0, True, True)

        def loop(i, carry):
            steps(i, True, False)
            return carry

        lax.fori_loop(1, last, loop, 0)
        steps(last, False, False)

        @pl.when(pl.program_id(0) == n_chunks - 1)
        def _():
            arrive()

    fr, fs = _scan_specs(b, t, True)
    br, bs = _scan_specs(b, t, False)
    y_shape = jax.ShapeDtypeStruct((b, t, D_RWKV), F32)
    return pl.pallas_call(
        body, name='scan_bwd', grid=(n_chunks,),
        in_specs=[fr] * 6 + [fs, fr] + [br] * 6 + [bs, br] + [HBM] * nr,
        out_specs=[fr] * 6 + [br] * 6 + [HBM] * nr,
        out_shape=[y_shape] * 12 + [jax.ShapeDtypeStruct(a.shape, a.dtype) for a in ride],
        scratch_shapes=[pltpu.VMEM((SROWS, SEG), F32)] * (8 * b) + _comm_sems(nr),
        compiler_params=_params("arbitrary"),
    )(r, v, kk, wf, kf, qf, sf, dyf, r, v, kk, wb, kb, qb, sb, dyb, *ride)


def _rope_tables(t):
    half = D_ROPE // 2
    inv_freq = jnp.power(ROPE_THETA, -jnp.arange(0, D_ROPE, 2, dtype=F32) / D_ROPE)
    ang = jnp.arange(t, dtype=F32)[:, None] * inv_freq[None, :]
    cos, sin, zero = jnp.cos(ang), jnp.sin(ang), jnp.zeros((t, half), F32)
    tail = HEAD_LANES - D_QK
    c = jnp.concatenate([jnp.ones((t, D_NOPE), F32), cos, cos, jnp.ones((t, tail), F32)], axis=1)
    s1 = jnp.concatenate([jnp.zeros((t, D_NOPE), F32), -sin, zero, jnp.zeros((t, tail), F32)], axis=1)
    s2 = jnp.concatenate([jnp.zeros((t, D_NOPE), F32), zero, sin, jnp.zeros((t, tail), F32)], axis=1)
    return c, s1, s2


@jax.custom_vjp
def _dot16(a, w):
    return jnp.dot(a.astype(BF16), w.astype(BF16), preferred_element_type=F32)


def _dot16_fwd(a, w):
    a16, w16 = a.astype(BF16), w.astype(BF16)
    return jnp.dot(a16, w16, preferred_element_type=F32), (a16, w16)


def _dot16_bwd(res, g):
    a16, w16 = res
    g16 = g.astype(BF16)
    return (lax.dot_general(g16, w16, (((1,), (1,)), ((), ())), preferred_element_type=F32),
            lax.dot_general(a16, g16, (((0,), (0,)), ((), ())), preferred_element_type=F32))


_dot16.defvjp(_dot16_fwd, _dot16_bwd)


def _head_sum_tile(x):
    outs = []
    ones = _seg_ones()
    for q in range(x.shape[1] // SEG):
        hi, lo = _split2(x[:, SEG * q:SEG * (q + 1)])
        outs.append(jnp.dot(hi, ones, preferred_element_type=F32) + jnp.dot(lo, ones, preferred_element_type=F32))
    return jnp.concatenate(outs, axis=1)


@jax.custom_vjp
def _hsum(x):
    return _head_sum_tile(x)


_hsum.defvjp(lambda x: (_head_sum_tile(x), None), lambda _, g: (_head_sum_tile(g),))


def _softplus(x):
    return jnp.maximum(x, 0.0) + jnp.log(1.0 + jnp.exp(-jnp.abs(x)))


def _rwkv_pre_fn(k, wdf, wdb, adf, adb, gd, w0f, w2f, w0b, w2b, a0f, a2f, a0b, a2b, g2, k_k, k_a):
    w_f = jnp.exp(-jnp.exp(-_softplus(-(w0f + _dot16(jnp.tanh(wdf), w2f))) - 0.5))
    w_b = jnp.exp(-jnp.exp(-_softplus(-(w0b + _dot16(jnp.tanh(wdb), w2b))) - 0.5))
    a_f = jax.nn.sigmoid(a0f + _dot16(adf, a2f))
    a_b = jax.nn.sigmoid(a0b + _dot16(adb, a2b))
    gate = _dot16(jax.nn.sigmoid(gd), g2)
    kk = k * k_k
    kk = kk / jnp.maximum(jnp.sqrt(_hsum(kk * kk)), L2_EPS)
    return (kk, w_f, k * (1.0 + (a_f - 1.0) * k_a), kk * a_f, w_b, k * (1.0 + (a_b - 1.0) * k_a), kk * a_b, gate)


def _rwkv_post_fn(y_f, y_b, r, k_f, k_b, v, gate, ln_g, ln_b, r_k):
    y = y_f + y_b
    yc = y - _hsum(y) * (1.0 / N)
    var = _hsum(yc * yc) * (1.0 / N)
    y = yc * lax.rsqrt(var + GN_EPS) * ln_g + ln_b
    return ((y + _hsum(r * (k_f + k_b) * r_k) * v) * gate,)


def _make_rowwise(fn, name, n_rows, tm):
    def specs(arrs, whole):
        if whole:
            return [pl.BlockSpec(a.shape, lambda i: (0, 0)) for a in arrs]
        return [pl.BlockSpec((tm, a.shape[1]), lambda i: (i, 0)) for a in arrs]

    def out_widths(rows, params):
        tiles = [jax.ShapeDtypeStruct((tm, a.shape[1]), F32) for a in rows]
        return [o.shape[1] for o in jax.eval_shape(fn, *tiles, *params)]

    def fwd_call(rows, params):
        m = rows[0].shape[0]
        n_in = len(rows) + len(params)
        outs = [jax.ShapeDtypeStruct((m, d), F32) for d in out_widths(rows, params)]

        def body(*refs):
            for o_ref, o in zip(refs[n_in:], fn(*[ref[...] for ref in refs[:n_in]])):
                o_ref[...] = o

        return pl.pallas_call(
            body, name=name + '_fwd', grid=(m // tm,), in_specs=specs(rows, False) + specs(params, True),
            out_specs=specs(outs, False), out_shape=outs, compiler_params=_params("parallel"),
        )(*rows, *params)

    def bwd_call(rows, params, cts):
        m = rows[0].shape[0]
        n_in = len(rows) + len(params)
        n_all = n_in + len(cts)
        outs = ([jax.ShapeDtypeStruct(a.shape, F32) for a in rows] + [jax.ShapeDtypeStruct(a.shape, F32) for a in params])

        def body(*refs):
            _, vjp = jax.vjp(fn, *[ref[...] for ref in refs[:n_in]])
            grads = vjp(tuple(ref[...] for ref in refs[n_in:n_all]))
            d_rows, d_params = refs[n_all:n_all + len(rows)], refs[n_all + len(rows):]
            for ref, g in zip(d_rows, grads[:len(rows)]):
                ref[...] = g

            @pl.when(pl.program_id(0) == 0)
            def _():
                for ref in d_params:
                    ref[...] = jnp.zeros_like(ref)

            for ref, g in zip(d_params, grads[len(rows):]):
                ref[...] += g

        return pl.pallas_call(
            body, name=name + '_bwd', grid=(m // tm,),
            in_specs=specs(rows, False) + specs(params, True) + specs(cts, False),
            out_specs=specs(rows, False) + specs(params, True), out_shape=outs, compiler_params=_params("arbitrary"),
        )(*rows, *params, *cts)

    @jax.custom_vjp
    def op(*args):
        return tuple(fwd_call(args[:n_rows], args[n_rows:]))

    def op_fwd(*args):
        return tuple(fwd_call(args[:n_rows], args[n_rows:])), args

    def op_bwd(args, cts):
        return tuple(bwd_call(args[:n_rows], args[n_rows:], cts))

    op.defvjp(op_fwd, op_bwd)
    return op


def _rwkv_operands(z, full, rep):
    b, t, _ = z.shape
    m = b * t
    z = _token_shift(z, rep['shift_mu_prev'], rep['shift_mu_next']).reshape(m, RWKV_COLS)
    cols, at = [], 0
    for width in RWKV_SPLITS:
        cols.append(z[:, at:at + width])
        at += width
    r, k, v, *lora_in = cols
    kk, w_f, k_f, q_f, w_b, k_b, q_b, gate = _make_rowwise(_rwkv_pre_fn, 'rwkv_pre', 6, _tile(m, ROW_TILE))(
        k, *lora_in, rep['decay_w0_fwd'], full['decay_w2_fwd'], rep['decay_w0_bwd'], full['decay_w2_bwd'],
        rep['iclr_a0_fwd'], full['iclr_a2_fwd'], rep['iclr_a0_bwd'], full['iclr_a2_bwd'], full['gate_g2'],
        rep['k_k'], rep['k_a'])
    return r, v, kk, w_f, k_f, q_f, w_b, k_b, q_b, gate


def _mla_mixer(z, full, rep, b, t):
    m = b * t
    c_q, c_kv, k_rope = z[:, :Q_LORA], z[:, Q_LORA:Q_LORA + KV_LORA], z[:, Q_LORA + KV_LORA:]
    w_uq = jnp.pad(full['w_uq'].reshape(Q_LORA, H, D_QK), ((0, 0), (0, 0), (0, HEAD_LANES - D_QK))).reshape(Q_LORA, H * HEAD_LANES)
    q = _make_mm('mm_uq')(_make_rms('rms_q')(c_q, rep['q_norm_g']), w_uq)
    kv = _make_mm('mm_ukv')(_make_rms('rms_kv')(c_kv, rep['kv_norm_g']), full['w_ukv'])
    kr = jnp.pad(k_rope, ((0, 0), (D_NOPE, HEAD_LANES - D_QK)))
    o = _attention(q.reshape(b, t, -1), kv.reshape(b, t, -1), kr.reshape(b, t, HEAD_LANES), _rope_tables(t))
    return _make_rms('rms_mla_out')(o.reshape(m, H * D_V), rep['mla_out_g'])


def _before_scan(full, rep, x):
    b, t, d = x.shape
    m = b * t
    n1 = _make_rms('rms_mix')(x.reshape(m, d), rep['ln_mix_g'])
    w_in = full['w_in']
    mla_cols = w_in.shape[1] - RWKV_COLS
    w_mla = jnp.pad(w_in[:, RWKV_COLS:], ((0, 0), (0, -mla_cols % LANES)))
    z_rwkv, z_mla = _make_mm_pair('mm_in')(n1, w_in[:, :RWKV_COLS], w_mla)
    return (*_rwkv_operands(z_rwkv.reshape(b, t, RWKV_COLS), full, rep),
            _mla_mixer(z_mla[:, :mla_cols], full, rep, b, t))


def _after_scan(full, rep, x, target, y_f, y_b, r, k_f, k_b, v, gate, y_mla):
    b, t, d = x.shape
    m = b * t
    xf = x.reshape(m, d)
    y_rwkv = _make_rowwise(_rwkv_post_fn, 'rwkv_post', 7, _tile(m, ROW_TILE))(
        y_f.reshape(m, D_RWKV), y_b.reshape(m, D_RWKV), r, k_f, k_b, v, gate,
        rep['ln_x_g'], rep['ln_x_b'], rep['r_k'].reshape(1, D_RWKV))[0]
    w_out = full['w_out']
    h = _make_mm_add('mm_out_rwkv')(xf, y_rwkv, w_out[:D_RWKV])
    h = _make_mm_add('mm_out_mla')(h, y_mla, w_out[D_RWKV:])
    n2, h = _make_rms_skip('rms_ffn')(h, rep['ln_ffn_g'])
    w_up, cw, cb = full['w_ffn_up'], full['ffn_conv_w'], rep['ffn_conv_b']
    u_gate, u_val = _make_mm_pair('mm_up')(n2, w_up[:, :D_FF], w_up[:, D_FF:])
    act = _conv_glu(u_gate.reshape(b, t, D_FF), u_val.reshape(b, t, D_FF),
                    cw[:, :D_FF], cw[:, D_FF:], cb[:, :D_FF], cb[:, D_FF:]).reshape(m, D_FF)
    h = _make_mm_add('mm_down')(h, act, full['w_ffn_down'])
    return _final_loss(h, rep['ln_final_g'], target.reshape(m, d))


def _mat(a):
    if a.ndim == 1:
        return a.reshape(1, -1)
    if a.ndim == 3:
        return a.reshape(a.shape[1:])
    return a


def _join(shards, name):
    if name in ROW:
        return shards.reshape(-1, shards.shape[-1])
    return shards.transpose(1, 0, 2).reshape(shards.shape[1], -1)


def _cut(whole, name):
    r, c = whole.shape
    if name in ROW:
        return whole.reshape(N_DEV, r // N_DEV, c)
    return whole.reshape(r, N_DEV, c // N_DEV).transpose(1, 0, 2)


def kernel(x, ln_mix_g, w_in, shift_mu_prev, shift_mu_next, decay_w0_fwd, decay_w2_fwd, decay_w0_bwd, decay_w2_bwd, iclr_a0_fwd, iclr_a2_fwd, iclr_a0_bwd, iclr_a2_bwd, gate_g2, k_k, k_a, r_k, ln_x_g, ln_x_b, q_norm_g, w_uq, kv_norm_g, w_ukv, mla_out_g, w_out, ln_ffn_g, w_ffn_up, ffn_conv_w, ffn_conv_b, w_ffn_down, ln_final_g, loss_target, m_ln_mix_g, m_w_in, m_shift_mu_prev, m_shift_mu_next, m_decay_w0_fwd, m_decay_w2_fwd, m_decay_w0_bwd, m_decay_w2_bwd, m_iclr_a0_fwd, m_iclr_a2_fwd, m_iclr_a0_bwd, m_iclr_a2_bwd, m_gate_g2, m_k_k, m_k_a, m_r_k, m_ln_x_g, m_ln_x_b, m_q_norm_g, m_w_uq, m_kv_norm_g, m_w_ukv, m_mla_out_g, m_w_out, m_ln_ffn_g, m_w_ffn_up, m_ffn_conv_w, m_ffn_conv_b, m_w_ffn_down, m_ln_final_g, v_ln_mix_g, v_w_in, v_shift_mu_prev, v_shift_mu_next, v_decay_w0_fwd, v_decay_w2_fwd, v_decay_w0_bwd, v_decay_w2_bwd, v_iclr_a0_fwd, v_iclr_a2_fwd, v_iclr_a0_bwd, v_iclr_a2_bwd, v_gate_g2, v_k_k, v_k_a, v_r_k, v_ln_x_g, v_ln_x_b, v_q_norm_g, v_w_uq, v_kv_norm_g, v_w_ukv, v_mla_out_g, v_w_out, v_ln_ffn_g, v_w_ffn_up, v_ffn_conv_w, v_ffn_conv_b, v_w_ffn_down, v_ln_final_g):
    given = dict(locals())
    w = {n: given[n] for n in WNAMES}
    mom = {n: given['m_' + n] for n in WNAMES}
    var = {n: given['v_' + n] for n in WNAMES}

    def split(names):
        return [n for n in names if n in BIG], [n for n in names if n not in BIG]

    def wire(names):
        big, small = split(names)
        pack = _pack([lax.bitcast_convert_type(_mat(w[n]), BF16) if n in EXACT else _mat(w[n]).astype(BF16) for n in small])
        return [_mat(w[n]).astype(BF16) for n in big] + [pack]

    def whole(names, gathered):
        big, small = split(names)
        out = {n: _join(g, n) for n, g in zip(big, gathered)}
        shapes = [_mat(w[n]).shape + ((2,) if n in EXACT else ()) for n in small]
        for n, s in zip(small, _unpack(gathered[-1], shapes, lead=1)):
            out[n] = _join(lax.bitcast_convert_type(s, F32) if n in EXACT else s.astype(F32), n)
        return out

    def grad_wire(names, grads):
        big, small = split(names)
        return [_cut(grads[n], n) for n in big] + [_pack([_cut(grads[n], n).astype(BF16) for n in small], lead=1)]

    early = [n for n in SHARDED if n not in LATE]
    rep = {n: _mat(w[n]) for n in REPLICATED}
    rep['r_k'] = w['r_k'].reshape(H, N)
    b, t, d = x.shape
    seq = lambda a: a.reshape(b, t, D_RWKV)
    flat = lambda a: a.reshape(b * t, D_RWKV)

    full_early = whole(early, _all_gather(wire(early), 'gather_weights'))
    ops, vjp_before = jax.vjp(_before_scan, full_early, rep, x)
    r, v, kk, w_f, k_f, q_f, w_b, k_b, q_b, gate, y_mla = ops
    scan_in = [seq(a) for a in (r, v, kk, w_f, k_f, q_f, w_b, k_b, q_b)]
    y_f, y_b, s_f, s_b, *late_gathered = _scan_fwd_call(*scan_in, wire(LATE))
    full_late = whole(LATE, late_gathered)
    loss_local, vjp_after = jax.vjp(_after_scan, full_late, rep, x, loss_target, y_f, y_b, r, k_f, k_b, v, gate, y_mla)

    g_late, g_rep_after, g_x_after, _, d_yf, d_yb, d_r, d_kf, d_kb, d_v, d_gate, d_ymla = vjp_after(jnp.ones((), F32))
    scan_out = _scan_bwd_call(*scan_in, s_f, s_b, d_yf, d_yb, grad_wire(LATE, g_late))
    parts_late = scan_out[12:]
    drf, dvf, dkkf, dwf, dkf, dqf, drb, dvb, dkkb, dwb, dkb, dqb = [flat(a) for a in scan_out[:12]]
    g_early, g_rep_before, g_x_before = vjp_before(
        (drf + drb + d_r, dvf + dvb + d_v, dkkf + dkkb, dwf, dkf + d_kf, dqf, dwb, dkb + d_kb, dqb, d_gate, d_ymla))
    g_rep = {n: g_rep_before[n] + g_rep_after[n] for n in rep}
    g_x = g_x_before + g_x_after
    parts_early = _grad_exchange(grad_wire(early, g_early), 'exchange_grads')

    s_out = [{}, {}, {}, {}]
    for names, parts, tag in ((early, parts_early, 'early'), (LATE, parts_late, 'late')):
        big, small = split(names)
        for n, p in zip(big, parts):
            res = _sum_adamw(p, _mat(w[n]), _mat(mom[n]), _mat(var[n]), 'adamw_' + n)
            for kind, o in enumerate(res):
                s_out[kind][n] = o.reshape(w[n].shape)
        res = _sum_adamw(parts[-1], _pack([w[n] for n in small]), _pack([mom[n] for n in small]),
                         _pack([var[n] for n in small]), 'adamw_small_' + tag)
        for kind, o in enumerate(res):
            s_out[kind].update(zip(small, _unpack(o, [w[n].shape for n in small])))

    zero = jnp.zeros((1,), F32)
    r_pack = _pack([g_rep[n] for n in REPLICATED] + [loss_local.reshape(1)])
    r_parts = _all_gather([r_pack], 'gather_small')[0]
    r_out = _sum_adamw(r_parts,_pack([w[n] for n in REPLICATED] + [zero]), _pack([mom[n] for n in REPLICATED] + [zero]),
                       _pack([var[n] for n in REPLICATED] + [zero]), 'adamw_replicated')
    r_out = [_unpack(o, [w[n].shape for n in REPLICATED] + [(1,)]) for o in r_out]

    loss = r_out[0][-1].reshape(())
    outs = [loss, g_x]
    for kind in range(4):
        by_name = dict(s_out[kind])
        by_name.update(zip(REPLICATED, r_out[kind][:-1]))
        outs += [by_name[n] for n in WNAMES]
    return tuple(outs)
```

```python
import functools

import jax
import jax.numpy as jnp
from jax import lax
from jax.experimental import pallas as pl
from jax.experimental.pallas import tpu as pltpu

F32 = jnp.float32
BF16 = jnp.bfloat16
MESH = pl.DeviceIdType.MESH

N_DEV = 8
LANES = 128
SUBLANES = 8
PACK_TILE = 2 * SUBLANES * LANES
PACK_ROWS = 512
ADAM_ROWS = 256
MM_TILE = 512
MM_TILE_WIDE = 1408
MM_K_WHOLE = 2816
VMEM_LIMIT = 56 * 1024 * 1024

H = 8
N = 64
D_RWKV = H * N
D_NOPE, D_ROPE, D_V = 64, 32, 64
D_QK = D_NOPE + D_ROPE
MLA_SCALE = D_QK ** -0.5
ROPE_THETA = 10000.0
RWKV_SPLITS = (D_RWKV, D_RWKV, D_RWKV, 64, 64, 64, 64, 128)
RWKV_COLS = sum(RWKV_SPLITS)
Q_LORA, KV_LORA = 768, 256
D_FF = 2816
NORM_EPS = 1e-6
GN_EPS = 64e-5
L2_EPS = 1e-12
ADAM_LR, ADAM_B1, ADAM_B2, ADAM_EPS, ADAM_WD, ADAM_STEP = 0.001, 0.9, 0.999, 1e-08, 0.01, 10

SCAN_CHUNK = 32
ATT_TQ = 256
SEG = 256
FFN_COLS = 256
ROW_TILE = 256
SHIFT_COLS = 384

WNAMES = ['ln_mix_g', 'w_in', 'shift_mu_prev', 'shift_mu_next', 'decay_w0_fwd', 'decay_w2_fwd', 'decay_w0_bwd',
          'decay_w2_bwd', 'iclr_a0_fwd', 'iclr_a2_fwd', 'iclr_a0_bwd', 'iclr_a2_bwd', 'gate_g2', 'k_k', 'k_a', 'r_k',
          'ln_x_g', 'ln_x_b', 'q_norm_g', 'w_uq', 'kv_norm_g', 'w_ukv', 'mla_out_g', 'w_out', 'ln_ffn_g', 'w_ffn_up',
          'ffn_conv_w', 'ffn_conv_b', 'w_ffn_down', 'ln_final_g']
COL = ('w_in', 'decay_w2_fwd', 'decay_w2_bwd', 'iclr_a2_fwd', 'iclr_a2_bwd', 'gate_g2', 'w_ukv', 'w_ffn_up', 'ffn_conv_w')
ROW = ('w_uq', 'w_out', 'w_ffn_down')
SHARDED = [n for n in WNAMES if n in COL or n in ROW]
REPLICATED = [n for n in WNAMES if n not in SHARDED]
EXACT = ('ffn_conv_w',)
LATE = ['w_out', 'w_ffn_up', 'ffn_conv_w', 'w_ffn_down']
BIG = ('w_in', 'w_uq', 'w_ukv', 'w_out', 'w_ffn_up', 'w_ffn_down')


def _params(*sem):
    return pltpu.CompilerParams(dimension_semantics=sem, vmem_limit_bytes=VMEM_LIMIT)


def _pack(arrs, lead=0):
    parts = []
    for a in arrs:
        head = a.shape[:lead]
        flat = a.reshape(head + (-1,))
        n = flat.shape[-1]
        n_pad = -(-n // PACK_TILE) * PACK_TILE
        flat = jnp.pad(flat, [(0, 0)] * lead + [(0, n_pad - n)])
        parts.append(flat.reshape(head + (n_pad // LANES, LANES)))
    out = jnp.concatenate(parts, axis=lead)
    rows = out.shape[lead]
    rows_pad = -(-rows // PACK_ROWS) * PACK_ROWS
    return jnp.pad(out, [(0, 0)] * lead + [(0, rows_pad - rows), (0, 0)])


def _unpack(packed, shapes, lead=0):
    outs, row = [], 0
    head = packed.shape[:lead]
    for shp in shapes:
        n = 1
        for s in shp:
            n *= s
        rows = -(-n // PACK_TILE) * (PACK_TILE // LANES)
        blk = lax.slice_in_dim(packed, row, row + rows, axis=lead)
        flat = blk.reshape(head + (rows * LANES,))
        outs.append(lax.slice_in_dim(flat, 0, n, axis=lead).reshape(head + tuple(shp)))
        row += rows
    return outs


PEERS = N_DEV - 1
HBM = pl.BlockSpec(memory_space=pl.ANY)


def _comm_sems(n):
    return [pltpu.SemaphoreType.DMA((PEERS * n,)), pltpu.SemaphoreType.DMA((PEERS * n,)), pltpu.SemaphoreType.DMA((n,))]


def _all_gather(xs, name):
    n = len(xs)

    def body(*refs):
        x_refs, out_refs, (send_sems, recv_sems, local_sems) = refs[:n], refs[n:2 * n], refs[2 * n:]
        mx, my, mc = lax.axis_index("x"), lax.axis_index("y"), lax.axis_index("c")
        me, sibling = (mx, my, mc), (mx, my, 1 - mc)
        chips = [(1 - mx, my), (mx, 1 - my), (1 - mx, 1 - my)]

        def slot(a, px, py, pc):
            return out_refs[a].at[4 * px + 2 * py + pc]

        def copy(a, k, block, to, src=None):
            return pltpu.make_async_remote_copy(
                src_ref=slot(a, *block) if src is None else src, dst_ref=slot(a, *block),
                send_sem=send_sems.at[PEERS * a + k], recv_sem=recv_sems.at[PEERS * a + k],
                device_id=to, device_id_type=MESH)

        mine = [pltpu.make_async_copy(x_refs[a], slot(a, *me), local_sems.at[a]) for a in range(n)]
        first, passed = [], []
        for a in range(n):
            mine[a].start()
            first.append(copy(a, 0, me, sibling, src=x_refs[a]))
            first += [copy(a, 1 + j, me, (*chip, mc), src=x_refs[a]) for j, chip in enumerate(chips)]
        for cp in first:
            cp.start()
        for j, chip in enumerate(chips):
            for a in range(n):
                copy(a, 1 + j, (*chip, mc), me).wait_recv()
                passed.append(copy(a, 4 + j, (*chip, mc), sibling))
                passed[-1].start()
        for a in range(n):
            copy(a, 0, sibling, me).wait_recv()
            for j, chip in enumerate(chips):
                copy(a, 4 + j, (*chip, 1 - mc), me).wait_recv()
        for cp in first + passed:
            cp.wait_send()
        for cp in mine:
            cp.wait()

    return pl.pallas_call(
        body, name=name, out_shape=[jax.ShapeDtypeStruct((N_DEV,) + x.shape, x.dtype) for x in xs],
        in_specs=[HBM] * n, out_specs=[HBM] * n, scratch_shapes=_comm_sems(n),
    )(*xs)


def _direct_exchange(src_refs, out_refs, send_sems, recv_sems, local_sems, per_peer):
    mx, my, mc = lax.axis_index("x"), lax.axis_index("y"), lax.axis_index("c")
    me = 4 * mx + 2 * my + mc

    def flip(v, bit):
        return 1 - v if bit else v

    def copies():
        mine, remote = [], []
        for a, (src, out) in enumerate(zip(src_refs, out_refs)):
            mine.append(pltpu.make_async_copy(src.at[me] if per_peer else src, out.at[me], local_sems.at[a]))
            for k in range(1, N_DEV):
                px, py, pc = flip(mx, k & 4), flip(my, k & 2), flip(mc, k & 1)
                remote.append(pltpu.make_async_remote_copy(
                    src_ref=src.at[4 * px + 2 * py + pc] if per_peer else src, dst_ref=out.at[me],
                    send_sem=send_sems.at[PEERS * a + k - 1], recv_sem=recv_sems.at[PEERS * a + k - 1],
                    device_id=(px, py, pc), device_id_type=MESH))
        return mine, remote

    def start():
        mine, remote = copies()
        for cp in mine + remote:
            cp.start()

    def wait():
        mine, remote = copies()
        for cp in remote:
            cp.wait_recv()
        for cp in remote:
            cp.wait_send()
        for cp in mine:
            cp.wait()

    return start, wait


def _grad_exchange(gs, name):
    n = len(gs)

    def body(*refs):
        start, wait = _direct_exchange(refs[:n], refs[n:2 * n], *refs[2 * n:], per_peer=True)
        start()
        wait()

    return pl.pallas_call(
        body, name=name, out_shape=[jax.ShapeDtypeStruct(g.shape, g.dtype) for g in gs],
        in_specs=[HBM] * n, out_specs=[HBM] * n, scratch_shapes=_comm_sems(n),
    )(*gs)


def _sum_adamw(parts, w, m, v, name):
    rows, cols = w.shape
    tr = next((t for t in range(ADAM_ROWS, 15, -16) if rows % t == 0), rows)
    c1 = 1.0 - ADAM_B1 ** ADAM_STEP
    c2 = 1.0 - ADAM_B2 ** ADAM_STEP

    def body(p_ref, w_ref, m_ref, v_ref, g_out, d_out, m_out, v_out):
        g = p_ref[0].astype(F32)
        for q in range(1, N_DEV):
            g = g + p_ref[q].astype(F32)
        m_new = ADAM_B1 * m_ref[...] + (1.0 - ADAM_B1) * g
        v_new = ADAM_B2 * v_ref[...] + (1.0 - ADAM_B2) * (g * g)
        m_hat = m_new / c1
        v_hat = v_new / c2
        g_out[...] = g
        d_out[...] = -ADAM_LR * (m_hat / (jnp.sqrt(v_hat) + ADAM_EPS) + ADAM_WD * w_ref[...])
        m_out[...] = m_new
        v_out[...] = v_new

    blk = pl.BlockSpec((tr, cols), lambda i: (i, 0))
    out = jax.ShapeDtypeStruct((rows, cols), F32)
    return pl.pallas_call(
        body, name=name, grid=(rows // tr,),
        in_specs=[pl.BlockSpec((N_DEV, tr, cols), lambda i: (0, i, 0)), blk, blk, blk],
        out_specs=[blk, blk, blk, blk], out_shape=[out, out, out, out],
        compiler_params=_params("parallel"),
    )(parts, w, m, v)


def _tile(dim, cap=MM_TILE):
    if dim <= cap:
        return dim
    for t in range(cap, LANES - 1, -LANES):
        if dim % t == 0:
            return t
    return dim


def _mm_call(a, b, form, name, out_dtype=F32, base=None):
    if form == 'nn':
        (m, k), n = a.shape, b.shape[1]
    elif form == 'nt':
        (m, k), n = a.shape, b.shape[0]
    else:
        (k, m), n = a.shape, b.shape[1]
    tk = k if (form == 'nn' and k <= MM_K_WHOLE) else _tile(k, MM_TILE_WIDE)
    tm = _tile(m, MM_TILE_WIDE if form == 'tn' else MM_TILE)
    tn = _tile(n, MM_TILE_WIDE)
    nk = k // tk
    contract = {'nn': ((1,), (0,)), 'nt': ((1,), (1,)), 'tn': ((0,), (0,))}[form]

    acc_in_out = nk == 1 or out_dtype == F32
    assert base is None or out_dtype == F32
    extra = [] if base is None else [base]

    def body(a_ref, b_ref, *rest):
        o_ref, acc = rest[len(extra)], rest[len(extra) + 1:]
        part = lax.dot_general(a_ref[...].astype(BF16), b_ref[...].astype(BF16), (contract, ((), ())),
                               preferred_element_type=F32)
        if nk == 1:
            o_ref[...] = (part + rest[0][...] if extra else part).astype(out_dtype)
            return
        acc_ref = o_ref if acc_in_out else acc[0]

        @pl.when(pl.program_id(2) == 0)
        def _():
            acc_ref[...] = part + rest[0][...] if extra else part

        @pl.when(pl.program_id(2) > 0)
        def _():
            acc_ref[...] += part

        if not acc_in_out:
            @pl.when(pl.program_id(2) == nk - 1)
            def _():
                o_ref[...] = acc_ref[...].astype(out_dtype)

    a_spec = pl.BlockSpec((tk, tm), lambda j, i, l: (l, i)) if form == 'tn' else pl.BlockSpec((tm, tk), lambda j, i, l: (i, l))
    b_spec = pl.BlockSpec((tn, tk), lambda j, i, l: (j, l)) if form == 'nt' else pl.BlockSpec((tk, tn), lambda j, i, l: (l, j))
    o_spec = pl.BlockSpec((tm, tn), lambda j, i, l: (i, j))
    return pl.pallas_call(
        body, name=name, grid=(n // tn, m // tm, nk),
        in_specs=[a_spec, b_spec] + [o_spec] * len(extra), out_specs=o_spec,
        out_shape=jax.ShapeDtypeStruct((m, n), out_dtype),
        scratch_shapes=[] if acc_in_out else [pltpu.VMEM((tm, tn), F32)],
        compiler_params=_params("parallel", "parallel", "arbitrary"),
    )(a, b, *extra)


def _make_mm(name):
    @jax.custom_vjp
    def mm(a, b):
        return _mm_call(a, b, 'nn', name + '_fwd')

    def fwd(a, b):
        return _mm_call(a, b, 'nn', name + '_fwd'), (a, b)

    def bwd(res, g):
        a, b = res
        return _mm_call(g, b, 'nt', name + '_da'), _mm_call(a, g, 'tn', name + '_db', out_dtype=BF16)

    mm.defvjp(fwd, bwd)
    return mm


def _make_mm_pair(name):
    def both(a, b1, b2):
        return _mm_call(a, b1, 'nn', name + '_1_fwd'), _mm_call(a, b2, 'nn', name + '_2_fwd')

    mm = jax.custom_vjp(both)

    def fwd(a, b1, b2):
        return both(a, b1, b2), (a, b1, b2)

    def bwd(res, gs):
        a, b1, b2 = res
        da = _mm_call(gs[1], b2, 'nt', name + '_2_da', base=_mm_call(gs[0], b1, 'nt', name + '_1_da'))
        return (da, _mm_call(a, gs[0], 'tn', name + '_1_db', out_dtype=BF16),
                _mm_call(a, gs[1], 'tn', name + '_2_db', out_dtype=BF16))

    mm.defvjp(fwd, bwd)
    return mm


def _make_mm_add(name):
    @jax.custom_vjp
    def mm(base, a, b):
        return _mm_call(a, b, 'nn', name + '_fwd', base=base)

    def fwd(base, a, b):
        return _mm_call(a, b, 'nn', name + '_fwd', base=base), (a, b)

    def bwd(res, g):
        a, b = res
        return g, _mm_call(g, b, 'nt', name + '_da'), _mm_call(a, g, 'tn', name + '_db', out_dtype=BF16)

    mm.defvjp(fwd, bwd)
    return mm


def _rms_fwd_call(x, g, name):
    m, d = x.shape
    tm = _tile(m)

    def body(x_ref, g_ref, o_ref):
        xv = x_ref[...]
        rinv = lax.rsqrt(jnp.mean(xv * xv, axis=-1, keepdims=True) + NORM_EPS)
        o_ref[...] = xv * rinv * g_ref[...]

    return pl.pallas_call(
        body, name=name, grid=(m // tm,),
        in_specs=[pl.BlockSpec((tm, d), lambda i: (i, 0)), pl.BlockSpec((1, d), lambda i: (0, 0))],
        out_specs=pl.BlockSpec((tm, d), lambda i: (i, 0)), out_shape=jax.ShapeDtypeStruct((m, d), F32),
        compiler_params=_params("parallel"),
    )(x, g)


def _rms_bwd_call(x, g, dy, name, d_skip=None):
    m, d = x.shape
    tm = _tile(m)
    extra = [] if d_skip is None else [d_skip]

    def body(x_ref, g_ref, dy_ref, *rest):
        dx_ref, dg_ref = rest[len(extra):]

        @pl.when(pl.program_id(0) == 0)
        def _():
            dg_ref[...] = jnp.zeros_like(dg_ref)

        xv, dyv = x_ref[...], dy_ref[...]
        rinv = lax.rsqrt(jnp.mean(xv * xv, axis=-1, keepdims=True) + NORM_EPS)
        xh = xv * rinv
        dg_ref[...] += jnp.sum(dyv * xh, axis=0, keepdims=True)
        dxh = dyv * g_ref[...]
        dx = rinv * (dxh - xh * jnp.mean(dxh * xh, axis=-1, keepdims=True))
        dx_ref[...] = dx + rest[0][...] if extra else dx

    row = pl.BlockSpec((tm, d), lambda i: (i, 0))
    vec = pl.BlockSpec((1, d), lambda i: (0, 0))
    return pl.pallas_call(
        body, name=name, grid=(m // tm,), in_specs=[row, vec, row] + [row] * len(extra), out_specs=[row, vec],
        out_shape=[jax.ShapeDtypeStruct((m, d), F32), jax.ShapeDtypeStruct((1, d), F32)],
        compiler_params=_params("arbitrary"),
    )(x, g, dy, *extra)


def _make_rms(name):
    @jax.custom_vjp
    def rms(x, g):
        return _rms_fwd_call(x, g, name + '_fwd')

    def fwd(x, g):
        return _rms_fwd_call(x, g, name + '_fwd'), (x, g)

    def bwd(res, dy):
        x, g = res
        dx, dg = _rms_bwd_call(x, g, dy, name + '_bwd')
        return dx, dg

    rms.defvjp(fwd, bwd)
    return rms


def _final_loss_call(x, g, target, ct=None):
    m, d = x.shape
    tm = _tile(m)
    row = pl.BlockSpec((tm, d), lambda i: (i, 0))
    vec = pl.BlockSpec((1, d), lambda i: (0, 0))
    acc = pl.BlockSpec((1, LANES), lambda i: (0, 0))

    def normed(x_ref, g_ref):
        xv = x_ref[...]
        rinv = lax.rsqrt(jnp.mean(xv * xv, axis=-1, keepdims=True) + NORM_EPS)
        return rinv, xv * rinv

    def fwd_body(x_ref, g_ref, t_ref, loss_ref):
        @pl.when(pl.program_id(0) == 0)
        def _():
            loss_ref[...] = jnp.zeros_like(loss_ref)

        _, xh = normed(x_ref, g_ref)
        err = xh * g_ref[...] - t_ref[...]
        loss_ref[...] += 0.5 * jnp.sum(jnp.mean(err * err, axis=-1, keepdims=True), axis=0, keepdims=True)

    def bwd_body(x_ref, g_ref, t_ref, ct_ref, dx_ref, dg_ref):
        @pl.when(pl.program_id(0) == 0)
        def _():
            dg_ref[...] = jnp.zeros_like(dg_ref)

        rinv, xh = normed(x_ref, g_ref)
        dyv = (xh * g_ref[...] - t_ref[...]) * (ct_ref[0:1, 0:1] * (1.0 / d))
        dg_ref[...] += jnp.sum(dyv * xh, axis=0, keepdims=True)
        dxh = dyv * g_ref[...]
        dx_ref[...] = rinv * (dxh - xh * jnp.mean(dxh * xh, axis=-1, keepdims=True))

    if ct is None:
        return pl.pallas_call(
            fwd_body, name='final_loss_fwd', grid=(m // tm,), in_specs=[row, vec, row], out_specs=acc,
            out_shape=jax.ShapeDtypeStruct((1, LANES), F32), compiler_params=_params("arbitrary"),
        )(x, g, target)
    return pl.pallas_call(
        bwd_body, name='final_loss_bwd', grid=(m // tm,), in_specs=[row, vec, row, acc], out_specs=[row, vec],
        out_shape=[jax.ShapeDtypeStruct((m, d), F32), jax.ShapeDtypeStruct((1, d), F32)],
        compiler_params=_params("arbitrary"),
    )(x, g, target, ct)


@jax.custom_vjp
def _final_loss(x, g, target):
    return _final_loss_call(x, g, target)[0, 0]


def _final_loss_bwd(res, ct):
    x, g, target = res
    dx, dg = _final_loss_call(x, g, target, ct=jnp.full((1, LANES), ct, F32))
    return dx, dg, jnp.zeros_like(target)


_final_loss.defvjp(lambda x, g, target: (_final_loss_call(x, g, target)[0, 0], (x, g, target)), _final_loss_bwd)


def _make_rms_skip(name):
    @jax.custom_vjp
    def rms(x, g):
        return _rms_fwd_call(x, g, name + '_fwd'), x

    def fwd(x, g):
        return (_rms_fwd_call(x, g, name + '_fwd'), x), (x, g)

    def bwd(res, cts):
        x, g = res
        dx, dg = _rms_bwd_call(x, g, cts[0], name + '_bwd', d_skip=cts[1])
        return dx, dg

    rms.defvjp(fwd, bwd)
    return rms


def _time_shifts(x):
    t = x.shape[0]
    rows = lax.broadcasted_iota(jnp.int32, x.shape, 0)
    return (jnp.where(rows == 0, 0.0, pltpu.roll(x, 1, 0)), jnp.where(rows == t - 1, 0.0, pltpu.roll(x, t - 1, 0)))


def _conv3(x, cw_ref, cb_ref):
    xp, xn = _time_shifts(x)
    return cw_ref[0:1, :] * xp + cw_ref[1:2, :] * x + cw_ref[2:3, :] * xn + cb_ref[...]


def _glu_specs(b, t, f):
    tc = _tile(f, FFN_COLS)
    seq = pl.BlockSpec((1, t, tc), lambda j, bi: (bi, 0, j))
    cw = pl.BlockSpec((3, tc), lambda j, bi: (0, j))
    cb = pl.BlockSpec((1, tc), lambda j, bi: (0, j))
    return tc, seq, cw, cb


def _glu_fwd_call(ug, uv, cwg, cwv, cbg, cbv):
    b, t, f = ug.shape
    tc, seq, cw, cb = _glu_specs(b, t, f)

    def body(ug_ref, uv_ref, cwg_ref, cwv_ref, cbg_ref, cbv_ref, o_ref):
        g = _conv3(ug_ref[0], cwg_ref, cbg_ref)
        o_ref[0] = g * jax.nn.sigmoid(g) * _conv3(uv_ref[0], cwv_ref, cbv_ref)

    return pl.pallas_call(
        body, name='glu_fwd', grid=(f // tc, b), in_specs=[seq, seq, cw, cw, cb, cb], out_specs=seq,
        out_shape=jax.ShapeDtypeStruct((b, t, f), F32), compiler_params=_params("parallel", "parallel"),
    )(ug, uv, cwg, cwv, cbg, cbv)


def _glu_bwd_call(ug, uv, cwg, cwv, cbg, cbv, dact):
    b, t, f = ug.shape
    tc, seq, cw, cb = _glu_specs(b, t, f)

    def body(ug_ref, uv_ref, cwg_ref, cwv_ref, cbg_ref, cbv_ref, da_ref,
             dug_ref, duv_ref, dcwg_ref, dcwv_ref, dcbg_ref, dcbv_ref):
        @pl.when(pl.program_id(1) == 0)
        def _():
            for ref in (dcwg_ref, dcwv_ref, dcbg_ref, dcbv_ref):
                ref[...] = jnp.zeros_like(ref)

        g = _conv3(ug_ref[0], cwg_ref, cbg_ref)
        v = _conv3(uv_ref[0], cwv_ref, cbv_ref)
        sig = jax.nn.sigmoid(g)
        da = da_ref[0]
        dv = da * (g * sig)
        dg = da * v * (sig * (1.0 + g * (1.0 - sig)))

        def conv_bwd(dc, x_ref, cw_ref, dx_ref, dcw_ref, dcb_ref):
            dcp, dcn = _time_shifts(dc)
            dx_ref[0] = cw_ref[0:1, :] * dcn + cw_ref[1:2, :] * dc + cw_ref[2:3, :] * dcp
            x = x_ref[0]
            for n, ds in enumerate((dcn, dc, dcp)):
                dcw_ref[n:n + 1, :] += jnp.sum(ds * x, axis=0, keepdims=True)
            dcb_ref[...] += jnp.sum(dc, axis=0, keepdims=True)

        conv_bwd(dg, ug_ref, cwg_ref, dug_ref, dcwg_ref, dcbg_ref)
        conv_bwd(dv, uv_ref, cwv_ref, duv_ref, dcwv_ref, dcbv_ref)

    big = jax.ShapeDtypeStruct((b, t, f), F32)
    return pl.pallas_call(
        body, name='glu_bwd', grid=(f // tc, b), in_specs=[seq, seq, cw, cw, cb, cb, seq],
        out_specs=[seq, seq, cw, cw, cb, cb],
        out_shape=[big, big, jax.ShapeDtypeStruct((3, f), F32), jax.ShapeDtypeStruct((3, f), F32),
                   jax.ShapeDtypeStruct((1, f), F32), jax.ShapeDtypeStruct((1, f), F32)],
        compiler_params=_params("parallel", "arbitrary"),
    )(ug, uv, cwg, cwv, cbg, cbv, dact)


def _shift_call(z, mu_p, mu_n, dzs=None):
    b, t, c = z.shape
    tc = _tile(c, SHIFT_COLS)
    seq = pl.BlockSpec((1, t, tc), lambda j, bi: (bi, 0, j))
    row = pl.BlockSpec((1, tc), lambda j, bi: (0, j))

    def fwd_body(z_ref, mp_ref, mn_ref, o_ref):
        x = z_ref[0]
        xp, xn = _time_shifts(x)
        o_ref[0] = x + mp_ref[...] * (xp - x) + mn_ref[...] * (xn - x)

    def bwd_body(z_ref, mp_ref, mn_ref, d_ref, dz_ref, dmp_ref, dmn_ref):
        @pl.when(pl.program_id(1) == 0)
        def _():
            dmp_ref[...] = jnp.zeros_like(dmp_ref)
            dmn_ref[...] = jnp.zeros_like(dmn_ref)

        x, d = z_ref[0], d_ref[0]
        xp, xn = _time_shifts(x)
        dp, dn = _time_shifts(d)
        mp, mn = mp_ref[...], mn_ref[...]
        dz_ref[0] = d * (1.0 - mp - mn) + mp * dn + mn * dp
        dmp_ref[...] += jnp.sum(d * (xp - x), axis=0, keepdims=True)
        dmn_ref[...] += jnp.sum(d * (xn - x), axis=0, keepdims=True)

    if dzs is None:
        return pl.pallas_call(
            fwd_body, name='shift_fwd', grid=(c // tc, b), in_specs=[seq, row, row], out_specs=seq,
            out_shape=jax.ShapeDtypeStruct(z.shape, F32), compiler_params=_params("parallel", "parallel"),
        )(z, mu_p, mu_n)
    return pl.pallas_call(
        bwd_body, name='shift_bwd', grid=(c // tc, b), in_specs=[seq, row, row, seq], out_specs=[seq, row, row],
        out_shape=[jax.ShapeDtypeStruct(z.shape, F32), jax.ShapeDtypeStruct(mu_p.shape, F32),
                   jax.ShapeDtypeStruct(mu_n.shape, F32)],
        compiler_params=_params("parallel", "arbitrary"),
    )(z, mu_p, mu_n, dzs)


@jax.custom_vjp
def _token_shift(z, mu_p, mu_n):
    return _shift_call(z, mu_p, mu_n)


_token_shift.defvjp(lambda z, mu_p, mu_n: (_shift_call(z, mu_p, mu_n), (z, mu_p, mu_n)),
                    lambda res, d: tuple(_shift_call(*res, dzs=d)))


@jax.custom_vjp
def _conv_glu(ug, uv, cwg, cwv, cbg, cbv):
    return _glu_fwd_call(ug, uv, cwg, cwv, cbg, cbv)


def _conv_glu_fwd(*args):
    return _glu_fwd_call(*args), args


def _conv_glu_bwd(res, dact):
    return tuple(_glu_bwd_call(*res, dact))


_conv_glu.defvjp(_conv_glu_fwd, _conv_glu_bwd)


HEAD_LANES = 2 * D_NOPE
PAIR = 2


def _lane(shape):
    return lax.broadcasted_iota(jnp.int32, shape, len(shape) - 1)


def _rope(x, c, s1, s2):
    return x * c + pltpu.roll(x, HEAD_LANES - D_ROPE // 2, 1) * s1 + pltpu.roll(x, D_ROPE // 2, 1) * s2


def _rope_t(g, c, s1, s2):
    return g * c + pltpu.roll(g * s1, D_ROPE // 2, 1) + pltpu.roll(g * s2, HEAD_LANES - D_ROPE // 2, 1)


def _attn_setup(kv_ref, kr_ref, tabs, k2, v16):
    c, s1, s2 = (tb[...] for tb in tabs)
    krr = _rope(kr_ref[0], c, s1, s2)
    v16[...] = kv_ref[0].astype(BF16)
    for hh in range(PAIR):
        slab = kv_ref[0, :, HEAD_LANES * hh:HEAD_LANES * (hh + 1)]
        k2[hh] = jnp.where(_lane(slab.shape) < D_NOPE, slab, krr).astype(BF16)


def _attn_queries(q_ref, tabs, rows, hh):
    c, s1, s2 = (tb[rows, :] for tb in tabs)
    return (_rope(q_ref[0, :, HEAD_LANES * hh:HEAD_LANES * (hh + 1)], c, s1, s2) * MLA_SCALE).astype(BF16), (c, s1, s2)


def _attn_specs(b, t, tq):
    qspec = pl.BlockSpec((1, tq, PAIR * HEAD_LANES), lambda bi, p, i: (bi, i, p))
    kvspec = pl.BlockSpec((1, t, PAIR * HEAD_LANES), lambda bi, p, i: (bi, 0, p))
    krspec = pl.BlockSpec((1, t, HEAD_LANES), lambda bi, p, i: (bi, 0, 0))
    tab = pl.BlockSpec((t, HEAD_LANES), lambda bi, p, i: (0, 0))
    ospec = pl.BlockSpec((1, tq, PAIR * D_V), lambda bi, p, i: (bi, i, p))
    lspec = pl.BlockSpec((1, 1, tq, HEAD_LANES), lambda bi, p, i: (bi, p, i, 0))
    return qspec, kvspec, krspec, tab, ospec, lspec


def _attn_fwd_call(q, kv, kr, tabs):
    b, t, _ = q.shape
    tq = min(ATT_TQ, t)
    qspec, kvspec, krspec, tab, ospec, lspec = _attn_specs(b, t, tq)

    def body(q_ref, kv_ref, kr_ref, c_ref, s1_ref, s2_ref, o_ref, lse_ref, k2, v16):
        tabs = (c_ref, s1_ref, s2_ref)

        @pl.when(pl.program_id(2) == 0)
        def _():
            _attn_setup(kv_ref, kr_ref, tabs, k2, v16)

        rows = pl.ds(pl.multiple_of(pl.program_id(2) * tq, tq), tq)
        outs, lses = [], []
        for hh in range(PAIR):
            qh, _ = _attn_queries(q_ref, tabs, rows, hh)
            s = lax.dot_general(qh, k2[hh], (((1,), (1,)), ((), ())), preferred_element_type=F32)
            m = jnp.max(s, axis=-1, keepdims=True)
            p = jnp.exp(s - m)
            l = jnp.sum(p, axis=-1, keepdims=True)
            slab16 = v16[:, HEAD_LANES * hh:HEAD_LANES * (hh + 1)]
            outs.append(jnp.dot(p.astype(BF16), slab16, preferred_element_type=F32) / l)
            lses.append(m + jnp.log(l))
        low = _lane(outs[0].shape) < D_V
        o_ref[0] = jnp.where(low, pltpu.roll(outs[0], D_V, 1), outs[1])
        lse_ref[0, 0] = jnp.where(low, lses[0], lses[1])

    return pl.pallas_call(
        body, name='attn_fwd', grid=(b, H // PAIR, t // tq),
        in_specs=[qspec, kvspec, krspec, tab, tab, tab], out_specs=[ospec, lspec],
        out_shape=[jax.ShapeDtypeStruct((b, t, H * D_V), F32), jax.ShapeDtypeStruct((b, H // PAIR, t, HEAD_LANES), F32)],
        scratch_shapes=[pltpu.VMEM((PAIR, t, HEAD_LANES), BF16), pltpu.VMEM((t, PAIR * HEAD_LANES), BF16)],
        compiler_params=_params("parallel", "parallel", "arbitrary"),
    )(q, kv, kr, *tabs)


def _attn_bwd_call(q, kv, kr, tabs, o, lse, do):
    b, t, _ = q.shape
    tq = min(ATT_TQ, t)
    n_q = t // tq
    qspec, kvspec, krspec, tab, ospec, lspec = _attn_specs(b, t, tq)

    def body(q_ref, kv_ref, kr_ref, c_ref, s1_ref, s2_ref, o_ref, lse_ref, do_ref, dq_ref, dkv_ref, dkr_ref, k2, v16, dk2):
        tabs = (c_ref, s1_ref, s2_ref)
        pair, step = pl.program_id(1), pl.program_id(2)

        @pl.when(step == 0)
        def _():
            _attn_setup(kv_ref, kr_ref, tabs, k2, v16)
            dk2[...] = jnp.zeros_like(dk2)
            dkv_ref[...] = jnp.zeros_like(dkv_ref)

        @pl.when((step == 0) & (pair == 0))
        def _():
            dkr_ref[...] = jnp.zeros_like(dkr_ref)

        rows = pl.ds(pl.multiple_of(step * tq, tq), tq)
        dov, ov = do_ref[0], o_ref[0]
        lane = _lane(dov.shape)
        upper = lane >= D_V
        for hh in range(PAIR):
            qh, qtabs = _attn_queries(q_ref, tabs, rows, hh)
            s = lax.dot_general(qh, k2[hh], (((1,), (1,)), ((), ())), preferred_element_type=F32)
            p = jnp.exp(s - lse_ref[0, 0, :, D_V * hh:D_V * hh + 1])
            mine = upper if hh else ~upper
            delta = jnp.sum(jnp.where(mine, dov * ov, 0.0), axis=-1, keepdims=True)
            do_h = jnp.where(upper, dov if hh else pltpu.roll(dov, D_V, 1), 0.0).astype(BF16)
            slab16 = v16[:, HEAD_LANES * hh:HEAD_LANES * (hh + 1)]
            dp = lax.dot_general(do_h, slab16, (((1,), (1,)), ((), ())), preferred_element_type=F32)
            ds = (p * (dp - delta)).astype(BF16)
            dqh = jnp.dot(ds, k2[hh], preferred_element_type=F32) * MLA_SCALE
            dq_ref[0, :, HEAD_LANES * hh:HEAD_LANES * (hh + 1)] = _rope_t(dqh, *qtabs)
            dk2[hh] += lax.dot_general(ds, qh, (((0,), (0,)), ((), ())), preferred_element_type=F32)
            dkv_ref[0, :, HEAD_LANES * hh:HEAD_LANES * (hh + 1)] += lax.dot_general(
                p.astype(BF16), do_h, (((0,), (0,)), ((), ())), preferred_element_type=F32)

        @pl.when(step == n_q - 1)
        def _():
            c, s1, s2 = (tb[...] for tb in tabs)
            for hh in range(PAIR):
                g = dk2[hh]
                key_lane = _lane(g.shape)
                dkv_ref[0, :, HEAD_LANES * hh:HEAD_LANES * (hh + 1)] += jnp.where(key_lane < D_NOPE, g, 0.0)
                dkr_ref[0] += _rope_t(jnp.where(key_lane >= D_NOPE, g, 0.0), c, s1, s2)

    return pl.pallas_call(
        body, name='attn_bwd', grid=(b, H // PAIR, n_q),
        in_specs=[qspec, kvspec, krspec, tab, tab, tab, ospec, lspec, ospec], out_specs=[qspec, kvspec, krspec],
        out_shape=[jax.ShapeDtypeStruct(q.shape, F32), jax.ShapeDtypeStruct(kv.shape, F32), jax.ShapeDtypeStruct(kr.shape, F32)],
        scratch_shapes=[pltpu.VMEM((PAIR, t, HEAD_LANES), BF16), pltpu.VMEM((t, PAIR * HEAD_LANES), BF16),
                        pltpu.VMEM((PAIR, t, HEAD_LANES), F32)],
        compiler_params=_params("parallel", "arbitrary", "arbitrary"),
    )(q, kv, kr, *tabs, o, lse, do)


@jax.custom_vjp
def _attention(q, kv, kr, tabs):
    return _attn_fwd_call(q, kv, kr, tabs)[0]


def _attention_fwd(q, kv, kr, tabs):
    o, lse = _attn_fwd_call(q, kv, kr, tabs)
    return o, (q, kv, kr, tabs, o, lse)


def _attention_bwd(res, do):
    q, kv, kr, tabs, o, lse = res
    return (*_attn_bwd_call(q, kv, kr, tabs, o, lse, do), tuple(jnp.zeros_like(tb) for tb in tabs))


_attention.defvjp(_attention_fwd, _attention_bwd)


SROWS = N * D_RWKV // SEG


def _seg_ones():
    r = lax.broadcasted_iota(jnp.int32, (SEG, SEG), 0) // N
    c = lax.broadcasted_iota(jnp.int32, (SEG, SEG), 1) // N
    return (r == c).astype(BF16)


def _eye_mask():
    r = lax.broadcasted_iota(jnp.int32, (SROWS, SEG), 0) & (N - 1)
    c = lax.broadcasted_iota(jnp.int32, (SROWS, SEG), 1) & (N - 1)
    return r == c


def _row2(ref, bi, ti, dtype=F32):
    parts = [jnp.broadcast_to(ref[bi, pl.ds(ti, 1), pl.ds(SEG * q, SEG)].astype(dtype), (N, SEG))
             for q in range(D_RWKV // SEG)]
    return jnp.concatenate(parts, axis=0)


def _split2(x):
    hi = x.astype(BF16)
    return hi, (x - hi.astype(F32)).astype(BF16)


def _col_sum(x):
    return jnp.concatenate([jnp.sum(x[N * q:N * (q + 1)], axis=0, keepdims=True) for q in range(D_RWKV // SEG)], axis=1)


def _scan_specs(b, t, rev):
    nc = t // SCAN_CHUNK
    if rev:
        return (pl.BlockSpec((b, SCAN_CHUNK, D_RWKV), lambda c: (0, nc - 1 - c, 0)),
                pl.BlockSpec((b, SCAN_CHUNK, SROWS, SEG), lambda c: (0, nc - 1 - c, 0, 0)))
    return (pl.BlockSpec((b, SCAN_CHUNK, D_RWKV), lambda c: (0, c, 0)),
            pl.BlockSpec((b, SCAN_CHUNK, SROWS, SEG), lambda c: (0, c, 0, 0)))


def _scan_fwd_call(r, v, kk, wf, kf, qf, wb, kb, qb, ride):
    b, t, _ = r.shape
    n_chunks = t // SCAN_CHUNK
    last = SCAN_CHUNK - 1
    nr = len(ride)

    def body(rf, vf, kkf, wf_, kf_, qf_, rb, vb, kkb, wb_, kb_, qb_, *rest):
        (yf, yb, sf, sb), scratch = rest[nr:nr + 4], rest[2 * nr + 4:]
        states = scratch[:2 * b]
        send, arrive = _direct_exchange(rest[:nr], rest[nr + 4:2 * nr + 4], *scratch[2 * b:], per_peer=False)

        @pl.when(pl.program_id(0) == 0)
        def _():
            send()
            for st in states:
                st[...] = jnp.zeros_like(st)

        ones, mask = _seg_ones(), _eye_mask()
        zero16 = jnp.zeros((), BF16)
        chains = []
        for bi in range(b):
            chains.append((rf, vf, kkf, wf_, kf_, qf_, yf, sf, states[2 * bi], bi, False))
            chains.append((rb, vb, kkb, wb_, kb_, qb_, yb, sb, states[2 * bi + 1], bi, True))

        def tix(i, rev):
            return last - i if rev else i

        def put_y(y_, bi, ti, ycol):
            y_[bi, pl.ds(ti, 1), :] = _col_sum(jnp.where(mask, ycol, 0.0))

        def steps(i, with_y):
            u_parts, v_parts, y_parts = [], [], []
            for (r_, v_, kk_, w_, k_, q_, y_, s_, st, bi, rev) in chains:
                ti = tix(i, rev)
                s = st[...]
                s_[bi, ti] = s
                u_parts.append((s * _row2(kk_, bi, ti)).astype(BF16))
                v_parts.append(jnp.where(mask, _row2(v_, bi, ti, BF16), zero16))
                if with_y:
                    y_parts.append((s * _row2(r_, bi, tix(i - 1, rev))).astype(BF16))
            seg = lambda parts: jnp.dot(jnp.concatenate(parts, axis=0), ones, preferred_element_type=F32)
            res_v, res_u = seg(v_parts), seg(u_parts)
            res_y = seg(y_parts) if with_y else None
            for n, (r_, v_, kk_, w_, k_, q_, y_, s_, st, bi, rev) in enumerate(chains):
                ti = tix(i, rev)
                u, vcol = res_u[n * SROWS:(n + 1) * SROWS], res_v[n * SROWS:(n + 1) * SROWS]
                if with_y:
                    put_y(y_, bi, tix(i - 1, rev), res_y[n * SROWS:(n + 1) * SROWS])
                st[...] = st[...] * _row2(w_, bi, ti) - u * _row2(q_, bi, ti) + vcol * _row2(k_, bi, ti)

        steps(0, False)

        def loop(i, carry):
            steps(i, True)
            return carry

        lax.fori_loop(1, SCAN_CHUNK, loop, 0, unroll=5)
        parts = [(c[8][...] * _row2(c[0], c[9], tix(last, c[10]))).astype(BF16) for c in chains]
        res = jnp.dot(jnp.concatenate(parts, axis=0), ones, preferred_element_type=F32)
        for n, c in enumerate(chains):
            put_y(c[6], c[9], tix(last, c[10]), res[n * SROWS:(n + 1) * SROWS])

        @pl.when(pl.program_id(0) == n_chunks - 1)
        def _():
            arrive()

    fr, fs = _scan_specs(b, t, False)
    br, bs = _scan_specs(b, t, True)
    y_shape = jax.ShapeDtypeStruct((b, t, D_RWKV), F32)
    s_shape = jax.ShapeDtypeStruct((b, t, SROWS, SEG), F32)
    return pl.pallas_call(
        body, name='scan_fwd', grid=(n_chunks,),
        in_specs=[fr] * 6 + [br] * 6 + [HBM] * nr, out_specs=[fr, br, fs, bs] + [HBM] * nr,
        out_shape=[y_shape, y_shape, s_shape, s_shape] + [jax.ShapeDtypeStruct((N_DEV,) + a.shape, a.dtype) for a in ride],
        scratch_shapes=[pltpu.VMEM((SROWS, SEG), F32)] * (2 * b) + _comm_sems(nr),
        compiler_params=_params("arbitrary"),
    )(r, v, kk, wf, kf, qf, r, v, kk, wb, kb, qb, *ride)


def _scan_bwd_call(r, v, kk, wf, kf, qf, wb, kb, qb, sf, sb, dyf, dyb, ride):
    b, t, _ = r.shape
    n_chunks = t // SCAN_CHUNK
    last = SCAN_CHUNK - 1
    nr = len(ride)

    def body(rf, vf, kkf, wf_, kf_, qf_, sf_, dyf_, rb, vb, kkb, wb_, kb_, qb_, sb_, dyb_, *rest):
        drf, dvf, dkkf, dwf, dkf, dqf, drb, dvb, dkkb, dwb, dkb, dqb = rest[nr:nr + 12]
        scratch = rest[2 * nr + 12:]
        send, arrive = _direct_exchange(rest[:nr], rest[nr + 12:2 * nr + 12], *scratch[8 * b:], per_peer=True)

        @pl.when(pl.program_id(0) == 0)
        def _():
            send()
            for n in range(2 * b):
                scratch[4 * n][...] = jnp.zeros_like(scratch[4 * n])

        ones, mask = _seg_ones(), _eye_mask()
        zero16 = jnp.zeros((), BF16)
        chains = []
        for bi in range(b):
            chains.append((rf, vf, kkf, wf_, kf_, qf_, sf_, dyf_, (drf, dvf, dkkf, dwf, dkf, dqf),
                           scratch[8 * bi:8 * bi + 4], bi, True))
            chains.append((rb, vb, kkb, wb_, kb_, qb_, sb_, dyb_, (drb, dvb, dkkb, dwb, dkb, dqb),
                           scratch[8 * bi + 4:8 * bi + 8], bi, False))

        def tix(i, rev):
            return last - i if rev else i

        def state_free_parts(v_, dy_, kk_, s_, bi, ti):
            return [jnp.where(mask, _row2(v_, bi, ti, BF16), zero16), jnp.where(mask, _row2(dy_, bi, ti, BF16), zero16),
                    (s_[bi, ti] * _row2(kk_, bi, ti)).astype(BF16)]

        def keep(scr, res, off):
            for n in range(3):
                scr[1 + n][...] = res[off + n * SROWS:off + (n + 1) * SROWS]
            return off + 3 * SROWS

        def first():
            parts = []
            for (r_, v_, kk_, w_, k_, q_, s_, dy_, outs, scr, bi, rev) in chains:
                parts += state_free_parts(v_, dy_, kk_, s_, bi, tix(0, rev))
            res = jnp.dot(jnp.concatenate(parts, axis=0), ones, preferred_element_type=F32)
            off = 0
            for c in chains:
                off = keep(c[9], res, off)

        def steps(i, has_next, recompute):
            parts = []
            for (r_, v_, kk_, w_, k_, q_, s_, dy_, outs, scr, bi, rev) in chains:
                ti = tix(i, rev)
                gst, vc, dc, uc = scr
                dycol = dc[...]
                if recompute:
                    sc = s_[bi, ti] * _row2(w_, bi, ti) - uc[...] * _row2(q_, bi, ti) + vc[...] * _row2(k_, bi, ti)
                else:
                    sc = s_[bi, tix(i - 1, rev)]
                outs[0][bi, pl.ds(ti, 1), :] = _col_sum(sc * dycol)
                g = gst[...] + dycol * _row2(r_, bi, ti)
                gst[...] = g
                parts.append((g * _row2(q_, bi, ti)).astype(BF16))
                parts.append((g * _row2(k_, bi, ti)).astype(BF16))
                if has_next:
                    parts += state_free_parts(v_, dy_, kk_, s_, bi, tix(i + 1, rev))
            res = jnp.dot(jnp.concatenate(parts, axis=0), ones, preferred_element_type=F32)
            off = 0
            for (r_, v_, kk_, w_, k_, q_, s_, dy_, outs, scr, bi, rev) in chains:
                ti = tix(i, rev)
                gst, vc, dc, uc = scr
                dr_, dv_, dkk_, dw_, dk_, dq_ = outs

                def put(ref, val, sign=1.0):
                    ref[bi, pl.ds(ti, 1), :] = sign * _col_sum(val)

                gq = res[off:off + SROWS]
                put(dv_, jnp.where(mask, res[off + SROWS:off + 2 * SROWS], 0.0))
                off += 2 * SROWS
                g, sp = gst[...], s_[bi, ti]
                put(dk_, g * vc[...])
                put(dw_, g * sp)
                put(dq_, g * uc[...], -1.0)
                put(dkk_, sp * gq, -1.0)
                gst[...] = g * _row2(w_, bi, ti) - gq * _row2(kk_, bi, ti)
                if has_next:
                    off = keep(scr, res, off)

        first()
        steps(0, True, True)

        def loop(i, carry):
            steps(i, True, False)
            return carry

        lax.fori_loop(1, last, loop, 0)
        steps(last, False, False)

        @pl.when(pl.program_id(0) == n_chunks - 1)
        def _():
            arrive()

    fr, fs = _scan_specs(b, t, True)
    br, bs = _scan_specs(b, t, False)
    y_shape = jax.ShapeDtypeStruct((b, t, D_RWKV), F32)
    return pl.pallas_call(
        body, name='scan_bwd', grid=(n_chunks,),
        in_specs=[fr] * 6 + [fs, fr] + [br] * 6 + [bs, br] + [HBM] * nr,
        out_specs=[fr] * 6 + [br] * 6 + [HBM] * nr,
        out_shape=[y_shape] * 12 + [jax.ShapeDtypeStruct(a.shape, a.dtype) for a in ride],
        scratch_shapes=[pltpu.VMEM((SROWS, SEG), F32)] * (8 * b) + _comm_sems(nr),
        compiler_params=_params("arbitrary"),
    )(r, v, kk, wf, kf, qf, sf, dyf, r, v, kk, wb, kb, qb, sb, dyb, *ride)


def _rope_tables(t):
    half = D_ROPE // 2
    inv_freq = jnp.power(ROPE_THETA, -jnp.arange(0, D_ROPE, 2, dtype=F32) / D_ROPE)
    ang = jnp.arange(t, dtype=F32)[:, None] * inv_freq[None, :]
    cos, sin, zero = jnp.cos(ang), jnp.sin(ang), jnp.zeros((t, half), F32)
    tail = HEAD_LANES - D_QK
    c = jnp.concatenate([jnp.ones((t, D_NOPE), F32), cos, cos, jnp.ones((t, tail), F32)], axis=1)
    s1 = jnp.concatenate([jnp.zeros((t, D_NOPE), F32), -sin, zero, jnp.zeros((t, tail), F32)], axis=1)
    s2 = jnp.concatenate([jnp.zeros((t, D_NOPE), F32), zero, sin, jnp.zeros((t, tail), F32)], axis=1)
    return c, s1, s2


@jax.custom_vjp
def _dot16(a, w):
    return jnp.dot(a.astype(BF16), w.astype(BF16), preferred_element_type=F32)


def _dot16_fwd(a, w):
    a16, w16 = a.astype(BF16), w.astype(BF16)
    return jnp.dot(a16, w16, preferred_element_type=F32), (a16, w16)


def _dot16_bwd(res, g):
    a16, w16 = res
    g16 = g.astype(BF16)
    return (lax.dot_general(g16, w16, (((1,), (1,)), ((), ())), preferred_element_type=F32),
            lax.dot_general(a16, g16, (((0,), (0,)), ((), ())), preferred_element_type=F32))


_dot16.defvjp(_dot16_fwd, _dot16_bwd)


def _head_sum_tile(x):
    outs = []
    ones = _seg_ones()
    for q in range(x.shape[1] // SEG):
        hi, lo = _split2(x[:, SEG * q:SEG * (q + 1)])
        outs.append(jnp.dot(hi, ones, preferred_element_type=F32) + jnp.dot(lo, ones, preferred_element_type=F32))
    return jnp.concatenate(outs, axis=1)


@jax.custom_vjp
def _hsum(x):
    return _head_sum_tile(x)


_hsum.defvjp(lambda x: (_head_sum_tile(x), None), lambda _, g: (_head_sum_tile(g),))


def _softplus(x):
    return jnp.maximum(x, 0.0) + jnp.log(1.0 + jnp.exp(-jnp.abs(x)))


def _rwkv_pre_fn(k, wdf, wdb, adf, adb, gd, w0f, w2f, w0b, w2b, a0f, a2f, a0b, a2b, g2, k_k, k_a):
    w_f = jnp.exp(-jnp.exp(-_softplus(-(w0f + _dot16(jnp.tanh(wdf), w2f))) - 0.5))
    w_b = jnp.exp(-jnp.exp(-_softplus(-(w0b + _dot16(jnp.tanh(wdb), w2b))) - 0.5))
    a_f = jax.nn.sigmoid(a0f + _dot16(adf, a2f))
    a_b = jax.nn.sigmoid(a0b + _dot16(adb, a2b))
    gate = _dot16(jax.nn.sigmoid(gd), g2)
    kk = k * k_k
    kk = kk / jnp.maximum(jnp.sqrt(_hsum(kk * kk)), L2_EPS)
    return (kk, w_f, k * (1.0 + (a_f - 1.0) * k_a), kk * a_f, w_b, k * (1.0 + (a_b - 1.0) * k_a), kk * a_b, gate)


def _rwkv_post_fn(y_f, y_b, r, k_f, k_b, v, gate, ln_g, ln_b, r_k):
    y = y_f + y_b
    yc = y - _hsum(y) * (1.0 / N)
    var = _hsum(yc * yc) * (1.0 / N)
    y = yc * lax.rsqrt(var + GN_EPS) * ln_g + ln_b
    return ((y + _hsum(r * (k_f + k_b) * r_k) * v) * gate,)


def _make_rowwise(fn, name, n_rows, tm):
    def specs(arrs, whole):
        if whole:
            return [pl.BlockSpec(a.shape, lambda i: (0, 0)) for a in arrs]
        return [pl.BlockSpec((tm, a.shape[1]), lambda i: (i, 0)) for a in arrs]

    def out_widths(rows, params):
        tiles = [jax.ShapeDtypeStruct((tm, a.shape[1]), F32) for a in rows]
        return [o.shape[1] for o in jax.eval_shape(fn, *tiles, *params)]

    def fwd_call(rows, params):
        m = rows[0].shape[0]
        n_in = len(rows) + len(params)
        outs = [jax.ShapeDtypeStruct((m, d), F32) for d in out_widths(rows, params)]

        def body(*refs):
            for o_ref, o in zip(refs[n_in:], fn(*[ref[...] for ref in refs[:n_in]])):
                o_ref[...] = o

        return pl.pallas_call(
            body, name=name + '_fwd', grid=(m // tm,), in_specs=specs(rows, False) + specs(params, True),
            out_specs=specs(outs, False), out_shape=outs, compiler_params=_params("parallel"),
        )(*rows, *params)

    def bwd_call(rows, params, cts):
        m = rows[0].shape[0]
        n_in = len(rows) + len(params)
        n_all = n_in + len(cts)
        outs = ([jax.ShapeDtypeStruct(a.shape, F32) for a in rows] + [jax.ShapeDtypeStruct(a.shape, F32) for a in params])

        def body(*refs):
            _, vjp = jax.vjp(fn, *[ref[...] for ref in refs[:n_in]])
            grads = vjp(tuple(ref[...] for ref in refs[n_in:n_all]))
            d_rows, d_params = refs[n_all:n_all + len(rows)], refs[n_all + len(rows):]
            for ref, g in zip(d_rows, grads[:len(rows)]):
                ref[...] = g

            @pl.when(pl.program_id(0) == 0)
            def _():
                for ref in d_params:
                    ref[...] = jnp.zeros_like(ref)

            for ref, g in zip(d_params, grads[len(rows):]):
                ref[...] += g

        return pl.pallas_call(
            body, name=name + '_bwd', grid=(m // tm,),
            in_specs=specs(rows, False) + specs(params, True) + specs(cts, False),
            out_specs=specs(rows, False) + specs(params, True), out_shape=outs, compiler_params=_params("arbitrary"),
        )(*rows, *params, *cts)

    @jax.custom_vjp
    def op(*args):
        return tuple(fwd_call(args[:n_rows], args[n_rows:]))

    def op_fwd(*args):
        return tuple(fwd_call(args[:n_rows], args[n_rows:])), args

    def op_bwd(args, cts):
        return tuple(bwd_call(args[:n_rows], args[n_rows:], cts))

    op.defvjp(op_fwd, op_bwd)
    return op


def _rwkv_operands(z, full, rep):
    b, t, _ = z.shape
    m = b * t
    z = _token_shift(z, rep['shift_mu_prev'], rep['shift_mu_next']).reshape(m, RWKV_COLS)
    cols, at = [], 0
    for width in RWKV_SPLITS:
        cols.append(z[:, at:at + width])
        at += width
    r, k, v, *lora_in = cols
    kk, w_f, k_f, q_f, w_b, k_b, q_b, gate = _make_rowwise(_rwkv_pre_fn, 'rwkv_pre', 6, _tile(m, ROW_TILE))(
        k, *lora_in, rep['decay_w0_fwd'], full['decay_w2_fwd'], rep['decay_w0_bwd'], full['decay_w2_bwd'],
        rep['iclr_a0_fwd'], full['iclr_a2_fwd'], rep['iclr_a0_bwd'], full['iclr_a2_bwd'], full['gate_g2'],
        rep['k_k'], rep['k_a'])
    return r, v, kk, w_f, k_f, q_f, w_b, k_b, q_b, gate


def _mla_mixer(z, full, rep, b, t):
    m = b * t
    c_q, c_kv, k_rope = z[:, :Q_LORA], z[:, Q_LORA:Q_LORA + KV_LORA], z[:, Q_LORA + KV_LORA:]
    w_uq = jnp.pad(full['w_uq'].reshape(Q_LORA, H, D_QK), ((0, 0), (0, 0), (0, HEAD_LANES - D_QK))).reshape(Q_LORA, H * HEAD_LANES)
    q = _make_mm('mm_uq')(_make_rms('rms_q')(c_q, rep['q_norm_g']), w_uq)
    kv = _make_mm('mm_ukv')(_make_rms('rms_kv')(c_kv, rep['kv_norm_g']), full['w_ukv'])
    kr = jnp.pad(k_rope, ((0, 0), (D_NOPE, HEAD_LANES - D_QK)))
    o = _attention(q.reshape(b, t, -1), kv.reshape(b, t, -1), kr.reshape(b, t, HEAD_LANES), _rope_tables(t))
    return _make_rms('rms_mla_out')(o.reshape(m, H * D_V), rep['mla_out_g'])


def _before_scan(full, rep, x):
    b, t, d = x.shape
    m = b * t
    n1 = _make_rms('rms_mix')(x.reshape(m, d), rep['ln_mix_g'])
    w_in = full['w_in']
    mla_cols = w_in.shape[1] - RWKV_COLS
    w_mla = jnp.pad(w_in[:, RWKV_COLS:], ((0, 0), (0, -mla_cols % LANES)))
    z_rwkv, z_mla = _make_mm_pair('mm_in')(n1, w_in[:, :RWKV_COLS], w_mla)
    return (*_rwkv_operands(z_rwkv.reshape(b, t, RWKV_COLS), full, rep),
            _mla_mixer(z_mla[:, :mla_cols], full, rep, b, t))


def _after_scan(full, rep, x, target, y_f, y_b, r, k_f, k_b, v, gate, y_mla):
    b, t, d = x.shape
    m = b * t
    xf = x.reshape(m, d)
    y_rwkv = _make_rowwise(_rwkv_post_fn, 'rwkv_post', 7, _tile(m, ROW_TILE))(
        y_f.reshape(m, D_RWKV), y_b.reshape(m, D_RWKV), r, k_f, k_b, v, gate,
        rep['ln_x_g'], rep['ln_x_b'], rep['r_k'].reshape(1, D_RWKV))[0]
    w_out = full['w_out']
    h = _make_mm_add('mm_out_rwkv')(xf, y_rwkv, w_out[:D_RWKV])
    h = _make_mm_add('mm_out_mla')(h, y_mla, w_out[D_RWKV:])
    n2, h = _make_rms_skip('rms_ffn')(h, rep['ln_ffn_g'])
    w_up, cw, cb = full['w_ffn_up'], full['ffn_conv_w'], rep['ffn_conv_b']
    u_gate, u_val = _make_mm_pair('mm_up')(n2, w_up[:, :D_FF], w_up[:, D_FF:])
    act = _conv_glu(u_gate.reshape(b, t, D_FF), u_val.reshape(b, t, D_FF),
                    cw[:, :D_FF], cw[:, D_FF:], cb[:, :D_FF], cb[:, D_FF:]).reshape(m, D_FF)
    h = _make_mm_add('mm_down')(h, act, full['w_ffn_down'])
    return _final_loss(h, rep['ln_final_g'], target.reshape(m, d))


def _mat(a):
    if a.ndim == 1:
        return a.reshape(1, -1)
    if a.ndim == 3:
        return a.reshape(a.shape[1:])
    return a


def _join(shards, name):
    if name in ROW:
        return shards.reshape(-1, shards.shape[-1])
    return shards.transpose(1, 0, 2).reshape(shards.shape[1], -1)


def _cut(whole, name):
    r, c = whole.shape
    if name in ROW:
        return whole.reshape(N_DEV, r // N_DEV, c)
    return whole.reshape(r, N_DEV, c // N_DEV).transpose(1, 0, 2)


def kernel(x, ln_mix_g, w_in, shift_mu_prev, shift_mu_next, decay_w0_fwd, decay_w2_fwd, decay_w0_bwd, decay_w2_bwd, iclr_a0_fwd, iclr_a2_fwd, iclr_a0_bwd, iclr_a2_bwd, gate_g2, k_k, k_a, r_k, ln_x_g, ln_x_b, q_norm_g, w_uq, kv_norm_g, w_ukv, mla_out_g, w_out, ln_ffn_g, w_ffn_up, ffn_conv_w, ffn_conv_b, w_ffn_down, ln_final_g, loss_target, m_ln_mix_g, m_w_in, m_shift_mu_prev, m_shift_mu_next, m_decay_w0_fwd, m_decay_w2_fwd, m_decay_w0_bwd, m_decay_w2_bwd, m_iclr_a0_fwd, m_iclr_a2_fwd, m_iclr_a0_bwd, m_iclr_a2_bwd, m_gate_g2, m_k_k, m_k_a, m_r_k, m_ln_x_g, m_ln_x_b, m_q_norm_g, m_w_uq, m_kv_norm_g, m_w_ukv, m_mla_out_g, m_w_out, m_ln_ffn_g, m_w_ffn_up, m_ffn_conv_w, m_ffn_conv_b, m_w_ffn_down, m_ln_final_g, v_ln_mix_g, v_w_in, v_shift_mu_prev, v_shift_mu_next, v_decay_w0_fwd, v_decay_w2_fwd, v_decay_w0_bwd, v_decay_w2_bwd, v_iclr_a0_fwd, v_iclr_a2_fwd, v_iclr_a0_bwd, v_iclr_a2_bwd, v_gate_g2, v_k_k, v_k_a, v_r_k, v_ln_x_g, v_ln_x_b, v_q_norm_g, v_w_uq, v_kv_norm_g, v_w_ukv, v_mla_out_g, v_w_out, v_ln_ffn_g, v_w_ffn_up, v_ffn_conv_w, v_ffn_conv_b, v_w_ffn_down, v_ln_final_g):
    given = dict(locals())
    w = {n: given[n] for n in WNAMES}
    mom = {n: given['m_' + n] for n in WNAMES}
    var = {n: given['v_' + n] for n in WNAMES}

    def split(names):
        return [n for n in names if n in BIG], [n for n in names if n not in BIG]

    def wire(names):
        big, small = split(names)
        pack = _pack([lax.bitcast_convert_type(_mat(w[n]), BF16) if n in EXACT else _mat(w[n]).astype(BF16) for n in small])
        return [_mat(w[n]).astype(BF16) for n in big] + [pack]

    def whole(names, gathered):
        big, small = split(names)
        out = {n: _join(g, n) for n, g in zip(big, gathered)}
        shapes = [_mat(w[n]).shape + ((2,) if n in EXACT else ()) for n in small]
        for n, s in zip(small, _unpack(gathered[-1], shapes, lead=1)):
            out[n] = _join(lax.bitcast_convert_type(s, F32) if n in EXACT else s.astype(F32), n)
        return out

    def grad_wire(names, grads):
        big, small = split(names)
        return [_cut(grads[n], n) for n in big] + [_pack([_cut(grads[n], n).astype(BF16) for n in small], lead=1)]

    early = [n for n in SHARDED if n not in LATE]
    rep = {n: _mat(w[n]) for n in REPLICATED}
    rep['r_k'] = w['r_k'].reshape(H, N)
    b, t, d = x.shape
    seq = lambda a: a.reshape(b, t, D_RWKV)
    flat = lambda a: a.reshape(b * t, D_RWKV)

    full_early = whole(early, _all_gather(wire(early), 'gather_weights'))
    ops, vjp_before = jax.vjp(_before_scan, full_early, rep, x)
    r, v, kk, w_f, k_f, q_f, w_b, k_b, q_b, gate, y_mla = ops
    scan_in = [seq(a) for a in (r, v, kk, w_f, k_f, q_f, w_b, k_b, q_b)]
    y_f, y_b, s_f, s_b, *late_gathered = _scan_fwd_call(*scan_in, wire(LATE))
    full_late = whole(LATE, late_gathered)
    loss_local, vjp_after = jax.vjp(_after_scan, full_late, rep, x, loss_target, y_f, y_b, r, k_f, k_b, v, gate, y_mla)

    g_late, g_rep_after, g_x_after, _, d_yf, d_yb, d_r, d_kf, d_kb, d_v, d_gate, d_ymla = vjp_after(jnp.ones((), F32))
    scan_out = _scan_bwd_call(*scan_in, s_f, s_b, d_yf, d_yb, grad_wire(LATE, g_late))
    parts_late = scan_out[12:]
    drf, dvf, dkkf, dwf, dkf, dqf, drb, dvb, dkkb, dwb, dkb, dqb = [flat(a) for a in scan_out[:12]]
    g_early, g_rep_before, g_x_before = vjp_before(
        (drf + drb + d_r, dvf + dvb + d_v, dkkf + dkkb, dwf, dkf + d_kf, dqf, dwb, dkb + d_kb, dqb, d_gate, d_ymla))
    g_rep = {n: g_rep_before[n] + g_rep_after[n] for n in rep}
    g_x = g_x_before + g_x_after
    parts_early = _grad_exchange(grad_wire(early, g_early), 'exchange_grads')

    s_out = [{}, {}, {}, {}]
    for names, parts, tag in ((early, parts_early, 'early'), (LATE, parts_late, 'late')):
        big, small = split(names)
        for n, p in zip(big, parts):
            res = _sum_adamw(p, _mat(w[n]), _mat(mom[n]), _mat(var[n]), 'adamw_' + n)
            for kind, o in enumerate(res):
                s_out[kind][n] = o.reshape(w[n].shape)
        res = _sum_adamw(parts[-1], _pack([w[n] for n in small]), _pack([mom[n] for n in small]),
                         _pack([var[n] for n in small]), 'adamw_small_' + tag)
        for kind, o in enumerate(res):
            s_out[kind].update(zip(small, _unpack(o, [w[n].shape for n in small])))

    zero = jnp.zeros((1,), F32)
    r_pack = _pack([g_rep[n] for n in REPLICATED] + [loss_local.reshape(1)])
    r_parts = _all_gather([r_pack], 'gather_small')[0]
    r_out = _sum_adamw(r_parts,_pack([w[n] for n in REPLICATED] + [zero]), _pack([mom[n] for n in REPLICATED] + [zero]),
                       _pack([var[n] for n in REPLICATED] + [zero]), 'adamw_replicated')
    r_out = [_unpack(o, [w[n].shape for n in REPLICATED] + [(1,)]) for o in r_out]

    loss = r_out[0][-1].reshape(())
    outs = [loss, g_x]
    for kind in range(4):
        by_name = dict(s_out[kind])
        by_name.update(zip(REPLICATED, r_out[kind][:-1]))
        outs += [by_name[n] for n in WNAMES]
    return tuple(outs)
```

```python
import functools

import jax
import jax.numpy as jnp
from jax import lax
from jax.experimental import pallas as pl
from jax.experimental.pallas import tpu as pltpu

F32 = jnp.float32
BF16 = jnp.bfloat16
MESH = pl.DeviceIdType.MESH

N_DEV = 8
LANES = 128
SUBLANES = 8
PACK_TILE = 2 * SUBLANES * LANES
PACK_ROWS = 512
ADAM_ROWS = 256
MM_TILE = 512
MM_TILE_WIDE = 1408
MM_K_WHOLE = 2816
VMEM_LIMIT = 56 * 1024 * 1024

H = 8
N = 64
D_RWKV = H * N
D_NOPE, D_ROPE, D_V = 64, 32, 64
D_QK = D_NOPE + D_ROPE
MLA_SCALE = D_QK ** -0.5
ROPE_THETA = 10000.0
RWKV_SPLITS = (D_RWKV, D_RWKV, D_RWKV, 64, 64, 64, 64, 128)
RWKV_COLS = sum(RWKV_SPLITS)
Q_LORA, KV_LORA = 768, 256
D_FF = 2816
NORM_EPS = 1e-6
GN_EPS = 64e-5
L2_EPS = 1e-12
ADAM_LR, ADAM_B1, ADAM_B2, ADAM_EPS, ADAM_WD, ADAM_STEP = 0.001, 0.9, 0.999, 1e-08, 0.01, 10

SCAN_CHUNK = 32
ATT_TQ = 256
SEG = 256
FFN_COLS = 256
ROW_TILE = 256
SHIFT_COLS = 384

WNAMES = ['ln_mix_g', 'w_in', 'shift_mu_prev', 'shift_mu_next', 'decay_w0_fwd', 'decay_w2_fwd', 'decay_w0_bwd',
          'decay_w2_bwd', 'iclr_a0_fwd', 'iclr_a2_fwd', 'iclr_a0_bwd', 'iclr_a2_bwd', 'gate_g2', 'k_k', 'k_a', 'r_k',
          'ln_x_g', 'ln_x_b', 'q_norm_g', 'w_uq', 'kv_norm_g', 'w_ukv', 'mla_out_g', 'w_out', 'ln_ffn_g', 'w_ffn_up',
          'ffn_conv_w', 'ffn_conv_b', 'w_ffn_down', 'ln_final_g']
COL = ('w_in', 'decay_w2_fwd', 'decay_w2_bwd', 'iclr_a2_fwd', 'iclr_a2_bwd', 'gate_g2', 'w_ukv', 'w_ffn_up', 'ffn_conv_w')
ROW = ('w_uq', 'w_out', 'w_ffn_down')
SHARDED = [n for n in WNAMES if n in COL or n in ROW]
REPLICATED = [n for n in WNAMES if n not in SHARDED]
EXACT = ('ffn_conv_w',)
LATE = ['w_out', 'w_ffn_up', 'ffn_conv_w', 'w_ffn_down']
BIG = ('w_in', 'w_uq', 'w_ukv', 'w_out', 'w_ffn_up', 'w_ffn_down')


def _params(*sem):
    return pltpu.CompilerParams(dimension_semantics=sem, vmem_limit_bytes=VMEM_LIMIT)


def _pack(arrs, lead=0):
    parts = []
    for a in arrs:
        head = a.shape[:lead]
        flat = a.reshape(head + (-1,))
        n = flat.shape[-1]
        n_pad = -(-n // PACK_TILE) * PACK_TILE
        flat = jnp.pad(flat, [(0, 0)] * lead + [(0, n_pad - n)])
        parts.append(flat.reshape(head + (n_pad // LANES, LANES)))
    out = jnp.concatenate(parts, axis=lead)
    rows = out.shape[lead]
    rows_pad = -(-rows // PACK_ROWS) * PACK_ROWS
    return jnp.pad(out, [(0, 0)] * lead + [(0, rows_pad - rows), (0, 0)])


def _unpack(packed, shapes, lead=0):
    outs, row = [], 0
    head = packed.shape[:lead]
    for shp in shapes:
        n = 1
        for s in shp:
            n *= s
        rows = -(-n // PACK_TILE) * (PACK_TILE // LANES)
        blk = lax.slice_in_dim(packed, row, row + rows, axis=lead)
        flat = blk.reshape(head + (rows * LANES,))
        outs.append(lax.slice_in_dim(flat, 0, n, axis=lead).reshape(head + tuple(shp)))
        row += rows
    return outs


PEERS = N_DEV - 1
HBM = pl.BlockSpec(memory_space=pl.ANY)


def _comm_sems(n):
    return [pltpu.SemaphoreType.DMA((PEERS * n,)), pltpu.SemaphoreType.DMA((PEERS * n,)), pltpu.SemaphoreType.DMA((n,))]


def _all_gather(xs, name):
    n = len(xs)

    def body(*refs):
        x_refs, out_refs, (send_sems, recv_sems, local_sems) = refs[:n], refs[n:2 * n], refs[2 * n:]
        mx, my, mc = lax.axis_index("x"), lax.axis_index("y"), lax.axis_index("c")
        me, sibling = (mx, my, mc), (mx, my, 1 - mc)
        chips = [(1 - mx, my), (mx, 1 - my), (1 - mx, 1 - my)]

        def slot(a, px, py, pc):
            return out_refs[a].at[4 * px + 2 * py + pc]

        def copy(a, k, block, to, src=None):
            return pltpu.make_async_remote_copy(
                src_ref=slot(a, *block) if src is None else src, dst_ref=slot(a, *block),
                send_sem=send_sems.at[PEERS * a + k], recv_sem=recv_sems.at[PEERS * a + k],
                device_id=to, device_id_type=MESH)

        mine = [pltpu.make_async_copy(x_refs[a], slot(a, *me), local_sems.at[a]) for a in range(n)]
        first, passed = [], []
        for a in range(n):
            mine[a].start()
            first.append(copy(a, 0, me, sibling, src=x_refs[a]))
            first += [copy(a, 1 + j, me, (*chip, mc), src=x_refs[a]) for j, chip in enumerate(chips)]
        for cp in first:
            cp.start()
        for j, chip in enumerate(chips):
            for a in range(n):
                copy(a, 1 + j, (*chip, mc), me).wait_recv()
                passed.append(copy(a, 4 + j, (*chip, mc), sibling))
                passed[-1].start()
        for a in range(n):
            copy(a, 0, sibling, me).wait_recv()
            for j, chip in enumerate(chips):
                copy(a, 4 + j, (*chip, 1 - mc), me).wait_recv()
        for cp in first + passed:
            cp.wait_send()
        for cp in mine:
            cp.wait()

    return pl.pallas_call(
        body, name=name, out_shape=[jax.ShapeDtypeStruct((N_DEV,) + x.shape, x.dtype) for x in xs],
        in_specs=[HBM] * n, out_specs=[HBM] * n, scratch_shapes=_comm_sems(n),
    )(*xs)


def _direct_exchange(src_refs, out_refs, send_sems, recv_sems, local_sems, per_peer):
    mx, my, mc = lax.axis_index("x"), lax.axis_index("y"), lax.axis_index("c")
    me = 4 * mx + 2 * my + mc

    def flip(v, bit):
        return 1 - v if bit else v

    def copies():
        mine, remote = [], []
        for a, (src, out) in enumerate(zip(src_refs, out_refs)):
            mine.append(pltpu.make_async_copy(src.at[me] if per_peer else src, out.at[me], local_sems.at[a]))
            for k in range(1, N_DEV):
                px, py, pc = flip(mx, k & 4), flip(my, k & 2), flip(mc, k & 1)
                remote.append(pltpu.make_async_remote_copy(
                    src_ref=src.at[4 * px + 2 * py + pc] if per_peer else src, dst_ref=out.at[me],
                    send_sem=send_sems.at[PEERS * a + k - 1], recv_sem=recv_sems.at[PEERS * a + k - 1],
                    device_id=(px, py, pc), device_id_type=MESH))
        return mine, remote

    def start():
        mine, remote = copies()
        for cp in mine + remote:
            cp.start()

    def wait():
        mine, remote = copies()
        for cp in remote:
            cp.wait_recv()
        for cp in remote:
            cp.wait_send()
        for cp in mine:
            cp.wait()

    return start, wait


def _grad_exchange(gs, name):
    n = len(gs)

    def body(*refs):
        start, wait = _direct_exchange(refs[:n], refs[n:2 * n], *refs[2 * n:], per_peer=True)
        start()
        wait()

    return pl.pallas_call(
        body, name=name, out_shape=[jax.ShapeDtypeStruct(g.shape, g.dtype) for g in gs],
        in_specs=[HBM] * n, out_specs=[HBM] * n, scratch_shapes=_comm_sems(n),
    )(*gs)


def _sum_adamw(parts, w, m, v, name):
    rows, cols = w.shape
    tr = next((t for t in range(ADAM_ROWS, 15, -16) if rows % t == 0), rows)
    c1 = 1.0 - ADAM_B1 ** ADAM_STEP
    c2 = 1.0 - ADAM_B2 ** ADAM_STEP

    def body(p_ref, w_ref, m_ref, v_ref, g_out, d_out, m_out, v_out):
        g = p_ref[0].astype(F32)
        for q in range(1, N_DEV):
            g = g + p_ref[q].astype(F32)
        m_new = ADAM_B1 * m_ref[...] + (1.0 - ADAM_B1) * g
        v_new = ADAM_B2 * v_ref[...] + (1.0 - ADAM_B2) * (g * g)
        m_hat = m_new / c1
        v_hat = v_new / c2
        g_out[...] = g
        d_out[...] = -ADAM_LR * (m_hat / (jnp.sqrt(v_hat) + ADAM_EPS) + ADAM_WD * w_ref[...])
        m_out[...] = m_new
        v_out[...] = v_new

    blk = pl.BlockSpec((tr, cols), lambda i: (i, 0))
    out = jax.ShapeDtypeStruct((rows, cols), F32)
    return pl.pallas_call(
        body, name=name, grid=(rows // tr,),
        in_specs=[pl.BlockSpec((N_DEV, tr, cols), lambda i: (0, i, 0)), blk, blk, blk],
        out_specs=[blk, blk, blk, blk], out_shape=[out, out, out, out],
        compiler_params=_params("parallel"),
    )(parts, w, m, v)


def _tile(dim, cap=MM_TILE):
    if dim <= cap:
        return dim
    for t in range(cap, LANES - 1, -LANES):
        if dim % t == 0:
            return t
    return dim


def _mm_call(a, b, form, name, out_dtype=F32, base=None):
    if form == 'nn':
        (m, k), n = a.shape, b.shape[1]
    elif form == 'nt':
        (m, k), n = a.shape, b.shape[0]
    else:
        (k, m), n = a.shape, b.shape[1]
    tk = k if (form == 'nn' and k <= MM_K_WHOLE) else _tile(k, MM_TILE_WIDE)
    tm = _tile(m, MM_TILE_WIDE if form == 'tn' else MM_TILE)
    tn = _tile(n, MM_TILE_WIDE)
    nk = k // tk
    contract = {'nn': ((1,), (0,)), 'nt': ((1,), (1,)), 'tn': ((0,), (0,))}[form]

    acc_in_out = nk == 1 or out_dtype == F32
    assert base is None or out_dtype == F32
    extra = [] if base is None else [base]

    def body(a_ref, b_ref, *rest):
        o_ref, acc = rest[len(extra)], rest[len(extra) + 1:]
        part = lax.dot_general(a_ref[...].astype(BF16), b_ref[...].astype(BF16), (contract, ((), ())),
                               preferred_element_type=F32)
        if nk == 1:
            o_ref[...] = (part + rest[0][...] if extra else part).astype(out_dtype)
            return
        acc_ref = o_ref if acc_in_out else acc[0]

        @pl.when(pl.program_id(2) == 0)
        def _():
            acc_ref[...] = part + rest[0][...] if extra else part

        @pl.when(pl.program_id(2) > 0)
        def _():
            acc_ref[...] += part

        if not acc_in_out:
            @pl.when(pl.program_id(2) == nk - 1)
            def _():
                o_ref[...] = acc_ref[...].astype(out_dtype)

    a_spec = pl.BlockSpec((tk, tm), lambda j, i, l: (l, i)) if form == 'tn' else pl.BlockSpec((tm, tk), lambda j, i, l: (i, l))
    b_spec = pl.BlockSpec((tn, tk), lambda j, i, l: (j, l)) if form == 'nt' else pl.BlockSpec((tk, tn), lambda j, i, l: (l, j))
    o_spec = pl.BlockSpec((tm, tn), lambda j, i, l: (i, j))
    return pl.pallas_call(
        body, name=name, grid=(n // tn, m // tm, nk),
        in_specs=[a_spec, b_spec] + [o_spec] * len(extra), out_specs=o_spec,
        out_shape=jax.ShapeDtypeStruct((m, n), out_dtype),
        scratch_shapes=[] if acc_in_out else [pltpu.VMEM((tm, tn), F32)],
        compiler_params=_params("parallel", "parallel", "arbitrary"),
    )(a, b, *extra)


def _make_mm(name):
    @jax.custom_vjp
    def mm(a, b):
        return _mm_call(a, b, 'nn', name + '_fwd')

    def fwd(a, b):
        return _mm_call(a, b, 'nn', name + '_fwd'), (a, b)

    def bwd(res, g):
        a, b = res
        return _mm_call(g, b, 'nt', name + '_da'), _mm_call(a, g, 'tn', name + '_db', out_dtype=BF16)

    mm.defvjp(fwd, bwd)
    return mm


def _make_mm_pair(name):
    def both(a, b1, b2):
        return _mm_call(a, b1, 'nn', name + '_1_fwd'), _mm_call(a, b2, 'nn', name + '_2_fwd')

    mm = jax.custom_vjp(both)

    def fwd(a, b1, b2):
        return both(a, b1, b2), (a, b1, b2)

    def bwd(res, gs):
        a, b1, b2 = res
        da = _mm_call(gs[1], b2, 'nt', name + '_2_da', base=_mm_call(gs[0], b1, 'nt', name + '_1_da'))
        return (da, _mm_call(a, gs[0], 'tn', name + '_1_db', out_dtype=BF16),
                _mm_call(a, gs[1], 'tn', name + '_2_db', out_dtype=BF16))

    mm.defvjp(fwd, bwd)
    return mm


def _make_mm_add(name):
    @jax.custom_vjp
    def mm(base, a, b):
        return _mm_call(a, b, 'nn', name + '_fwd', base=base)

    def fwd(base, a, b):
        return _mm_call(a, b, 'nn', name + '_fwd', base=base), (a, b)

    def bwd(res, g):
        a, b = res
        return g, _mm_call(g, b, 'nt', name + '_da'), _mm_call(a, g, 'tn', name + '_db', out_dtype=BF16)

    mm.defvjp(fwd, bwd)
    return mm


def _rms_fwd_call(x, g, name):
    m, d = x.shape
    tm = _tile(m)

    def body(x_ref, g_ref, o_ref):
        xv = x_ref[...]
        rinv = lax.rsqrt(jnp.mean(xv * xv, axis=-1, keepdims=True) + NORM_EPS)
        o_ref[...] = xv * rinv * g_ref[...]

    return pl.pallas_call(
        body, name=name, grid=(m // tm,),
        in_specs=[pl.BlockSpec((tm, d), lambda i: (i, 0)), pl.BlockSpec((1, d), lambda i: (0, 0))],
        out_specs=pl.BlockSpec((tm, d), lambda i: (i, 0)), out_shape=jax.ShapeDtypeStruct((m, d), F32),
        compiler_params=_params("parallel"),
    )(x, g)


def _rms_bwd_call(x, g, dy, name, d_skip=None):
    m, d = x.shape
    tm = _tile(m)
    extra = [] if d_skip is None else [d_skip]

    def body(x_ref, g_ref, dy_ref, *rest):
        dx_ref, dg_ref = rest[len(extra):]

        @pl.when(pl.program_id(0) == 0)
        def _():
            dg_ref[...] = jnp.zeros_like(dg_ref)

        xv, dyv = x_ref[...], dy_ref[...]
        rinv = lax.rsqrt(jnp.mean(xv * xv, axis=-1, keepdims=True) + NORM_EPS)
        xh = xv * rinv
        dg_ref[...] += jnp.sum(dyv * xh, axis=0, keepdims=True)
        dxh = dyv * g_ref[...]
        dx = rinv * (dxh - xh * jnp.mean(dxh * xh, axis=-1, keepdims=True))
        dx_ref[...] = dx + rest[0][...] if extra else dx

    row = pl.BlockSpec((tm, d), lambda i: (i, 0))
    vec = pl.BlockSpec((1, d), lambda i: (0, 0))
    return pl.pallas_call(
        body, name=name, grid=(m // tm,), in_specs=[row, vec, row] + [row] * len(extra), out_specs=[row, vec],
        out_shape=[jax.ShapeDtypeStruct((m, d), F32), jax.ShapeDtypeStruct((1, d), F32)],
        compiler_params=_params("arbitrary"),
    )(x, g, dy, *extra)


def _make_rms(name):
    @jax.custom_vjp
    def rms(x, g):
        return _rms_fwd_call(x, g, name + '_fwd')

    def fwd(x, g):
        return _rms_fwd_call(x, g, name + '_fwd'), (x, g)

    def bwd(res, dy):
        x, g = res
        dx, dg = _rms_bwd_call(x, g, dy, name + '_bwd')
        return dx, dg

    rms.defvjp(fwd, bwd)
    return rms


def _final_loss_call(x, g, target, ct=None):
    m, d = x.shape
    tm = _tile(m)
    row = pl.BlockSpec((tm, d), lambda i: (i, 0))
    vec = pl.BlockSpec((1, d), lambda i: (0, 0))
    acc = pl.BlockSpec((1, LANES), lambda i: (0, 0))

    def normed(x_ref, g_ref):
        xv = x_ref[...]
        rinv = lax.rsqrt(jnp.mean(xv * xv, axis=-1, keepdims=True) + NORM_EPS)
        return rinv, xv * rinv

    def fwd_body(x_ref, g_ref, t_ref, loss_ref):
        @pl.when(pl.program_id(0) == 0)
        def _():
            loss_ref[...] = jnp.zeros_like(loss_ref)

        _, xh = normed(x_ref, g_ref)
        err = xh * g_ref[...] - t_ref[...]
        loss_ref[...] += 0.5 * jnp.sum(jnp.mean(err * err, axis=-1, keepdims=True), axis=0, keepdims=True)

    def bwd_body(x_ref, g_ref, t_ref, ct_ref, dx_ref, dg_ref):
        @pl.when(pl.program_id(0) == 0)
        def _():
            dg_ref[...] = jnp.zeros_like(dg_ref)

        rinv, xh = normed(x_ref, g_ref)
        dyv = (xh * g_ref[...] - t_ref[...]) * (ct_ref[0:1, 0:1] * (1.0 / d))
        dg_ref[...] += jnp.sum(dyv * xh, axis=0, keepdims=True)
        dxh = dyv * g_ref[...]
        dx_ref[...] = rinv * (dxh - xh * jnp.mean(dxh * xh, axis=-1, keepdims=True))

    if ct is None:
        return pl.pallas_call(
            fwd_body, name='final_loss_fwd', grid=(m // tm,), in_specs=[row, vec, row], out_specs=acc,
            out_shape=jax.ShapeDtypeStruct((1, LANES), F32), compiler_params=_params("arbitrary"),
        )(x, g, target)
    return pl.pallas_call(
        bwd_body, name='final_loss_bwd', grid=(m // tm,), in_specs=[row, vec, row, acc], out_specs=[row, vec],
        out_shape=[jax.ShapeDtypeStruct((m, d), F32), jax.ShapeDtypeStruct((1, d), F32)],
        compiler_params=_params("arbitrary"),
    )(x, g, target, ct)


@jax.custom_vjp
def _final_loss(x, g, target):
    return _final_loss_call(x, g, target)[0, 0]


def _final_loss_bwd(res, ct):
    x, g, target = res
    dx, dg = _final_loss_call(x, g, target, ct=jnp.full((1, LANES), ct, F32))
    return dx, dg, jnp.zeros_like(target)


_final_loss.defvjp(lambda x, g, target: (_final_loss_call(x, g, target)[0, 0], (x, g, target)), _final_loss_bwd)


def _make_rms_skip(name):
    @jax.custom_vjp
    def rms(x, g):
        return _rms_fwd_call(x, g, name + '_fwd'), x

    def fwd(x, g):
        return (_rms_fwd_call(x, g, name + '_fwd'), x), (x, g)

    def bwd(res, cts):
        x, g = res
        dx, dg = _rms_bwd_call(x, g, cts[0], name + '_bwd', d_skip=cts[1])
        return dx, dg

    rms.defvjp(fwd, bwd)
    return rms


def _time_shifts(x):
    t = x.shape[0]
    rows = lax.broadcasted_iota(jnp.int32, x.shape, 0)
    return (jnp.where(rows == 0, 0.0, pltpu.roll(x, 1, 0)), jnp.where(rows == t - 1, 0.0, pltpu.roll(x, t - 1, 0)))


def _conv3(x, cw_ref, cb_ref):
    xp, xn = _time_shifts(x)
    return cw_ref[0:1, :] * xp + cw_ref[1:2, :] * x + cw_ref[2:3, :] * xn + cb_ref[...]


def _glu_specs(b, t, f):
    tc = _tile(f, FFN_COLS)
    seq = pl.BlockSpec((1, t, tc), lambda j, bi: (bi, 0, j))
    cw = pl.BlockSpec((3, tc), lambda j, bi: (0, j))
    cb = pl.BlockSpec((1, tc), lambda j, bi: (0, j))
    return tc, seq, cw, cb


def _glu_fwd_call(ug, uv, cwg, cwv, cbg, cbv):
    b, t, f = ug.shape
    tc, seq, cw, cb = _glu_specs(b, t, f)

    def body(ug_ref, uv_ref, cwg_ref, cwv_ref, cbg_ref, cbv_ref, o_ref):
        g = _conv3(ug_ref[0], cwg_ref, cbg_ref)
        o_ref[0] = g * jax.nn.sigmoid(g) * _conv3(uv_ref[0], cwv_ref, cbv_ref)

    return pl.pallas_call(
        body, name='glu_fwd', grid=(f // tc, b), in_specs=[seq, seq, cw, cw, cb, cb], out_specs=seq,
        out_shape=jax.ShapeDtypeStruct((b, t, f), F32), compiler_params=_params("parallel", "parallel"),
    )(ug, uv, cwg, cwv, cbg, cbv)


def _glu_bwd_call(ug, uv, cwg, cwv, cbg, cbv, dact):
    b, t, f = ug.shape
    tc, seq, cw, cb = _glu_specs(b, t, f)

    def body(ug_ref, uv_ref, cwg_ref, cwv_ref, cbg_ref, cbv_ref, da_ref,
             dug_ref, duv_ref, dcwg_ref, dcwv_ref, dcbg_ref, dcbv_ref):
        @pl.when(pl.program_id(1) == 0)
        def _():
            for ref in (dcwg_ref, dcwv_ref, dcbg_ref, dcbv_ref):
                ref[...] = jnp.zeros_like(ref)

        g = _conv3(ug_ref[0], cwg_ref, cbg_ref)
        v = _conv3(uv_ref[0], cwv_ref, cbv_ref)
        sig = jax.nn.sigmoid(g)
        da = da_ref[0]
        dv = da * (g * sig)
        dg = da * v * (sig * (1.0 + g * (1.0 - sig)))

        def conv_bwd(dc, x_ref, cw_ref, dx_ref, dcw_ref, dcb_ref):
            dcp, dcn = _time_shifts(dc)
            dx_ref[0] = cw_ref[0:1, :] * dcn + cw_ref[1:2, :] * dc + cw_ref[2:3, :] * dcp
            x = x_ref[0]
            for n, ds in enumerate((dcn, dc, dcp)):
                dcw_ref[n:n + 1, :] += jnp.sum(ds * x, axis=0, keepdims=True)
            dcb_ref[...] += jnp.sum(dc, axis=0, keepdims=True)

        conv_bwd(dg, ug_ref, cwg_ref, dug_ref, dcwg_ref, dcbg_ref)
        conv_bwd(dv, uv_ref, cwv_ref, duv_ref, dcwv_ref, dcbv_ref)

    big = jax.ShapeDtypeStruct((b, t, f), F32)
    return pl.pallas_call(
        body, name='glu_bwd', grid=(f // tc, b), in_specs=[seq, seq, cw, cw, cb, cb, seq],
        out_specs=[seq, seq, cw, cw, cb, cb],
        out_shape=[big, big, jax.ShapeDtypeStruct((3, f), F32), jax.ShapeDtypeStruct((3, f), F32),
                   jax.ShapeDtypeStruct((1, f), F32), jax.ShapeDtypeStruct((1, f), F32)],
        compiler_params=_params("parallel", "arbitrary"),
    )(ug, uv, cwg, cwv, cbg, cbv, dact)


def _shift_call(z, mu_p, mu_n, dzs=None):
    b, t, c = z.shape
    tc = _tile(c, SHIFT_COLS)
    seq = pl.BlockSpec((1, t, tc), lambda j, bi: (bi, 0, j))
    row = pl.BlockSpec((1, tc), lambda j, bi: (0, j))

    def fwd_body(z_ref, mp_ref, mn_ref, o_ref):
        x = z_ref[0]
        xp, xn = _time_shifts(x)
        o_ref[0] = x + mp_ref[...] * (xp - x) + mn_ref[...] * (xn - x)

    def bwd_body(z_ref, mp_ref, mn_ref, d_ref, dz_ref, dmp_ref, dmn_ref):
        @pl.when(pl.program_id(1) == 0)
        def _():
            dmp_ref[...] = jnp.zeros_like(dmp_ref)
            dmn_ref[...] = jnp.zeros_like(dmn_ref)

        x, d = z_ref[0], d_ref[0]
        xp, xn = _time_shifts(x)
        dp, dn = _time_shifts(d)
        mp, mn = mp_ref[...], mn_ref[...]
        dz_ref[0] = d * (1.0 - mp - mn) + mp * dn + mn * dp
        dmp_ref[...] += jnp.sum(d * (xp - x), axis=0, keepdims=True)
        dmn_ref[...] += jnp.sum(d * (xn - x), axis=0, keepdims=True)

    if dzs is None:
        return pl.pallas_call(
            fwd_body, name='shift_fwd', grid=(c // tc, b), in_specs=[seq, row, row], out_specs=seq,
            out_shape=jax.ShapeDtypeStruct(z.shape, F32), compiler_params=_params("parallel", "parallel"),
        )(z, mu_p, mu_n)
    return pl.pallas_call(
        bwd_body, name='shift_bwd', grid=(c // tc, b), in_specs=[seq, row, row, seq], out_specs=[seq, row, row],
        out_shape=[jax.ShapeDtypeStruct(z.shape, F32), jax.ShapeDtypeStruct(mu_p.shape, F32),
                   jax.ShapeDtypeStruct(mu_n.shape, F32)],
        compiler_params=_params("parallel", "arbitrary"),
    )(z, mu_p, mu_n, dzs)


@jax.custom_vjp
def _token_shift(z, mu_p, mu_n):
    return _shift_call(z, mu_p, mu_n)


_token_shift.defvjp(lambda z, mu_p, mu_n: (_shift_call(z, mu_p, mu_n), (z, mu_p, mu_n)),
                    lambda res, d: tuple(_shift_call(*res, dzs=d)))


@jax.custom_vjp
def _conv_glu(ug, uv, cwg, cwv, cbg, cbv):
    return _glu_fwd_call(ug, uv, cwg, cwv, cbg, cbv)


def _conv_glu_fwd(*args):
    return _glu_fwd_call(*args), args


def _conv_glu_bwd(res, dact):
    return tuple(_glu_bwd_call(*res, dact))


_conv_glu.defvjp(_conv_glu_fwd, _conv_glu_bwd)


HEAD_LANES = 2 * D_NOPE
PAIR = 2


def _lane(shape):
    return lax.broadcasted_iota(jnp.int32, shape, len(shape) - 1)


def _rope(x, c, s1, s2):
    return x * c + pltpu.roll(x, HEAD_LANES - D_ROPE // 2, 1) * s1 + pltpu.roll(x, D_ROPE // 2, 1) * s2


def _rope_t(g, c, s1, s2):
    return g * c + pltpu.roll(g * s1, D_ROPE // 2, 1) + pltpu.roll(g * s2, HEAD_LANES - D_ROPE // 2, 1)


def _attn_setup(kv_ref, kr_ref, tabs, k2, v16):
    c, s1, s2 = (tb[...] for tb in tabs)
    krr = _rope(kr_ref[0], c, s1, s2)
    v16[...] = kv_ref[0].astype(BF16)
    for hh in range(PAIR):
        slab = kv_ref[0, :, HEAD_LANES * hh:HEAD_LANES * (hh + 1)]
        k2[hh] = jnp.where(_lane(slab.shape) < D_NOPE, slab, krr).astype(BF16)


def _attn_queries(q_ref, tabs, rows, hh):
    c, s1, s2 = (tb[rows, :] for tb in tabs)
    return (_rope(q_ref[0, :, HEAD_LANES * hh:HEAD_LANES * (hh + 1)], c, s1, s2) * MLA_SCALE).astype(BF16), (c, s1, s2)


def _attn_specs(b, t, tq):
    qspec = pl.BlockSpec((1, tq, PAIR * HEAD_LANES), lambda bi, p, i: (bi, i, p))
    kvspec = pl.BlockSpec((1, t, PAIR * HEAD_LANES), lambda bi, p, i: (bi, 0, p))
    krspec = pl.BlockSpec((1, t, HEAD_LANES), lambda bi, p, i: (bi, 0, 0))
    tab = pl.BlockSpec((t, HEAD_LANES), lambda bi, p, i: (0, 0))
    ospec = pl.BlockSpec((1, tq, PAIR * D_V), lambda bi, p, i: (bi, i, p))
    lspec = pl.BlockSpec((1, 1, tq, HEAD_LANES), lambda bi, p, i: (bi, p, i, 0))
    return qspec, kvspec, krspec, tab, ospec, lspec


def _attn_fwd_call(q, kv, kr, tabs):
    b, t, _ = q.shape
    tq = min(ATT_TQ, t)
    qspec, kvspec, krspec, tab, ospec, lspec = _attn_specs(b, t, tq)

    def body(q_ref, kv_ref, kr_ref, c_ref, s1_ref, s2_ref, o_ref, lse_ref, k2, v16):
        tabs = (c_ref, s1_ref, s2_ref)

        @pl.when(pl.program_id(2) == 0)
        def _():
            _attn_setup(kv_ref, kr_ref, tabs, k2, v16)

        rows = pl.ds(pl.multiple_of(pl.program_id(2) * tq, tq), tq)
        outs, lses = [], []
        for hh in range(PAIR):
            qh, _ = _attn_queries(q_ref, tabs, rows, hh)
            s = lax.dot_general(qh, k2[hh], (((1,), (1,)), ((), ())), preferred_element_type=F32)
            m = jnp.max(s, axis=-1, keepdims=True)
            p = jnp.exp(s - m)
            l = jnp.sum(p, axis=-1, keepdims=True)
            slab16 = v16[:, HEAD_LANES * hh:HEAD_LANES * (hh + 1)]
            outs.append(jnp.dot(p.astype(BF16), slab16, preferred_element_type=F32) / l)
            lses.append(m + jnp.log(l))
        low = _lane(outs[0].shape) < D_V
        o_ref[0] = jnp.where(low, pltpu.roll(outs[0], D_V, 1), outs[1])
        lse_ref[0, 0] = jnp.where(low, lses[0], lses[1])

    return pl.pallas_call(
        body, name='attn_fwd', grid=(b, H // PAIR, t // tq),
        in_specs=[qspec, kvspec, krspec, tab, tab, tab], out_specs=[ospec, lspec],
        out_shape=[jax.ShapeDtypeStruct((b, t, H * D_V), F32), jax.ShapeDtypeStruct((b, H // PAIR, t, HEAD_LANES), F32)],
        scratch_shapes=[pltpu.VMEM((PAIR, t, HEAD_LANES), BF16), pltpu.VMEM((t, PAIR * HEAD_LANES), BF16)],
        compiler_params=_params("parallel", "parallel", "arbitrary"),
    )(q, kv, kr, *tabs)


def _attn_bwd_call(q, kv, kr, tabs, o, lse, do):
    b, t, _ = q.shape
    tq = min(ATT_TQ, t)
    n_q = t // tq
    qspec, kvspec, krspec, tab, ospec, lspec = _attn_specs(b, t, tq)

    def body(q_ref, kv_ref, kr_ref, c_ref, s1_ref, s2_ref, o_ref, lse_ref, do_ref, dq_ref, dkv_ref, dkr_ref, k2, v16, dk2):
        tabs = (c_ref, s1_ref, s2_ref)
        pair, step = pl.program_id(1), pl.program_id(2)

        @pl.when(step == 0)
        def _():
            _attn_setup(kv_ref, kr_ref, tabs, k2, v16)
            dk2[...] = jnp.zeros_like(dk2)
            dkv_ref[...] = jnp.zeros_like(dkv_ref)

        @pl.when((step == 0) & (pair == 0))
        def _():
            dkr_ref[...] = jnp.zeros_like(dkr_ref)

        rows = pl.ds(pl.multiple_of(step * tq, tq), tq)
        dov, ov = do_ref[0], o_ref[0]
        lane = _lane(dov.shape)
        upper = lane >= D_V
        for hh in range(PAIR):
            qh, qtabs = _attn_queries(q_ref, tabs, rows, hh)
            s = lax.dot_general(qh, k2[hh], (((1,), (1,)), ((), ())), preferred_element_type=F32)
            p = jnp.exp(s - lse_ref[0, 0, :, D_V * hh:D_V * hh + 1])
            mine = upper if hh else ~upper
            delta = jnp.sum(jnp.where(mine, dov * ov, 0.0), axis=-1, keepdims=True)
            do_h = jnp.where(upper, dov if hh else pltpu.roll(dov, D_V, 1), 0.0).astype(BF16)
            slab16 = v16[:, HEAD_LANES * hh:HEAD_LANES * (hh + 1)]
            dp = lax.dot_general(do_h, slab16, (((1,), (1,)), ((), ())), preferred_element_type=F32)
            ds = (p * (dp - delta)).astype(BF16)
            dqh = jnp.dot(ds, k2[hh], preferred_element_type=F32) * MLA_SCALE
            dq_ref[0, :, HEAD_LANES * hh:HEAD_LANES * (hh + 1)] = _rope_t(dqh, *qtabs)
            dk2[hh] += lax.dot_general(ds, qh, (((0,), (0,)), ((), ())), preferred_element_type=F32)
            dkv_ref[0, :, HEAD_LANES * hh:HEAD_LANES * (hh + 1)] += lax.dot_general(
                p.astype(BF16), do_h, (((0,), (0,)), ((), ())), preferred_element_type=F32)

        @pl.when(step == n_q - 1)
        def _():
            c, s1, s2 = (tb[...] for tb in tabs)
            for hh in range(PAIR):
                g = dk2[hh]
                key_lane = _lane(g.shape)
                dkv_ref[0, :, HEAD_LANES * hh:HEAD_LANES * (hh + 1)] += jnp.where(key_lane < D_NOPE, g, 0.0)
                dkr_ref[0] += _rope_t(jnp.where(key_lane >= D_NOPE, g, 0.0), c, s1, s2)

    return pl.pallas_call(
        body, name='attn_bwd', grid=(b, H // PAIR, n_q),
        in_specs=[qspec, kvspec, krspec, tab, tab, tab, ospec, lspec, ospec], out_specs=[qspec, kvspec, krspec],
        out_shape=[jax.ShapeDtypeStruct(q.shape, F32), jax.ShapeDtypeStruct(kv.shape, F32), jax.ShapeDtypeStruct(kr.shape, F32)],
        scratch_shapes=[pltpu.VMEM((PAIR, t, HEAD_LANES), BF16), pltpu.VMEM((t, PAIR * HEAD_LANES), BF16),
                        pltpu.VMEM((PAIR, t, HEAD_LANES), F32)],
        compiler_params=_params("parallel", "arbitrary", "arbitrary"),
    )(q, kv, kr, *tabs, o, lse, do)


@jax.custom_vjp
def _attention(q, kv, kr, tabs):
    return _attn_fwd_call(q, kv, kr, tabs)[0]


def _attention_fwd(q, kv, kr, tabs):
    o, lse = _attn_fwd_call(q, kv, kr, tabs)
    return o, (q, kv, kr, tabs, o, lse)


def _attention_bwd(res, do):
    q, kv, kr, tabs, o, lse = res
    return (*_attn_bwd_call(q, kv, kr, tabs, o, lse, do), tuple(jnp.zeros_like(tb) for tb in tabs))


_attention.defvjp(_attention_fwd, _attention_bwd)


SROWS = N * D_RWKV // SEG


def _seg_ones():
    r = lax.broadcasted_iota(jnp.int32, (SEG, SEG), 0) // N
    c = lax.broadcasted_iota(jnp.int32, (SEG, SEG), 1) // N
    return (r == c).astype(BF16)


def _eye_mask():
    r = lax.broadcasted_iota(jnp.int32, (SROWS, SEG), 0) & (N - 1)
    c = lax.broadcasted_iota(jnp.int32, (SROWS, SEG), 1) & (N - 1)
    return r == c


def _row2(ref, bi, ti, dtype=F32):
    parts = [jnp.broadcast_to(ref[bi, pl.ds(ti, 1), pl.ds(SEG * q, SEG)].astype(dtype), (N, SEG))
             for q in range(D_RWKV // SEG)]
    return jnp.concatenate(parts, axis=0)


def _split2(x):
    hi = x.astype(BF16)
    return hi, (x - hi.astype(F32)).astype(BF16)


def _col_sum(x):
    return jnp.concatenate([jnp.sum(x[N * q:N * (q + 1)], axis=0, keepdims=True) for q in range(D_RWKV // SEG)], axis=1)


def _diag_row(col, mask):
    halves = []
    for g in range(D_RWKV // SEG):
        acc = jnp.zeros((SUBLANES, SEG), F32)
        for k in range(N // SUBLANES):
            rows = slice(N * g + SUBLANES * k, N * g + SUBLANES * (k + 1))
            acc = jnp.where(mask[rows], col[rows], acc)
        halves.append(jnp.sum(acc, axis=0, keepdims=True))
    return jnp.concatenate(halves, axis=1)


def _scan_specs(b, t, rev):
    nc = t // SCAN_CHUNK
    if rev:
        return (pl.BlockSpec((b, SCAN_CHUNK, D_RWKV), lambda c: (0, nc - 1 - c, 0)),
                pl.BlockSpec((b, SCAN_CHUNK, SROWS, SEG), lambda c: (0, nc - 1 - c, 0, 0)))
    return (pl.BlockSpec((b, SCAN_CHUNK, D_RWKV), lambda c: (0, c, 0)),
            pl.BlockSpec((b, SCAN_CHUNK, SROWS, SEG), lambda c: (0, c, 0, 0)))


def _scan_fwd_call(r, v, kk, wf, kf, qf, wb, kb, qb, ride):
    b, t, _ = r.shape
    n_chunks = t // SCAN_CHUNK
    last = SCAN_CHUNK - 1
    nr = len(ride)

    def body(rf, vf, kkf, wf_, kf_, qf_, rb, vb, kkb, wb_, kb_, qb_, *rest):
        (yf, yb, sf, sb), scratch = rest[nr:nr + 4], rest[2 * nr + 4:]
        states = scratch[:2 * b]
        send, arrive = _direct_exchange(rest[:nr], rest[nr + 4:2 * nr + 4], *scratch[2 * b:], per_peer=False)

        @pl.when(pl.program_id(0) == 0)
        def _():
            send()
            for st in states:
                st[...] = jnp.zeros_like(st)

        ones, mask = _seg_ones(), _eye_mask()
        zero16 = jnp.zeros((), BF16)
        chains = []
        for bi in range(b):
            chains.append((rf, vf, kkf, wf_, kf_, qf_, yf, sf, states[2 * bi], bi, False))
            chains.append((rb, vb, kkb, wb_, kb_, qb_, yb, sb, states[2 * bi + 1], bi, True))

        def tix(i, rev):
            return last - i if rev else i

        def put_y(y_, bi, ti, ycol):
            y_[bi, pl.ds(ti, 1), :] = _diag_row(ycol, mask)

        def steps(i, with_y):
            u_parts, v_parts, y_parts = [], [], []
            for (r_, v_, kk_, w_, k_, q_, y_, s_, st, bi, rev) in chains:
                ti = tix(i, rev)
                s = st[...]
                s_[bi, ti] = s
                u_parts.append((s * _row2(kk_, bi, ti)).astype(BF16))
                v_parts.append(jnp.where(mask, _row2(v_, bi, ti, BF16), zero16))
                if with_y:
                    y_parts.append((s * _row2(r_, bi, tix(i - 1, rev))).astype(BF16))
            seg = lambda parts: jnp.dot(jnp.concatenate(parts, axis=0), ones, preferred_element_type=F32)
            res_v, res_u = seg(v_parts), seg(u_parts)
            res_y = seg(y_parts) if with_y else None
            for n, (r_, v_, kk_, w_, k_, q_, y_, s_, st, bi, rev) in enumerate(chains):
                ti = tix(i, rev)
                u, vcol = res_u[n * SROWS:(n + 1) * SROWS], res_v[n * SROWS:(n + 1) * SROWS]
                if with_y:
                    put_y(y_, bi, tix(i - 1, rev), res_y[n * SROWS:(n + 1) * SROWS])
                st[...] = st[...] * _row2(w_, bi, ti) - u * _row2(q_, bi, ti) + vcol * _row2(k_, bi, ti)

        steps(0, False)

        def loop(i, carry):
            steps(i, True)
            return carry

        lax.fori_loop(1, SCAN_CHUNK, loop, 0, unroll=5)
        parts = [(c[8][...] * _row2(c[0], c[9], tix(last, c[10]))).astype(BF16) for c in chains]
        res = jnp.dot(jnp.concatenate(parts, axis=0), ones, preferred_element_type=F32)
        for n, c in enumerate(chains):
            put_y(c[6], c[9], tix(last, c[10]), res[n * SROWS:(n + 1) * SROWS])

        @pl.when(pl.program_id(0) == n_chunks - 1)
        def _():
            arrive()

    fr, fs = _scan_specs(b, t, False)
    br, bs = _scan_specs(b, t, True)
    y_shape = jax.ShapeDtypeStruct((b, t, D_RWKV), F32)
    s_shape = jax.ShapeDtypeStruct((b, t, SROWS, SEG), F32)
    return pl.pallas_call(
        body, name='scan_fwd', grid=(n_chunks,),
        in_specs=[fr] * 6 + [br] * 6 + [HBM] * nr, out_specs=[fr, br, fs, bs] + [HBM] * nr,
        out_shape=[y_shape, y_shape, s_shape, s_shape] + [jax.ShapeDtypeStruct((N_DEV,) + a.shape, a.dtype) for a in ride],
        scratch_shapes=[pltpu.VMEM((SROWS, SEG), F32)] * (2 * b) + _comm_sems(nr),
        compiler_params=_params("arbitrary"),
    )(r, v, kk, wf, kf, qf, r, v, kk, wb, kb, qb, *ride)


def _scan_bwd_call(r, v, kk, wf, kf, qf, wb, kb, qb, sf, sb, dyf, dyb, ride):
    b, t, _ = r.shape
    n_chunks = t // SCAN_CHUNK
    last = SCAN_CHUNK - 1
    nr = len(ride)

    def body(rf, vf, kkf, wf_, kf_, qf_, sf_, dyf_, rb, vb, kkb, wb_, kb_, qb_, sb_, dyb_, *rest):
        drf, dvf, dkkf, dwf, dkf, dqf, drb, dvb, dkkb, dwb, dkb, dqb = rest[nr:nr + 12]
        scratch = rest[2 * nr + 12:]
        send, arrive = _direct_exchange(rest[:nr], rest[nr + 12:2 * nr + 12], *scratch[8 * b:], per_peer=True)

        @pl.when(pl.program_id(0) == 0)
        def _():
            send()
            for n in range(2 * b):
                scratch[4 * n][...] = jnp.zeros_like(scratch[4 * n])

        ones, mask = _seg_ones(), _eye_mask()
        zero16 = jnp.zeros((), BF16)
        chains = []
        for bi in range(b):
            chains.append((rf, vf, kkf, wf_, kf_, qf_, sf_, dyf_, (drf, dvf, dkkf, dwf, dkf, dqf),
                           scratch[8 * bi:8 * bi + 4], bi, True))
            chains.append((rb, vb, kkb, wb_, kb_, qb_, sb_, dyb_, (drb, dvb, dkkb, dwb, dkb, dqb),
                           scratch[8 * bi + 4:8 * bi + 8], bi, False))

        def tix(i, rev):
            return last - i if rev else i

        def state_free_parts(v_, dy_, kk_, s_, bi, ti):
            return [jnp.where(mask, _row2(v_, bi, ti, BF16), zero16), jnp.where(mask, _row2(dy_, bi, ti, BF16), zero16),
                    (s_[bi, ti] * _row2(kk_, bi, ti)).astype(BF16)]

        def keep(scr, res, off):
            for n in range(3):
                scr[1 + n][...] = res[off + n * SROWS:off + (n + 1) * SROWS]
            return off + 3 * SROWS

        def first():
            parts = []
            for (r_, v_, kk_, w_, k_, q_, s_, dy_, outs, scr, bi, rev) in chains:
                parts += state_free_parts(v_, dy_, kk_, s_, bi, tix(0, rev))
            res = jnp.dot(jnp.concatenate(parts, axis=0), ones, preferred_element_type=F32)
            off = 0
            for c in chains:
                off = keep(c[9], res, off)

        def steps(i, has_next, recompute):
            parts = []
            for (r_, v_, kk_, w_, k_, q_, s_, dy_, outs, scr, bi, rev) in chains:
                ti = tix(i, rev)
                gst, vc, dc, uc = scr
                dycol = dc[...]
                if recompute:
                    sc = s_[bi, ti] * _row2(w_, bi, ti) - uc[...] * _row2(q_, bi, ti) + vc[...] * _row2(k_, bi, ti)
                else:
                    sc = s_[bi, tix(i - 1, rev)]
                outs[0][bi, pl.ds(ti, 1), :] = _col_sum(sc * dycol)
                g = gst[...] + dycol * _row2(r_, bi, ti)
                gst[...] = g
                parts.append((g * _row2(q_, bi, ti)).astype(BF16))
                parts.append((g * _row2(k_, bi, ti)).astype(BF16))
                if has_next:
                    parts += state_free_parts(v_, dy_, kk_, s_, bi, tix(i + 1, rev))
            res = jnp.dot(jnp.concatenate(parts, axis=0), ones, preferred_element_type=F32)
            off = 0
            for (r_, v_, kk_, w_, k_, q_, s_, dy_, outs, scr, bi, rev) in chains:
                ti = tix(i, rev)
                gst, vc, dc, uc = scr
                dr_, dv_, dkk_, dw_, dk_, dq_ = outs

                def put(ref, val, sign=1.0):
                    ref[bi, pl.ds(ti, 1), :] = sign * _col_sum(val)

                gq = res[off:off + SROWS]
                dv_[bi, pl.ds(ti, 1), :] = _diag_row(res[off + SROWS:off + 2 * SROWS], mask)
                off += 2 * SROWS
                g, sp = gst[...], s_[bi, ti]
                put(dk_, g * vc[...])
                put(dw_, g * sp)
                put(dq_, g * uc[...], -1.0)
                put(dkk_, sp * gq, -1.0)
                gst[...] = g * _row2(w_, bi, ti) - gq * _row2(kk_, bi, ti)
                if has_next:
                    off = keep(scr, res, off)

        first()
        steps(0, True, True)

        def loop(i, carry):
            steps(i, True, False)
            return carry

        lax.fori_loop(1, last, loop, 0)
        steps(last, False, False)

        @pl.when(pl.program_id(0) == n_chunks - 1)
        def _():
            arrive()

    fr, fs = _scan_specs(b, t, True)
    br, bs = _scan_specs(b, t, False)
    y_shape = jax.ShapeDtypeStruct((b, t, D_RWKV), F32)
    return pl.pallas_call(
        body, name='scan_bwd', grid=(n_chunks,),
        in_specs=[fr] * 6 + [fs, fr] + [br] * 6 + [bs, br] + [HBM] * nr,
        out_specs=[fr] * 6 + [br] * 6 + [HBM] * nr,
        out_shape=[y_shape] * 12 + [jax.ShapeDtypeStruct(a.shape, a.dtype) for a in ride],
        scratch_shapes=[pltpu.VMEM((SROWS, SEG), F32)] * (8 * b) + _comm_sems(nr),
        compiler_params=_params("arbitrary"),
    )(r, v, kk, wf, kf, qf, sf, dyf, r, v, kk, wb, kb, qb, sb, dyb, *ride)


def _rope_tables(t):
    half = D_ROPE // 2
    inv_freq = jnp.power(ROPE_THETA, -jnp.arange(0, D_ROPE, 2, dtype=F32) / D_ROPE)
    ang = jnp.arange(t, dtype=F32)[:, None] * inv_freq[None, :]
    cos, sin, zero = jnp.cos(ang), jnp.sin(ang), jnp.zeros((t, half), F32)
    tail = HEAD_LANES - D_QK
    c = jnp.concatenate([jnp.ones((t, D_NOPE), F32), cos, cos, jnp.ones((t, tail), F32)], axis=1)
    s1 = jnp.concatenate([jnp.zeros((t, D_NOPE), F32), -sin, zero, jnp.zeros((t, tail), F32)], axis=1)
    s2 = jnp.concatenate([jnp.zeros((t, D_NOPE), F32), zero, sin, jnp.zeros((t, tail), F32)], axis=1)
    return c, s1, s2


@jax.custom_vjp
def _dot16(a, w):
    return jnp.dot(a.astype(BF16), w.astype(BF16), preferred_element_type=F32)


def _dot16_fwd(a, w):
    a16, w16 = a.astype(BF16), w.astype(BF16)
    return jnp.dot(a16, w16, preferred_element_type=F32), (a16, w16)


def _dot16_bwd(res, g):
    a16, w16 = res
    g16 = g.astype(BF16)
    return (lax.dot_general(g16, w16, (((1,), (1,)), ((), ())), preferred_element_type=F32),
            lax.dot_general(a16, g16, (((0,), (0,)), ((), ())), preferred_element_type=F32))


_dot16.defvjp(_dot16_fwd, _dot16_bwd)


def _head_sum_tile(x):
    outs = []
    ones = _seg_ones()
    for q in range(x.shape[1] // SEG):
        hi, lo = _split2(x[:, SEG * q:SEG * (q + 1)])
        outs.append(jnp.dot(hi, ones, preferred_element_type=F32) + jnp.dot(lo, ones, preferred_element_type=F32))
    return jnp.concatenate(outs, axis=1)


@jax.custom_vjp
def _hsum(x):
    return _head_sum_tile(x)


_hsum.defvjp(lambda x: (_head_sum_tile(x), None), lambda _, g: (_head_sum_tile(g),))


def _softplus(x):
    return jnp.maximum(x, 0.0) + jnp.log(1.0 + jnp.exp(-jnp.abs(x)))


def _rwkv_pre_fn(k, wdf, wdb, adf, adb, gd, w0f, w2f, w0b, w2b, a0f, a2f, a0b, a2b, g2, k_k, k_a):
    w_f = jnp.exp(-jnp.exp(-_softplus(-(w0f + _dot16(jnp.tanh(wdf), w2f))) - 0.5))
    w_b = jnp.exp(-jnp.exp(-_softplus(-(w0b + _dot16(jnp.tanh(wdb), w2b))) - 0.5))
    a_f = jax.nn.sigmoid(a0f + _dot16(adf, a2f))
    a_b = jax.nn.sigmoid(a0b + _dot16(adb, a2b))
    gate = _dot16(jax.nn.sigmoid(gd), g2)
    kk = k * k_k
    kk = kk / jnp.maximum(jnp.sqrt(_hsum(kk * kk)), L2_EPS)
    return (kk, w_f, k * (1.0 + (a_f - 1.0) * k_a), kk * a_f, w_b, k * (1.0 + (a_b - 1.0) * k_a), kk * a_b, gate)


def _rwkv_post_fn(y_f, y_b, r, k_f, k_b, v, gate, ln_g, ln_b, r_k):
    y = y_f + y_b
    yc = y - _hsum(y) * (1.0 / N)
    var = _hsum(yc * yc) * (1.0 / N)
    y = yc * lax.rsqrt(var + GN_EPS) * ln_g + ln_b
    return ((y + _hsum(r * (k_f + k_b) * r_k) * v) * gate,)


def _make_rowwise(fn, name, n_rows, tm):
    def specs(arrs, whole):
        if whole:
            return [pl.BlockSpec(a.shape, lambda i: (0, 0)) for a in arrs]
        return [pl.BlockSpec((tm, a.shape[1]), lambda i: (i, 0)) for a in arrs]

    def out_widths(rows, params):
        tiles = [jax.ShapeDtypeStruct((tm, a.shape[1]), F32) for a in rows]
        return [o.shape[1] for o in jax.eval_shape(fn, *tiles, *params)]

    def fwd_call(rows, params):
        m = rows[0].shape[0]
        n_in = len(rows) + len(params)
        outs = [jax.ShapeDtypeStruct((m, d), F32) for d in out_widths(rows, params)]

        def body(*refs):
            for o_ref, o in zip(refs[n_in:], fn(*[ref[...] for ref in refs[:n_in]])):
                o_ref[...] = o

        return pl.pallas_call(
            body, name=name + '_fwd', grid=(m // tm,), in_specs=specs(rows, False) + specs(params, True),
            out_specs=specs(outs, False), out_shape=outs, compiler_params=_params("parallel"),
        )(*rows, *params)

    def bwd_call(rows, params, cts):
        m = rows[0].shape[0]
        n_in = len(rows) + len(params)
        n_all = n_in + len(cts)
        outs = ([jax.ShapeDtypeStruct(a.shape, F32) for a in rows] + [jax.ShapeDtypeStruct(a.shape, F32) for a in params])

        def body(*refs):
            _, vjp = jax.vjp(fn, *[ref[...] for ref in refs[:n_in]])
            grads = vjp(tuple(ref[...] for ref in refs[n_in:n_all]))
            d_rows, d_params = refs[n_all:n_all + len(rows)], refs[n_all + len(rows):]
            for ref, g in zip(d_rows, grads[:len(rows)]):
                ref[...] = g

            @pl.when(pl.program_id(0) == 0)
            def _():
                for ref in d_params:
                    ref[...] = jnp.zeros_like(ref)

            for ref, g in zip(d_params, grads[len(rows):]):
                ref[...] += g

        return pl.pallas_call(
            body, name=name + '_bwd', grid=(m // tm,),
            in_specs=specs(rows, False) + specs(params, True) + specs(cts, False),
            out_specs=specs(rows, False) + specs(params, True), out_shape=outs, compiler_params=_params("arbitrary"),
        )(*rows, *params, *cts)

    @jax.custom_vjp
    def op(*args):
        return tuple(fwd_call(args[:n_rows], args[n_rows:]))

    def op_fwd(*args):
        return tuple(fwd_call(args[:n_rows], args[n_rows:])), args

    def op_bwd(args, cts):
        return tuple(bwd_call(args[:n_rows], args[n_rows:], cts))

    op.defvjp(op_fwd, op_bwd)
    return op


def _rwkv_operands(z, full, rep):
    b, t, _ = z.shape
    m = b * t
    z = _token_shift(z, rep['shift_mu_prev'], rep['shift_mu_next']).reshape(m, RWKV_COLS)
    cols, at = [], 0
    for width in RWKV_SPLITS:
        cols.append(z[:, at:at + width])
        at += width
    r, k, v, *lora_in = cols
    kk, w_f, k_f, q_f, w_b, k_b, q_b, gate = _make_rowwise(_rwkv_pre_fn, 'rwkv_pre', 6, _tile(m, ROW_TILE))(
        k, *lora_in, rep['decay_w0_fwd'], full['decay_w2_fwd'], rep['decay_w0_bwd'], full['decay_w2_bwd'],
        rep['iclr_a0_fwd'], full['iclr_a2_fwd'], rep['iclr_a0_bwd'], full['iclr_a2_bwd'], full['gate_g2'],
        rep['k_k'], rep['k_a'])
    return r, v, kk, w_f, k_f, q_f, w_b, k_b, q_b, gate


def _mla_mixer(z, full, rep, b, t):
    m = b * t
    c_q, c_kv, k_rope = z[:, :Q_LORA], z[:, Q_LORA:Q_LORA + KV_LORA], z[:, Q_LORA + KV_LORA:]
    w_uq = jnp.pad(full['w_uq'].reshape(Q_LORA, H, D_QK), ((0, 0), (0, 0), (0, HEAD_LANES - D_QK))).reshape(Q_LORA, H * HEAD_LANES)
    q = _make_mm('mm_uq')(_make_rms('rms_q')(c_q, rep['q_norm_g']), w_uq)
    kv = _make_mm('mm_ukv')(_make_rms('rms_kv')(c_kv, rep['kv_norm_g']), full['w_ukv'])
    kr = jnp.pad(k_rope, ((0, 0), (D_NOPE, HEAD_LANES - D_QK)))
    o = _attention(q.reshape(b, t, -1), kv.reshape(b, t, -1), kr.reshape(b, t, HEAD_LANES), _rope_tables(t))
    return _make_rms('rms_mla_out')(o.reshape(m, H * D_V), rep['mla_out_g'])


def _before_scan(full, rep, x):
    b, t, d = x.shape
    m = b * t
    n1 = _make_rms('rms_mix')(x.reshape(m, d), rep['ln_mix_g'])
    w_in = full['w_in']
    mla_cols = w_in.shape[1] - RWKV_COLS
    w_mla = jnp.pad(w_in[:, RWKV_COLS:], ((0, 0), (0, -mla_cols % LANES)))
    z_rwkv, z_mla = _make_mm_pair('mm_in')(n1, w_in[:, :RWKV_COLS], w_mla)
    return (*_rwkv_operands(z_rwkv.reshape(b, t, RWKV_COLS), full, rep),
            _mla_mixer(z_mla[:, :mla_cols], full, rep, b, t))


def _after_scan(full, rep, x, target, y_f, y_b, r, k_f, k_b, v, gate, y_mla):
    b, t, d = x.shape
    m = b * t
    xf = x.reshape(m, d)
    y_rwkv = _make_rowwise(_rwkv_post_fn, 'rwkv_post', 7, _tile(m, ROW_TILE))(
        y_f.reshape(m, D_RWKV), y_b.reshape(m, D_RWKV), r, k_f, k_b, v, gate,
        rep['ln_x_g'], rep['ln_x_b'], rep['r_k'].reshape(1, D_RWKV))[0]
    w_out = full['w_out']
    h = _make_mm_add('mm_out_rwkv')(xf, y_rwkv, w_out[:D_RWKV])
    h = _make_mm_add('mm_out_mla')(h, y_mla, w_out[D_RWKV:])
    n2, h = _make_rms_skip('rms_ffn')(h, rep['ln_ffn_g'])
    w_up, cw, cb = full['w_ffn_up'], full['ffn_conv_w'], rep['ffn_conv_b']
    u_gate, u_val = _make_mm_pair('mm_up')(n2, w_up[:, :D_FF], w_up[:, D_FF:])
    act = _conv_glu(u_gate.reshape(b, t, D_FF), u_val.reshape(b, t, D_FF),
                    cw[:, :D_FF], cw[:, D_FF:], cb[:, :D_FF], cb[:, D_FF:]).reshape(m, D_FF)
    h = _make_mm_add('mm_down')(h, act, full['w_ffn_down'])
    return _final_loss(h, rep['ln_final_g'], target.reshape(m, d))


def _mat(a):
    if a.ndim == 1:
        return a.reshape(1, -1)
    if a.ndim == 3:
        return a.reshape(a.shape[1:])
    return a


def _join(shards, name):
    if name in ROW:
        return shards.reshape(-1, shards.shape[-1])
    return shards.transpose(1, 0, 2).reshape(shards.shape[1], -1)


def _cut(whole, name):
    r, c = whole.shape
    if name in ROW:
        return whole.reshape(N_DEV, r // N_DEV, c)
    return whole.reshape(r, N_DEV, c // N_DEV).transpose(1, 0, 2)


def kernel(x, ln_mix_g, w_in, shift_mu_prev, shift_mu_next, decay_w0_fwd, decay_w2_fwd, decay_w0_bwd, decay_w2_bwd, iclr_a0_fwd, iclr_a2_fwd, iclr_a0_bwd, iclr_a2_bwd, gate_g2, k_k, k_a, r_k, ln_x_g, ln_x_b, q_norm_g, w_uq, kv_norm_g, w_ukv, mla_out_g, w_out, ln_ffn_g, w_ffn_up, ffn_conv_w, ffn_conv_b, w_ffn_down, ln_final_g, loss_target, m_ln_mix_g, m_w_in, m_shift_mu_prev, m_shift_mu_next, m_decay_w0_fwd, m_decay_w2_fwd, m_decay_w0_bwd, m_decay_w2_bwd, m_iclr_a0_fwd, m_iclr_a2_fwd, m_iclr_a0_bwd, m_iclr_a2_bwd, m_gate_g2, m_k_k, m_k_a, m_r_k, m_ln_x_g, m_ln_x_b, m_q_norm_g, m_w_uq, m_kv_norm_g, m_w_ukv, m_mla_out_g, m_w_out, m_ln_ffn_g, m_w_ffn_up, m_ffn_conv_w, m_ffn_conv_b, m_w_ffn_down, m_ln_final_g, v_ln_mix_g, v_w_in, v_shift_mu_prev, v_shift_mu_next, v_decay_w0_fwd, v_decay_w2_fwd, v_decay_w0_bwd, v_decay_w2_bwd, v_iclr_a0_fwd, v_iclr_a2_fwd, v_iclr_a0_bwd, v_iclr_a2_bwd, v_gate_g2, v_k_k, v_k_a, v_r_k, v_ln_x_g, v_ln_x_b, v_q_norm_g, v_w_uq, v_kv_norm_g, v_w_ukv, v_mla_out_g, v_w_out, v_ln_ffn_g, v_w_ffn_up, v_ffn_conv_w, v_ffn_conv_b, v_w_ffn_down, v_ln_final_g):
    given = dict(locals())
    w = {n: given[n] for n in WNAMES}
    mom = {n: given['m_' + n] for n in WNAMES}
    var = {n: given['v_' + n] for n in WNAMES}

    def split(names):
        return [n for n in names if n in BIG], [n for n in names if n not in BIG]

    def wire(names):
        big, small = split(names)
        pack = _pack([lax.bitcast_convert_type(_mat(w[n]), BF16) if n in EXACT else _mat(w[n]).astype(BF16) for n in small])
        return [_mat(w[n]).astype(BF16) for n in big] + [pack]

    def whole(names, gathered):
        big, small = split(names)
        out = {n: _join(g, n) for n, g in zip(big, gathered)}
        shapes = [_mat(w[n]).shape + ((2,) if n in EXACT else ()) for n in small]
        for n, s in zip(small, _unpack(gathered[-1], shapes, lead=1)):
            out[n] = _join(lax.bitcast_convert_type(s, F32) if n in EXACT else s.astype(F32), n)
        return out

    def grad_wire(names, grads):
        big, small = split(names)
        return [_cut(grads[n], n) for n in big] + [_pack([_cut(grads[n], n).astype(BF16) for n in small], lead=1)]

    early = [n for n in SHARDED if n not in LATE]
    rep = {n: _mat(w[n]) for n in REPLICATED}
    rep['r_k'] = w['r_k'].reshape(H, N)
    b, t, d = x.shape
    seq = lambda a: a.reshape(b, t, D_RWKV)
    flat = lambda a: a.reshape(b * t, D_RWKV)

    full_early = whole(early, _all_gather(wire(early), 'gather_weights'))
    ops, vjp_before = jax.vjp(_before_scan, full_early, rep, x)
    r, v, kk, w_f, k_f, q_f, w_b, k_b, q_b, gate, y_mla = ops
    scan_in = [seq(a) for a in (r, v, kk, w_f, k_f, q_f, w_b, k_b, q_b)]
    y_f, y_b, s_f, s_b, *late_gathered = _scan_fwd_call(*scan_in, wire(LATE))
    full_late = whole(LATE, late_gathered)
    loss_local, vjp_after = jax.vjp(_after_scan, full_late, rep, x, loss_target, y_f, y_b, r, k_f, k_b, v, gate, y_mla)

    g_late, g_rep_after, g_x_after, _, d_yf, d_yb, d_r, d_kf, d_kb, d_v, d_gate, d_ymla = vjp_after(jnp.ones((), F32))
    scan_out = _scan_bwd_call(*scan_in, s_f, s_b, d_yf, d_yb, grad_wire(LATE, g_late))
    parts_late = scan_out[12:]
    drf, dvf, dkkf, dwf, dkf, dqf, drb, dvb, dkkb, dwb, dkb, dqb = [flat(a) for a in scan_out[:12]]
    g_early, g_rep_before, g_x_before = vjp_before(
        (drf + drb + d_r, dvf + dvb + d_v, dkkf + dkkb, dwf, dkf + d_kf, dqf, dwb, dkb + d_kb, dqb, d_gate, d_ymla))
    g_rep = {n: g_rep_before[n] + g_rep_after[n] for n in rep}
    g_x = g_x_before + g_x_after
    parts_early = _grad_exchange(grad_wire(early, g_early), 'exchange_grads')

    s_out = [{}, {}, {}, {}]
    for names, parts, tag in ((early, parts_early, 'early'), (LATE, parts_late, 'late')):
        big, small = split(names)
        for n, p in zip(big, parts):
            res = _sum_adamw(p, _mat(w[n]), _mat(mom[n]), _mat(var[n]), 'adamw_' + n)
            for kind, o in enumerate(res):
                s_out[kind][n] = o.reshape(w[n].shape)
        res = _sum_adamw(parts[-1], _pack([w[n] for n in small]), _pack([mom[n] for n in small]),
                         _pack([var[n] for n in small]), 'adamw_small_' + tag)
        for kind, o in enumerate(res):
            s_out[kind].update(zip(small, _unpack(o, [w[n].shape for n in small])))

    zero = jnp.zeros((1,), F32)
    r_pack = _pack([g_rep[n] for n in REPLICATED] + [loss_local.reshape(1)])
    r_parts = _all_gather([r_pack], 'gather_small')[0]
    r_out = _sum_adamw(r_parts,_pack([w[n] for n in REPLICATED] + [zero]), _pack([mom[n] for n in REPLICATED] + [zero]),
                       _pack([var[n] for n in REPLICATED] + [zero]), 'adamw_replicated')
    r_out = [_unpack(o, [w[n].shape for n in REPLICATED] + [(1,)]) for o in r_out]

    loss = r_out[0][-1].reshape(())
    outs = [loss, g_x]
    for kind in range(4):
        by_name = dict(s_out[kind])
        by_name.update(zip(REPLICATED, r_out[kind][:-1]))
        outs += [by_name[n] for n in WNAMES]
    return tuple(outs)
```

```python
import functools

import jax
import jax.numpy as jnp
from jax import lax
from jax.experimental import pallas as pl
from jax.experimental.pallas import tpu as pltpu

F32 = jnp.float32
BF16 = jnp.bfloat16
MESH = pl.DeviceIdType.MESH

N_DEV = 8
LANES = 128
SUBLANES = 8
PACK_TILE = 2 * SUBLANES * LANES
PACK_ROWS = 512
ADAM_ROWS = 256
MM_TILE = 512
MM_TILE_WIDE = 1408
MM_K_WHOLE = 2816
VMEM_LIMIT = 56 * 1024 * 1024

H = 8
N = 64
D_RWKV = H * N
D_NOPE, D_ROPE, D_V = 64, 32, 64
D_QK = D_NOPE + D_ROPE
MLA_SCALE = D_QK ** -0.5
ROPE_THETA = 10000.0
RWKV_SPLITS = (D_RWKV, D_RWKV, D_RWKV, 64, 64, 64, 64, 128)
RWKV_COLS = sum(RWKV_SPLITS)
Q_LORA, KV_LORA = 768, 256
D_FF = 2816
NORM_EPS = 1e-6
GN_EPS = 64e-5
L2_EPS = 1e-12
ADAM_LR, ADAM_B1, ADAM_B2, ADAM_EPS, ADAM_WD, ADAM_STEP = 0.001, 0.9, 0.999, 1e-08, 0.01, 10

SCAN_CHUNK = 32
ATT_TQ = 256
SEG = 256
FFN_COLS = 256
ROW_TILE = 256
SHIFT_COLS = 384

WNAMES = ['ln_mix_g', 'w_in', 'shift_mu_prev', 'shift_mu_next', 'decay_w0_fwd', 'decay_w2_fwd', 'decay_w0_bwd',
          'decay_w2_bwd', 'iclr_a0_fwd', 'iclr_a2_fwd', 'iclr_a0_bwd', 'iclr_a2_bwd', 'gate_g2', 'k_k', 'k_a', 'r_k',
          'ln_x_g', 'ln_x_b', 'q_norm_g', 'w_uq', 'kv_norm_g', 'w_ukv', 'mla_out_g', 'w_out', 'ln_ffn_g', 'w_ffn_up',
          'ffn_conv_w', 'ffn_conv_b', 'w_ffn_down', 'ln_final_g']
COL = ('w_in', 'decay_w2_fwd', 'decay_w2_bwd', 'iclr_a2_fwd', 'iclr_a2_bwd', 'gate_g2', 'w_ukv', 'w_ffn_up', 'ffn_conv_w')
ROW = ('w_uq', 'w_out', 'w_ffn_down')
SHARDED = [n for n in WNAMES if n in COL or n in ROW]
REPLICATED = [n for n in WNAMES if n not in SHARDED]
EXACT = ('ffn_conv_w',)
LATE = ['w_out', 'w_ffn_up', 'ffn_conv_w', 'w_ffn_down']
BIG = ('w_in', 'w_uq', 'w_ukv', 'w_out', 'w_ffn_up', 'w_ffn_down')


def _params(*sem):
    return pltpu.CompilerParams(dimension_semantics=sem, vmem_limit_bytes=VMEM_LIMIT)


def _pack(arrs, lead=0):
    parts = []
    for a in arrs:
        head = a.shape[:lead]
        flat = a.reshape(head + (-1,))
        n = flat.shape[-1]
        n_pad = -(-n // PACK_TILE) * PACK_TILE
        flat = jnp.pad(flat, [(0, 0)] * lead + [(0, n_pad - n)])
        parts.append(flat.reshape(head + (n_pad // LANES, LANES)))
    out = jnp.concatenate(parts, axis=lead)
    rows = out.shape[lead]
    rows_pad = -(-rows // PACK_ROWS) * PACK_ROWS
    return jnp.pad(out, [(0, 0)] * lead + [(0, rows_pad - rows), (0, 0)])


def _unpack(packed, shapes, lead=0):
    outs, row = [], 0
    head = packed.shape[:lead]
    for shp in shapes:
        n = 1
        for s in shp:
            n *= s
        rows = -(-n // PACK_TILE) * (PACK_TILE // LANES)
        blk = lax.slice_in_dim(packed, row, row + rows, axis=lead)
        flat = blk.reshape(head + (rows * LANES,))
        outs.append(lax.slice_in_dim(flat, 0, n, axis=lead).reshape(head + tuple(shp)))
        row += rows
    return outs


PEERS = N_DEV - 1
HBM = pl.BlockSpec(memory_space=pl.ANY)


def _comm_sems(n):
    return [pltpu.SemaphoreType.DMA((PEERS * n,)), pltpu.SemaphoreType.DMA((PEERS * n,)), pltpu.SemaphoreType.DMA((n,))]


def _all_gather(xs, name):
    n = len(xs)

    def body(*refs):
        x_refs, out_refs, (send_sems, recv_sems, local_sems) = refs[:n], refs[n:2 * n], refs[2 * n:]
        mx, my, mc = lax.axis_index("x"), lax.axis_index("y"), lax.axis_index("c")
        me, sibling = (mx, my, mc), (mx, my, 1 - mc)
        chips = [(1 - mx, my), (mx, 1 - my), (1 - mx, 1 - my)]

        def slot(a, px, py, pc):
            return out_refs[a].at[4 * px + 2 * py + pc]

        def copy(a, k, block, to, src=None):
            return pltpu.make_async_remote_copy(
                src_ref=slot(a, *block) if src is None else src, dst_ref=slot(a, *block),
                send_sem=send_sems.at[PEERS * a + k], recv_sem=recv_sems.at[PEERS * a + k],
                device_id=to, device_id_type=MESH)

        mine = [pltpu.make_async_copy(x_refs[a], slot(a, *me), local_sems.at[a]) for a in range(n)]
        first, passed = [], []
        for a in range(n):
            mine[a].start()
            first.append(copy(a, 0, me, sibling, src=x_refs[a]))
            first += [copy(a, 1 + j, me, (*chip, mc), src=x_refs[a]) for j, chip in enumerate(chips)]
        for cp in first:
            cp.start()
        for j, chip in enumerate(chips):
            for a in range(n):
                copy(a, 1 + j, (*chip, mc), me).wait_recv()
                passed.append(copy(a, 4 + j, (*chip, mc), sibling))
                passed[-1].start()
        for a in range(n):
            copy(a, 0, sibling, me).wait_recv()
            for j, chip in enumerate(chips):
                copy(a, 4 + j, (*chip, 1 - mc), me).wait_recv()
        for cp in first + passed:
            cp.wait_send()
        for cp in mine:
            cp.wait()

    return pl.pallas_call(
        body, name=name, out_shape=[jax.ShapeDtypeStruct((N_DEV,) + x.shape, x.dtype) for x in xs],
        in_specs=[HBM] * n, out_specs=[HBM] * n, scratch_shapes=_comm_sems(n),
    )(*xs)


def _direct_exchange(src_refs, out_refs, send_sems, recv_sems, local_sems, per_peer):
    mx, my, mc = lax.axis_index("x"), lax.axis_index("y"), lax.axis_index("c")
    me = 4 * mx + 2 * my + mc

    def flip(v, bit):
        return 1 - v if bit else v

    def copies():
        mine, remote = [], []
        for a, (src, out) in enumerate(zip(src_refs, out_refs)):
            mine.append(pltpu.make_async_copy(src.at[me] if per_peer else src, out.at[me], local_sems.at[a]))
            for k in range(1, N_DEV):
                px, py, pc = flip(mx, k & 4), flip(my, k & 2), flip(mc, k & 1)
                remote.append(pltpu.make_async_remote_copy(
                    src_ref=src.at[4 * px + 2 * py + pc] if per_peer else src, dst_ref=out.at[me],
                    send_sem=send_sems.at[PEERS * a + k - 1], recv_sem=recv_sems.at[PEERS * a + k - 1],
                    device_id=(px, py, pc), device_id_type=MESH))
        return mine, remote

    def start():
        mine, remote = copies()
        for cp in mine + remote:
            cp.start()

    def wait():
        mine, remote = copies()
        for cp in remote:
            cp.wait_recv()
        for cp in remote:
            cp.wait_send()
        for cp in mine:
            cp.wait()

    return start, wait


def _grad_exchange(gs, xs, name):
    n, nx = len(gs), len(xs)

    def body(*refs):
        ins, outs, sems = refs[:n + nx], refs[n + nx:2 * (n + nx)], refs[2 * (n + nx):]
        start, wait = _direct_exchange(ins[:n], outs[:n], *sems[:3], per_peer=True)
        start_x, wait_x = _direct_exchange(ins[n:], outs[n:], *sems[3:], per_peer=False)
        start()
        start_x()
        wait_x()
        wait()

    return pl.pallas_call(
        body, name=name,
        out_shape=[jax.ShapeDtypeStruct(g.shape, g.dtype) for g in gs] + [jax.ShapeDtypeStruct((N_DEV,) + x.shape, x.dtype) for x in xs],
        in_specs=[HBM] * (n + nx), out_specs=[HBM] * (n + nx), scratch_shapes=_comm_sems(n) + _comm_sems(nx),
    )(*gs, *xs)


def _sum_adamw(parts, w, m, v, name):
    rows, cols = w.shape
    tr = next((t for t in range(ADAM_ROWS, 15, -16) if rows % t == 0), rows)
    c1 = 1.0 - ADAM_B1 ** ADAM_STEP
    c2 = 1.0 - ADAM_B2 ** ADAM_STEP

    def body(p_ref, w_ref, m_ref, v_ref, g_out, d_out, m_out, v_out):
        g = p_ref[0].astype(F32)
        for q in range(1, N_DEV):
            g = g + p_ref[q].astype(F32)
        m_new = ADAM_B1 * m_ref[...] + (1.0 - ADAM_B1) * g
        v_new = ADAM_B2 * v_ref[...] + (1.0 - ADAM_B2) * (g * g)
        m_hat = m_new / c1
        v_hat = v_new / c2
        g_out[...] = g
        d_out[...] = -ADAM_LR * (m_hat / (jnp.sqrt(v_hat) + ADAM_EPS) + ADAM_WD * w_ref[...])
        m_out[...] = m_new
        v_out[...] = v_new

    blk = pl.BlockSpec((tr, cols), lambda i: (i, 0))
    out = jax.ShapeDtypeStruct((rows, cols), F32)
    return pl.pallas_call(
        body, name=name, grid=(rows // tr,),
        in_specs=[pl.BlockSpec((N_DEV, tr, cols), lambda i: (0, i, 0)), blk, blk, blk],
        out_specs=[blk, blk, blk, blk], out_shape=[out, out, out, out],
        compiler_params=_params("parallel"),
    )(parts, w, m, v)


def _tile(dim, cap=MM_TILE):
    if dim <= cap:
        return dim
    for t in range(cap, LANES - 1, -LANES):
        if dim % t == 0:
            return t
    return dim


def _mm_call(a, b, form, name, out_dtype=F32, base=None):
    if form == 'nn':
        (m, k), n = a.shape, b.shape[1]
    elif form == 'nt':
        (m, k), n = a.shape, b.shape[0]
    else:
        (k, m), n = a.shape, b.shape[1]
    tk = k if (form == 'nn' and k <= MM_K_WHOLE) else _tile(k, MM_TILE_WIDE)
    tm = _tile(m, MM_TILE_WIDE if form == 'tn' else MM_TILE)
    tn = _tile(n, MM_TILE_WIDE)
    nk = k // tk
    contract = {'nn': ((1,), (0,)), 'nt': ((1,), (1,)), 'tn': ((0,), (0,))}[form]

    acc_in_out = nk == 1 or out_dtype == F32
    assert base is None or out_dtype == F32
    extra = [] if base is None else [base]

    def body(a_ref, b_ref, *rest):
        o_ref, acc = rest[len(extra)], rest[len(extra) + 1:]
        part = lax.dot_general(a_ref[...].astype(BF16), b_ref[...].astype(BF16), (contract, ((), ())),
                               preferred_element_type=F32)
        if nk == 1:
            o_ref[...] = (part + rest[0][...] if extra else part).astype(out_dtype)
            return
        acc_ref = o_ref if acc_in_out else acc[0]

        @pl.when(pl.program_id(2) == 0)
        def _():
            acc_ref[...] = part + rest[0][...] if extra else part

        @pl.when(pl.program_id(2) > 0)
        def _():
            acc_ref[...] += part

        if not acc_in_out:
            @pl.when(pl.program_id(2) == nk - 1)
            def _():
                o_ref[...] = acc_ref[...].astype(out_dtype)

    a_spec = pl.BlockSpec((tk, tm), lambda j, i, l: (l, i)) if form == 'tn' else pl.BlockSpec((tm, tk), lambda j, i, l: (i, l))
    b_spec = pl.BlockSpec((tn, tk), lambda j, i, l: (j, l)) if form == 'nt' else pl.BlockSpec((tk, tn), lambda j, i, l: (l, j))
    o_spec = pl.BlockSpec((tm, tn), lambda j, i, l: (i, j))
    return pl.pallas_call(
        body, name=name, grid=(n // tn, m // tm, nk),
        in_specs=[a_spec, b_spec] + [o_spec] * len(extra), out_specs=o_spec,
        out_shape=jax.ShapeDtypeStruct((m, n), out_dtype),
        scratch_shapes=[] if acc_in_out else [pltpu.VMEM((tm, tn), F32)],
        compiler_params=_params("parallel", "parallel", "arbitrary"),
    )(a, b, *extra)


def _make_mm(name):
    @jax.custom_vjp
    def mm(a, b):
        return _mm_call(a, b, 'nn', name + '_fwd')

    def fwd(a, b):
        return _mm_call(a, b, 'nn', name + '_fwd'), (a, b)

    def bwd(res, g):
        a, b = res
        return _mm_call(g, b, 'nt', name + '_da'), _mm_call(a, g, 'tn', name + '_db', out_dtype=BF16)

    mm.defvjp(fwd, bwd)
    return mm


def _make_mm_pair(name):
    def both(a, b1, b2):
        return _mm_call(a, b1, 'nn', name + '_1_fwd'), _mm_call(a, b2, 'nn', name + '_2_fwd')

    mm = jax.custom_vjp(both)

    def fwd(a, b1, b2):
        return both(a, b1, b2), (a, b1, b2)

    def bwd(res, gs):
        a, b1, b2 = res
        da = _mm_call(gs[1], b2, 'nt', name + '_2_da', base=_mm_call(gs[0], b1, 'nt', name + '_1_da'))
        return (da, _mm_call(a, gs[0], 'tn', name + '_1_db', out_dtype=BF16),
                _mm_call(a, gs[1], 'tn', name + '_2_db', out_dtype=BF16))

    mm.defvjp(fwd, bwd)
    return mm


def _make_mm_add(name):
    @jax.custom_vjp
    def mm(base, a, b):
        return _mm_call(a, b, 'nn', name + '_fwd', base=base)

    def fwd(base, a, b):
        return _mm_call(a, b, 'nn', name + '_fwd', base=base), (a, b)

    def bwd(res, g):
        a, b = res
        return g, _mm_call(g, b, 'nt', name + '_da'), _mm_call(a, g, 'tn', name + '_db', out_dtype=BF16)

    mm.defvjp(fwd, bwd)
    return mm


def _rms_fwd_call(x, g, name):
    m, d = x.shape
    tm = _tile(m)

    def body(x_ref, g_ref, o_ref):
        xv = x_ref[...]
        rinv = lax.rsqrt(jnp.mean(xv * xv, axis=-1, keepdims=True) + NORM_EPS)
        o_ref[...] = xv * rinv * g_ref[...]

    return pl.pallas_call(
        body, name=name, grid=(m // tm,),
        in_specs=[pl.BlockSpec((tm, d), lambda i: (i, 0)), pl.BlockSpec((1, d), lambda i: (0, 0))],
        out_specs=pl.BlockSpec((tm, d), lambda i: (i, 0)), out_shape=jax.ShapeDtypeStruct((m, d), F32),
        compiler_params=_params("parallel"),
    )(x, g)


def _rms_bwd_call(x, g, dy, name, d_skip=None):
    m, d = x.shape
    tm = _tile(m)
    extra = [] if d_skip is None else [d_skip]

    def body(x_ref, g_ref, dy_ref, *rest):
        dx_ref, dg_ref = rest[len(extra):]

        @pl.when(pl.program_id(0) == 0)
        def _():
            dg_ref[...] = jnp.zeros_like(dg_ref)

        xv, dyv = x_ref[...], dy_ref[...]
        rinv = lax.rsqrt(jnp.mean(xv * xv, axis=-1, keepdims=True) + NORM_EPS)
        xh = xv * rinv
        dg_ref[...] += jnp.sum(dyv * xh, axis=0, keepdims=True)
        dxh = dyv * g_ref[...]
        dx = rinv * (dxh - xh * jnp.mean(dxh * xh, axis=-1, keepdims=True))
        dx_ref[...] = dx + rest[0][...] if extra else dx

    row = pl.BlockSpec((tm, d), lambda i: (i, 0))
    vec = pl.BlockSpec((1, d), lambda i: (0, 0))
    return pl.pallas_call(
        body, name=name, grid=(m // tm,), in_specs=[row, vec, row] + [row] * len(extra), out_specs=[row, vec],
        out_shape=[jax.ShapeDtypeStruct((m, d), F32), jax.ShapeDtypeStruct((1, d), F32)],
        compiler_params=_params("arbitrary"),
    )(x, g, dy, *extra)


def _make_rms(name):
    @jax.custom_vjp
    def rms(x, g):
        return _rms_fwd_call(x, g, name + '_fwd')

    def fwd(x, g):
        return _rms_fwd_call(x, g, name + '_fwd'), (x, g)

    def bwd(res, dy):
        x, g = res
        dx, dg = _rms_bwd_call(x, g, dy, name + '_bwd')
        return dx, dg

    rms.defvjp(fwd, bwd)
    return rms


def _final_loss_call(x, g, target, ct=None):
    m, d = x.shape
    tm = _tile(m)
    row = pl.BlockSpec((tm, d), lambda i: (i, 0))
    vec = pl.BlockSpec((1, d), lambda i: (0, 0))
    acc = pl.BlockSpec((1, LANES), lambda i: (0, 0))

    def normed(x_ref, g_ref):
        xv = x_ref[...]
        rinv = lax.rsqrt(jnp.mean(xv * xv, axis=-1, keepdims=True) + NORM_EPS)
        return rinv, xv * rinv

    def fwd_body(x_ref, g_ref, t_ref, loss_ref):
        @pl.when(pl.program_id(0) == 0)
        def _():
            loss_ref[...] = jnp.zeros_like(loss_ref)

        _, xh = normed(x_ref, g_ref)
        err = xh * g_ref[...] - t_ref[...]
        loss_ref[...] += 0.5 * jnp.sum(jnp.mean(err * err, axis=-1, keepdims=True), axis=0, keepdims=True)

    def bwd_body(x_ref, g_ref, t_ref, ct_ref, dx_ref, dg_ref):
        @pl.when(pl.program_id(0) == 0)
        def _():
            dg_ref[...] = jnp.zeros_like(dg_ref)

        rinv, xh = normed(x_ref, g_ref)
        dyv = (xh * g_ref[...] - t_ref[...]) * (ct_ref[0:1, 0:1] * (1.0 / d))
        dg_ref[...] += jnp.sum(dyv * xh, axis=0, keepdims=True)
        dxh = dyv * g_ref[...]
        dx_ref[...] = rinv * (dxh - xh * jnp.mean(dxh * xh, axis=-1, keepdims=True))

    if ct is None:
        return pl.pallas_call(
            fwd_body, name='final_loss_fwd', grid=(m // tm,), in_specs=[row, vec, row], out_specs=acc,
            out_shape=jax.ShapeDtypeStruct((1, LANES), F32), compiler_params=_params("arbitrary"),
        )(x, g, target)
    return pl.pallas_call(
        bwd_body, name='final_loss_bwd', grid=(m // tm,), in_specs=[row, vec, row, acc], out_specs=[row, vec],
        out_shape=[jax.ShapeDtypeStruct((m, d), F32), jax.ShapeDtypeStruct((1, d), F32)],
        compiler_params=_params("arbitrary"),
    )(x, g, target, ct)


@jax.custom_vjp
def _final_loss(x, g, target):
    return _final_loss_call(x, g, target)[0, 0]


def _final_loss_bwd(res, ct):
    x, g, target = res
    dx, dg = _final_loss_call(x, g, target, ct=jnp.full((1, LANES), ct, F32))
    return dx, dg, jnp.zeros_like(target)


_final_loss.defvjp(lambda x, g, target: (_final_loss_call(x, g, target)[0, 0], (x, g, target)), _final_loss_bwd)


def _make_rms_skip(name):
    @jax.custom_vjp
    def rms(x, g):
        return _rms_fwd_call(x, g, name + '_fwd'), x

    def fwd(x, g):
        return (_rms_fwd_call(x, g, name + '_fwd'), x), (x, g)

    def bwd(res, cts):
        x, g = res
        dx, dg = _rms_bwd_call(x, g, cts[0], name + '_bwd', d_skip=cts[1])
        return dx, dg

    rms.defvjp(fwd, bwd)
    return rms


def _time_shifts(x):
    t = x.shape[0]
    rows = lax.broadcasted_iota(jnp.int32, x.shape, 0)
    return (jnp.where(rows == 0, 0.0, pltpu.roll(x, 1, 0)), jnp.where(rows == t - 1, 0.0, pltpu.roll(x, t - 1, 0)))


def _conv3(x, cw_ref, cb_ref):
    xp, xn = _time_shifts(x)
    return cw_ref[0:1, :] * xp + cw_ref[1:2, :] * x + cw_ref[2:3, :] * xn + cb_ref[...]


def _glu_specs(b, t, f):
    tc = _tile(f, FFN_COLS)
    seq = pl.BlockSpec((1, t, tc), lambda j, bi: (bi, 0, j))
    cw = pl.BlockSpec((3, tc), lambda j, bi: (0, j))
    cb = pl.BlockSpec((1, tc), lambda j, bi: (0, j))
    return tc, seq, cw, cb


def _glu_fwd_call(ug, uv, cwg, cwv, cbg, cbv):
    b, t, f = ug.shape
    tc, seq, cw, cb = _glu_specs(b, t, f)

    def body(ug_ref, uv_ref, cwg_ref, cwv_ref, cbg_ref, cbv_ref, o_ref):
        g = _conv3(ug_ref[0], cwg_ref, cbg_ref)
        o_ref[0] = g * jax.nn.sigmoid(g) * _conv3(uv_ref[0], cwv_ref, cbv_ref)

    return pl.pallas_call(
        body, name='glu_fwd', grid=(f // tc, b), in_specs=[seq, seq, cw, cw, cb, cb], out_specs=seq,
        out_shape=jax.ShapeDtypeStruct((b, t, f), F32), compiler_params=_params("parallel", "parallel"),
    )(ug, uv, cwg, cwv, cbg, cbv)


def _glu_bwd_call(ug, uv, cwg, cwv, cbg, cbv, dact):
    b, t, f = ug.shape
    tc, seq, cw, cb = _glu_specs(b, t, f)

    def body(ug_ref, uv_ref, cwg_ref, cwv_ref, cbg_ref, cbv_ref, da_ref,
             dug_ref, duv_ref, dcwg_ref, dcwv_ref, dcbg_ref, dcbv_ref):
        @pl.when(pl.program_id(1) == 0)
        def _():
            for ref in (dcwg_ref, dcwv_ref, dcbg_ref, dcbv_ref):
                ref[...] = jnp.zeros_like(ref)

        g = _conv3(ug_ref[0], cwg_ref, cbg_ref)
        v = _conv3(uv_ref[0], cwv_ref, cbv_ref)
        sig = jax.nn.sigmoid(g)
        da = da_ref[0]
        dv = da * (g * sig)
        dg = da * v * (sig * (1.0 + g * (1.0 - sig)))

        def conv_bwd(dc, x_ref, cw_ref, dx_ref, dcw_ref, dcb_ref):
            dcp, dcn = _time_shifts(dc)
            dx_ref[0] = cw_ref[0:1, :] * dcn + cw_ref[1:2, :] * dc + cw_ref[2:3, :] * dcp
            x = x_ref[0]
            for n, ds in enumerate((dcn, dc, dcp)):
                dcw_ref[n:n + 1, :] += jnp.sum(ds * x, axis=0, keepdims=True)
            dcb_ref[...] += jnp.sum(dc, axis=0, keepdims=True)

        conv_bwd(dg, ug_ref, cwg_ref, dug_ref, dcwg_ref, dcbg_ref)
        conv_bwd(dv, uv_ref, cwv_ref, duv_ref, dcwv_ref, dcbv_ref)

    big = jax.ShapeDtypeStruct((b, t, f), F32)
    return pl.pallas_call(
        body, name='glu_bwd', grid=(f // tc, b), in_specs=[seq, seq, cw, cw, cb, cb, seq],
        out_specs=[seq, seq, cw, cw, cb, cb],
        out_shape=[big, big, jax.ShapeDtypeStruct((3, f), F32), jax.ShapeDtypeStruct((3, f), F32),
                   jax.ShapeDtypeStruct((1, f), F32), jax.ShapeDtypeStruct((1, f), F32)],
        compiler_params=_params("parallel", "arbitrary"),
    )(ug, uv, cwg, cwv, cbg, cbv, dact)


def _shift_call(z, mu_p, mu_n, dzs=None):
    b, t, c = z.shape
    tc = _tile(c, SHIFT_COLS)
    seq = pl.BlockSpec((1, t, tc), lambda j, bi: (bi, 0, j))
    row = pl.BlockSpec((1, tc), lambda j, bi: (0, j))

    def fwd_body(z_ref, mp_ref, mn_ref, o_ref):
        x = z_ref[0]
        xp, xn = _time_shifts(x)
        o_ref[0] = x + mp_ref[...] * (xp - x) + mn_ref[...] * (xn - x)

    def bwd_body(z_ref, mp_ref, mn_ref, d_ref, dz_ref, dmp_ref, dmn_ref):
        @pl.when(pl.program_id(1) == 0)
        def _():
            dmp_ref[...] = jnp.zeros_like(dmp_ref)
            dmn_ref[...] = jnp.zeros_like(dmn_ref)

        x, d = z_ref[0], d_ref[0]
        xp, xn = _time_shifts(x)
        dp, dn = _time_shifts(d)
        mp, mn = mp_ref[...], mn_ref[...]
        dz_ref[0] = d * (1.0 - mp - mn) + mp * dn + mn * dp
        dmp_ref[...] += jnp.sum(d * (xp - x), axis=0, keepdims=True)
        dmn_ref[...] += jnp.sum(d * (xn - x), axis=0, keepdims=True)

    if dzs is None:
        return pl.pallas_call(
            fwd_body, name='shift_fwd', grid=(c // tc, b), in_specs=[seq, row, row], out_specs=seq,
            out_shape=jax.ShapeDtypeStruct(z.shape, F32), compiler_params=_params("parallel", "parallel"),
        )(z, mu_p, mu_n)
    return pl.pallas_call(
        bwd_body, name='shift_bwd', grid=(c // tc, b), in_specs=[seq, row, row, seq], out_specs=[seq, row, row],
        out_shape=[jax.ShapeDtypeStruct(z.shape, F32), jax.ShapeDtypeStruct(mu_p.shape, F32),
                   jax.ShapeDtypeStruct(mu_n.shape, F32)],
        compiler_params=_params("parallel", "arbitrary"),
    )(z, mu_p, mu_n, dzs)


@jax.custom_vjp
def _token_shift(z, mu_p, mu_n):
    return _shift_call(z, mu_p, mu_n)


_token_shift.defvjp(lambda z, mu_p, mu_n: (_shift_call(z, mu_p, mu_n), (z, mu_p, mu_n)),
                    lambda res, d: tuple(_shift_call(*res, dzs=d)))


@jax.custom_vjp
def _conv_glu(ug, uv, cwg, cwv, cbg, cbv):
    return _glu_fwd_call(ug, uv, cwg, cwv, cbg, cbv)


def _conv_glu_fwd(*args):
    return _glu_fwd_call(*args), args


def _conv_glu_bwd(res, dact):
    return tuple(_glu_bwd_call(*res, dact))


_conv_glu.defvjp(_conv_glu_fwd, _conv_glu_bwd)


HEAD_LANES = 2 * D_NOPE
PAIR = 2


def _lane(shape):
    return lax.broadcasted_iota(jnp.int32, shape, len(shape) - 1)


def _rope(x, c, s1, s2):
    return x * c + pltpu.roll(x, HEAD_LANES - D_ROPE // 2, 1) * s1 + pltpu.roll(x, D_ROPE // 2, 1) * s2


def _rope_t(g, c, s1, s2):
    return g * c + pltpu.roll(g * s1, D_ROPE // 2, 1) + pltpu.roll(g * s2, HEAD_LANES - D_ROPE // 2, 1)


def _attn_setup(kv_ref, kr_ref, tabs, k2, v16):
    c, s1, s2 = (tb[...] for tb in tabs)
    krr = _rope(kr_ref[0], c, s1, s2)
    v16[...] = kv_ref[0].astype(BF16)
    for hh in range(PAIR):
        slab = kv_ref[0, :, HEAD_LANES * hh:HEAD_LANES * (hh + 1)]
        k2[hh] = jnp.where(_lane(slab.shape) < D_NOPE, slab, krr).astype(BF16)


def _attn_queries(q_ref, tabs, rows, hh):
    c, s1, s2 = (tb[rows, :] for tb in tabs)
    return (_rope(q_ref[0, :, HEAD_LANES * hh:HEAD_LANES * (hh + 1)], c, s1, s2) * MLA_SCALE).astype(BF16), (c, s1, s2)


def _attn_specs(b, t, tq):
    qspec = pl.BlockSpec((1, tq, PAIR * HEAD_LANES), lambda bi, p, i: (bi, i, p))
    kvspec = pl.BlockSpec((1, t, PAIR * HEAD_LANES), lambda bi, p, i: (bi, 0, p))
    krspec = pl.BlockSpec((1, t, HEAD_LANES), lambda bi, p, i: (bi, 0, 0))
    tab = pl.BlockSpec((t, HEAD_LANES), lambda bi, p, i: (0, 0))
    ospec = pl.BlockSpec((1, tq, PAIR * D_V), lambda bi, p, i: (bi, i, p))
    lspec = pl.BlockSpec((1, 1, tq, HEAD_LANES), lambda bi, p, i: (bi, p, i, 0))
    return qspec, kvspec, krspec, tab, ospec, lspec


def _attn_fwd_call(q, kv, kr, tabs):
    b, t, _ = q.shape
    tq = min(ATT_TQ, t)
    qspec, kvspec, krspec, tab, ospec, lspec = _attn_specs(b, t, tq)

    def body(q_ref, kv_ref, kr_ref, c_ref, s1_ref, s2_ref, o_ref, lse_ref, k2, v16):
        tabs = (c_ref, s1_ref, s2_ref)

        @pl.when(pl.program_id(2) == 0)
        def _():
            _attn_setup(kv_ref, kr_ref, tabs, k2, v16)

        rows = pl.ds(pl.multiple_of(pl.program_id(2) * tq, tq), tq)
        outs, lses = [], []
        for hh in range(PAIR):
            qh, _ = _attn_queries(q_ref, tabs, rows, hh)
            s = lax.dot_general(qh, k2[hh], (((1,), (1,)), ((), ())), preferred_element_type=F32)
            m = jnp.max(s, axis=-1, keepdims=True)
            p = jnp.exp(s - m)
            l = jnp.sum(p, axis=-1, keepdims=True)
            slab16 = v16[:, HEAD_LANES * hh:HEAD_LANES * (hh + 1)]
            outs.append(jnp.dot(p.astype(BF16), slab16, preferred_element_type=F32) / l)
            lses.append(m + jnp.log(l))
        low = _lane(outs[0].shape) < D_V
        o_ref[0] = jnp.where(low, pltpu.roll(outs[0], D_V, 1), outs[1])
        lse_ref[0, 0] = jnp.where(low, lses[0], lses[1])

    return pl.pallas_call(
        body, name='attn_fwd', grid=(b, H // PAIR, t // tq),
        in_specs=[qspec, kvspec, krspec, tab, tab, tab], out_specs=[ospec, lspec],
        out_shape=[jax.ShapeDtypeStruct((b, t, H * D_V), F32), jax.ShapeDtypeStruct((b, H // PAIR, t, HEAD_LANES), F32)],
        scratch_shapes=[pltpu.VMEM((PAIR, t, HEAD_LANES), BF16), pltpu.VMEM((t, PAIR * HEAD_LANES), BF16)],
        compiler_params=_params("parallel", "parallel", "arbitrary"),
    )(q, kv, kr, *tabs)


def _attn_bwd_call(q, kv, kr, tabs, o, lse, do):
    b, t, _ = q.shape
    tq = min(ATT_TQ, t)
    n_q = t // tq
    qspec, kvspec, krspec, tab, ospec, lspec = _attn_specs(b, t, tq)

    def body(q_ref, kv_ref, kr_ref, c_ref, s1_ref, s2_ref, o_ref, lse_ref, do_ref, dq_ref, dkv_ref, dkr_ref, k2, v16, dk2):
        tabs = (c_ref, s1_ref, s2_ref)
        pair, step = pl.program_id(1), pl.program_id(2)

        @pl.when(step == 0)
        def _():
            _attn_setup(kv_ref, kr_ref, tabs, k2, v16)
            dk2[...] = jnp.zeros_like(dk2)
            dkv_ref[...] = jnp.zeros_like(dkv_ref)

        @pl.when((step == 0) & (pair == 0))
        def _():
            dkr_ref[...] = jnp.zeros_like(dkr_ref)

        rows = pl.ds(pl.multiple_of(step * tq, tq), tq)
        dov, ov = do_ref[0], o_ref[0]
        lane = _lane(dov.shape)
        upper = lane >= D_V
        for hh in range(PAIR):
            qh, qtabs = _attn_queries(q_ref, tabs, rows, hh)
            s = lax.dot_general(qh, k2[hh], (((1,), (1,)), ((), ())), preferred_element_type=F32)
            p = jnp.exp(s - lse_ref[0, 0, :, D_V * hh:D_V * hh + 1])
            mine = upper if hh else ~upper
            delta = jnp.sum(jnp.where(mine, dov * ov, 0.0), axis=-1, keepdims=True)
            do_h = jnp.where(upper, dov if hh else pltpu.roll(dov, D_V, 1), 0.0).astype(BF16)
            slab16 = v16[:, HEAD_LANES * hh:HEAD_LANES * (hh + 1)]
            dp = lax.dot_general(do_h, slab16, (((1,), (1,)), ((), ())), preferred_element_type=F32)
            ds = (p * (dp - delta)).astype(BF16)
            dqh = jnp.dot(ds, k2[hh], preferred_element_type=F32) * MLA_SCALE
            dq_ref[0, :, HEAD_LANES * hh:HEAD_LANES * (hh + 1)] = _rope_t(dqh, *qtabs)
            dk2[hh] += lax.dot_general(ds, qh, (((0,), (0,)), ((), ())), preferred_element_type=F32)
            dkv_ref[0, :, HEAD_LANES * hh:HEAD_LANES * (hh + 1)] += lax.dot_general(
                p.astype(BF16), do_h, (((0,), (0,)), ((), ())), preferred_element_type=F32)

        @pl.when(step == n_q - 1)
        def _():
            c, s1, s2 = (tb[...] for tb in tabs)
            for hh in range(PAIR):
                g = dk2[hh]
                key_lane = _lane(g.shape)
                dkv_ref[0, :, HEAD_LANES * hh:HEAD_LANES * (hh + 1)] += jnp.where(key_lane < D_NOPE, g, 0.0)
                dkr_ref[0] += _rope_t(jnp.where(key_lane >= D_NOPE, g, 0.0), c, s1, s2)

    return pl.pallas_call(
        body, name='attn_bwd', grid=(b, H // PAIR, n_q),
        in_specs=[qspec, kvspec, krspec, tab, tab, tab, ospec, lspec, ospec], out_specs=[qspec, kvspec, krspec],
        out_shape=[jax.ShapeDtypeStruct(q.shape, F32), jax.ShapeDtypeStruct(kv.shape, F32), jax.ShapeDtypeStruct(kr.shape, F32)],
        scratch_shapes=[pltpu.VMEM((PAIR, t, HEAD_LANES), BF16), pltpu.VMEM((t, PAIR * HEAD_LANES), BF16),
                        pltpu.VMEM((PAIR, t, HEAD_LANES), F32)],
        compiler_params=_params("parallel", "arbitrary", "arbitrary"),
    )(q, kv, kr, *tabs, o, lse, do)


@jax.custom_vjp
def _attention(q, kv, kr, tabs):
    return _attn_fwd_call(q, kv, kr, tabs)[0]


def _attention_fwd(q, kv, kr, tabs):
    o, lse = _attn_fwd_call(q, kv, kr, tabs)
    return o, (q, kv, kr, tabs, o, lse)


def _attention_bwd(res, do):
    q, kv, kr, tabs, o, lse = res
    return (*_attn_bwd_call(q, kv, kr, tabs, o, lse, do), tuple(jnp.zeros_like(tb) for tb in tabs))


_attention.defvjp(_attention_fwd, _attention_bwd)


SROWS = N * D_RWKV // SEG


def _seg_ones():
    r = lax.broadcasted_iota(jnp.int32, (SEG, SEG), 0) // N
    c = lax.broadcasted_iota(jnp.int32, (SEG, SEG), 1) // N
    return (r == c).astype(BF16)


def _eye_mask():
    r = lax.broadcasted_iota(jnp.int32, (SROWS, SEG), 0) & (N - 1)
    c = lax.broadcasted_iota(jnp.int32, (SROWS, SEG), 1) & (N - 1)
    return r == c


def _row2(ref, bi, ti, dtype=F32):
    parts = [jnp.broadcast_to(ref[bi, pl.ds(ti, 1), pl.ds(SEG * q, SEG)].astype(dtype), (N, SEG))
             for q in range(D_RWKV // SEG)]
    return jnp.concatenate(parts, axis=0)


def _split2(x):
    hi = x.astype(BF16)
    return hi, (x - hi.astype(F32)).astype(BF16)


def _col_sum(x):
    return jnp.concatenate([jnp.sum(x[N * q:N * (q + 1)], axis=0, keepdims=True) for q in range(D_RWKV // SEG)], axis=1)


def _diag_row(col, mask):
    halves = []
    for g in range(D_RWKV // SEG):
        acc = jnp.zeros((SUBLANES, SEG), F32)
        for k in range(N // SUBLANES):
            rows = slice(N * g + SUBLANES * k, N * g + SUBLANES * (k + 1))
            acc = jnp.where(mask[rows], col[rows], acc)
        halves.append(jnp.sum(acc, axis=0, keepdims=True))
    return jnp.concatenate(halves, axis=1)


def _scan_specs(b, t, rev):
    nc = t // SCAN_CHUNK
    if rev:
        return (pl.BlockSpec((b, SCAN_CHUNK, D_RWKV), lambda c: (0, nc - 1 - c, 0)),
                pl.BlockSpec((b, SCAN_CHUNK, SROWS, SEG), lambda c: (0, nc - 1 - c, 0, 0)))
    return (pl.BlockSpec((b, SCAN_CHUNK, D_RWKV), lambda c: (0, c, 0)),
            pl.BlockSpec((b, SCAN_CHUNK, SROWS, SEG), lambda c: (0, c, 0, 0)))


def _scan_fwd_call(r, v, kk, wf, kf, qf, wb, kb, qb, ride):
    b, t, _ = r.shape
    n_chunks = t // SCAN_CHUNK
    last = SCAN_CHUNK - 1
    nr = len(ride)

    def body(rf, vf, kkf, wf_, kf_, qf_, rb, vb, kkb, wb_, kb_, qb_, *rest):
        (yf, yb, sf, sb), scratch = rest[nr:nr + 4], rest[2 * nr + 4:]
        states = scratch[:2 * b]
        send, arrive = _direct_exchange(rest[:nr], rest[nr + 4:2 * nr + 4], *scratch[2 * b:], per_peer=False)

        @pl.when(pl.program_id(0) == 0)
        def _():
            send()
            for st in states:
                st[...] = jnp.zeros_like(st)

        ones, mask = _seg_ones(), _eye_mask()
        zero16 = jnp.zeros((), BF16)
        chains = []
        for bi in range(b):
            chains.append((rf, vf, kkf, wf_, kf_, qf_, yf, sf, states[2 * bi], bi, False))
            chains.append((rb, vb, kkb, wb_, kb_, qb_, yb, sb, states[2 * bi + 1], bi, True))

        def tix(i, rev):
            return last - i if rev else i

        def put_y(y_, bi, ti, ycol):
            y_[bi, pl.ds(ti, 1), :] = _diag_row(ycol, mask)

        def steps(i, with_y):
            u_parts, v_parts, y_parts = [], [], []
            for (r_, v_, kk_, w_, k_, q_, y_, s_, st, bi, rev) in chains:
                ti = tix(i, rev)
                s = st[...]
                s_[bi, ti] = s
                u_parts.append((s * _row2(kk_, bi, ti)).astype(BF16))
                v_parts.append(jnp.where(mask, _row2(v_, bi, ti, BF16), zero16))
                if with_y:
                    y_parts.append((s * _row2(r_, bi, tix(i - 1, rev))).astype(BF16))
            seg = lambda parts: jnp.dot(jnp.concatenate(parts, axis=0), ones, preferred_element_type=F32)
            res_v, res_u = seg(v_parts), seg(u_parts)
            res_y = seg(y_parts) if with_y else None
            for n, (r_, v_, kk_, w_, k_, q_, y_, s_, st, bi, rev) in enumerate(chains):
                ti = tix(i, rev)
                u, vcol = res_u[n * SROWS:(n + 1) * SROWS], res_v[n * SROWS:(n + 1) * SROWS]
                if with_y:
                    put_y(y_, bi, tix(i - 1, rev), res_y[n * SROWS:(n + 1) * SROWS])
                st[...] = st[...] * _row2(w_, bi, ti) - u * _row2(q_, bi, ti) + vcol * _row2(k_, bi, ti)

        steps(0, False)

        def loop(i, carry):
            steps(i, True)
            return carry

        lax.fori_loop(1, SCAN_CHUNK, loop, 0, unroll=5)
        parts = [(c[8][...] * _row2(c[0], c[9], tix(last, c[10]))).astype(BF16) for c in chains]
        res = jnp.dot(jnp.concatenate(parts, axis=0), ones, preferred_element_type=F32)
        for n, c in enumerate(chains):
            put_y(c[6], c[9], tix(last, c[10]), res[n * SROWS:(n + 1) * SROWS])

        @pl.when(pl.program_id(0) == n_chunks - 1)
        def _():
            arrive()

    fr, fs = _scan_specs(b, t, False)
    br, bs = _scan_specs(b, t, True)
    y_shape = jax.ShapeDtypeStruct((b, t, D_RWKV), F32)
    s_shape = jax.ShapeDtypeStruct((b, t, SROWS, SEG), F32)
    return pl.pallas_call(
        body, name='scan_fwd', grid=(n_chunks,),
        in_specs=[fr] * 6 + [br] * 6 + [HBM] * nr, out_specs=[fr, br, fs, bs] + [HBM] * nr,
        out_shape=[y_shape, y_shape, s_shape, s_shape] + [jax.ShapeDtypeStruct((N_DEV,) + a.shape, a.dtype) for a in ride],
        scratch_shapes=[pltpu.VMEM((SROWS, SEG), F32)] * (2 * b) + _comm_sems(nr),
        compiler_params=_params("arbitrary"),
    )(r, v, kk, wf, kf, qf, r, v, kk, wb, kb, qb, *ride)


def _scan_bwd_call(r, v, kk, wf, kf, qf, wb, kb, qb, sf, sb, dyf, dyb, ride):
    b, t, _ = r.shape
    n_chunks = t // SCAN_CHUNK
    last = SCAN_CHUNK - 1
    nr = len(ride)

    def body(rf, vf, kkf, wf_, kf_, qf_, sf_, dyf_, rb, vb, kkb, wb_, kb_, qb_, sb_, dyb_, *rest):
        drf, dvf, dkkf, dwf, dkf, dqf, drb, dvb, dkkb, dwb, dkb, dqb = rest[nr:nr + 12]
        scratch = rest[2 * nr + 12:]
        send, arrive = _direct_exchange(rest[:nr], rest[nr + 12:2 * nr + 12], *scratch[8 * b:], per_peer=True)

        @pl.when(pl.program_id(0) == 0)
        def _():
            send()
            for n in range(2 * b):
                scratch[4 * n][...] = jnp.zeros_like(scratch[4 * n])

        ones, mask = _seg_ones(), _eye_mask()
        zero16 = jnp.zeros((), BF16)
        chains = []
        for bi in range(b):
            chains.append((rf, vf, kkf, wf_, kf_, qf_, sf_, dyf_, (drf, dvf, dkkf, dwf, dkf, dqf),
                           scratch[8 * bi:8 * bi + 4], bi, True))
            chains.append((rb, vb, kkb, wb_, kb_, qb_, sb_, dyb_, (drb, dvb, dkkb, dwb, dkb, dqb),
                           scratch[8 * bi + 4:8 * bi + 8], bi, False))

        def tix(i, rev):
            return last - i if rev else i

        def state_free_parts(v_, dy_, kk_, s_, bi, ti):
            return [jnp.where(mask, _row2(v_, bi, ti, BF16), zero16), jnp.where(mask, _row2(dy_, bi, ti, BF16), zero16),
                    (s_[bi, ti] * _row2(kk_, bi, ti)).astype(BF16)]

        def keep(scr, res, off):
            for n in range(3):
                scr[1 + n][...] = res[off + n * SROWS:off + (n + 1) * SROWS]
            return off + 3 * SROWS

        def first():
            parts = []
            for (r_, v_, kk_, w_, k_, q_, s_, dy_, outs, scr, bi, rev) in chains:
                parts += state_free_parts(v_, dy_, kk_, s_, bi, tix(0, rev))
            res = jnp.dot(jnp.concatenate(parts, axis=0), ones, preferred_element_type=F32)
            off = 0
            for c in chains:
                off = keep(c[9], res, off)

        def steps(i, has_next, recompute):
            parts = []
            for (r_, v_, kk_, w_, k_, q_, s_, dy_, outs, scr, bi, rev) in chains:
                ti = tix(i, rev)
                gst, vc, dc, uc = scr
                dycol = dc[...]
                if recompute:
                    sc = s_[bi, ti] * _row2(w_, bi, ti) - uc[...] * _row2(q_, bi, ti) + vc[...] * _row2(k_, bi, ti)
                else:
                    sc = s_[bi, tix(i - 1, rev)]
                outs[0][bi, pl.ds(ti, 1), :] = _col_sum(sc * dycol)
                g = gst[...] + dycol * _row2(r_, bi, ti)
                gst[...] = g
                parts.append((g * _row2(q_, bi, ti)).astype(BF16))
                parts.append((g * _row2(k_, bi, ti)).astype(BF16))
                if has_next:
                    parts += state_free_parts(v_, dy_, kk_, s_, bi, tix(i + 1, rev))
            res = jnp.dot(jnp.concatenate(parts, axis=0), ones, preferred_element_type=F32)
            off = 0
            for (r_, v_, kk_, w_, k_, q_, s_, dy_, outs, scr, bi, rev) in chains:
                ti = tix(i, rev)
                gst, vc, dc, uc = scr
                dr_, dv_, dkk_, dw_, dk_, dq_ = outs

                def put(ref, val, sign=1.0):
                    ref[bi, pl.ds(ti, 1), :] = sign * _col_sum(val)

                gq = res[off:off + SROWS]
                dv_[bi, pl.ds(ti, 1), :] = _diag_row(res[off + SROWS:off + 2 * SROWS], mask)
                off += 2 * SROWS
                g, sp = gst[...], s_[bi, ti]
                put(dk_, g * vc[...])
                put(dw_, g * sp)
                put(dq_, g * uc[...], -1.0)
                put(dkk_, sp * gq, -1.0)
                gst[...] = g * _row2(w_, bi, ti) - gq * _row2(kk_, bi, ti)
                if has_next:
                    off = keep(scr, res, off)

        first()
        steps(0, True, True)

        def loop(i, carry):
            steps(i, True, False)
            return carry

        lax.fori_loop(1, last, loop, 0)
        steps(last, False, False)

        @pl.when(pl.program_id(0) == n_chunks - 1)
        def _():
            arrive()

    fr, fs = _scan_specs(b, t, True)
    br, bs = _scan_specs(b, t, False)
    y_shape = jax.ShapeDtypeStruct((b, t, D_RWKV), F32)
    return pl.pallas_call(
        body, name='scan_bwd', grid=(n_chunks,),
        in_specs=[fr] * 6 + [fs, fr] + [br] * 6 + [bs, br] + [HBM] * nr,
        out_specs=[fr] * 6 + [br] * 6 + [HBM] * nr,
        out_shape=[y_shape] * 12 + [jax.ShapeDtypeStruct(a.shape, a.dtype) for a in ride],
        scratch_shapes=[pltpu.VMEM((SROWS, SEG), F32)] * (8 * b) + _comm_sems(nr),
        compiler_params=_params("arbitrary"),
    )(r, v, kk, wf, kf, qf, sf, dyf, r, v, kk, wb, kb, qb, sb, dyb, *ride)


def _rope_tables(t):
    half = D_ROPE // 2
    inv_freq = jnp.power(ROPE_THETA, -jnp.arange(0, D_ROPE, 2, dtype=F32) / D_ROPE)
    ang = jnp.arange(t, dtype=F32)[:, None] * inv_freq[None, :]
    cos, sin, zero = jnp.cos(ang), jnp.sin(ang), jnp.zeros((t, half), F32)
    tail = HEAD_LANES - D_QK
    c = jnp.concatenate([jnp.ones((t, D_NOPE), F32), cos, cos, jnp.ones((t, tail), F32)], axis=1)
    s1 = jnp.concatenate([jnp.zeros((t, D_NOPE), F32), -sin, zero, jnp.zeros((t, tail), F32)], axis=1)
    s2 = jnp.concatenate([jnp.zeros((t, D_NOPE), F32), zero, sin, jnp.zeros((t, tail), F32)], axis=1)
    return c, s1, s2


@jax.custom_vjp
def _dot16(a, w):
    return jnp.dot(a.astype(BF16), w.astype(BF16), preferred_element_type=F32)


def _dot16_fwd(a, w):
    a16, w16 = a.astype(BF16), w.astype(BF16)
    return jnp.dot(a16, w16, preferred_element_type=F32), (a16, w16)


def _dot16_bwd(res, g):
    a16, w16 = res
    g16 = g.astype(BF16)
    return (lax.dot_general(g16, w16, (((1,), (1,)), ((), ())), preferred_element_type=F32),
            lax.dot_general(a16, g16, (((0,), (0,)), ((), ())), preferred_element_type=F32))


_dot16.defvjp(_dot16_fwd, _dot16_bwd)


def _head_sum_tile(x):
    outs = []
    ones = _seg_ones()
    for q in range(x.shape[1] // SEG):
        hi, lo = _split2(x[:, SEG * q:SEG * (q + 1)])
        outs.append(jnp.dot(hi, ones, preferred_element_type=F32) + jnp.dot(lo, ones, preferred_element_type=F32))
    return jnp.concatenate(outs, axis=1)


@jax.custom_vjp
def _hsum(x):
    return _head_sum_tile(x)


_hsum.defvjp(lambda x: (_head_sum_tile(x), None), lambda _, g: (_head_sum_tile(g),))


def _softplus(x):
    return jnp.maximum(x, 0.0) + jnp.log(1.0 + jnp.exp(-jnp.abs(x)))


def _rwkv_pre_fn(k, wdf, wdb, adf, adb, gd, w0f, w2f, w0b, w2b, a0f, a2f, a0b, a2b, g2, k_k, k_a):
    w_f = jnp.exp(-jnp.exp(-_softplus(-(w0f + _dot16(jnp.tanh(wdf), w2f))) - 0.5))
    w_b = jnp.exp(-jnp.exp(-_softplus(-(w0b + _dot16(jnp.tanh(wdb), w2b))) - 0.5))
    a_f = jax.nn.sigmoid(a0f + _dot16(adf, a2f))
    a_b = jax.nn.sigmoid(a0b + _dot16(adb, a2b))
    gate = _dot16(jax.nn.sigmoid(gd), g2)
    kk = k * k_k
    kk = kk / jnp.maximum(jnp.sqrt(_hsum(kk * kk)), L2_EPS)
    return (kk, w_f, k * (1.0 + (a_f - 1.0) * k_a), kk * a_f, w_b, k * (1.0 + (a_b - 1.0) * k_a), kk * a_b, gate)


def _rwkv_post_fn(y_f, y_b, r, k_f, k_b, v, gate, ln_g, ln_b, r_k):
    y = y_f + y_b
    yc = y - _hsum(y) * (1.0 / N)
    var = _hsum(yc * yc) * (1.0 / N)
    y = yc * lax.rsqrt(var + GN_EPS) * ln_g + ln_b
    return ((y + _hsum(r * (k_f + k_b) * r_k) * v) * gate,)


def _make_rowwise(fn, name, n_rows, tm):
    def specs(arrs, whole):
        if whole:
            return [pl.BlockSpec(a.shape, lambda i: (0, 0)) for a in arrs]
        return [pl.BlockSpec((tm, a.shape[1]), lambda i: (i, 0)) for a in arrs]

    def out_widths(rows, params):
        tiles = [jax.ShapeDtypeStruct((tm, a.shape[1]), F32) for a in rows]
        return [o.shape[1] for o in jax.eval_shape(fn, *tiles, *params)]

    def fwd_call(rows, params):
        m = rows[0].shape[0]
        n_in = len(rows) + len(params)
        outs = [jax.ShapeDtypeStruct((m, d), F32) for d in out_widths(rows, params)]

        def body(*refs):
            for o_ref, o in zip(refs[n_in:], fn(*[ref[...] for ref in refs[:n_in]])):
                o_ref[...] = o

        return pl.pallas_call(
            body, name=name + '_fwd', grid=(m // tm,), in_specs=specs(rows, False) + specs(params, True),
            out_specs=specs(outs, False), out_shape=outs, compiler_params=_params("parallel"),
        )(*rows, *params)

    def bwd_call(rows, params, cts):
        m = rows[0].shape[0]
        n_in = len(rows) + len(params)
        n_all = n_in + len(cts)
        outs = ([jax.ShapeDtypeStruct(a.shape, F32) for a in rows] + [jax.ShapeDtypeStruct(a.shape, F32) for a in params])

        def body(*refs):
            _, vjp = jax.vjp(fn, *[ref[...] for ref in refs[:n_in]])
            grads = vjp(tuple(ref[...] for ref in refs[n_in:n_all]))
            d_rows, d_params = refs[n_all:n_all + len(rows)], refs[n_all + len(rows):]
            for ref, g in zip(d_rows, grads[:len(rows)]):
                ref[...] = g

            @pl.when(pl.program_id(0) == 0)
            def _():
                for ref in d_params:
                    ref[...] = jnp.zeros_like(ref)

            for ref, g in zip(d_params, grads[len(rows):]):
                ref[...] += g

        return pl.pallas_call(
            body, name=name + '_bwd', grid=(m // tm,),
            in_specs=specs(rows, False) + specs(params, True) + specs(cts, False),
            out_specs=specs(rows, False) + specs(params, True), out_shape=outs, compiler_params=_params("arbitrary"),
        )(*rows, *params, *cts)

    @jax.custom_vjp
    def op(*args):
        return tuple(fwd_call(args[:n_rows], args[n_rows:]))

    def op_fwd(*args):
        return tuple(fwd_call(args[:n_rows], args[n_rows:])), args

    def op_bwd(args, cts):
        return tuple(bwd_call(args[:n_rows], args[n_rows:], cts))

    op.defvjp(op_fwd, op_bwd)
    return op


def _rwkv_operands(z, full, rep):
    b, t, _ = z.shape
    m = b * t
    z = _token_shift(z, rep['shift_mu_prev'], rep['shift_mu_next']).reshape(m, RWKV_COLS)
    cols, at = [], 0
    for width in RWKV_SPLITS:
        cols.append(z[:, at:at + width])
        at += width
    r, k, v, *lora_in = cols
    kk, w_f, k_f, q_f, w_b, k_b, q_b, gate = _make_rowwise(_rwkv_pre_fn, 'rwkv_pre', 6, _tile(m, ROW_TILE))(
        k, *lora_in, rep['decay_w0_fwd'], full['decay_w2_fwd'], rep['decay_w0_bwd'], full['decay_w2_bwd'],
        rep['iclr_a0_fwd'], full['iclr_a2_fwd'], rep['iclr_a0_bwd'], full['iclr_a2_bwd'], full['gate_g2'],
        rep['k_k'], rep['k_a'])
    return r, v, kk, w_f, k_f, q_f, w_b, k_b, q_b, gate


def _mla_mixer(z, full, rep, b, t):
    m = b * t
    c_q, c_kv, k_rope = z[:, :Q_LORA], z[:, Q_LORA:Q_LORA + KV_LORA], z[:, Q_LORA + KV_LORA:]
    w_uq = jnp.pad(full['w_uq'].reshape(Q_LORA, H, D_QK), ((0, 0), (0, 0), (0, HEAD_LANES - D_QK))).reshape(Q_LORA, H * HEAD_LANES)
    q = _make_mm('mm_uq')(_make_rms('rms_q')(c_q, rep['q_norm_g']), w_uq)
    kv = _make_mm('mm_ukv')(_make_rms('rms_kv')(c_kv, rep['kv_norm_g']), full['w_ukv'])
    kr = jnp.pad(k_rope, ((0, 0), (D_NOPE, HEAD_LANES - D_QK)))
    o = _attention(q.reshape(b, t, -1), kv.reshape(b, t, -1), kr.reshape(b, t, HEAD_LANES), _rope_tables(t))
    return _make_rms('rms_mla_out')(o.reshape(m, H * D_V), rep['mla_out_g'])


def _before_scan(full, rep, x):
    b, t, d = x.shape
    m = b * t
    n1 = _make_rms('rms_mix')(x.reshape(m, d), rep['ln_mix_g'])
    w_in = full['w_in']
    mla_cols = w_in.shape[1] - RWKV_COLS
    w_mla = jnp.pad(w_in[:, RWKV_COLS:], ((0, 0), (0, -mla_cols % LANES)))
    z_rwkv, z_mla = _make_mm_pair('mm_in')(n1, w_in[:, :RWKV_COLS], w_mla)
    return (*_rwkv_operands(z_rwkv.reshape(b, t, RWKV_COLS), full, rep),
            _mla_mixer(z_mla[:, :mla_cols], full, rep, b, t))


def _after_scan(full, rep, x, target, y_f, y_b, r, k_f, k_b, v, gate, y_mla):
    b, t, d = x.shape
    m = b * t
    xf = x.reshape(m, d)
    y_rwkv = _make_rowwise(_rwkv_post_fn, 'rwkv_post', 7, _tile(m, ROW_TILE))(
        y_f.reshape(m, D_RWKV), y_b.reshape(m, D_RWKV), r, k_f, k_b, v, gate,
        rep['ln_x_g'], rep['ln_x_b'], rep['r_k'].reshape(1, D_RWKV))[0]
    w_out = full['w_out']
    h = _make_mm_add('mm_out_rwkv')(xf, y_rwkv, w_out[:D_RWKV])
    h = _make_mm_add('mm_out_mla')(h, y_mla, w_out[D_RWKV:])
    n2, h = _make_rms_skip('rms_ffn')(h, rep['ln_ffn_g'])
    w_up, cw, cb = full['w_ffn_up'], full['ffn_conv_w'], rep['ffn_conv_b']
    u_gate, u_val = _make_mm_pair('mm_up')(n2, w_up[:, :D_FF], w_up[:, D_FF:])
    act = _conv_glu(u_gate.reshape(b, t, D_FF), u_val.reshape(b, t, D_FF),
                    cw[:, :D_FF], cw[:, D_FF:], cb[:, :D_FF], cb[:, D_FF:]).reshape(m, D_FF)
    h = _make_mm_add('mm_down')(h, act, full['w_ffn_down'])
    return _final_loss(h, rep['ln_final_g'], target.reshape(m, d))


def _mat(a):
    if a.ndim == 1:
        return a.reshape(1, -1)
    if a.ndim == 3:
        return a.reshape(a.shape[1:])
    return a


def _join(shards, name):
    if name in ROW:
        return shards.reshape(-1, shards.shape[-1])
    return shards.transpose(1, 0, 2).reshape(shards.shape[1], -1)


def _cut(whole, name):
    r, c = whole.shape
    if name in ROW:
        return whole.reshape(N_DEV, r // N_DEV, c)
    return whole.reshape(r, N_DEV, c // N_DEV).transpose(1, 0, 2)


def kernel(x, ln_mix_g, w_in, shift_mu_prev, shift_mu_next, decay_w0_fwd, decay_w2_fwd, decay_w0_bwd, decay_w2_bwd, iclr_a0_fwd, iclr_a2_fwd, iclr_a0_bwd, iclr_a2_bwd, gate_g2, k_k, k_a, r_k, ln_x_g, ln_x_b, q_norm_g, w_uq, kv_norm_g, w_ukv, mla_out_g, w_out, ln_ffn_g, w_ffn_up, ffn_conv_w, ffn_conv_b, w_ffn_down, ln_final_g, loss_target, m_ln_mix_g, m_w_in, m_shift_mu_prev, m_shift_mu_next, m_decay_w0_fwd, m_decay_w2_fwd, m_decay_w0_bwd, m_decay_w2_bwd, m_iclr_a0_fwd, m_iclr_a2_fwd, m_iclr_a0_bwd, m_iclr_a2_bwd, m_gate_g2, m_k_k, m_k_a, m_r_k, m_ln_x_g, m_ln_x_b, m_q_norm_g, m_w_uq, m_kv_norm_g, m_w_ukv, m_mla_out_g, m_w_out, m_ln_ffn_g, m_w_ffn_up, m_ffn_conv_w, m_ffn_conv_b, m_w_ffn_down, m_ln_final_g, v_ln_mix_g, v_w_in, v_shift_mu_prev, v_shift_mu_next, v_decay_w0_fwd, v_decay_w2_fwd, v_decay_w0_bwd, v_decay_w2_bwd, v_iclr_a0_fwd, v_iclr_a2_fwd, v_iclr_a0_bwd, v_iclr_a2_bwd, v_gate_g2, v_k_k, v_k_a, v_r_k, v_ln_x_g, v_ln_x_b, v_q_norm_g, v_w_uq, v_kv_norm_g, v_w_ukv, v_mla_out_g, v_w_out, v_ln_ffn_g, v_w_ffn_up, v_ffn_conv_w, v_ffn_conv_b, v_w_ffn_down, v_ln_final_g):
    given = dict(locals())
    w = {n: given[n] for n in WNAMES}
    mom = {n: given['m_' + n] for n in WNAMES}
    var = {n: given['v_' + n] for n in WNAMES}

    def split(names):
        return [n for n in names if n in BIG], [n for n in names if n not in BIG]

    def wire(names):
        big, small = split(names)
        pack = _pack([lax.bitcast_convert_type(_mat(w[n]), BF16) if n in EXACT else _mat(w[n]).astype(BF16) for n in small])
        return [_mat(w[n]).astype(BF16) for n in big] + [pack]

    def whole(names, gathered):
        big, small = split(names)
        out = {n: _join(g, n) for n, g in zip(big, gathered)}
        shapes = [_mat(w[n]).shape + ((2,) if n in EXACT else ()) for n in small]
        for n, s in zip(small, _unpack(gathered[-1], shapes, lead=1)):
            out[n] = _join(lax.bitcast_convert_type(s, F32) if n in EXACT else s.astype(F32), n)
        return out

    def grad_wire(names, grads):
        big, small = split(names)
        return [_cut(grads[n], n) for n in big] + [_pack([_cut(grads[n], n).astype(BF16) for n in small], lead=1)]

    early = [n for n in SHARDED if n not in LATE]
    rep = {n: _mat(w[n]) for n in REPLICATED}
    rep['r_k'] = w['r_k'].reshape(H, N)
    b, t, d = x.shape
    seq = lambda a: a.reshape(b, t, D_RWKV)
    flat = lambda a: a.reshape(b * t, D_RWKV)

    full_early = whole(early, _all_gather(wire(early), 'gather_weights'))
    ops, vjp_before = jax.vjp(_before_scan, full_early, rep, x)
    r, v, kk, w_f, k_f, q_f, w_b, k_b, q_b, gate, y_mla = ops
    scan_in = [seq(a) for a in (r, v, kk, w_f, k_f, q_f, w_b, k_b, q_b)]
    y_f, y_b, s_f, s_b, *late_gathered = _scan_fwd_call(*scan_in, wire(LATE))
    full_late = whole(LATE, late_gathered)
    loss_local, vjp_after = jax.vjp(_after_scan, full_late, rep, x, loss_target, y_f, y_b, r, k_f, k_b, v, gate, y_mla)

    g_late, g_rep_after, g_x_after, _, d_yf, d_yb, d_r, d_kf, d_kb, d_v, d_gate, d_ymla = vjp_after(jnp.ones((), F32))
    scan_out = _scan_bwd_call(*scan_in, s_f, s_b, d_yf, d_yb, grad_wire(LATE, g_late))
    parts_late = scan_out[12:]
    drf, dvf, dkkf, dwf, dkf, dqf, drb, dvb, dkkb, dwb, dkb, dqb = [flat(a) for a in scan_out[:12]]
    g_early, g_rep_before, g_x_before = vjp_before(
        (drf + drb + d_r, dvf + dvb + d_v, dkkf + dkkb, dwf, dkf + d_kf, dqf, dwb, dkb + d_kb, dqb, d_gate, d_ymla))
    g_rep = {n: g_rep_before[n] + g_rep_after[n] for n in rep}
    g_x = g_x_before + g_x_after
    r_pack = _pack([g_rep[n] for n in REPLICATED] + [loss_local.reshape(1)])
    *parts_early, r_parts = _grad_exchange(grad_wire(early, g_early), [r_pack], 'exchange_grads')

    s_out = [{}, {}, {}, {}]
    for names, parts, tag in ((early, parts_early, 'early'), (LATE, parts_late, 'late')):
        big, small = split(names)
        for n, p in zip(big, parts):
            res = _sum_adamw(p, _mat(w[n]), _mat(mom[n]), _mat(var[n]), 'adamw_' + n)
            for kind, o in enumerate(res):
                s_out[kind][n] = o.reshape(w[n].shape)
        res = _sum_adamw(parts[-1], _pack([w[n] for n in small]), _pack([mom[n] for n in small]),
                         _pack([var[n] for n in small]), 'adamw_small_' + tag)
        for kind, o in enumerate(res):
            s_out[kind].update(zip(small, _unpack(o, [w[n].shape for n in small])))

    zero = jnp.zeros((1,), F32)
    r_out = _sum_adamw(r_parts,_pack([w[n] for n in REPLICATED] + [zero]), _pack([mom[n] for n in REPLICATED] + [zero]),
                       _pack([var[n] for n in REPLICATED] + [zero]), 'adamw_replicated')
    r_out = [_unpack(o, [w[n].shape for n in REPLICATED] + [(1,)]) for o in r_out]

    loss = r_out[0][-1].reshape(())
    outs = [loss, g_x]
    for kind in range(4):
        by_name = dict(s_out[kind])
        by_name.update(zip(REPLICATED, r_out[kind][:-1]))
        outs += [by_name[n] for n in WNAMES]
    return tuple(outs)
```
